```python
import jax, jax.numpy as jnp
from jax import lax
import numpy as np

D_MODEL = 2048
BATCH = 32
SEQ = 256
DEPTH = 1
DEC_BATCH = 2
DEC_SEQ = 2048
PAST_LEN = 512

GRID_W = 64
MIX_DIM = D_MODEL
RET_DIM = MIX_DIM // 2
MLSTM_DIM = MIX_DIM - RET_DIM
HEAD_DIM = 256
RET_HEADS = RET_DIM // HEAD_DIM
MLSTM_HEADS = MLSTM_DIM // HEAD_DIM
D_FF = ((8 * D_MODEL // 3 + 127) // 128) * 128
CONV_W = 3
CHUNK = 128
ROPE_BASE = 10000.0
N_IN = 4 * RET_DIM + 4 * MLSTM_DIM + 4 * MLSTM_HEADS
ALPHA = (2.0 * DEPTH) ** 0.25
BETA = (8.0 * DEPTH) ** -0.25

kernel_name = 'hybrid_retention_mlstm_flow_step'

F32 = jnp.float32


def _ln(x, eps=1e-6):
    xf = x.astype(F32)
    mu = jnp.mean(xf, axis=-1, keepdims=True)
    var = jnp.mean(jnp.square(xf - mu), axis=-1, keepdims=True)
    return ((xf - mu) * lax.rsqrt(var + eps)).astype(x.dtype)


def _seq_conv(u, w, grid):
    B, L, C = u.shape
    if grid:
        rows = L // GRID_W
        u = u.reshape(B * rows, GRID_W, C)
    n = u.shape[1]
    p = CONV_W // 2
    up = jnp.pad(u, ((0, 0), (p, p), (0, 0)))
    y = up[:, 0:n] * w[0]
    for t in range(1, CONV_W):
        y = y + up[:, t:t + n] * w[t]
    return y.reshape(B, L, C)


def _rope_2d(x):
    L, hd = x.shape[1], x.shape[-1]
    quarter = hd // 4
    half = hd // 2
    t = jnp.arange(L)
    row = (t // GRID_W).astype(F32)
    col = (t % GRID_W).astype(F32)
    inv = ROPE_BASE ** (-jnp.arange(quarter, dtype=F32) / quarter)

    def rot(xh, pos):
        ang = pos[:, None] * inv
        cos = jnp.cos(ang)[None, :, None, :].astype(x.dtype)
        sin = jnp.sin(ang)[None, :, None, :].astype(x.dtype)
        x1, x2 = xh[..., :quarter], xh[..., quarter:]
        return jnp.concatenate([x1 * cos - x2 * sin, x1 * sin + x2 * cos], axis=-1)

    return jnp.concatenate([rot(x[..., :half], row), rot(x[..., half:], col)], axis=-1)


def _to_chunks(a):
    B, L, H, d = a.shape
    return a.astype(F32).reshape(B, L // CHUNK, CHUNK, H, d).transpose(1, 0, 3, 2, 4)


def _gate_chunks(a):
    B, L, H = a.shape
    return a.astype(F32).reshape(B, L // CHUNK, CHUNK, H).transpose(1, 0, 3, 2)


def _from_chunks(o, B, L):
    H, d = o.shape[2], o.shape[-1]
    return o.transpose(1, 0, 3, 2, 4).reshape(B, L, H, d)


def _retention_dir(q, k, v, log_gamma, s0):
    B, L = q.shape[0], q.shape[1]
    pos = jnp.arange(CHUNK, dtype=F32)
    diff = pos[:, None] - pos[None, :]
    lg = log_gamma.astype(F32)
    dmask = jnp.where(diff >= 0, jnp.exp(lg[:, None, None] * jnp.maximum(diff, 0.0)), 0.0)
    q_dec = jnp.exp(lg[:, None] * (pos + 1.0))[..., None]
    k_dec = jnp.exp(lg[:, None] * (CHUNK - 1.0 - pos))[..., None]
    c_dec = jnp.exp(lg * CHUNK)[:, None, None]

    def step(s, xs):
        qb, kb, vb = xs
        att = jnp.einsum('bhik,bhjk->bhij', qb, kb) * dmask
        o = jnp.einsum('bhij,bhjv->bhiv', att, vb) + jnp.einsum('bhik,bhkv->bhiv', qb, s) * q_dec
        s = s * c_dec + jnp.einsum('bhjk,bhjv->bhkv', kb * k_dec, vb)
        return s, o

    s_fin, o = lax.scan(step, s0.astype(F32), (_to_chunks(q), _to_chunks(k), _to_chunks(v)))
    return _from_chunks(o, B, L).astype(q.dtype), s_fin


def _mlstm_dir(q, k, v, i_pre, log_f, c0, n0, m0):
    B, L = q.shape[0], q.shape[1]
    pos = jnp.arange(CHUNK)
    causal = pos[:, None] >= pos[None, :]

    def step(carry, xs):
        C, nv, m = carry
        qb, kb, vb, ib, fb = xs
        b = jnp.cumsum(fb, axis=-1)
        dm = jnp.where(causal, b[..., :, None] - b[..., None, :] + ib[..., None, :], -jnp.inf)
        inter = b + m[..., None]
        m_t = jnp.maximum(inter, jnp.max(dm, axis=-1))
        w = jnp.exp(dm - m_t[..., None])
        sp = jnp.exp(inter - m_t)
        s = jnp.einsum('bhik,bhjk->bhij', qb, kb) * w
        num = jnp.einsum('bhij,bhjv->bhiv', s, vb) + jnp.einsum('bhik,bhkv->bhiv', qb, C) * sp[..., None]
        den = jnp.sum(s, axis=-1) + jnp.einsum('bhik,bhk->bhi', qb, nv) * sp
        h = num / jnp.maximum(jnp.abs(den), jnp.exp(-m_t))[..., None]
        b_last = b[..., -1]
        g = b_last[..., None] - b + ib
        m_new = jnp.maximum(b_last + m, jnp.max(g, axis=-1))
        wk = jnp.exp(g - m_new[..., None])
        sc = jnp.exp(b_last + m - m_new)
        C = C * sc[..., None, None] + jnp.einsum('bhjk,bhjv->bhkv', kb * wk[..., None], vb)
        nv = nv * sc[..., None] + jnp.einsum('bhjk,bhj->bhk', kb, wk)
        return (C, nv, m_new), h

    (c_fin, n_fin, m_fin), h = lax.scan(
        step, (c0.astype(F32), n0.astype(F32), m0.astype(F32)),
        (_to_chunks(q), _to_chunks(k), _to_chunks(v), _gate_chunks(i_pre), _gate_chunks(log_f)))
    return _from_chunks(h, B, L).astype(q.dtype), c_fin, n_fin, m_fin


def _mixer(h, grid, s_ret0, c0, n0, m0, w_in, b_gate, conv_qk, ret_theta, gn_ret, gn_mlstm, w_out):
    B, L, _ = h.shape
    R, M, HM = RET_DIM, MLSTM_DIM, MLSTM_HEADS
    proj = h @ w_in
    rq, rk, rv, rg = (proj[..., i * R:(i + 1) * R] for i in range(4))
    off = 4 * R
    mq, mk, mv, mo = (proj[..., off + i * M:off + (i + 1) * M] for i in range(4))
    gates = (proj[..., off + 4 * M:] + b_gate).astype(F32).reshape(B, L, 4, HM)

    rq = rq.reshape(B, L, RET_HEADS, HEAD_DIM)
    rk = rk.reshape(B, L, RET_HEADS, HEAD_DIM) * (HEAD_DIM ** -0.5)
    rv = rv.reshape(B, L, RET_HEADS, HEAD_DIM)
    if grid:
        rq, rk = _rope_2d(rq), _rope_2d(rk)
    log_gamma = jax.nn.log_sigmoid(ret_theta.astype(F32))
    o_f, s_f = _retention_dir(rq, rk, rv, log_gamma[0], s_ret0[:, 0])
    o_b, s_b = _retention_dir(rq[:, ::-1], rk[:, ::-1], rv[:, ::-1], log_gamma[1], s_ret0[:, 1])
    ret_out = (_ln(o_f + o_b[:, ::-1]) * gn_ret).reshape(B, L, R) * jax.nn.silu(rg)

    qk = jax.nn.silu(_seq_conv(jnp.concatenate([mq, mk], axis=-1), conv_qk, grid))
    mq = qk[..., :M].reshape(B, L, HM, HEAD_DIM)
    mk = qk[..., M:].reshape(B, L, HM, HEAD_DIM) * (HEAD_DIM ** -0.5)
    mv = mv.reshape(B, L, HM, HEAD_DIM)
    i_f, lf_f = gates[:, :, 0], jax.nn.log_sigmoid(gates[:, :, 1])
    i_b, lf_b = gates[:, :, 2], jax.nn.log_sigmoid(gates[:, :, 3])
    h_f, cf, nf, mf = _mlstm_dir(mq, mk, mv, i_f, lf_f, c0[:, 0], n0[:, 0], m0[:, 0])
    h_b, cb, nb, mb = _mlstm_dir(mq[:, ::-1], mk[:, ::-1], mv[:, ::-1], i_b[:, ::-1], lf_b[:, ::-1],
                                 c0[:, 1], n0[:, 1], m0[:, 1])
    mlstm_out = jax.nn.sigmoid(mo) * (_ln(h_f + h_b[:, ::-1]) * gn_mlstm).reshape(B, L, M)

    out = jnp.concatenate([ret_out, mlstm_out], axis=-1) @ w_out
    states = (jnp.stack([s_f, s_b], axis=1), jnp.stack([cf, cb], axis=1),
              jnp.stack([nf, nb], axis=1), jnp.stack([mf, mb], axis=1))
    return out, states


def _ffn(h, grid, w_up, conv_ff, w_down):
    ug = h @ w_up
    u, g = ug[..., :D_FF], ug[..., D_FF:]
    return (jax.nn.silu(_seq_conv(u, conv_ff, grid)) * g) @ w_down


def _layer(x, mod, grid, s_ret0, c0, n0, m0, lp):
    (w_in, b_gate, conv_qk, ret_theta, gn_ret, gn_mlstm, w_out,
     ln1_g, ln1_b, w_up, conv_ff, w_down, ln2_g, ln2_b) = lp
    sh1, sc1, g1, sh2, sc2, g2 = jnp.split(mod, 6, axis=-1)
    h = _ln(x) * (1.0 + sc1) + sh1
    mix, states = _mixer(h, grid, s_ret0, c0, n0, m0, w_in, b_gate, conv_qk, ret_theta,
                         gn_ret, gn_mlstm, w_out)
    x = _ln(ALPHA * x + g1 * mix) * ln1_g + ln1_b
    h = _ln(x) * (1.0 + sc2) + sh2
    x = _ln(ALPHA * x + g2 * _ffn(h, grid, w_up, conv_ff, w_down)) * ln2_g + ln2_b
    return x, states


def setup_inputs(seed: int = 0) -> dict:
    key = jax.random.key(seed)
    ks = jax.random.split(key, 32)

    def nrm(k, shape, s):
        return s * jax.random.normal(k, shape, F32)

    D, HR, HM, HD = D_MODEL, RET_HEADS, MLSTM_HEADS, HEAD_DIM
    base_gamma = 1.0 - 2.0 ** (-5.0 - np.arange(HR, dtype=np.float32))
    theta0 = jnp.asarray(np.log(base_gamma / (1.0 - base_gamma)), F32)
    i_bias = nrm(ks[20], (DEPTH, 2, HM), 0.1)
    f_bias = jnp.asarray(np.linspace(3.0, 6.0, HM), F32) + nrm(ks[21], (DEPTH, 2, HM), 0.1)
    b_gate = jnp.stack([i_bias[:, 0], f_bias[:, 0], i_bias[:, 1], f_bias[:, 1]], axis=1).reshape(DEPTH, 4 * HM)
    return {
        'x_prompt': nrm(ks[0], (BATCH, SEQ, D), 1.0),
        'x_sample': nrm(ks[1], (DEC_BATCH, DEC_SEQ, D), 1.0),
        'state_ret': nrm(ks[2], (DEC_BATCH, DEPTH, 2, HR, HD, HD), 0.3),
        'state_mlstm_C': nrm(ks[3], (DEC_BATCH, DEPTH, 2, HM, HD, HD), 0.1),
        'state_mlstm_n': nrm(ks[4], (DEC_BATCH, DEPTH, 2, HM, HD), 0.1),
        'state_mlstm_m': 1.0 + nrm(ks[5], (DEC_BATCH, DEPTH, 2, HM), 0.5),
        'c': nrm(ks[6], (DEC_BATCH, D), 1.0),
        'c_ctx': nrm(ks[7], (D,), 1.0),
        'w_mod': nrm(ks[8], (DEPTH, D, 6 * D), 0.5 * D ** -0.5),
        'b_mod': nrm(ks[9], (DEPTH, 6 * D), 0.02),
        'w_in': nrm(ks[10], (DEPTH, D, N_IN), D ** -0.5),
        'b_gate': b_gate,
        'conv_qk': nrm(ks[11], (DEPTH, CONV_W, 2 * MLSTM_DIM), CONV_W ** -0.5),
        'ret_theta': theta0[None, None, :] + nrm(ks[12], (DEPTH, 2, HR), 0.05),
        'gn_ret': 1.0 + nrm(ks[13], (DEPTH, HR, HD), 0.02),
        'gn_mlstm': 1.0 + nrm(ks[14], (DEPTH, HM, HD), 0.02),
        'w_out': nrm(ks[15], (DEPTH, MIX_DIM, D), BETA * MIX_DIM ** -0.5),
        'ln1_g': 1.0 + nrm(ks[16], (DEPTH, D), 0.02),
        'ln1_b': nrm(ks[17], (DEPTH, D), 0.02),
        'w_up': nrm(ks[18], (DEPTH, D, 2 * D_FF), D ** -0.5),
        'conv_ff': nrm(ks[19], (DEPTH, CONV_W, D_FF), CONV_W ** -0.5),
        'w_down': nrm(ks[22], (DEPTH, D_FF, D), BETA * D_FF ** -0.5),
        'ln2_g': 1.0 + nrm(ks[23], (DEPTH, D), 0.02),
        'ln2_b': nrm(ks[24], (DEPTH, D), 0.02),
    }


def reference(x_prompt, x_sample, state_ret, state_mlstm_C, state_mlstm_n, state_mlstm_m, c, c_ctx,
              w_mod, b_mod, w_in, b_gate, conv_qk, ret_theta, gn_ret, gn_mlstm, w_out,
              ln1_g, ln1_b, w_up, conv_ff, w_down, ln2_g, ln2_b):
    Bp = x_prompt.shape[0]
    zero_ret = jnp.zeros((Bp, 2, RET_HEADS, HEAD_DIM, HEAD_DIM), F32)
    zero_C = jnp.zeros((Bp, 2, MLSTM_HEADS, HEAD_DIM, HEAD_DIM), F32)
    zero_n = jnp.zeros((Bp, 2, MLSTM_HEADS, HEAD_DIM), F32)
    zero_m = jnp.zeros((Bp, 2, MLSTM_HEADS), F32)
    y_prompt, y_sample = x_prompt, x_sample
    ret_list, C_list, n_list, m_list = [], [], [], []
    for l in range(DEPTH):
        lp = (w_in[l], b_gate[l], conv_qk[l], ret_theta[l], gn_ret[l], gn_mlstm[l], w_out[l],
              ln1_g[l], ln1_b[l], w_up[l], conv_ff[l], w_down[l], ln2_g[l], ln2_b[l])
        mod_ctx = (jax.nn.silu(c_ctx) @ w_mod[l] + b_mod[l])[None, None, :]
        mod_lat = (jax.nn.silu(c) @ w_mod[l] + b_mod[l])[:, None, :]
        y_prompt, st = _layer(y_prompt, mod_ctx, False, zero_ret, zero_C, zero_n, zero_m, lp)
        ret_list.append(st[0])
        C_list.append(st[1])
        n_list.append(st[2])
        m_list.append(st[3])
        y_sample, _ = _layer(y_sample, mod_lat, True, state_ret[:, l], state_mlstm_C[:, l],
                             state_mlstm_n[:, l], state_mlstm_m[:, l], lp)
    new_state_ret = jnp.stack(ret_list, axis=1)
    new_state_mlstm_C = jnp.stack(C_list, axis=1)
    new_state_mlstm_n = jnp.stack(n_list, axis=1)
    new_state_mlstm_m = jnp.stack(m_list, axis=1)
    return (y_prompt, y_sample, new_state_ret, new_state_mlstm_C, new_state_mlstm_n, new_state_mlstm_m)
```

```python
import functools

import jax
import jax.numpy as jnp
from jax import lax
from jax.experimental import pallas as pl
from jax.experimental.pallas import tpu as pltpu

F32 = jnp.float32
BF16 = jnp.bfloat16

D_MODEL = 2048
GRID_W = 64
HEAD_DIM = 256
N_HEADS = 4
GROUP_DIM = N_HEADS * HEAD_DIM
N_PROJ = 8 * GROUP_DIM
N_GATES = 4 * N_HEADS
D_FF = 5504
CHUNK = 128
ROPE_BASE = 10000.0
ALPHA = 2.0 ** 0.25
LN_EPS = 1e-6
K_SCALE = HEAD_DIM ** -0.5

LANES = 128
V7X_VMEM_LIMIT_CAP = 60 * 1024 * 1024
VMEM_TEMP_ALLOWANCE = 16 * 1024 * 1024

PROJ_TN = GROUP_DIM
FF_TN = 512
D_FF_PAD = ((D_FF + FF_TN - 1) // FF_TN) * FF_TN


def _vmem_limit(block_bytes, scratch_bytes=0):
    est = 2 * sum(block_bytes) + scratch_bytes + VMEM_TEMP_ALLOWANCE
    return int(min(V7X_VMEM_LIMIT_CAP, est))


def _nbytes(shape, dtype):
    n = 1
    for s in shape:
        n *= s
    return n * jnp.dtype(dtype).itemsize


def _ln_rows(x):
    mu = jnp.mean(x, axis=-1, keepdims=True)
    xc = x - mu
    var = jnp.mean(xc * xc, axis=-1, keepdims=True)
    return xc * lax.rsqrt(var + LN_EPS)


def _silu(x):
    return x * jax.nn.sigmoid(x)


def _log_sigmoid(x):
    return jnp.minimum(x, 0.0) - jnp.log1p(jnp.exp(-jnp.abs(x)))


def _conv3_rows(u, w, period):
    rows = u.shape[0]
    t = lax.broadcasted_iota(jnp.int32, (rows, 1), 0) & (period - 1)
    prev = jnp.where(t == 0, 0.0, pltpu.roll(u, 1, 0))
    nxt = jnp.where(t == period - 1, 0.0, pltpu.roll(u, rows - 1, 0))
    return prev * w[0:1] + u * w[1:2] + nxt * w[2:3]


def _mod_kernel(c_ref, w_ref, b_ref, o_ref):
    a = _silu(c_ref[...]).astype(BF16)
    o_ref[...] = jnp.dot(a, w_ref[...].astype(BF16), preferred_element_type=F32) + b_ref[...]


def _mod_call(cvec, w_mod, b_mod):
    rows, n = cvec.shape[0], w_mod.shape[1]
    tn = 1024
    blocks = [_nbytes((rows, D_MODEL), F32), _nbytes((D_MODEL, tn), F32),
              _nbytes((8, tn), F32), _nbytes((rows, tn), F32)]
    return pl.pallas_call(
        _mod_kernel,
        grid=(n // tn,),
        in_specs=[pl.BlockSpec((rows, D_MODEL), lambda j: (0, 0)),
                  pl.BlockSpec((D_MODEL, tn), lambda j: (0, j)),
                  pl.BlockSpec((1, tn), lambda j: (0, j))],
        out_specs=pl.BlockSpec((rows, tn), lambda j: (0, j)),
        out_shape=jax.ShapeDtypeStruct((rows, n), F32),
        compiler_params=pltpu.CompilerParams(
            dimension_semantics=("arbitrary",), vmem_limit_bytes=_vmem_limit(blocks)),
        name="mod",
    )(cvec, w_mod, b_mod)


def _mix_in_kernel(*refs, tm, period, grid_mode):
    if grid_mode:
        (x_ref, sh_ref, sc_ref, w_ref, wg_ref, bg_ref, cv_ref, cos_ref, sin_ref,
         proj_ref, gates_ref, h_scr) = refs
    else:
        (x_ref, sh_ref, sc_ref, w_ref, wg_ref, bg_ref, cv_ref,
         proj_ref, gates_ref, h_scr) = refs
    n = pl.program_id(1)

    @pl.when(n == 0)
    def _():
        h = _ln_rows(x_ref[...]) * (1.0 + sc_ref[0]) + sh_ref[0]
        hb = h.astype(BF16)
        h_scr[...] = hb
        g = lax.dot_general(wg_ref[...], hb, (((1,), (1,)), ((), ())),
                            preferred_element_type=F32) + bg_ref[...]
        for s in range(tm // LANES):
            gates_ref[s] = g[:, s * LANES:(s + 1) * LANES]

    acc = jnp.dot(h_scr[...], w_ref[...], preferred_element_type=F32)

    def rope_store(a):
        for s in range(PROJ_TN // LANES):
            xs = a[:, s * LANES:(s + 1) * LANES]
            t = (s % 2) * LANES
            y = xs * cos_ref[:, t:t + LANES] + pltpu.roll(xs, LANES // 2, 1) * sin_ref[:, t:t + LANES]
            proj_ref[:, s * LANES:(s + 1) * LANES] = y.astype(BF16)

    plain = (n == 2) | (n == 3) | (n == 6) | (n == 7)
    if not grid_mode:
        plain = plain | (n == 0)

    @pl.when(plain)
    def _():
        proj_ref[...] = acc.astype(BF16)

    if grid_mode:
        @pl.when(n == 0)
        def _():
            rope_store(acc)

        @pl.when(n == 1)
        def _():
            rope_store(acc * K_SCALE)
    else:
        @pl.when(n == 1)
        def _():
            proj_ref[...] = (acc * K_SCALE).astype(BF16)

    @pl.when(n == 4)
    def _():
        proj_ref[...] = _silu(_conv3_rows(acc, cv_ref[...], period)).astype(BF16)

    @pl.when(n == 5)
    def _():
        proj_ref[...] = (_silu(_conv3_rows(acc, cv_ref[...], period)) * K_SCALE).astype(BF16)


def _mix_in_call(x2d, mod3, mod_row, w_in_b, w_gt, b_gate_col, conv_qk, rope, *, seq_len, period, tm):
    tokens = x2d.shape[0]
    grid_mode = rope is not None
    tiles_per_seq = seq_len // tm

    in_specs = [
        pl.BlockSpec((tm, D_MODEL), lambda m, n: (m, 0)),
        pl.BlockSpec((1, 1, D_MODEL), lambda m, n: (mod_row(m) * 6 + 0, 0, 0)),
        pl.BlockSpec((1, 1, D_MODEL), lambda m, n: (mod_row(m) * 6 + 1, 0, 0)),
        pl.BlockSpec((D_MODEL, PROJ_TN), lambda m, n: (0, n)),
        pl.BlockSpec((N_GATES, D_MODEL), lambda m, n: (0, 0)),
        pl.BlockSpec((N_GATES, 1), lambda m, n: (0, 0)),
        pl.BlockSpec((3, PROJ_TN), lambda m, n: (0, jnp.clip(n - 4, 0, 1))),
    ]
    args = [x2d, mod3, mod3, w_in_b, w_gt, b_gate_col, conv_qk]
    blocks = [_nbytes((tm, D_MODEL), F32), _nbytes((D_MODEL, PROJ_TN), BF16),
              _nbytes((N_GATES, D_MODEL), BF16), _nbytes((tm, PROJ_TN), BF16),
              _nbytes((tm // LANES, N_GATES, LANES), F32), _nbytes((8, PROJ_TN), F32)]
    if grid_mode:
        in_specs += [pl.BlockSpec((tm, HEAD_DIM), lambda m, n: (m % tiles_per_seq, 0))] * 2
        args += list(rope)
        blocks += [_nbytes((tm, HEAD_DIM), F32)] * 2

    return pl.pallas_call(
        functools.partial(_mix_in_kernel, tm=tm, period=period, grid_mode=grid_mode),
        grid=(tokens // tm, N_PROJ // PROJ_TN),
        in_specs=in_specs,
        out_specs=[pl.BlockSpec((tm, PROJ_TN), lambda m, n: (m, n)),
                   pl.BlockSpec((tm // LANES, N_GATES, LANES), lambda m, n: (m, 0, 0))],
        out_shape=[jax.ShapeDtypeStruct((tokens, N_PROJ), BF16),
                   jax.ShapeDtypeStruct((tokens // LANES, N_GATES, LANES), F32)],
        scratch_shapes=[pltpu.VMEM((tm, D_MODEL), BF16)],
        compiler_params=pltpu.CompilerParams(
            dimension_semantics=("parallel", "arbitrary"),
            vmem_limit_bytes=_vmem_limit(blocks, _nbytes((tm, D_MODEL), BF16))),
        name="mix_in_grid" if grid_mode else "mix_in_seq",
    )(*args)


def _chunk_loop(nc, body, reverse=False):
    if nc <= 2:
        for i in range(nc):
            body(nc - 1 - i if reverse else i)
    else:
        def step(i, carry):
            body(nc - 1 - i if reverse else i)
            return carry
        lax.fori_loop(0, nc, step, 0)


def _rows(c):
    start = c * CHUNK
    if not isinstance(start, int):
        start = pl.multiple_of(start, CHUNK)
    return pl.ds(start, CHUNK)


def _ret_kernel(*refs, nc, has_init, emit_states):
    refs = list(refs)
    theta_ref, q_ref, k_ref, v_ref, rg_ref, gn_ref = refs[:6]
    pos = 6
    s0_ref = None
    if has_init:
        s0_ref = refs[pos]
        pos += 1
    o_ref = refs[pos]
    pos += 1
    st_ref = None
    if emit_states:
        st_ref = refs[pos]
        pos += 1
    hist, s_scr = refs[pos], refs[pos + 1]

    h = pl.program_id(1)
    lg_f = _log_sigmoid(jnp.full((1, LANES), theta_ref[0, h], F32))
    lg_b = _log_sigmoid(jnp.full((1, LANES), theta_ref[1, h], F32))
    lg_f1, lg_b1 = lg_f[:, :1], lg_b[:, :1]

    ii = lax.broadcasted_iota(jnp.int32, (CHUNK, CHUNK), 0)
    jj = lax.broadcasted_iota(jnp.int32, (CHUNK, CHUNK), 1)
    d = (ii - jj).astype(F32)
    decay = (jnp.where(d >= 0, jnp.exp(lg_f * jnp.maximum(d, 0.0)), 0.0)
             + jnp.where(d <= 0, jnp.exp(lg_b * jnp.maximum(-d, 0.0)), 0.0))
    p = lax.broadcasted_iota(jnp.int32, (CHUNK, 1), 0).astype(F32)
    qdec_f = jnp.exp(lg_f1 * (p + 1.0))
    kdec_f = jnp.exp(lg_f1 * (CHUNK - 1.0 - p))
    cdec_f = jnp.exp(lg_f1 * float(CHUNK))
    qdec_b = jnp.exp(lg_b1 * (CHUNK - p))
    kdec_b = jnp.exp(lg_b1 * p)
    cdec_b = jnp.exp(lg_b1 * float(CHUNK))

    def kv_update(r, kdec, cdec):
        kd = (k_ref[r, :].astype(F32) * kdec).T.astype(BF16)
        s_scr[...] = s_scr[...] * cdec + jnp.dot(kd, v_ref[r, :], preferred_element_type=F32)

    s_scr[...] = s0_ref[0, 0, 0, 0] if has_init else jnp.zeros((HEAD_DIM, HEAD_DIM), F32)

    def fwd(c):
        hist[c] = s_scr[...].astype(BF16)
        kv_update(_rows(c), kdec_f, cdec_f)

    _chunk_loop(nc, fwd)
    if emit_states:
        st_ref[0, 0, 0, 0] = s_scr[...]

    s_scr[...] = s0_ref[0, 0, 1, 0] if has_init else jnp.zeros((HEAD_DIM, HEAD_DIM), F32)

    def bwd(c):
        r = _rows(c)
        qc, kc, vc = q_ref[r, :], k_ref[r, :], v_ref[r, :]
        a = lax.dot_general(qc, kc, (((1,), (1,)), ((), ())), preferred_element_type=F32)
        att = (a * decay).astype(BF16)
        o = (jnp.dot(att, vc, preferred_element_type=F32)
             + jnp.dot(qc, hist[c], preferred_element_type=F32) * qdec_f
             + jnp.dot(qc, s_scr[...].astype(BF16), preferred_element_type=F32) * qdec_b)
        y = _ln_rows(o) * gn_ref[0] * _silu(rg_ref[r, :].astype(F32))
        o_ref[r, :] = y.astype(BF16)
        kv_update(r, kdec_b, cdec_b)

    _chunk_loop(nc, bwd, reverse=True)
    if emit_states:
        st_ref[0, 0, 1, 0] = s_scr[...]


def _head_block(seq_len, col0):
    return pl.BlockSpec((seq_len, HEAD_DIM), lambda b, h: (b, col0 + h))


def _ret_call(proj, theta, gn, s0, *, batch, seq_len, emit_states):
    nc = seq_len // CHUNK
    has_init = s0 is not None
    state_spec = pl.BlockSpec((1, 1, 2, 1, HEAD_DIM, HEAD_DIM), lambda b, h: (b, 0, 0, h, 0, 0))
    in_specs = [pl.BlockSpec(memory_space=pltpu.SMEM),
                _head_block(seq_len, 0), _head_block(seq_len, 4), _head_block(seq_len, 8),
                _head_block(seq_len, 12),
                pl.BlockSpec((1, 1, HEAD_DIM), lambda b, h: (h, 0, 0))]
    args = [theta, proj, proj, proj, proj, gn]
    if has_init:
        in_specs.append(state_spec)
        args.append(s0)
    out_specs = [pl.BlockSpec((seq_len, HEAD_DIM), lambda b, h: (b, h))]
    out_shape = [jax.ShapeDtypeStruct((batch * seq_len, GROUP_DIM), BF16)]
    blocks = [_nbytes((seq_len, HEAD_DIM), BF16)] * 5
    if has_init:
        blocks.append(_nbytes((2, HEAD_DIM, HEAD_DIM), F32))
    if emit_states:
        out_specs.append(state_spec)
        out_shape.append(jax.ShapeDtypeStruct((batch, 1, 2, N_HEADS, HEAD_DIM, HEAD_DIM), F32))
        blocks.append(_nbytes((2, HEAD_DIM, HEAD_DIM), F32))
    scratch = [pltpu.VMEM((nc, HEAD_DIM, HEAD_DIM), BF16), pltpu.VMEM((HEAD_DIM, HEAD_DIM), F32)]
    scratch_bytes = _nbytes((nc, HEAD_DIM, HEAD_DIM), BF16) + _nbytes((HEAD_DIM, HEAD_DIM), F32)
    return pl.pallas_call(
        functools.partial(_ret_kernel, nc=nc, has_init=has_init, emit_states=emit_states),
        grid=(batch, N_HEADS),
        in_specs=in_specs, out_specs=out_specs, out_shape=out_shape,
        scratch_shapes=scratch,
        compiler_params=pltpu.CompilerParams(
            dimension_semantics=("parallel", "arbitrary"),
            vmem_limit_bytes=_vmem_limit(blocks, scratch_bytes)),
        name="ret_grid" if has_init else "ret_seq",
    )(*args)


def _cumsum_lanes(x, lane, reverse):
    s = 1
    while s < LANES:
        if reverse:
            x = x + jnp.where(lane < LANES - s, pltpu.roll(x, LANES - s, 1), 0.0)
        else:
            x = x + jnp.where(lane >= s, pltpu.roll(x, s, 1), 0.0)
        s *= 2
    return x


def _mlstm_kernel(*refs, nc, has_init, emit_states):
    refs = list(refs)
    q_ref, k_ref, v_ref, mo_ref, g_ref, gn_ref = refs[:6]
    pos = 6
    c0_ref = n0_ref = m0_ref = None
    if has_init:
        c0_ref, n0_ref, m0_ref = refs[pos:pos + 3]
        pos += 3
    o_ref = refs[pos]
    pos += 1
    c_out = n_out = m_out = None
    if emit_states:
        c_out, n_out, m_out = refs[pos:pos + 3]
        pos += 3
    c_hist, n_hist, m_hist, c_scr, n_scr, m_scr = refs[pos:pos + 6]

    b = pl.program_id(0)
    h = pl.program_id(1)
    ii = lax.broadcasted_iota(jnp.int32, (CHUNK, CHUNK), 0)
    jj = lax.broadcasted_iota(jnp.int32, (CHUNK, CHUNK), 1)
    eye = ii == jj
    lower = ii >= jj
    upper = ii <= jj
    lane16 = lax.broadcasted_iota(jnp.int32, (N_GATES, LANES), 1)
    row16 = lax.broadcasted_iota(jnp.int32, (N_GATES, LANES), 0)

    def col(row):
        return jnp.sum(jnp.where(eye, row, 0.0), axis=1, keepdims=True)

    def gate_rows(c):
        g = g_ref[c]
        ls = _log_sigmoid(g)
        pre = _cumsum_lanes(ls, lane16, False)
        suf = _cumsum_lanes(ls, lane16, True)

        def pick(x, kind):
            return jnp.sum(jnp.where(row16 == kind * N_HEADS + h, x, 0.0), axis=0, keepdims=True)

        return pick(g, 0), pick(pre, 1), pick(g, 2), pick(suf, 3)

    def direction_out(a, qc, qf, vc, c_b, n_row, m, b_row, i_row, mask):
        b_col = col(b_row)
        dm = jnp.where(mask, b_col - b_row + i_row, -jnp.inf)
        inter = b_col + m
        mt = jnp.maximum(inter, jnp.max(dm, axis=1, keepdims=True))
        w = jnp.exp(dm - mt)
        sp = jnp.exp(inter - mt)
        s = a * w
        num = (jnp.dot(s.astype(BF16), vc, preferred_element_type=F32)
               + jnp.dot(qc, c_b, preferred_element_type=F32) * sp)
        den = (jnp.sum(s, axis=1, keepdims=True)
               + jnp.sum(qf * n_row, axis=1, keepdims=True) * sp)
        return num / jnp.maximum(jnp.abs(den), jnp.exp(-mt))

    def state_update(r, b_row, i_row, last):
        m = m_scr[:, :1]
        b_last = b_row[:, last:last + 1]
        g = b_last - b_row + i_row
        m_new = jnp.maximum(b_last + m, jnp.max(g, axis=1, keepdims=True))
        wk = jnp.exp(g - m_new)
        sc = jnp.exp(b_last + m - m_new)
        kw = k_ref[r, :].astype(F32) * col(wk)
        c_scr[...] = c_scr[...] * sc + jnp.dot(kw.T.astype(BF16), v_ref[r, :],
                                               preferred_element_type=F32)
        n_scr[...] = n_scr[...] * sc + jnp.sum(kw, axis=0, keepdims=True)
        m_scr[...] = jnp.broadcast_to(m_new, (1, LANES))

    def init_state(direction):
        if has_init:
            c_scr[...] = c0_ref[0, 0, direction, 0]
            n_scr[...] = n0_ref[0, direction, 0]
            m_scr[...] = jnp.full((1, LANES), m0_ref[b * 2 * N_HEADS + direction * N_HEADS + h], F32)
        else:
            c_scr[...] = jnp.zeros((HEAD_DIM, HEAD_DIM), F32)
            n_scr[...] = jnp.zeros((1, HEAD_DIM), F32)
            m_scr[...] = jnp.zeros((1, LANES), F32)

    def emit_state(direction):
        c_out[0, 0, direction, 0] = c_scr[...]
        n_out[0, direction, 0] = n_scr[...]

    init_state(0)

    def fwd(c):
        c_hist[c] = c_scr[...].astype(BF16)
        n_hist[c] = n_scr[...]
        m_hist[c] = m_scr[...]
        i_f, b_f, _, _ = gate_rows(c)
        state_update(_rows(c), b_f, i_f, CHUNK - 1)

    _chunk_loop(nc, fwd)
    if emit_states:
        emit_state(0)
        m_fwd = m_scr[...]

    init_state(1)

    def bwd(c):
        r = _rows(c)
        i_f, b_f, i_b, b_b = gate_rows(c)
        qc, kc, vc = q_ref[r, :], k_ref[r, :], v_ref[r, :]
        qf = qc.astype(F32)
        a = lax.dot_general(qc, kc, (((1,), (1,)), ((), ())), preferred_element_type=F32)
        h_f = direction_out(a, qc, qf, vc, c_hist[c], n_hist[c], m_hist[c][:, :1], b_f, i_f, lower)
        h_b = direction_out(a, qc, qf, vc, c_scr[...].astype(BF16), n_scr[...], m_scr[:, :1],
                            b_b, i_b, upper)
        y = jax.nn.sigmoid(mo_ref[r, :].astype(F32)) * (_ln_rows(h_f + h_b) * gn_ref[0])
        o_ref[r, :] = y.astype(BF16)
        state_update(r, b_b, i_b, 0)

    _chunk_loop(nc, bwd, reverse=True)
    if emit_states:
        emit_state(1)
        row8 = lax.broadcasted_iota(jnp.int32, (8, LANES), 0)
        m_out[0, 0] = jnp.where(row8 == 0, m_fwd, jnp.where(row8 == 1, m_scr[...], 0.0))


def _mlstm_call(proj, gates, gn, init, *, batch, seq_len, emit_states):
    nc = seq_len // CHUNK
    has_init = init is not None
    c_spec = pl.BlockSpec((1, 1, 2, 1, HEAD_DIM, HEAD_DIM), lambda b, h: (b, 0, 0, h, 0, 0))
    n_spec = pl.BlockSpec((1, 2, 1, 1, HEAD_DIM), lambda b, h: (b, 0, h, 0, 0))
    in_specs = [_head_block(seq_len, 16), _head_block(seq_len, 20), _head_block(seq_len, 24),
                _head_block(seq_len, 28),
                pl.BlockSpec((nc, N_GATES, LANES), lambda b, h: (b, 0, 0)),
                pl.BlockSpec((1, 1, HEAD_DIM), lambda b, h: (h, 0, 0))]
    args = [proj, proj, proj, proj, gates, gn]
    blocks = [_nbytes((seq_len, HEAD_DIM), BF16)] * 5 + [_nbytes((nc, N_GATES, LANES), F32)]
    if has_init:
        in_specs += [c_spec, n_spec, pl.BlockSpec(memory_space=pltpu.SMEM)]
        args += list(init)
        blocks.append(_nbytes((2, HEAD_DIM, HEAD_DIM), F32))
    out_specs = [pl.BlockSpec((seq_len, HEAD_DIM), lambda b, h: (b, h))]
    out_shape = [jax.ShapeDtypeStruct((batch * seq_len, GROUP_DIM), BF16)]
    if emit_states:
        out_specs += [c_spec, n_spec, pl.BlockSpec((1, 1, 8, LANES), lambda b, h: (b, h, 0, 0))]
        out_shape += [jax.ShapeDtypeStruct((batch, 1, 2, N_HEADS, HEAD_DIM, HEAD_DIM), F32),
                      jax.ShapeDtypeStruct((batch, 2, N_HEADS, 1, HEAD_DIM), F32),
                      jax.ShapeDtypeStruct((batch, N_HEADS, 8, LANES), F32)]
        blocks.append(_nbytes((2, HEAD_DIM, HEAD_DIM), F32))
    scratch = [pltpu.VMEM((nc, HEAD_DIM, HEAD_DIM), BF16), pltpu.VMEM((nc, 1, HEAD_DIM), F32),
               pltpu.VMEM((nc, 1, LANES), F32), pltpu.VMEM((HEAD_DIM, HEAD_DIM), F32),
               pltpu.VMEM((1, HEAD_DIM), F32), pltpu.VMEM((1, LANES), F32)]
    scratch_bytes = (_nbytes((nc, HEAD_DIM, HEAD_DIM), BF16) + _nbytes((HEAD_DIM, HEAD_DIM), F32)
                     + 2 * _nbytes((nc, 8, HEAD_DIM), F32))
    return pl.pallas_call(
        functools.partial(_mlstm_kernel, nc=nc, has_init=has_init, emit_states=emit_states),
        grid=(batch, N_HEADS),
        in_specs=in_specs, out_specs=out_specs, out_shape=out_shape,
        scratch_shapes=scratch,
        compiler_params=pltpu.CompilerParams(
            dimension_semantics=("parallel", "arbitrary"),
            vmem_limit_bytes=_vmem_limit(blocks, scratch_bytes)),
        name="mlstm_grid" if has_init else "mlstm_seq",
    )(*args)


def _mix_out_kernel(mr_ref, mm_ref, w_ref, x_ref, g1_ref, lg_ref, lb_ref, o_ref):
    mix = (jnp.dot(mr_ref[...], w_ref[0:GROUP_DIM, :], preferred_element_type=F32)
           + jnp.dot(mm_ref[...], w_ref[GROUP_DIM:, :], preferred_element_type=F32))
    o_ref[...] = _ln_rows(ALPHA * x_ref[...] + g1_ref[0] * mix) * lg_ref[...] + lb_ref[...]


def _mix_out_call(mix_r, mix_m, w_out_b, x2d, mod3, mod_row, ln_g, ln_b, *, tm):
    tokens = x2d.shape[0]
    blocks = [_nbytes((tm, GROUP_DIM), BF16)] * 2 + [_nbytes((D_MODEL, D_MODEL), BF16)] \
        + [_nbytes((tm, D_MODEL), F32)] * 2
    return pl.pallas_call(
        _mix_out_kernel,
        grid=(tokens // tm,),
        in_specs=[pl.BlockSpec((tm, GROUP_DIM), lambda m: (m, 0)),
                  pl.BlockSpec((tm, GROUP_DIM), lambda m: (m, 0)),
                  pl.BlockSpec((D_MODEL, D_MODEL), lambda m: (0, 0)),
                  pl.BlockSpec((tm, D_MODEL), lambda m: (m, 0)),
                  pl.BlockSpec((1, 1, D_MODEL), lambda m: (mod_row(m) * 6 + 2, 0, 0)),
                  pl.BlockSpec((1, D_MODEL), lambda m: (0, 0)),
                  pl.BlockSpec((1, D_MODEL), lambda m: (0, 0))],
        out_specs=pl.BlockSpec((tm, D_MODEL), lambda m: (m, 0)),
        out_shape=jax.ShapeDtypeStruct((tokens, D_MODEL), F32),
        compiler_params=pltpu.CompilerParams(
            dimension_semantics=("parallel",), vmem_limit_bytes=_vmem_limit(blocks)),
        name="mix_out",
    )(mix_r, mix_m, w_out_b, x2d, mod3, ln_g, ln_b)


def _ffn_kernel(x_ref, sh_ref, sc_ref, g2_ref, wu_ref, wg_ref, cv_ref, wd_ref, lg_ref, lb_ref,
                o_ref, h_scr, acc_scr, *, period, nj):
    j = pl.program_id(1)

    @pl.when(j == 0)
    def _():
        h = _ln_rows(x_ref[...]) * (1.0 + sc_ref[0]) + sh_ref[0]
        h_scr[...] = h.astype(BF16)
        acc_scr[...] = jnp.zeros_like(acc_scr)

    hb = h_scr[...]
    u = jnp.dot(hb, wu_ref[...], preferred_element_type=F32)
    g = jnp.dot(hb, wg_ref[...], preferred_element_type=F32)
    hid = (_silu(_conv3_rows(u, cv_ref[...], period)) * g).astype(BF16)
    acc_scr[...] += jnp.dot(hid, wd_ref[...], preferred_element_type=F32)

    @pl.when(j == nj - 1)
    def _():
        y = ALPHA * x_ref[...] + g2_ref[0] * acc_scr[...]
        o_ref[...] = _ln_rows(y) * lg_ref[...] + lb_ref[...]


def _ffn_call(x1, mod3, mod_row, w_u, w_g, conv_ff, w_d, ln_g, ln_b, *, period, tm):
    tokens = x1.shape[0]
    nj = D_FF_PAD // FF_TN
    blocks = [_nbytes((tm, D_MODEL), F32)] * 2 + [_nbytes((D_MODEL, FF_TN), BF16)] * 3 \
        + [_nbytes((8, FF_TN), F32)]
    scratch_bytes = _nbytes((tm, D_MODEL), BF16) + _nbytes((tm, D_MODEL), F32)
    return pl.pallas_call(
        functools.partial(_ffn_kernel, period=period, nj=nj),
        grid=(tokens // tm, nj),
        in_specs=[pl.BlockSpec((tm, D_MODEL), lambda m, j: (m, 0)),
                  pl.BlockSpec((1, 1, D_MODEL), lambda m, j: (mod_row(m) * 6 + 3, 0, 0)),
                  pl.BlockSpec((1, 1, D_MODEL), lambda m, j: (mod_row(m) * 6 + 4, 0, 0)),
                  pl.BlockSpec((1, 1, D_MODEL), lambda m, j: (mod_row(m) * 6 + 5, 0, 0)),
                  pl.BlockSpec((D_MODEL, FF_TN), lambda m, j: (0, j)),
                  pl.BlockSpec((D_MODEL, FF_TN), lambda m, j: (0, j)),
                  pl.BlockSpec((3, FF_TN), lambda m, j: (0, j)),
                  pl.BlockSpec((FF_TN, D_MODEL), lambda m, j: (j, 0)),
                  pl.BlockSpec((1, D_MODEL), lambda m, j: (0, 0)),
                  pl.BlockSpec((1, D_MODEL), lambda m, j: (0, 0))],
        out_specs=pl.BlockSpec((tm, D_MODEL), lambda m, j: (m, 0)),
        out_shape=jax.ShapeDtypeStruct((tokens, D_MODEL), F32),
        scratch_shapes=[pltpu.VMEM((tm, D_MODEL), BF16), pltpu.VMEM((tm, D_MODEL), F32)],
        compiler_params=pltpu.CompilerParams(
            dimension_semantics=("parallel", "arbitrary"),
            vmem_limit_bytes=_vmem_limit(blocks, scratch_bytes)),
        name="ffn",
    )(x1, mod3, mod3, mod3, w_u, w_g, conv_ff, w_d, ln_g, ln_b)


def _rope_tables(seq_len):
    quarter = HEAD_DIM // 4
    t = jnp.arange(seq_len)
    row = (t // GRID_W).astype(F32)
    col = (t % GRID_W).astype(F32)
    inv = ROPE_BASE ** (-jnp.arange(quarter, dtype=F32) / quarter)
    ang_r, ang_c = row[:, None] * inv, col[:, None] * inv
    cos_t = jnp.concatenate([jnp.cos(ang_r)] * 2 + [jnp.cos(ang_c)] * 2, axis=1)
    sin_t = jnp.concatenate([-jnp.sin(ang_r), jnp.sin(ang_r), -jnp.sin(ang_c), jnp.sin(ang_c)], axis=1)
    return cos_t, sin_t


def kernel(x_prompt, x_sample, state_ret, state_mlstm_C, state_mlstm_n, state_mlstm_m, c, c_ctx,
           w_mod, b_mod, w_in, b_gate, conv_qk, ret_theta, gn_ret, gn_mlstm, w_out,
           ln1_g, ln1_b, w_up, conv_ff, w_down, ln2_g, ln2_b):
    bp, lp, _ = x_prompt.shape
    bs, ls, _ = x_sample.shape
    layer = 0
    tm = 512

    cvec = jnp.concatenate([c_ctx[None], c, jnp.zeros((8 - 1 - bs, D_MODEL), F32)], axis=0)
    mod = _mod_call(cvec, w_mod[layer], b_mod[layer][None])
    mod3 = mod.reshape(8 * 6, 1, D_MODEL)

    def row_prompt(m):
        return 0

    def row_sample(m):
        return 1 + (m * tm) // ls

    w_l = w_in[layer]
    w_in_b = w_l[:, :N_PROJ].astype(BF16)
    w_gt = w_l[:, N_PROJ:].T.astype(BF16)
    b_gate_col = b_gate[layer].reshape(N_GATES, 1)
    w_out_b = w_out[layer].astype(BF16)
    pad = D_FF_PAD - D_FF
    w_u = jnp.pad(w_up[layer][:, :D_FF], ((0, 0), (0, pad))).astype(BF16)
    w_g = jnp.pad(w_up[layer][:, D_FF:], ((0, 0), (0, pad))).astype(BF16)
    w_d = jnp.pad(w_down[layer], ((0, pad), (0, 0))).astype(BF16)
    conv_ff_p = jnp.pad(conv_ff[layer], ((0, 0), (0, pad)))
    gn_r = gn_ret[layer].reshape(N_HEADS, 1, HEAD_DIM)
    gn_m = gn_mlstm[layer].reshape(N_HEADS, 1, HEAD_DIM)
    theta = ret_theta[layer]
    ln1 = (ln1_g[layer][None], ln1_b[layer][None])
    ln2 = (ln2_g[layer][None], ln2_b[layer][None])

    def run_group(x, mod_row, rope, period, ret_init, mlstm_init, emit_states):
        batch, seq_len, _ = x.shape
        x2d = x.reshape(batch * seq_len, D_MODEL)
        proj, gates = _mix_in_call(x2d, mod3, mod_row, w_in_b, w_gt, b_gate_col, conv_qk[layer], rope,
                                   seq_len=seq_len, period=period, tm=tm)
        ret = _ret_call(proj, theta, gn_r, ret_init, batch=batch, seq_len=seq_len,
                        emit_states=emit_states)
        mls = _mlstm_call(proj, gates, gn_m, mlstm_init, batch=batch, seq_len=seq_len,
                          emit_states=emit_states)
        x1 = _mix_out_call(ret[0], mls[0], w_out_b, x2d, mod3, mod_row, *ln1, tm=tm)
        y = _ffn_call(x1, mod3, mod_row, w_u, w_g, conv_ff_p, w_d, *ln2, period=period, tm=tm)
        return y.reshape(batch, seq_len, D_MODEL), ret[1:], mls[1:]

    y_prompt, ret_states, mls_states = run_group(x_prompt, row_prompt, None, lp, None, None, True)

    n0 = state_mlstm_n[:, layer].reshape(bs, 2, N_HEADS, 1, HEAD_DIM)
    m0 = state_mlstm_m[:, layer].reshape(bs * 2 * N_HEADS)
    y_sample, _, _ = run_group(x_sample, row_sample, _rope_tables(ls), GRID_W, state_ret,
                               (state_mlstm_C, n0, m0), False)

    new_ret = ret_states[0]
    new_c, new_n, new_m = mls_states
    new_n = new_n.reshape(bp, 1, 2, N_HEADS, HEAD_DIM)
    new_m = jnp.transpose(new_m[:, :, :2, 0], (0, 2, 1))[:, None]
    return (y_prompt, y_sample, new_ret, new_c, new_n, new_m)
```

```python
import functools

import jax
import jax.numpy as jnp
from jax import lax
from jax.experimental import pallas as pl
from jax.experimental.pallas import tpu as pltpu

F32 = jnp.float32
BF16 = jnp.bfloat16

D_MODEL = 2048
GRID_W = 64
HEAD_DIM = 256
N_HEADS = 4
GROUP_DIM = N_HEADS * HEAD_DIM
N_PROJ = 8 * GROUP_DIM
N_GATES = 4 * N_HEADS
D_FF = 5504
CHUNK = 128
ROPE_BASE = 10000.0
ALPHA = 2.0 ** 0.25
LN_EPS = 1e-6
K_SCALE = HEAD_DIM ** -0.5

LANES = 128
V7X_VMEM_LIMIT_CAP = 60 * 1024 * 1024
VMEM_TEMP_ALLOWANCE = 16 * 1024 * 1024

PROJ_TN = GROUP_DIM
N_STORED = 6 * GROUP_DIM
FF_TN = 512
D_FF_PAD = ((D_FF + FF_TN - 1) // FF_TN) * FF_TN


def _vmem_limit(block_bytes, scratch_bytes=0):
    est = 2 * sum(block_bytes) + scratch_bytes + VMEM_TEMP_ALLOWANCE
    return int(min(V7X_VMEM_LIMIT_CAP, est))


def _nbytes(shape, dtype):
    n = 1
    for s in shape:
        n *= s
    return n * jnp.dtype(dtype).itemsize


def _ln_rows(x):
    mu = jnp.mean(x, axis=-1, keepdims=True)
    xc = x - mu
    var = jnp.mean(xc * xc, axis=-1, keepdims=True)
    return xc * lax.rsqrt(var + LN_EPS)


def _silu(x):
    return x * jax.nn.sigmoid(x)


def _log_sigmoid(x):
    return jnp.minimum(x, 0.0) - jnp.log1p(jnp.exp(-jnp.abs(x)))


def _conv3_rows(u, w, period):
    rows = u.shape[0]
    t = lax.broadcasted_iota(jnp.int32, (rows, 1), 0) & (period - 1)
    prev = jnp.where(t == 0, 0.0, pltpu.roll(u, 1, 0))
    nxt = jnp.where(t == period - 1, 0.0, pltpu.roll(u, rows - 1, 0))
    return prev * w[0:1] + u * w[1:2] + nxt * w[2:3]


def _cumsum_lanes(x, lane, reverse):
    s = 1
    while s < LANES:
        if reverse:
            x = x + jnp.where(lane < LANES - s, pltpu.roll(x, LANES - s, 1), 0.0)
        else:
            x = x + jnp.where(lane >= s, pltpu.roll(x, s, 1), 0.0)
        s *= 2
    return x


def _mod_kernel(c_ref, w_ref, b_ref, o_ref):
    a = _silu(c_ref[...]).astype(BF16)
    o_ref[...] = jnp.dot(a, w_ref[...].astype(BF16), preferred_element_type=F32) + b_ref[...]


def _mod_call(cvec, w_mod, b_mod):
    rows, n = cvec.shape[0], w_mod.shape[1]
    tn = 1024
    blocks = [_nbytes((rows, D_MODEL), F32), _nbytes((D_MODEL, tn), F32),
              _nbytes((8, tn), F32), _nbytes((rows, tn), F32)]
    return pl.pallas_call(
        _mod_kernel,
        grid=(n // tn,),
        in_specs=[pl.BlockSpec((rows, D_MODEL), lambda j: (0, 0)),
                  pl.BlockSpec((D_MODEL, tn), lambda j: (0, j)),
                  pl.BlockSpec((1, tn), lambda j: (0, j))],
        out_specs=pl.BlockSpec((rows, tn), lambda j: (0, j)),
        out_shape=jax.ShapeDtypeStruct((rows, n), F32),
        compiler_params=pltpu.CompilerParams(
            dimension_semantics=("arbitrary",), vmem_limit_bytes=_vmem_limit(blocks)),
        name="mod",
    )(cvec, w_mod, b_mod)


def _mix_in_kernel(*refs, tm, period, grid_mode):
    if grid_mode:
        (x_ref, sh_ref, sc_ref, w_ref, wg_ref, bg_ref, cv_ref, cos_ref, sin_ref,
         proj_ref, kt_ref, gates_ref, h_scr) = refs
    else:
        (x_ref, sh_ref, sc_ref, w_ref, wg_ref, bg_ref, cv_ref,
         proj_ref, kt_ref, gates_ref, h_scr) = refs
    n = pl.program_id(1)
    slabs = tm // LANES

    @pl.when(n == 0)
    def _():
        h = _ln_rows(x_ref[...]) * (1.0 + sc_ref[0]) + sh_ref[0]
        hb = h.astype(BF16)
        h_scr[...] = hb
        g = lax.dot_general(wg_ref[...], hb, (((1,), (1,)), ((), ())),
                            preferred_element_type=F32) + bg_ref[...]
        row = lax.broadcasted_iota(jnp.int32, (N_GATES, LANES), 0)
        lane = lax.broadcasted_iota(jnp.int32, (N_GATES, LANES), 1)
        kind = lax.shift_right_logical(row, 2)
        for s in range(slabs):
            gs = g[:, s * LANES:(s + 1) * LANES]
            ls = _log_sigmoid(gs)
            gates_ref[s] = jnp.where(kind == 1, _cumsum_lanes(ls, lane, False),
                                     jnp.where(kind == 3, _cumsum_lanes(ls, lane, True), gs))

    acc = jnp.dot(h_scr[...], w_ref[...], preferred_element_type=F32)

    def rope(a):
        out = []
        for s in range(PROJ_TN // LANES):
            xs = a[:, s * LANES:(s + 1) * LANES]
            t = (s % 2) * LANES
            out.append(xs * cos_ref[:, t:t + LANES]
                       + pltpu.roll(xs, LANES // 2, 1) * sin_ref[:, t:t + LANES])
        return jnp.concatenate(out, axis=1)

    def store_transposed(y):
        yt = y.T.astype(BF16)
        for s in range(slabs):
            kt_ref[s] = yt[:, s * LANES:(s + 1) * LANES]

    plain = (n == 2) | (n == 3) | (n == 6) | (n == 7)
    if not grid_mode:
        plain = plain | (n == 0)

    @pl.when(plain)
    def _():
        proj_ref[...] = acc.astype(BF16)

    if grid_mode:
        @pl.when(n == 0)
        def _():
            proj_ref[...] = rope(acc).astype(BF16)

    @pl.when(n == 1)
    def _():
        k = acc * K_SCALE
        store_transposed(rope(k) if grid_mode else k)

    @pl.when(n == 4)
    def _():
        proj_ref[...] = _silu(_conv3_rows(acc, cv_ref[...], period)).astype(BF16)

    @pl.when(n == 5)
    def _():
        store_transposed(_silu(_conv3_rows(acc, cv_ref[...], period)) * K_SCALE)


def _mix_in_call(x2d, mod3, mod_row, w_in_b, w_gt, b_gate_col, conv_qk, rope, *, seq_len, period, tm):
    tokens = x2d.shape[0]
    grid_mode = rope is not None
    tiles_per_seq = seq_len // tm
    slabs = tm // LANES

    def stored_col(n):
        return n - (n >= 1).astype(jnp.int32) - (n >= 5).astype(jnp.int32)

    in_specs = [
        pl.BlockSpec((tm, D_MODEL), lambda m, n: (m, 0)),
        pl.BlockSpec((1, 1, D_MODEL), lambda m, n: (mod_row(m) * 6 + 0, 0, 0)),
        pl.BlockSpec((1, 1, D_MODEL), lambda m, n: (mod_row(m) * 6 + 1, 0, 0)),
        pl.BlockSpec((D_MODEL, PROJ_TN), lambda m, n: (0, n)),
        pl.BlockSpec((N_GATES, D_MODEL), lambda m, n: (0, 0)),
        pl.BlockSpec((N_GATES, 1), lambda m, n: (0, 0)),
        pl.BlockSpec((3, PROJ_TN), lambda m, n: (0, jnp.clip(n - 4, 0, 1))),
    ]
    args = [x2d, mod3, mod3, w_in_b, w_gt, b_gate_col, conv_qk]
    blocks = [_nbytes((tm, D_MODEL), F32), _nbytes((D_MODEL, PROJ_TN), BF16),
              _nbytes((N_GATES, D_MODEL), BF16), _nbytes((tm, PROJ_TN), BF16) * 2,
              _nbytes((slabs, N_GATES, LANES), F32), _nbytes((8, PROJ_TN), F32)]
    if grid_mode:
        in_specs += [pl.BlockSpec((tm, HEAD_DIM), lambda m, n: (m % tiles_per_seq, 0))] * 2
        args += list(rope)
        blocks += [_nbytes((tm, HEAD_DIM), F32)] * 2

    return pl.pallas_call(
        functools.partial(_mix_in_kernel, tm=tm, period=period, grid_mode=grid_mode),
        grid=(tokens // tm, N_PROJ // PROJ_TN),
        in_specs=in_specs,
        out_specs=[pl.BlockSpec((tm, PROJ_TN), lambda m, n: (m, stored_col(n))),
                   pl.BlockSpec((slabs, GROUP_DIM, LANES),
                                lambda m, n: (m, (n >= 5).astype(jnp.int32), 0)),
                   pl.BlockSpec((slabs, N_GATES, LANES), lambda m, n: (m, 0, 0))],
        out_shape=[jax.ShapeDtypeStruct((tokens, N_STORED), BF16),
                   jax.ShapeDtypeStruct((tokens // LANES, 2 * GROUP_DIM, LANES), BF16),
                   jax.ShapeDtypeStruct((tokens // LANES, N_GATES, LANES), F32)],
        scratch_shapes=[pltpu.VMEM((tm, D_MODEL), BF16)],
        compiler_params=pltpu.CompilerParams(
            dimension_semantics=("parallel", "arbitrary"),
            vmem_limit_bytes=_vmem_limit(blocks, _nbytes((tm, D_MODEL), BF16))),
        name="mix_in_grid" if grid_mode else "mix_in_seq",
    )(*args)


_NN = (((2,), (1,)), ((0,), (0,)))
_NT = (((2,), (2,)), ((0,), (0,)))


def _bdot(x, y, dims):
    return lax.dot_general(x, y, dims, preferred_element_type=F32)


def _chunk_loop(nc, body, reverse=False):
    if nc <= 2:
        for i in range(nc):
            body(nc - 1 - i if reverse else i)
    else:
        def step(i, carry):
            body(nc - 1 - i if reverse else i)
            return carry
        lax.fori_loop(0, nc, step, 0)


def _rows(c):
    start = c * CHUNK
    if not isinstance(start, int):
        start = pl.multiple_of(start, CHUNK)
    return pl.ds(start, CHUNK)


def _is_chunk(c, value):
    return isinstance(c, int) and c == value


def _heads(ref, r, hp):
    return jnp.stack([ref[r, hh * HEAD_DIM:(hh + 1) * HEAD_DIM] for hh in range(hp)])


def _group_block(seq_len, hp, group):
    per = N_HEADS // hp
    return pl.BlockSpec((seq_len, hp * HEAD_DIM), lambda b, hg: (b, group * per + hg))


def _kt_block(nc, hp, group):
    per = N_HEADS // hp
    return pl.BlockSpec((nc, hp * HEAD_DIM, LANES), lambda b, hg: (b, group * per + hg, 0))


def _ret_kernel(*refs, nc, hp, has_init, emit_states):
    refs = list(refs)
    theta_ref, q_ref, kt_ref, v_ref, rg_ref, gn_ref = refs[:6]
    pos = 6
    s0_ref = None
    if has_init:
        s0_ref = refs[pos]
        pos += 1
    o_ref = refs[pos]
    pos += 1
    st_ref = None
    if emit_states:
        st_ref = refs[pos]
        pos += 1
    hist, s_scr, decay_scr, qdec_scr, kdec_scr, cdec_scr = refs[pos:pos + 6]
    hg = pl.program_id(1)
    hsel = pl.ds(hg * hp, hp)

    @pl.when(pl.program_id(0) == 0)
    def _():
        ii = lax.broadcasted_iota(jnp.int32, (CHUNK, CHUNK), 0)
        jj = lax.broadcasted_iota(jnp.int32, (CHUNK, CHUNK), 1)
        d = (ii - jj).astype(F32)
        p_col = lax.broadcasted_iota(jnp.int32, (CHUNK, HEAD_DIM), 0).astype(F32)
        p_row = lax.broadcasted_iota(jnp.int32, (1, LANES), 1).astype(F32)
        for hh in range(hp):
            h = hg * hp + hh
            lg_f = _log_sigmoid(jnp.full((1, LANES), theta_ref[0, h], F32))
            lg_b = _log_sigmoid(jnp.full((1, LANES), theta_ref[1, h], F32))
            decay_scr[h] = (jnp.where(d >= 0, jnp.exp(lg_f * jnp.maximum(d, 0.0)), 0.0)
                            + jnp.where(d <= 0, jnp.exp(lg_b * jnp.maximum(-d, 0.0)), 0.0))
            qdec_scr[0, h] = jnp.exp(lg_f[:, :1] * (p_col + 1.0))
            qdec_scr[1, h] = jnp.exp(lg_b[:, :1] * (CHUNK - p_col))
            kdec_scr[0, h] = jnp.exp(lg_f * (CHUNK - 1.0 - p_row))
            kdec_scr[1, h] = jnp.exp(lg_b * p_row)
            cdec_scr[0, h] = jnp.exp(lg_f * float(CHUNK))
            cdec_scr[1, h] = jnp.exp(lg_b * float(CHUNK))

    def kv_update(c, v, direction):
        kd = (kt_ref[c].reshape(hp, HEAD_DIM, CHUNK).astype(F32)
              * kdec_scr[direction, hsel]).astype(BF16)
        s_scr[...] = s_scr[...] * cdec_scr[direction, hsel][:, :, :1] + _bdot(kd, v, _NN)

    def init_state(direction):
        s_scr[...] = (s0_ref[0, 0, direction] if has_init
                      else jnp.zeros((hp, HEAD_DIM, HEAD_DIM), F32))

    init_state(0)

    def fwd(c):
        hist[c] = s_scr[...].astype(BF16)
        kv_update(c, _heads(v_ref, _rows(c), hp), 0)

    _chunk_loop(nc, fwd)
    if emit_states:
        st_ref[0, 0, 0] = s_scr[...]

    init_state(1)

    def bwd(c):
        r = _rows(c)
        q, v = _heads(q_ref, r, hp), _heads(v_ref, r, hp)
        kt = kt_ref[c].reshape(hp, HEAD_DIM, CHUNK)
        att = (_bdot(q, kt, _NN) * decay_scr[hsel]).astype(BF16)
        o = _bdot(att, v, _NN)
        if has_init or not _is_chunk(c, 0):
            o = o + _bdot(q, hist[c], _NN) * qdec_scr[0, hsel]
        if has_init or not _is_chunk(c, nc - 1):
            o = o + _bdot(q, s_scr[...].astype(BF16), _NN) * qdec_scr[1, hsel]
        y = _ln_rows(o)
        for hh in range(hp):
            cs = slice(hh * HEAD_DIM, (hh + 1) * HEAD_DIM)
            o_ref[r, cs] = (y[hh] * gn_ref[:, cs] * _silu(rg_ref[r, cs].astype(F32))).astype(BF16)
        kv_update(c, v, 1)

    _chunk_loop(nc, bwd, reverse=True)
    if emit_states:
        st_ref[0, 0, 1] = s_scr[...]


def _ret_call(proj, kt, theta, gn, s0, *, batch, seq_len, hp, emit_states):
    nc = seq_len // CHUNK
    has_init = s0 is not None
    assert N_HEADS % hp == 0
    state_spec = pl.BlockSpec((1, 1, 2, hp, HEAD_DIM, HEAD_DIM), lambda b, hg: (b, 0, 0, hg, 0, 0))
    in_specs = [pl.BlockSpec(memory_space=pltpu.SMEM),
                _group_block(seq_len, hp, 0), _kt_block(nc, hp, 0),
                _group_block(seq_len, hp, 1), _group_block(seq_len, hp, 2),
                pl.BlockSpec((1, hp * HEAD_DIM), lambda b, hg: (0, hg))]
    args = [theta, proj, kt, proj, proj, gn]
    blocks = [_nbytes((seq_len, hp * HEAD_DIM), BF16)] * 5
    if has_init:
        in_specs.append(state_spec)
        args.append(s0)
        blocks.append(_nbytes((2, hp, HEAD_DIM, HEAD_DIM), F32))
    out_specs = [pl.BlockSpec((seq_len, hp * HEAD_DIM), lambda b, hg: (b, hg))]
    out_shape = [jax.ShapeDtypeStruct((batch * seq_len, GROUP_DIM), BF16)]
    if emit_states:
        out_specs.append(state_spec)
        out_shape.append(jax.ShapeDtypeStruct((batch, 1, 2, N_HEADS, HEAD_DIM, HEAD_DIM), F32))
        blocks.append(_nbytes((2, hp, HEAD_DIM, HEAD_DIM), F32))
    scratch_defs = [((nc, hp, HEAD_DIM, HEAD_DIM), BF16), ((hp, HEAD_DIM, HEAD_DIM), F32),
                    ((N_HEADS, CHUNK, CHUNK), F32), ((2, N_HEADS, CHUNK, HEAD_DIM), F32),
                    ((2, N_HEADS, 1, LANES), F32), ((2, N_HEADS, 1, LANES), F32)]
    return pl.pallas_call(
        functools.partial(_ret_kernel, nc=nc, hp=hp, has_init=has_init, emit_states=emit_states),
        grid=(batch, N_HEADS // hp),
        in_specs=in_specs, out_specs=out_specs, out_shape=out_shape,
        scratch_shapes=[pltpu.VMEM(s, d) for s, d in scratch_defs],
        compiler_params=pltpu.CompilerParams(
            dimension_semantics=("arbitrary", "arbitrary"),
            vmem_limit_bytes=_vmem_limit(blocks, sum(_nbytes(s, d) for s, d in scratch_defs))),
        name="ret_grid" if has_init else "ret_seq",
    )(*args)


def _mlstm_kernel(*refs, nc, hp, has_init, emit_states):
    refs = list(refs)
    q_ref, kt_ref, v_ref, mo_ref, g_ref, gn_ref = refs[:6]
    pos = 6
    c0_ref = n0_ref = m0_ref = None
    if has_init:
        c0_ref, n0_ref, m0_ref = refs[pos:pos + 3]
        pos += 3
    o_ref = refs[pos]
    pos += 1
    c_out = n_out = m_out = None
    if emit_states:
        c_out, n_out, m_out = refs[pos:pos + 3]
        pos += 3
    c_hist, n_hist, m_hist, c_scr, n_scr, m_scr = refs[pos:pos + 6]

    b = pl.program_id(0)
    hg = pl.program_id(1)
    ii = lax.broadcasted_iota(jnp.int32, (CHUNK, CHUNK), 0)
    jj = lax.broadcasted_iota(jnp.int32, (CHUNK, CHUNK), 1)
    eye = ii == jj
    lower = ii >= jj
    upper = ii <= jj

    def col(row):
        return jnp.sum(jnp.where(eye, row, 0.0), axis=2, keepdims=True)

    def gate_rows(c, kind):
        base = kind * N_HEADS + hg * hp
        return jnp.stack([g_ref[c, pl.ds(base + hh, 1), :] for hh in range(hp)])

    def direction_out(a, q, qf, v, c_b, n_row, m, b_row, i_row, mask):
        b_col = col(b_row)
        dm = jnp.where(mask, b_col - b_row + i_row, -jnp.inf)
        inter = b_col + m
        mt = jnp.maximum(inter, jnp.max(dm, axis=2, keepdims=True))
        w = jnp.exp(dm - mt)
        sp = jnp.exp(inter - mt)
        s = a * w
        num = _bdot(s.astype(BF16), v, _NN)
        den = jnp.sum(s, axis=2, keepdims=True)
        if c_b is not None:
            num = num + _bdot(q, c_b, _NN) * sp
            den = den + jnp.sum(qf * n_row, axis=2, keepdims=True) * sp
        return num / jnp.maximum(jnp.abs(den), jnp.exp(-mt))

    def state_update(c, v, b_row, i_row, last):
        m = m_scr[...][:, :, :1]
        b_last = b_row[:, :, last:last + 1]
        g = b_last - b_row + i_row
        m_new = jnp.maximum(b_last + m, jnp.max(g, axis=2, keepdims=True))
        wk = jnp.exp(g - m_new)
        sc = jnp.exp(b_last + m - m_new)
        kt = kt_ref[c].reshape(hp, HEAD_DIM, CHUNK)
        kw = (kt.astype(F32) * wk).astype(BF16)
        c_scr[...] = c_scr[...] * sc + _bdot(kw, v, _NN)
        wk8 = jnp.broadcast_to(wk, (hp, 8, CHUNK)).astype(BF16)
        n_scr[...] = n_scr[...] * sc + _bdot(wk8, kt, _NT)[:, :1, :]
        m_scr[...] = jnp.broadcast_to(m_new, (hp, 1, LANES))

    def init_state(direction):
        if has_init:
            c_scr[...] = c0_ref[0, 0, direction]
            n_scr[...] = jnp.stack([n0_ref[0, 0, direction, pl.ds(hg * hp + hh, 1), :]
                                    for hh in range(hp)])
            m_scr[...] = jnp.stack([
                jnp.full((1, LANES), m0_ref[(b * 2 + direction) * N_HEADS + hg * hp + hh], F32)
                for hh in range(hp)])
        else:
            c_scr[...] = jnp.zeros((hp, HEAD_DIM, HEAD_DIM), F32)
            n_scr[...] = jnp.zeros((hp, 1, HEAD_DIM), F32)
            m_scr[...] = jnp.zeros((hp, 1, LANES), F32)

    def emit_state(direction):
        c_out[0, 0, direction] = c_scr[...]
        for hh in range(hp):
            n_out[0, 0, direction, pl.ds(hh, 1), :] = n_scr[hh]
            m_out[0, pl.ds(direction * N_HEADS + hh, 1), :] = m_scr[hh]

    init_state(0)

    def fwd(c):
        c_hist[c] = c_scr[...].astype(BF16)
        n_hist[c] = n_scr[...]
        m_hist[c] = m_scr[...]
        state_update(c, _heads(v_ref, _rows(c), hp), gate_rows(c, 1), gate_rows(c, 0), CHUNK - 1)

    _chunk_loop(nc, fwd)
    if emit_states:
        emit_state(0)

    init_state(1)

    def bwd(c):
        r = _rows(c)
        i_f, b_f, i_b, b_b = (gate_rows(c, kind) for kind in (0, 1, 2, 3))
        q, v = _heads(q_ref, r, hp), _heads(v_ref, r, hp)
        qf = q.astype(F32)
        a = _bdot(q, kt_ref[c].reshape(hp, HEAD_DIM, CHUNK), _NN)
        zero_f = not has_init and _is_chunk(c, 0)
        zero_b = not has_init and _is_chunk(c, nc - 1)
        h_f = direction_out(a, q, qf, v, None if zero_f else c_hist[c], n_hist[c],
                            m_hist[c][:, :, :1], b_f, i_f, lower)
        h_b = direction_out(a, q, qf, v, None if zero_b else c_scr[...].astype(BF16),
                            n_scr[...], m_scr[...][:, :, :1], b_b, i_b, upper)
        y = _ln_rows(h_f + h_b)
        for hh in range(hp):
            cs = slice(hh * HEAD_DIM, (hh + 1) * HEAD_DIM)
            o_ref[r, cs] = (jax.nn.sigmoid(mo_ref[r, cs].astype(F32))
                            * (y[hh] * gn_ref[:, cs])).astype(BF16)
        state_update(c, v, b_b, i_b, 0)

    _chunk_loop(nc, bwd, reverse=True)
    if emit_states:
        emit_state(1)


def _mlstm_call(proj, kt, gates, gn, init, *, batch, seq_len, hp, emit_states):
    nc = seq_len // CHUNK
    has_init = init is not None
    assert N_HEADS % hp == 0 and (hp == N_HEADS or not emit_states)
    c_spec = pl.BlockSpec((1, 1, 2, hp, HEAD_DIM, HEAD_DIM), lambda b, hg: (b, 0, 0, hg, 0, 0))
    n_spec = pl.BlockSpec((1, 1, 2, N_HEADS, HEAD_DIM), lambda b, hg: (b, 0, 0, 0, 0))
    in_specs = [_group_block(seq_len, hp, 3), _kt_block(nc, hp, 1),
                _group_block(seq_len, hp, 4), _group_block(seq_len, hp, 5),
                pl.BlockSpec((nc, N_GATES, LANES), lambda b, hg: (b, 0, 0)),
                pl.BlockSpec((1, hp * HEAD_DIM), lambda b, hg: (0, hg))]
    args = [proj, kt, proj, proj, gates, gn]
    blocks = [_nbytes((seq_len, hp * HEAD_DIM), BF16)] * 5 + [_nbytes((nc, N_GATES, LANES), F32)]
    if has_init:
        in_specs += [c_spec, n_spec, pl.BlockSpec(memory_space=pltpu.SMEM)]
        args += list(init)
        blocks.append(_nbytes((2, hp, HEAD_DIM, HEAD_DIM), F32))
    out_specs = [pl.BlockSpec((seq_len, hp * HEAD_DIM), lambda b, hg: (b, hg))]
    out_shape = [jax.ShapeDtypeStruct((batch * seq_len, GROUP_DIM), BF16)]
    if emit_states:
        out_specs += [c_spec, n_spec, pl.BlockSpec((1, 2 * N_HEADS, LANES), lambda b, hg: (b, 0, 0))]
        out_shape += [jax.ShapeDtypeStruct((batch, 1, 2, N_HEADS, HEAD_DIM, HEAD_DIM), F32),
                      jax.ShapeDtypeStruct((batch, 1, 2, N_HEADS, HEAD_DIM), F32),
                      jax.ShapeDtypeStruct((batch, 2 * N_HEADS, LANES), F32)]
        blocks.append(_nbytes((2, hp, HEAD_DIM, HEAD_DIM), F32))
    scratch_defs = [((nc, hp, HEAD_DIM, HEAD_DIM), BF16), ((nc, hp, 1, HEAD_DIM), F32),
                    ((nc, hp, 1, LANES), F32), ((hp, HEAD_DIM, HEAD_DIM), F32),
                    ((hp, 1, HEAD_DIM), F32), ((hp, 1, LANES), F32)]
    scratch_bytes = sum(_nbytes(s, d) for s, d in scratch_defs) + 16 * nc * hp * HEAD_DIM * 4
    return pl.pallas_call(
        functools.partial(_mlstm_kernel, nc=nc, hp=hp, has_init=has_init, emit_states=emit_states),
        grid=(batch, N_HEADS // hp),
        in_specs=in_specs, out_specs=out_specs, out_shape=out_shape,
        scratch_shapes=[pltpu.VMEM(s, d) for s, d in scratch_defs],
        compiler_params=pltpu.CompilerParams(
            dimension_semantics=("arbitrary", "arbitrary"),
            vmem_limit_bytes=_vmem_limit(blocks, scratch_bytes)),
        name="mlstm_grid" if has_init else "mlstm_seq",
    )(*args)


def _mix_out_kernel(mr_ref, mm_ref, w_ref, x_ref, g1_ref, lg_ref, lb_ref, o_ref):
    mix = (jnp.dot(mr_ref[...], w_ref[0:GROUP_DIM, :], preferred_element_type=F32)
           + jnp.dot(mm_ref[...], w_ref[GROUP_DIM:, :], preferred_element_type=F32))
    o_ref[...] = _ln_rows(ALPHA * x_ref[...] + g1_ref[0] * mix) * lg_ref[...] + lb_ref[...]


def _mix_out_call(mix_r, mix_m, w_out_b, x2d, mod3, mod_row, ln_g, ln_b, *, tm):
    tokens = x2d.shape[0]
    blocks = [_nbytes((tm, GROUP_DIM), BF16)] * 2 + [_nbytes((D_MODEL, D_MODEL), BF16)] \
        + [_nbytes((tm, D_MODEL), F32)] * 2
    return pl.pallas_call(
        _mix_out_kernel,
        grid=(tokens // tm,),
        in_specs=[pl.BlockSpec((tm, GROUP_DIM), lambda m: (m, 0)),
                  pl.BlockSpec((tm, GROUP_DIM), lambda m: (m, 0)),
                  pl.BlockSpec((D_MODEL, D_MODEL), lambda m: (0, 0)),
                  pl.BlockSpec((tm, D_MODEL), lambda m: (m, 0)),
                  pl.BlockSpec((1, 1, D_MODEL), lambda m: (mod_row(m) * 6 + 2, 0, 0)),
                  pl.BlockSpec((1, D_MODEL), lambda m: (0, 0)),
                  pl.BlockSpec((1, D_MODEL), lambda m: (0, 0))],
        out_specs=pl.BlockSpec((tm, D_MODEL), lambda m: (m, 0)),
        out_shape=jax.ShapeDtypeStruct((tokens, D_MODEL), F32),
        compiler_params=pltpu.CompilerParams(
            dimension_semantics=("parallel",), vmem_limit_bytes=_vmem_limit(blocks)),
        name="mix_out",
    )(mix_r, mix_m, w_out_b, x2d, mod3, ln_g, ln_b)


def _ffn_kernel(x_ref, sh_ref, sc_ref, g2_ref, wu_ref, wg_ref, cv_ref, wd_ref, lg_ref, lb_ref,
                o_ref, h_scr, acc_scr, *, period, nj):
    j = pl.program_id(1)

    @pl.when(j == 0)
    def _():
        h = _ln_rows(x_ref[...]) * (1.0 + sc_ref[0]) + sh_ref[0]
        h_scr[...] = h.astype(BF16)
        acc_scr[...] = jnp.zeros_like(acc_scr)

    hb = h_scr[...]
    u = jnp.dot(hb, wu_ref[...], preferred_element_type=F32)
    g = jnp.dot(hb, wg_ref[...], preferred_element_type=F32)
    hid = (_silu(_conv3_rows(u, cv_ref[...], period)) * g).astype(BF16)
    acc_scr[...] += jnp.dot(hid, wd_ref[...], preferred_element_type=F32)

    @pl.when(j == nj - 1)
    def _():
        y = ALPHA * x_ref[...] + g2_ref[0] * acc_scr[...]
        o_ref[...] = _ln_rows(y) * lg_ref[...] + lb_ref[...]


def _ffn_call(x1, mod3, mod_row, w_u, w_g, conv_ff, w_d, ln_g, ln_b, *, period, tm):
    tokens = x1.shape[0]
    nj = D_FF_PAD // FF_TN
    blocks = [_nbytes((tm, D_MODEL), F32)] * 2 + [_nbytes((D_MODEL, FF_TN), BF16)] * 3 \
        + [_nbytes((8, FF_TN), F32)]
    scratch_bytes = _nbytes((tm, D_MODEL), BF16) + _nbytes((tm, D_MODEL), F32)
    return pl.pallas_call(
        functools.partial(_ffn_kernel, period=period, nj=nj),
        grid=(tokens // tm, nj),
        in_specs=[pl.BlockSpec((tm, D_MODEL), lambda m, j: (m, 0)),
                  pl.BlockSpec((1, 1, D_MODEL), lambda m, j: (mod_row(m) * 6 + 3, 0, 0)),
                  pl.BlockSpec((1, 1, D_MODEL), lambda m, j: (mod_row(m) * 6 + 4, 0, 0)),
                  pl.BlockSpec((1, 1, D_MODEL), lambda m, j: (mod_row(m) * 6 + 5, 0, 0)),
                  pl.BlockSpec((D_MODEL, FF_TN), lambda m, j: (0, j)),
                  pl.BlockSpec((D_MODEL, FF_TN), lambda m, j: (0, j)),
                  pl.BlockSpec((3, FF_TN), lambda m, j: (0, j)),
                  pl.BlockSpec((FF_TN, D_MODEL), lambda m, j: (j, 0)),
                  pl.BlockSpec((1, D_MODEL), lambda m, j: (0, 0)),
                  pl.BlockSpec((1, D_MODEL), lambda m, j: (0, 0))],
        out_specs=pl.BlockSpec((tm, D_MODEL), lambda m, j: (m, 0)),
        out_shape=jax.ShapeDtypeStruct((tokens, D_MODEL), F32),
        scratch_shapes=[pltpu.VMEM((tm, D_MODEL), BF16), pltpu.VMEM((tm, D_MODEL), F32)],
        compiler_params=pltpu.CompilerParams(
            dimension_semantics=("parallel", "arbitrary"),
            vmem_limit_bytes=_vmem_limit(blocks, scratch_bytes)),
        name="ffn",
    )(x1, mod3, mod3, mod3, w_u, w_g, conv_ff, w_d, ln_g, ln_b)


def _rope_tables(seq_len):
    quarter = HEAD_DIM // 4
    t = jnp.arange(seq_len)
    row = (t // GRID_W).astype(F32)
    col = (t % GRID_W).astype(F32)
    inv = ROPE_BASE ** (-jnp.arange(quarter, dtype=F32) / quarter)
    ang_r, ang_c = row[:, None] * inv, col[:, None] * inv
    cos_t = jnp.concatenate([jnp.cos(ang_r)] * 2 + [jnp.cos(ang_c)] * 2, axis=1)
    sin_t = jnp.concatenate([-jnp.sin(ang_r), jnp.sin(ang_r), -jnp.sin(ang_c), jnp.sin(ang_c)], axis=1)
    return cos_t, sin_t


def kernel(x_prompt, x_sample, state_ret, state_mlstm_C, state_mlstm_n, state_mlstm_m, c, c_ctx,
           w_mod, b_mod, w_in, b_gate, conv_qk, ret_theta, gn_ret, gn_mlstm, w_out,
           ln1_g, ln1_b, w_up, conv_ff, w_down, ln2_g, ln2_b):
    bp, lp, _ = x_prompt.shape
    bs, ls, _ = x_sample.shape
    layer = 0
    tm = 512

    cvec = jnp.concatenate([c_ctx[None], c, jnp.zeros((8 - 1 - bs, D_MODEL), F32)], axis=0)
    mod = _mod_call(cvec, w_mod[layer], b_mod[layer][None])
    mod3 = mod.reshape(8 * 6, 1, D_MODEL)

    def row_prompt(m):
        return 0

    def row_sample(m):
        return 1 + (m * tm) // ls

    w_l = w_in[layer]
    w_in_b = w_l[:, :N_PROJ].astype(BF16)
    w_gt = w_l[:, N_PROJ:].T.astype(BF16)
    b_gate_col = b_gate[layer].reshape(N_GATES, 1)
    w_out_b = w_out[layer].astype(BF16)
    pad = D_FF_PAD - D_FF
    w_u = jnp.pad(w_up[layer][:, :D_FF], ((0, 0), (0, pad))).astype(BF16)
    w_g = jnp.pad(w_up[layer][:, D_FF:], ((0, 0), (0, pad))).astype(BF16)
    w_d = jnp.pad(w_down[layer], ((0, pad), (0, 0))).astype(BF16)
    conv_ff_p = jnp.pad(conv_ff[layer], ((0, 0), (0, pad)))
    gn_r = gn_ret[layer].reshape(1, GROUP_DIM)
    gn_m = gn_mlstm[layer].reshape(1, GROUP_DIM)
    theta = ret_theta[layer]
    ln1 = (ln1_g[layer][None], ln1_b[layer][None])
    ln2 = (ln2_g[layer][None], ln2_b[layer][None])

    def run_group(x, mod_row, rope, period, hp, ret_init, mlstm_init, emit_states):
        batch, seq_len, _ = x.shape
        x2d = x.reshape(batch * seq_len, D_MODEL)
        proj, kt, gates = _mix_in_call(x2d, mod3, mod_row, w_in_b, w_gt, b_gate_col, conv_qk[layer],
                                       rope, seq_len=seq_len, period=period, tm=tm)
        ret = _ret_call(proj, kt, theta, gn_r, ret_init, batch=batch, seq_len=seq_len, hp=hp,
                        emit_states=emit_states)
        mls = _mlstm_call(proj, kt, gates, gn_m, mlstm_init, batch=batch, seq_len=seq_len, hp=hp,
                          emit_states=emit_states)
        x1 = _mix_out_call(ret[0], mls[0], w_out_b, x2d, mod3, mod_row, *ln1, tm=tm)
        y = _ffn_call(x1, mod3, mod_row, w_u, w_g, conv_ff_p, w_d, *ln2, period=period, tm=tm)
        return y.reshape(batch, seq_len, D_MODEL), ret[1:], mls[1:]

    y_prompt, ret_states, mls_states = run_group(x_prompt, row_prompt, None, lp, N_HEADS,
                                                 None, None, True)

    m0 = state_mlstm_m.reshape(-1)
    y_sample, _, _ = run_group(x_sample, row_sample, _rope_tables(ls), GRID_W, 2, state_ret,
                               (state_mlstm_C, state_mlstm_n, m0), False)

    new_ret = ret_states[0]
    new_c, new_n, new_m = mls_states
    new_m = new_m[:, :, 0].reshape(bp, 1, 2, N_HEADS)
    return (y_prompt, y_sample, new_ret, new_c, new_n, new_m)
```

```python
import functools

import jax
import jax.numpy as jnp
from jax import lax
from jax.experimental import pallas as pl
from jax.experimental.pallas import tpu as pltpu

F32 = jnp.float32
BF16 = jnp.bfloat16

D_MODEL = 2048
GRID_W = 64
HEAD_DIM = 256
N_HEADS = 4
GROUP_DIM = N_HEADS * HEAD_DIM
N_PROJ = 8 * GROUP_DIM
N_GATES = 4 * N_HEADS
D_FF = 5504
CHUNK = 128
ROPE_BASE = 10000.0
ALPHA = 2.0 ** 0.25
LN_EPS = 1e-6
K_SCALE = HEAD_DIM ** -0.5

LANES = 128
V7X_VMEM_LIMIT_CAP = 60 * 1024 * 1024
VMEM_TEMP_ALLOWANCE = 16 * 1024 * 1024

PROJ_TN = 512
TILES_PER_GROUP = GROUP_DIM // PROJ_TN
MIX_ROW_CHUNK = 256
N_STORED = 6 * GROUP_DIM
FF_TN = 512
D_FF_PAD = ((D_FF + FF_TN - 1) // FF_TN) * FF_TN


def _vmem_limit(block_bytes, scratch_bytes=0):
    est = 2 * sum(block_bytes) + scratch_bytes + VMEM_TEMP_ALLOWANCE
    return int(min(V7X_VMEM_LIMIT_CAP, est))


def _nbytes(shape, dtype):
    n = 1
    for s in shape:
        n *= s
    return n * jnp.dtype(dtype).itemsize


def _ln_rows(x):
    mu = jnp.mean(x, axis=-1, keepdims=True)
    xc = x - mu
    var = jnp.mean(xc * xc, axis=-1, keepdims=True)
    return xc * lax.rsqrt(var + LN_EPS)


def _silu(x):
    return x * jax.nn.sigmoid(x)


def _log_sigmoid(x):
    return jnp.minimum(x, 0.0) - jnp.log1p(jnp.exp(-jnp.abs(x)))


def _conv3_rows(u, w, period):
    rows = u.shape[0]
    t = lax.broadcasted_iota(jnp.int32, (rows, 1), 0) & (period - 1)
    prev = jnp.where(t == 0, 0.0, pltpu.roll(u, 1, 0))
    nxt = jnp.where(t == period - 1, 0.0, pltpu.roll(u, rows - 1, 0))
    return prev * w[0:1] + u * w[1:2] + nxt * w[2:3]


def _cumsum_lanes(x, lane, reverse):
    s = 1
    while s < LANES:
        if reverse:
            x = x + jnp.where(lane < LANES - s, pltpu.roll(x, LANES - s, 1), 0.0)
        else:
            x = x + jnp.where(lane >= s, pltpu.roll(x, s, 1), 0.0)
        s *= 2
    return x


def _mod_kernel(c_ref, w_ref, b_ref, o_ref):
    a = _silu(c_ref[...]).astype(BF16)
    o_ref[...] = jnp.dot(a, w_ref[...].astype(BF16), preferred_element_type=F32) + b_ref[...]


def _mod_call(cvec, w_mod, b_mod):
    rows, n = cvec.shape[0], w_mod.shape[1]
    tn = 1024
    blocks = [_nbytes((rows, D_MODEL), F32), _nbytes((D_MODEL, tn), F32),
              _nbytes((8, tn), F32), _nbytes((rows, tn), F32)]
    return pl.pallas_call(
        _mod_kernel,
        grid=(n // tn,),
        in_specs=[pl.BlockSpec((rows, D_MODEL), lambda j: (0, 0)),
                  pl.BlockSpec((D_MODEL, tn), lambda j: (0, j)),
                  pl.BlockSpec((1, tn), lambda j: (0, j))],
        out_specs=pl.BlockSpec((rows, tn), lambda j: (0, j)),
        out_shape=jax.ShapeDtypeStruct((rows, n), F32),
        compiler_params=pltpu.CompilerParams(
            dimension_semantics=("arbitrary",), vmem_limit_bytes=_vmem_limit(blocks)),
        name="mod",
    )(cvec, w_mod, b_mod)


def _mix_in_kernel(*refs, tm, period, grid_mode):
    if grid_mode:
        (x_ref, sh_ref, sc_ref, w_ref, wg_ref, bg_ref, cv_ref, cos_ref, sin_ref,
         proj_ref, kt_ref, gates_ref, h_scr) = refs
    else:
        (x_ref, sh_ref, sc_ref, w_ref, wg_ref, bg_ref, cv_ref,
         proj_ref, kt_ref, gates_ref, h_scr) = refs
    n = pl.program_id(1)
    group = lax.div(n, TILES_PER_GROUP)
    slabs = tm // LANES

    @pl.when(n == 0)
    def _():
        h = _ln_rows(x_ref[...]) * (1.0 + sc_ref[0]) + sh_ref[0]
        hb = h.astype(BF16)
        h_scr[...] = hb
        g = lax.dot_general(wg_ref[...], hb, (((1,), (1,)), ((), ())),
                            preferred_element_type=F32) + bg_ref[...]
        row = lax.broadcasted_iota(jnp.int32, (N_GATES, LANES), 0)
        lane = lax.broadcasted_iota(jnp.int32, (N_GATES, LANES), 1)
        kind = lax.shift_right_logical(row, 2)
        for s in range(slabs):
            gs = g[:, s * LANES:(s + 1) * LANES]
            ls = _log_sigmoid(gs)
            gates_ref[s] = jnp.where(kind == 1, _cumsum_lanes(ls, lane, False),
                                     jnp.where(kind == 3, _cumsum_lanes(ls, lane, True), gs))

    def chunks():
        wb = w_ref[...].astype(BF16)
        for rc in range(tm // MIX_ROW_CHUNK):
            rows = slice(rc * MIX_ROW_CHUNK, (rc + 1) * MIX_ROW_CHUNK)
            yield rc, rows, jnp.dot(h_scr[rows, :], wb, preferred_element_type=F32)

    def rope(a, rows):
        out = []
        for s in range(PROJ_TN // LANES):
            xs = a[:, s * LANES:(s + 1) * LANES]
            t = (s % 2) * LANES
            out.append(xs * cos_ref[rows, t:t + LANES]
                       + pltpu.roll(xs, LANES // 2, 1) * sin_ref[rows, t:t + LANES])
        return jnp.concatenate(out, axis=1)

    def store_transposed(rc, y):
        yt = y.T.astype(BF16)
        per = MIX_ROW_CHUNK // LANES
        for s in range(per):
            kt_ref[rc * per + s] = yt[:, s * LANES:(s + 1) * LANES]

    plain = (group == 2) | (group == 3) | (group == 6) | (group == 7)
    if not grid_mode:
        plain = plain | (group == 0)

    @pl.when(plain)
    def _():
        for _, rows, acc in chunks():
            proj_ref[rows, :] = acc.astype(BF16)

    if grid_mode:
        @pl.when(group == 0)
        def _():
            for _, rows, acc in chunks():
                proj_ref[rows, :] = rope(acc, rows).astype(BF16)

    @pl.when(group == 1)
    def _():
        for rc, rows, acc in chunks():
            k = acc * K_SCALE
            store_transposed(rc, rope(k, rows) if grid_mode else k)

    @pl.when(group == 4)
    def _():
        for _, rows, acc in chunks():
            proj_ref[rows, :] = _silu(_conv3_rows(acc, cv_ref[...], period)).astype(BF16)

    @pl.when(group == 5)
    def _():
        for rc, _, acc in chunks():
            store_transposed(rc, _silu(_conv3_rows(acc, cv_ref[...], period)) * K_SCALE)


def _mix_in_call(x2d, mod3, mod_row, w_in, w_gt, b_gate_col, conv_qk, rope, *, seq_len, period, tm):
    tokens = x2d.shape[0]
    grid_mode = rope is not None
    tiles_per_seq = seq_len // tm
    slabs = tm // LANES
    tpg = TILES_PER_GROUP

    def stored_col(n):
        return n - jnp.clip(n - (tpg - 1), 0, tpg) - jnp.clip(n - (5 * tpg - 1), 0, tpg)

    def kt_row(n):
        return jnp.clip(n - tpg, 0, tpg - 1) + jnp.clip(n - (5 * tpg - 1), 0, tpg)

    in_specs = [
        pl.BlockSpec((tm, D_MODEL), lambda m, n: (m, 0)),
        pl.BlockSpec((1, 1, D_MODEL), lambda m, n: (mod_row(m) * 6 + 0, 0, 0)),
        pl.BlockSpec((1, 1, D_MODEL), lambda m, n: (mod_row(m) * 6 + 1, 0, 0)),
        pl.BlockSpec((D_MODEL, PROJ_TN), lambda m, n: (0, n)),
        pl.BlockSpec((N_GATES, D_MODEL), lambda m, n: (0, 0)),
        pl.BlockSpec((N_GATES, 1), lambda m, n: (0, 0)),
        pl.BlockSpec((3, PROJ_TN), lambda m, n: (0, jnp.clip(n - 4 * tpg, 0, 2 * tpg - 1))),
    ]
    args = [x2d, mod3, mod3, w_in, w_gt, b_gate_col, conv_qk]
    blocks = [_nbytes((tm, D_MODEL), F32), _nbytes((D_MODEL, PROJ_TN), F32),
              _nbytes((N_GATES, D_MODEL), BF16), _nbytes((tm, PROJ_TN), BF16) * 2,
              _nbytes((slabs, N_GATES, LANES), F32), _nbytes((8, PROJ_TN), F32)]
    if grid_mode:
        in_specs += [pl.BlockSpec((tm, HEAD_DIM), lambda m, n: (m % tiles_per_seq, 0))] * 2
        args += list(rope)
        blocks += [_nbytes((tm, HEAD_DIM), F32)] * 2
    scratch_defs = [((tm, D_MODEL), BF16)]
    temp_bytes = _nbytes((D_MODEL, PROJ_TN), BF16)

    return pl.pallas_call(
        functools.partial(_mix_in_kernel, tm=tm, period=period, grid_mode=grid_mode),
        grid=(tokens // tm, N_PROJ // PROJ_TN),
        in_specs=in_specs,
        out_specs=[pl.BlockSpec((tm, PROJ_TN), lambda m, n: (m, stored_col(n))),
                   pl.BlockSpec((slabs, PROJ_TN, LANES), lambda m, n: (m, kt_row(n), 0)),
                   pl.BlockSpec((slabs, N_GATES, LANES), lambda m, n: (m, 0, 0))],
        out_shape=[jax.ShapeDtypeStruct((tokens, N_STORED), BF16),
                   jax.ShapeDtypeStruct((tokens // LANES, 2 * GROUP_DIM, LANES), BF16),
                   jax.ShapeDtypeStruct((tokens // LANES, N_GATES, LANES), F32)],
        scratch_shapes=[pltpu.VMEM(s, d) for s, d in scratch_defs],
        compiler_params=pltpu.CompilerParams(
            dimension_semantics=("parallel", "arbitrary"),
            vmem_limit_bytes=_vmem_limit(
                blocks, temp_bytes + sum(_nbytes(s, d) for s, d in scratch_defs))),
        name="mix_in_grid" if grid_mode else "mix_in_seq",
    )(*args)


_NN = (((2,), (1,)), ((0,), (0,)))
_NT = (((2,), (2,)), ((0,), (0,)))


def _bdot(x, y, dims):
    return lax.dot_general(x, y, dims, preferred_element_type=F32)


def _chunk_loop(nc, body, reverse=False):
    if nc <= 2:
        for i in range(nc):
            body(nc - 1 - i if reverse else i)
    else:
        def step(i, carry):
            body(nc - 1 - i if reverse else i)
            return carry
        lax.fori_loop(0, nc, step, 0)


def _rows(c):
    start = c * CHUNK
    if not isinstance(start, int):
        start = pl.multiple_of(start, CHUNK)
    return pl.ds(start, CHUNK)


def _is_chunk(c, value):
    return isinstance(c, int) and c == value


def _heads(ref, r, hp):
    return jnp.stack([ref[r, hh * HEAD_DIM:(hh + 1) * HEAD_DIM] for hh in range(hp)])


def _group_block(seq_len, hp, group):
    per = N_HEADS // hp
    return pl.BlockSpec((seq_len, hp * HEAD_DIM), lambda b, hg: (b, group * per + hg))


def _kt_block(nc, hp, group):
    per = N_HEADS // hp
    return pl.BlockSpec((nc, hp * HEAD_DIM, LANES), lambda b, hg: (b, group * per + hg, 0))


def _ret_kernel(*refs, nc, hp, has_init, emit_states):
    refs = list(refs)
    theta_ref, q_ref, kt_ref, v_ref, rg_ref, gn_ref = refs[:6]
    pos = 6
    s0_ref = None
    if has_init:
        s0_ref = refs[pos]
        pos += 1
    o_ref = refs[pos]
    pos += 1
    st_ref = None
    if emit_states:
        st_ref = refs[pos]
        pos += 1
    hist, s_scr, decay_scr, qdec_scr, kdec_scr, cdec_scr = refs[pos:pos + 6]
    hg = pl.program_id(1)
    hsel = pl.ds(hg * hp, hp)

    @pl.when(pl.program_id(0) == 0)
    def _():
        ii = lax.broadcasted_iota(jnp.int32, (CHUNK, CHUNK), 0)
        jj = lax.broadcasted_iota(jnp.int32, (CHUNK, CHUNK), 1)
        d = (ii - jj).astype(F32)
        p_col = lax.broadcasted_iota(jnp.int32, (CHUNK, HEAD_DIM), 0).astype(F32)
        p_row = lax.broadcasted_iota(jnp.int32, (1, LANES), 1).astype(F32)
        for hh in range(hp):
            h = hg * hp + hh
            lg_f = _log_sigmoid(jnp.full((1, LANES), theta_ref[0, h], F32))
            lg_b = _log_sigmoid(jnp.full((1, LANES), theta_ref[1, h], F32))
            decay_scr[h] = (jnp.where(d >= 0, jnp.exp(lg_f * jnp.maximum(d, 0.0)), 0.0)
                            + jnp.where(d <= 0, jnp.exp(lg_b * jnp.maximum(-d, 0.0)), 0.0))
            qdec_scr[0, h] = jnp.exp(lg_f[:, :1] * (p_col + 1.0))
            qdec_scr[1, h] = jnp.exp(lg_b[:, :1] * (CHUNK - p_col))
            kdec_scr[0, h] = jnp.exp(lg_f * (CHUNK - 1.0 - p_row))
            kdec_scr[1, h] = jnp.exp(lg_b * p_row)
            cdec_scr[0, h] = jnp.exp(lg_f * float(CHUNK))
            cdec_scr[1, h] = jnp.exp(lg_b * float(CHUNK))

    def kv_update(c, v, direction):
        kd = (kt_ref[c].reshape(hp, HEAD_DIM, CHUNK).astype(F32)
              * kdec_scr[direction, hsel]).astype(BF16)
        s_scr[...] = s_scr[...] * cdec_scr[direction, hsel][:, :, :1] + _bdot(kd, v, _NN)

    def init_state(direction):
        s_scr[...] = (s0_ref[0, 0, direction] if has_init
                      else jnp.zeros((hp, HEAD_DIM, HEAD_DIM), F32))

    init_state(0)

    def fwd(c):
        hist[c] = s_scr[...].astype(BF16)
        kv_update(c, _heads(v_ref, _rows(c), hp), 0)

    _chunk_loop(nc, fwd)
    if emit_states:
        st_ref[0, 0, 0] = s_scr[...]

    init_state(1)

    def bwd(c):
        r = _rows(c)
        q, v = _heads(q_ref, r, hp), _heads(v_ref, r, hp)
        kt = kt_ref[c].reshape(hp, HEAD_DIM, CHUNK)
        att = (_bdot(q, kt, _NN) * decay_scr[hsel]).astype(BF16)
        o = _bdot(att, v, _NN)
        if has_init or not _is_chunk(c, 0):
            o = o + _bdot(q, hist[c], _NN) * qdec_scr[0, hsel]
        if has_init or not _is_chunk(c, nc - 1):
            o = o + _bdot(q, s_scr[...].astype(BF16), _NN) * qdec_scr[1, hsel]
        y = _ln_rows(o)
        for hh in range(hp):
            cs = slice(hh * HEAD_DIM, (hh + 1) * HEAD_DIM)
            o_ref[r, cs] = (y[hh] * gn_ref[:, cs] * _silu(rg_ref[r, cs].astype(F32))).astype(BF16)
        kv_update(c, v, 1)

    _chunk_loop(nc, bwd, reverse=True)
    if emit_states:
        st_ref[0, 0, 1] = s_scr[...]


def _ret_call(proj, kt, theta, gn, s0, *, batch, seq_len, hp, emit_states):
    nc = seq_len // CHUNK
    has_init = s0 is not None
    assert N_HEADS % hp == 0
    state_spec = pl.BlockSpec((1, 1, 2, hp, HEAD_DIM, HEAD_DIM), lambda b, hg: (b, 0, 0, hg, 0, 0))
    in_specs = [pl.BlockSpec(memory_space=pltpu.SMEM),
                _group_block(seq_len, hp, 0), _kt_block(nc, hp, 0),
                _group_block(seq_len, hp, 1), _group_block(seq_len, hp, 2),
                pl.BlockSpec((1, hp * HEAD_DIM), lambda b, hg: (0, hg))]
    args = [theta, proj, kt, proj, proj, gn]
    blocks = [_nbytes((seq_len, hp * HEAD_DIM), BF16)] * 5
    if has_init:
        in_specs.append(state_spec)
        args.append(s0)
        blocks.append(_nbytes((2, hp, HEAD_DIM, HEAD_DIM), F32))
    out_specs = [pl.BlockSpec((seq_len, hp * HEAD_DIM), lambda b, hg: (b, hg))]
    out_shape = [jax.ShapeDtypeStruct((batch * seq_len, GROUP_DIM), BF16)]
    if emit_states:
        out_specs.append(state_spec)
        out_shape.append(jax.ShapeDtypeStruct((batch, 1, 2, N_HEADS, HEAD_DIM, HEAD_DIM), F32))
        blocks.append(_nbytes((2, hp, HEAD_DIM, HEAD_DIM), F32))
    scratch_defs = [((nc, hp, HEAD_DIM, HEAD_DIM), BF16), ((hp, HEAD_DIM, HEAD_DIM), F32),
                    ((N_HEADS, CHUNK, CHUNK), F32), ((2, N_HEADS, CHUNK, HEAD_DIM), F32),
                    ((2, N_HEADS, 1, LANES), F32), ((2, N_HEADS, 1, LANES), F32)]
    return pl.pallas_call(
        functools.partial(_ret_kernel, nc=nc, hp=hp, has_init=has_init, emit_states=emit_states),
        grid=(batch, N_HEADS // hp),
        in_specs=in_specs, out_specs=out_specs, out_shape=out_shape,
        scratch_shapes=[pltpu.VMEM(s, d) for s, d in scratch_defs],
        compiler_params=pltpu.CompilerParams(
            dimension_semantics=("arbitrary", "arbitrary"),
            vmem_limit_bytes=_vmem_limit(blocks, sum(_nbytes(s, d) for s, d in scratch_defs))),
        name="ret_grid" if has_init else "ret_seq",
    )(*args)


def _mlstm_kernel(*refs, nc, hp, has_init, emit_states):
    refs = list(refs)
    q_ref, kt_ref, v_ref, mo_ref, g_ref, gn_ref = refs[:6]
    pos = 6
    c0_ref = n0_ref = m0_ref = None
    if has_init:
        c0_ref, n0_ref, m0_ref = refs[pos:pos + 3]
        pos += 3
    o_ref = refs[pos]
    pos += 1
    c_out = n_out = m_out = None
    if emit_states:
        c_out, n_out, m_out = refs[pos:pos + 3]
        pos += 3
    c_hist, n_hist, m_hist, c_scr, n_scr, m_scr = refs[pos:pos + 6]

    b = pl.program_id(0)
    hg = pl.program_id(1)
    ii = lax.broadcasted_iota(jnp.int32, (CHUNK, CHUNK), 0)
    jj = lax.broadcasted_iota(jnp.int32, (CHUNK, CHUNK), 1)
    eye = ii == jj
    lower = ii >= jj
    upper = ii <= jj

    def col(row):
        return jnp.sum(jnp.where(eye, row, 0.0), axis=2, keepdims=True)

    def gate_rows(c, kind):
        base = kind * N_HEADS + hg * hp
        return jnp.stack([g_ref[c, pl.ds(base + hh, 1), :] for hh in range(hp)])

    def direction_out(a, q, qf, v, c_b, n_row, m, b_row, i_row, mask):
        b_col = col(b_row)
        dm = jnp.where(mask, b_col - b_row + i_row, -jnp.inf)
        inter = b_col + m
        mt = jnp.maximum(inter, jnp.max(dm, axis=2, keepdims=True))
        w = jnp.exp(dm - mt)
        sp = jnp.exp(inter - mt)
        s = a * w
        num = _bdot(s.astype(BF16), v, _NN)
        den = jnp.sum(s, axis=2, keepdims=True)
        if c_b is not None:
            num = num + _bdot(q, c_b, _NN) * sp
            den = den + jnp.sum(qf * n_row, axis=2, keepdims=True) * sp
        return num / jnp.maximum(jnp.abs(den), jnp.exp(-mt))

    def state_update(c, v, b_row, i_row, last):
        m = m_scr[...][:, :, :1]
        b_last = b_row[:, :, last:last + 1]
        g = b_last - b_row + i_row
        m_new = jnp.maximum(b_last + m, jnp.max(g, axis=2, keepdims=True))
        wk = jnp.exp(g - m_new)
        sc = jnp.exp(b_last + m - m_new)
        kt = kt_ref[c].reshape(hp, HEAD_DIM, CHUNK)
        kw = (kt.astype(F32) * wk).astype(BF16)
        c_scr[...] = c_scr[...] * sc + _bdot(kw, v, _NN)
        wk8 = jnp.broadcast_to(wk, (hp, 8, CHUNK)).astype(BF16)
        n_scr[...] = n_scr[...] * sc + _bdot(wk8, kt, _NT)[:, :1, :]
        m_scr[...] = jnp.broadcast_to(m_new, (hp, 1, LANES))

    def init_state(direction):
        if has_init:
            c_scr[...] = c0_ref[0, 0, direction]
            n_scr[...] = jnp.stack([n0_ref[0, 0, direction, pl.ds(hg * hp + hh, 1), :]
                                    for hh in range(hp)])
            m_scr[...] = jnp.stack([
                jnp.full((1, LANES), m0_ref[(b * 2 + direction) * N_HEADS + hg * hp + hh], F32)
                for hh in range(hp)])
        else:
            c_scr[...] = jnp.zeros((hp, HEAD_DIM, HEAD_DIM), F32)
            n_scr[...] = jnp.zeros((hp, 1, HEAD_DIM), F32)
            m_scr[...] = jnp.zeros((hp, 1, LANES), F32)

    def emit_state(direction):
        c_out[0, 0, direction] = c_scr[...]
        for hh in range(hp):
            n_out[0, 0, direction, pl.ds(hh, 1), :] = n_scr[hh]
            m_out[0, pl.ds(direction * N_HEADS + hh, 1), :] = m_scr[hh]

    init_state(0)

    def fwd(c):
        c_hist[c] = c_scr[...].astype(BF16)
        n_hist[c] = n_scr[...]
        m_hist[c] = m_scr[...]
        state_update(c, _heads(v_ref, _rows(c), hp), gate_rows(c, 1), gate_rows(c, 0), CHUNK - 1)

    _chunk_loop(nc, fwd)
    if emit_states:
        emit_state(0)

    init_state(1)

    def bwd(c):
        r = _rows(c)
        i_f, b_f, i_b, b_b = (gate_rows(c, kind) for kind in (0, 1, 2, 3))
        q, v = _heads(q_ref, r, hp), _heads(v_ref, r, hp)
        qf = q.astype(F32)
        a = _bdot(q, kt_ref[c].reshape(hp, HEAD_DIM, CHUNK), _NN)
        zero_f = not has_init and _is_chunk(c, 0)
        zero_b = not has_init and _is_chunk(c, nc - 1)
        h_f = direction_out(a, q, qf, v, None if zero_f else c_hist[c], n_hist[c],
                            m_hist[c][:, :, :1], b_f, i_f, lower)
        h_b = direction_out(a, q, qf, v, None if zero_b else c_scr[...].astype(BF16),
                            n_scr[...], m_scr[...][:, :, :1], b_b, i_b, upper)
        y = _ln_rows(h_f + h_b)
        for hh in range(hp):
            cs = slice(hh * HEAD_DIM, (hh + 1) * HEAD_DIM)
            o_ref[r, cs] = (jax.nn.sigmoid(mo_ref[r, cs].astype(F32))
                            * (y[hh] * gn_ref[:, cs])).astype(BF16)
        state_update(c, v, b_b, i_b, 0)

    _chunk_loop(nc, bwd, reverse=True)
    if emit_states:
        emit_state(1)


def _mlstm_call(proj, kt, gates, gn, init, *, batch, seq_len, hp, emit_states):
    nc = seq_len // CHUNK
    has_init = init is not None
    assert N_HEADS % hp == 0 and (hp == N_HEADS or not emit_states)
    c_spec = pl.BlockSpec((1, 1, 2, hp, HEAD_DIM, HEAD_DIM), lambda b, hg: (b, 0, 0, hg, 0, 0))
    n_spec = pl.BlockSpec((1, 1, 2, N_HEADS, HEAD_DIM), lambda b, hg: (b, 0, 0, 0, 0))
    in_specs = [_group_block(seq_len, hp, 3), _kt_block(nc, hp, 1),
                _group_block(seq_len, hp, 4), _group_block(seq_len, hp, 5),
                pl.BlockSpec((nc, N_GATES, LANES), lambda b, hg: (b, 0, 0)),
                pl.BlockSpec((1, hp * HEAD_DIM), lambda b, hg: (0, hg))]
    args = [proj, kt, proj, proj, gates, gn]
    blocks = [_nbytes((seq_len, hp * HEAD_DIM), BF16)] * 5 + [_nbytes((nc, N_GATES, LANES), F32)]
    if has_init:
        in_specs += [c_spec, n_spec, pl.BlockSpec(memory_space=pltpu.SMEM)]
        args += list(init)
        blocks.append(_nbytes((2, hp, HEAD_DIM, HEAD_DIM), F32))
    out_specs = [pl.BlockSpec((seq_len, hp * HEAD_DIM), lambda b, hg: (b, hg))]
    out_shape = [jax.ShapeDtypeStruct((batch * seq_len, GROUP_DIM), BF16)]
    if emit_states:
        out_specs += [c_spec, n_spec, pl.BlockSpec((1, 2 * N_HEADS, LANES), lambda b, hg: (b, 0, 0))]
        out_shape += [jax.ShapeDtypeStruct((batch, 1, 2, N_HEADS, HEAD_DIM, HEAD_DIM), F32),
                      jax.ShapeDtypeStruct((batch, 1, 2, N_HEADS, HEAD_DIM), F32),
                      jax.ShapeDtypeStruct((batch, 2 * N_HEADS, LANES), F32)]
        blocks.append(_nbytes((2, hp, HEAD_DIM, HEAD_DIM), F32))
    scratch_defs = [((nc, hp, HEAD_DIM, HEAD_DIM), BF16), ((nc, hp, 1, HEAD_DIM), F32),
                    ((nc, hp, 1, LANES), F32), ((hp, HEAD_DIM, HEAD_DIM), F32),
                    ((hp, 1, HEAD_DIM), F32), ((hp, 1, LANES), F32)]
    scratch_bytes = sum(_nbytes(s, d) for s, d in scratch_defs) + 16 * nc * hp * HEAD_DIM * 4
    return pl.pallas_call(
        functools.partial(_mlstm_kernel, nc=nc, hp=hp, has_init=has_init, emit_states=emit_states),
        grid=(batch, N_HEADS // hp),
        in_specs=in_specs, out_specs=out_specs, out_shape=out_shape,
        scratch_shapes=[pltpu.VMEM(s, d) for s, d in scratch_defs],
        compiler_params=pltpu.CompilerParams(
            dimension_semantics=("arbitrary", "arbitrary"),
            vmem_limit_bytes=_vmem_limit(blocks, scratch_bytes)),
        name="mlstm_grid" if has_init else "mlstm_seq",
    )(*args)


def _mix_out_kernel(mr_ref, mm_ref, w_ref, x_ref, g1_ref, lg_ref, lb_ref, o_ref):
    for rc in range(o_ref.shape[0] // MIX_ROW_CHUNK):
        rows = slice(rc * MIX_ROW_CHUNK, (rc + 1) * MIX_ROW_CHUNK)
        mix = (jnp.dot(mr_ref[rows, :], w_ref[0:GROUP_DIM, :], preferred_element_type=F32)
               + jnp.dot(mm_ref[rows, :], w_ref[GROUP_DIM:, :], preferred_element_type=F32))
        o_ref[rows, :] = (_ln_rows(ALPHA * x_ref[rows, :] + g1_ref[0] * mix) * lg_ref[...]
                          + lb_ref[...])


def _mix_out_call(mix_r, mix_m, w_out_b, x2d, mod3, mod_row, ln_g, ln_b, *, tm):
    tokens = x2d.shape[0]
    blocks = [_nbytes((tm, GROUP_DIM), BF16)] * 2 + [_nbytes((D_MODEL, D_MODEL), BF16)] \
        + [_nbytes((tm, D_MODEL), F32)] * 2
    return pl.pallas_call(
        _mix_out_kernel,
        grid=(tokens // tm,),
        in_specs=[pl.BlockSpec((tm, GROUP_DIM), lambda m: (m, 0)),
                  pl.BlockSpec((tm, GROUP_DIM), lambda m: (m, 0)),
                  pl.BlockSpec((D_MODEL, D_MODEL), lambda m: (0, 0)),
                  pl.BlockSpec((tm, D_MODEL), lambda m: (m, 0)),
                  pl.BlockSpec((1, 1, D_MODEL), lambda m: (mod_row(m) * 6 + 2, 0, 0)),
                  pl.BlockSpec((1, D_MODEL), lambda m: (0, 0)),
                  pl.BlockSpec((1, D_MODEL), lambda m: (0, 0))],
        out_specs=pl.BlockSpec((tm, D_MODEL), lambda m: (m, 0)),
        out_shape=jax.ShapeDtypeStruct((tokens, D_MODEL), F32),
        compiler_params=pltpu.CompilerParams(
            dimension_semantics=("parallel",), vmem_limit_bytes=_vmem_limit(blocks)),
        name="mix_out",
    )(mix_r, mix_m, w_out_b, x2d, mod3, ln_g, ln_b)


def _ffn_kernel(x_ref, sh_ref, sc_ref, g2_ref, wu_ref, wg_ref, cv_ref, wd_ref, lg_ref, lb_ref,
                o_ref, h_scr, acc_scr, *, period, nj):
    j = pl.program_id(1)

    @pl.when(j == 0)
    def _():
        h = _ln_rows(x_ref[...]) * (1.0 + sc_ref[0]) + sh_ref[0]
        h_scr[...] = h.astype(BF16)
        acc_scr[...] = jnp.zeros_like(acc_scr)

    hb = h_scr[...]
    u = jnp.dot(hb, wu_ref[...], preferred_element_type=F32)
    g = jnp.dot(hb, wg_ref[...], preferred_element_type=F32)
    hid = (_silu(_conv3_rows(u, cv_ref[...], period)) * g).astype(BF16)
    acc_scr[...] += jnp.dot(hid, wd_ref[...], preferred_element_type=F32)

    @pl.when(j == nj - 1)
    def _():
        y = ALPHA * x_ref[...] + g2_ref[0] * acc_scr[...]
        o_ref[...] = _ln_rows(y) * lg_ref[...] + lb_ref[...]


def _ffn_call(x1, mod3, mod_row, w_u, w_g, conv_ff, w_d, ln_g, ln_b, *, period, tm):
    tokens = x1.shape[0]
    nj = D_FF_PAD // FF_TN
    blocks = [_nbytes((tm, D_MODEL), F32)] * 2 + [_nbytes((D_MODEL, FF_TN), BF16)] * 3 \
        + [_nbytes((8, FF_TN), F32)]
    scratch_bytes = _nbytes((tm, D_MODEL), BF16) + _nbytes((tm, D_MODEL), F32)
    return pl.pallas_call(
        functools.partial(_ffn_kernel, period=period, nj=nj),
        grid=(tokens // tm, nj),
        in_specs=[pl.BlockSpec((tm, D_MODEL), lambda m, j: (m, 0)),
                  pl.BlockSpec((1, 1, D_MODEL), lambda m, j: (mod_row(m) * 6 + 3, 0, 0)),
                  pl.BlockSpec((1, 1, D_MODEL), lambda m, j: (mod_row(m) * 6 + 4, 0, 0)),
                  pl.BlockSpec((1, 1, D_MODEL), lambda m, j: (mod_row(m) * 6 + 5, 0, 0)),
                  pl.BlockSpec((D_MODEL, FF_TN), lambda m, j: (0, j)),
                  pl.BlockSpec((D_MODEL, FF_TN), lambda m, j: (0, j)),
                  pl.BlockSpec((3, FF_TN), lambda m, j: (0, j)),
                  pl.BlockSpec((FF_TN, D_MODEL), lambda m, j: (j, 0)),
                  pl.BlockSpec((1, D_MODEL), lambda m, j: (0, 0)),
                  pl.BlockSpec((1, D_MODEL), lambda m, j: (0, 0))],
        out_specs=pl.BlockSpec((tm, D_MODEL), lambda m, j: (m, 0)),
        out_shape=jax.ShapeDtypeStruct((tokens, D_MODEL), F32),
        scratch_shapes=[pltpu.VMEM((tm, D_MODEL), BF16), pltpu.VMEM((tm, D_MODEL), F32)],
        compiler_params=pltpu.CompilerParams(
            dimension_semantics=("parallel", "arbitrary"),
            vmem_limit_bytes=_vmem_limit(blocks, scratch_bytes)),
        name="ffn",
    )(x1, mod3, mod3, mod3, w_u, w_g, conv_ff, w_d, ln_g, ln_b)


def _rope_tables(seq_len):
    quarter = HEAD_DIM // 4
    t = jnp.arange(seq_len)
    row = (t // GRID_W).astype(F32)
    col = (t % GRID_W).astype(F32)
    inv = ROPE_BASE ** (-jnp.arange(quarter, dtype=F32) / quarter)
    ang_r, ang_c = row[:, None] * inv, col[:, None] * inv
    cos_t = jnp.concatenate([jnp.cos(ang_r)] * 2 + [jnp.cos(ang_c)] * 2, axis=1)
    sin_t = jnp.concatenate([-jnp.sin(ang_r), jnp.sin(ang_r), -jnp.sin(ang_c), jnp.sin(ang_c)], axis=1)
    return cos_t, sin_t


def kernel(x_prompt, x_sample, state_ret, state_mlstm_C, state_mlstm_n, state_mlstm_m, c, c_ctx,
           w_mod, b_mod, w_in, b_gate, conv_qk, ret_theta, gn_ret, gn_mlstm, w_out,
           ln1_g, ln1_b, w_up, conv_ff, w_down, ln2_g, ln2_b):
    bp, lp, _ = x_prompt.shape
    bs, ls, _ = x_sample.shape
    layer = 0
    tm_mix, tm = 1024, 512

    cvec = jnp.concatenate([c_ctx[None], c, jnp.zeros((8 - 1 - bs, D_MODEL), F32)], axis=0)
    mod = _mod_call(cvec, w_mod[layer], b_mod[layer][None])
    mod3 = mod.reshape(8 * 6, 1, D_MODEL)

    def row_prompt(tile):
        return lambda m: 0

    def row_sample(tile):
        return lambda m: 1 + (m * tile) // ls

    w_l = w_in[layer]
    w_gt = w_l[:, N_PROJ:].T.astype(BF16)
    b_gate_col = b_gate[layer].reshape(N_GATES, 1)
    w_out_b = w_out[layer].astype(BF16)
    pad = D_FF_PAD - D_FF
    w_u = jnp.pad(w_up[layer][:, :D_FF], ((0, 0), (0, pad))).astype(BF16)
    w_g = jnp.pad(w_up[layer][:, D_FF:], ((0, 0), (0, pad))).astype(BF16)
    w_d = jnp.pad(w_down[layer], ((0, pad), (0, 0))).astype(BF16)
    conv_ff_p = jnp.pad(conv_ff[layer], ((0, 0), (0, pad)))
    gn_r = gn_ret[layer].reshape(1, GROUP_DIM)
    gn_m = gn_mlstm[layer].reshape(1, GROUP_DIM)
    theta = ret_theta[layer]
    ln1 = (ln1_g[layer][None], ln1_b[layer][None])
    ln2 = (ln2_g[layer][None], ln2_b[layer][None])

    def run_group(x, mod_row, rope, period, hp, ret_init, mlstm_init, emit_states):
        batch, seq_len, _ = x.shape
        x2d = x.reshape(batch * seq_len, D_MODEL)
        proj, kt, gates = _mix_in_call(x2d, mod3, mod_row(tm_mix), w_l, w_gt, b_gate_col,
                                       conv_qk[layer], rope, seq_len=seq_len, period=period,
                                       tm=tm_mix)
        ret = _ret_call(proj, kt, theta, gn_r, ret_init, batch=batch, seq_len=seq_len, hp=hp,
                        emit_states=emit_states)
        mls = _mlstm_call(proj, kt, gates, gn_m, mlstm_init, batch=batch, seq_len=seq_len, hp=hp,
                          emit_states=emit_states)
        x1 = _mix_out_call(ret[0], mls[0], w_out_b, x2d, mod3, mod_row(tm), *ln1, tm=tm)
        y = _ffn_call(x1, mod3, mod_row(tm), w_u, w_g, conv_ff_p, w_d, *ln2, period=period, tm=tm)
        return y.reshape(batch, seq_len, D_MODEL), ret[1:], mls[1:]

    y_prompt, ret_states, mls_states = run_group(x_prompt, row_prompt, None, lp, N_HEADS,
                                                 None, None, True)

    m0 = state_mlstm_m.reshape(-1)
    y_sample, _, _ = run_group(x_sample, row_sample, _rope_tables(ls), GRID_W, 2, state_ret,
                               (state_mlstm_C, state_mlstm_n, m0), False)

    new_ret = ret_states[0]
    new_c, new_n, new_m = mls_states
    new_m = new_m[:, :, 0].reshape(bp, 1, 2, N_HEADS)
    return (y_prompt, y_sample, new_ret, new_c, new_n, new_m)
```

```python
import functools

import jax
import jax.numpy as jnp
from jax import lax
from jax.experimental import pallas as pl
from jax.experimental.pallas import tpu as pltpu

F32 = jnp.float32
BF16 = jnp.bfloat16

D_MODEL = 2048
GRID_W = 64
HEAD_DIM = 256
N_HEADS = 4
GROUP_DIM = N_HEADS * HEAD_DIM
N_PROJ = 8 * GROUP_DIM
N_GATES = 4 * N_HEADS
D_FF = 5504
CHUNK = 128
ROPE_BASE = 10000.0
ALPHA = 2.0 ** 0.25
LN_EPS = 1e-6
K_SCALE = HEAD_DIM ** -0.5

LANES = 128
V7X_VMEM_LIMIT_CAP = 60 * 1024 * 1024
VMEM_TEMP_ALLOWANCE = 16 * 1024 * 1024

MIX_ROW_CHUNK = 256
N_STORED = 6 * GROUP_DIM
FF_TN = 512
D_FF_PAD = ((D_FF + FF_TN - 1) // FF_TN) * FF_TN


def _vmem_limit(block_bytes, scratch_bytes=0):
    est = 2 * sum(block_bytes) + scratch_bytes + VMEM_TEMP_ALLOWANCE
    return int(min(V7X_VMEM_LIMIT_CAP, est))


def _nbytes(shape, dtype):
    n = 1
    for s in shape:
        n *= s
    return n * jnp.dtype(dtype).itemsize


def _ln_rows(x):
    mu = jnp.mean(x, axis=-1, keepdims=True)
    xc = x - mu
    var = jnp.mean(xc * xc, axis=-1, keepdims=True)
    return xc * lax.rsqrt(var + LN_EPS)


def _silu(x):
    return x * jax.nn.sigmoid(x)


def _log_sigmoid(x):
    return jnp.minimum(x, 0.0) - jnp.log1p(jnp.exp(-jnp.abs(x)))


def _conv3_rows(u, w, period):
    rows = u.shape[0]
    t = lax.broadcasted_iota(jnp.int32, (rows, 1), 0) & (period - 1)
    prev = jnp.where(t == 0, 0.0, pltpu.roll(u, 1, 0))
    nxt = jnp.where(t == period - 1, 0.0, pltpu.roll(u, rows - 1, 0))
    return prev * w[0:1] + u * w[1:2] + nxt * w[2:3]


def _cumsum_lanes(x, lane, reverse):
    s = 1
    while s < LANES:
        if reverse:
            x = x + jnp.where(lane < LANES - s, pltpu.roll(x, LANES - s, 1), 0.0)
        else:
            x = x + jnp.where(lane >= s, pltpu.roll(x, s, 1), 0.0)
        s *= 2
    return x


def _mod_kernel(c_ref, w_ref, b_ref, o_ref):
    a = _silu(c_ref[...]).astype(BF16)
    o_ref[...] = jnp.dot(a, w_ref[...].astype(BF16), preferred_element_type=F32) + b_ref[...]


def _mod_call(cvec, w_mod, b_mod):
    rows, n = cvec.shape[0], w_mod.shape[1]
    tn = 1024
    blocks = [_nbytes((rows, D_MODEL), F32), _nbytes((D_MODEL, tn), F32),
              _nbytes((8, tn), F32), _nbytes((rows, tn), F32)]
    return pl.pallas_call(
        _mod_kernel,
        grid=(n // tn,),
        in_specs=[pl.BlockSpec((rows, D_MODEL), lambda j: (0, 0)),
                  pl.BlockSpec((D_MODEL, tn), lambda j: (0, j)),
                  pl.BlockSpec((1, tn), lambda j: (0, j))],
        out_specs=pl.BlockSpec((rows, tn), lambda j: (0, j)),
        out_shape=jax.ShapeDtypeStruct((rows, n), F32),
        compiler_params=pltpu.CompilerParams(
            dimension_semantics=("arbitrary",), vmem_limit_bytes=_vmem_limit(blocks)),
        name="mod",
    )(cvec, w_mod, b_mod)


def _ln_mod_kernel(x_ref, sh_ref, sc_ref, wg_ref, bg_ref, h_ref, gates_ref, *, tm):
    h = _ln_rows(x_ref[...]) * (1.0 + sc_ref[0]) + sh_ref[0]
    hb = h.astype(BF16)
    h_ref[...] = hb
    g = lax.dot_general(wg_ref[...].astype(BF16), hb, (((1,), (1,)), ((), ())),
                        preferred_element_type=F32) + bg_ref[...]
    row = lax.broadcasted_iota(jnp.int32, (N_GATES, LANES), 0)
    lane = lax.broadcasted_iota(jnp.int32, (N_GATES, LANES), 1)
    kind = lax.shift_right_logical(row, 2)
    for s in range(tm // LANES):
        gs = g[:, s * LANES:(s + 1) * LANES]
        ls = _log_sigmoid(gs)
        gates_ref[s] = jnp.where(kind == 1, _cumsum_lanes(ls, lane, False),
                                 jnp.where(kind == 3, _cumsum_lanes(ls, lane, True), gs))


def _ln_mod_call(x2d, mod3, mod_row, w_gate, b_gate_col, *, tm):
    tokens = x2d.shape[0]
    slabs = tm // LANES
    blocks = [_nbytes((tm, D_MODEL), F32), _nbytes((tm, D_MODEL), BF16),
              _nbytes((N_GATES, D_MODEL), F32), _nbytes((slabs, N_GATES, LANES), F32)]
    return pl.pallas_call(
        functools.partial(_ln_mod_kernel, tm=tm),
        grid=(tokens // tm,),
        in_specs=[pl.BlockSpec((tm, D_MODEL), lambda m: (m, 0)),
                  pl.BlockSpec((1, 1, D_MODEL), lambda m: (mod_row(m) * 6 + 0, 0, 0)),
                  pl.BlockSpec((1, 1, D_MODEL), lambda m: (mod_row(m) * 6 + 1, 0, 0)),
                  pl.BlockSpec((N_GATES, D_MODEL), lambda m: (0, 0)),
                  pl.BlockSpec((N_GATES, 1), lambda m: (0, 0))],
        out_specs=[pl.BlockSpec((tm, D_MODEL), lambda m: (m, 0)),
                   pl.BlockSpec((slabs, N_GATES, LANES), lambda m: (m, 0, 0))],
        out_shape=[jax.ShapeDtypeStruct((tokens, D_MODEL), BF16),
                   jax.ShapeDtypeStruct((tokens // LANES, N_GATES, LANES), F32)],
        compiler_params=pltpu.CompilerParams(
            dimension_semantics=("parallel",), vmem_limit_bytes=_vmem_limit(blocks)),
        name="ln_mod",
    )(x2d, mod3, mod3, w_gate, b_gate_col)


def _mix_in_kernel(*refs, tm, period, grid_mode):
    if grid_mode:
        h_ref, w_ref, cv_ref, cos_ref, sin_ref, proj_ref, kt_ref, wb_scr = refs
    else:
        h_ref, w_ref, cv_ref, proj_ref, kt_ref, wb_scr = refs
    group = pl.program_id(0)

    @pl.when(pl.program_id(1) == 0)
    def _():
        wb_scr[...] = w_ref[...].astype(BF16)

    def chunks():
        for rc in range(tm // MIX_ROW_CHUNK):
            rows = slice(rc * MIX_ROW_CHUNK, (rc + 1) * MIX_ROW_CHUNK)
            yield rc, rows, lax.dot_general(h_ref[rows, :], wb_scr[...], (((1,), (1,)), ((), ())),
                                            preferred_element_type=F32)

    def rope(a, rows):
        out = []
        for s in range(GROUP_DIM // LANES):
            xs = a[:, s * LANES:(s + 1) * LANES]
            t = (s % 2) * LANES
            out.append(xs * cos_ref[rows, t:t + LANES]
                       + pltpu.roll(xs, LANES // 2, 1) * sin_ref[rows, t:t + LANES])
        return jnp.concatenate(out, axis=1)

    def store_transposed(rc, y):
        yt = y.T.astype(BF16)
        per = MIX_ROW_CHUNK // LANES
        for s in range(per):
            kt_ref[rc * per + s] = yt[:, s * LANES:(s + 1) * LANES]

    plain = (group == 2) | (group == 3) | (group == 6) | (group == 7)
    if not grid_mode:
        plain = plain | (group == 0)

    @pl.when(plain)
    def _():
        for _, rows, acc in chunks():
            proj_ref[rows, :] = acc.astype(BF16)

    if grid_mode:
        @pl.when(group == 0)
        def _():
            for _, rows, acc in chunks():
                proj_ref[rows, :] = rope(acc, rows).astype(BF16)

    @pl.when(group == 1)
    def _():
        for rc, rows, acc in chunks():
            k = acc * K_SCALE
            store_transposed(rc, rope(k, rows) if grid_mode else k)

    @pl.when(group == 4)
    def _():
        for _, rows, acc in chunks():
            proj_ref[rows, :] = _silu(_conv3_rows(acc, cv_ref[...], period)).astype(BF16)

    @pl.when(group == 5)
    def _():
        for rc, _, acc in chunks():
            store_transposed(rc, _silu(_conv3_rows(acc, cv_ref[...], period)) * K_SCALE)


def _mix_in_call(h2d, w_in_t, conv_qk, rope, *, seq_len, period, tm):
    tokens = h2d.shape[0]
    grid_mode = rope is not None
    tiles_per_seq = seq_len // tm
    slabs = tm // LANES
    last = tokens // tm - 1

    def is_key(g):
        return (g == 1) | (g == 5)

    def proj_index(g, m):
        col = g - (g >= 1).astype(jnp.int32) - (g >= 5).astype(jnp.int32)
        return jnp.where(is_key(g), last, m), col

    def kt_index(g, m):
        row = jnp.where(g == 0, 0, jnp.where(is_key(g), m, last))
        return row, (g >= 5).astype(jnp.int32), 0

    in_specs = [
        pl.BlockSpec((tm, D_MODEL), lambda g, m: (m, 0)),
        pl.BlockSpec((GROUP_DIM, D_MODEL), lambda g, m: (g, 0)),
        pl.BlockSpec((3, GROUP_DIM), lambda g, m: (0, jnp.clip(g - 4, 0, 1))),
    ]
    args = [h2d, w_in_t, conv_qk]
    blocks = [_nbytes((tm, D_MODEL), BF16), _nbytes((GROUP_DIM, D_MODEL), F32),
              _nbytes((tm, GROUP_DIM), BF16) * 2, _nbytes((8, GROUP_DIM), F32)]
    if grid_mode:
        in_specs += [pl.BlockSpec((tm, HEAD_DIM),
                                  lambda g, m: (jnp.where(g <= 1, m % tiles_per_seq, 0), 0))] * 2
        args += list(rope)
        blocks += [_nbytes((tm, HEAD_DIM), F32)] * 2
    scratch_defs = [((GROUP_DIM, D_MODEL), BF16)]

    return pl.pallas_call(
        functools.partial(_mix_in_kernel, tm=tm, period=period, grid_mode=grid_mode),
        grid=(N_PROJ // GROUP_DIM, tokens // tm),
        in_specs=in_specs,
        out_specs=[pl.BlockSpec((tm, GROUP_DIM), lambda g, m: proj_index(g, m)),
                   pl.BlockSpec((slabs, GROUP_DIM, LANES), lambda g, m: kt_index(g, m))],
        out_shape=[jax.ShapeDtypeStruct((tokens, N_STORED), BF16),
                   jax.ShapeDtypeStruct((tokens // LANES, 2 * GROUP_DIM, LANES), BF16)],
        scratch_shapes=[pltpu.VMEM(s, d) for s, d in scratch_defs],
        compiler_params=pltpu.CompilerParams(
            dimension_semantics=("arbitrary", "arbitrary"),
            vmem_limit_bytes=_vmem_limit(blocks, sum(_nbytes(s, d) for s, d in scratch_defs))),
        name="mix_in_grid" if grid_mode else "mix_in_seq",
    )(*args)


_NN = (((2,), (1,)), ((0,), (0,)))
_NT = (((2,), (2,)), ((0,), (0,)))


def _bdot(x, y, dims):
    return lax.dot_general(x, y, dims, preferred_element_type=F32)


def _chunk_loop(nc, body, reverse=False):
    if nc <= 2:
        for i in range(nc):
            body(nc - 1 - i if reverse else i)
    else:
        def step(i, carry):
            body(nc - 1 - i if reverse else i)
            return carry
        lax.fori_loop(0, nc, step, 0)


def _rows(c):
    start = c * CHUNK
    if not isinstance(start, int):
        start = pl.multiple_of(start, CHUNK)
    return pl.ds(start, CHUNK)


def _is_chunk(c, value):
    return isinstance(c, int) and c == value


def _heads(ref, r, hp):
    return jnp.stack([ref[r, hh * HEAD_DIM:(hh + 1) * HEAD_DIM] for hh in range(hp)])


def _group_block(seq_len, hp, group):
    per = N_HEADS // hp
    return pl.BlockSpec((seq_len, hp * HEAD_DIM), lambda b, hg: (b, group * per + hg))


def _kt_block(nc, hp, group):
    per = N_HEADS // hp
    return pl.BlockSpec((nc, hp * HEAD_DIM, LANES), lambda b, hg: (b, group * per + hg, 0))


def _ret_kernel(*refs, nc, hp, has_init, emit_states):
    refs = list(refs)
    theta_ref, q_ref, kt_ref, v_ref, rg_ref, gn_ref = refs[:6]
    pos = 6
    s0_ref = None
    if has_init:
        s0_ref = refs[pos]
        pos += 1
    o_ref = refs[pos]
    pos += 1
    st_ref = None
    if emit_states:
        st_ref = refs[pos]
        pos += 1
    hist, s_scr, decay_scr, qdec_scr, kdec_scr, cdec_scr = refs[pos:pos + 6]
    hg = pl.program_id(1)
    hsel = pl.ds(hg * hp, hp)

    @pl.when(pl.program_id(0) == 0)
    def _():
        ii = lax.broadcasted_iota(jnp.int32, (CHUNK, CHUNK), 0)
        jj = lax.broadcasted_iota(jnp.int32, (CHUNK, CHUNK), 1)
        d = (ii - jj).astype(F32)
        p_col = lax.broadcasted_iota(jnp.int32, (CHUNK, HEAD_DIM), 0).astype(F32)
        p_row = lax.broadcasted_iota(jnp.int32, (1, LANES), 1).astype(F32)
        for hh in range(hp):
            h = hg * hp + hh
            lg_f = _log_sigmoid(jnp.full((1, LANES), theta_ref[0, h], F32))
            lg_b = _log_sigmoid(jnp.full((1, LANES), theta_ref[1, h], F32))
            decay_scr[h] = (jnp.where(d >= 0, jnp.exp(lg_f * jnp.maximum(d, 0.0)), 0.0)
                            + jnp.where(d <= 0, jnp.exp(lg_b * jnp.maximum(-d, 0.0)), 0.0))
            qdec_scr[0, h] = jnp.exp(lg_f[:, :1] * (p_col + 1.0))
            qdec_scr[1, h] = jnp.exp(lg_b[:, :1] * (CHUNK - p_col))
            kdec_scr[0, h] = jnp.exp(lg_f * (CHUNK - 1.0 - p_row))
            kdec_scr[1, h] = jnp.exp(lg_b * p_row)
            cdec_scr[0, h] = jnp.exp(lg_f * float(CHUNK))
            cdec_scr[1, h] = jnp.exp(lg_b * float(CHUNK))

    def kv_update(c, v, direction):
        kd = (kt_ref[c].reshape(hp, HEAD_DIM, CHUNK).astype(F32)
              * kdec_scr[direction, hsel]).astype(BF16)
        s_scr[...] = s_scr[...] * cdec_scr[direction, hsel][:, :, :1] + _bdot(kd, v, _NN)

    def init_state(direction):
        s_scr[...] = (s0_ref[0, 0, direction] if has_init
                      else jnp.zeros((hp, HEAD_DIM, HEAD_DIM), F32))

    init_state(0)

    def fwd(c):
        hist[c] = s_scr[...].astype(BF16)
        kv_update(c, _heads(v_ref, _rows(c), hp), 0)

    _chunk_loop(nc, fwd)
    if emit_states:
        st_ref[0, 0, 0] = s_scr[...]

    init_state(1)

    def bwd(c):
        r = _rows(c)
        q, v = _heads(q_ref, r, hp), _heads(v_ref, r, hp)
        kt = kt_ref[c].reshape(hp, HEAD_DIM, CHUNK)
        att = (_bdot(q, kt, _NN) * decay_scr[hsel]).astype(BF16)
        o = _bdot(att, v, _NN)
        if has_init or not _is_chunk(c, 0):
            o = o + _bdot(q, hist[c], _NN) * qdec_scr[0, hsel]
        if has_init or not _is_chunk(c, nc - 1):
            o = o + _bdot(q, s_scr[...].astype(BF16), _NN) * qdec_scr[1, hsel]
        y = _ln_rows(o)
        for hh in range(hp):
            cs = slice(hh * HEAD_DIM, (hh + 1) * HEAD_DIM)
            o_ref[r, cs] = (y[hh] * gn_ref[:, cs] * _silu(rg_ref[r, cs].astype(F32))).astype(BF16)
        kv_update(c, v, 1)

    _chunk_loop(nc, bwd, reverse=True)
    if emit_states:
        st_ref[0, 0, 1] = s_scr[...]


def _ret_call(proj, kt, theta, gn, s0, *, batch, seq_len, hp, emit_states):
    nc = seq_len // CHUNK
    has_init = s0 is not None
    assert N_HEADS % hp == 0
    state_spec = pl.BlockSpec((1, 1, 2, hp, HEAD_DIM, HEAD_DIM), lambda b, hg: (b, 0, 0, hg, 0, 0))
    in_specs = [pl.BlockSpec(memory_space=pltpu.SMEM),
                _group_block(seq_len, hp, 0), _kt_block(nc, hp, 0),
                _group_block(seq_len, hp, 1), _group_block(seq_len, hp, 2),
                pl.BlockSpec((1, hp * HEAD_DIM), lambda b, hg: (0, hg))]
    args = [theta, proj, kt, proj, proj, gn]
    blocks = [_nbytes((seq_len, hp * HEAD_DIM), BF16)] * 5
    if has_init:
        in_specs.append(state_spec)
        args.append(s0)
        blocks.append(_nbytes((2, hp, HEAD_DIM, HEAD_DIM), F32))
    out_specs = [pl.BlockSpec((seq_len, hp * HEAD_DIM), lambda b, hg: (b, hg))]
    out_shape = [jax.ShapeDtypeStruct((batch * seq_len, GROUP_DIM), BF16)]
    if emit_states:
        out_specs.append(state_spec)
        out_shape.append(jax.ShapeDtypeStruct((batch, 1, 2, N_HEADS, HEAD_DIM, HEAD_DIM), F32))
        blocks.append(_nbytes((2, hp, HEAD_DIM, HEAD_DIM), F32))
    scratch_defs = [((nc, hp, HEAD_DIM, HEAD_DIM), BF16), ((hp, HEAD_DIM, HEAD_DIM), F32),
                    ((N_HEADS, CHUNK, CHUNK), F32), ((2, N_HEADS, CHUNK, HEAD_DIM), F32),
                    ((2, N_HEADS, 1, LANES), F32), ((2, N_HEADS, 1, LANES), F32)]
    return pl.pallas_call(
        functools.partial(_ret_kernel, nc=nc, hp=hp, has_init=has_init, emit_states=emit_states),
        grid=(batch, N_HEADS // hp),
        in_specs=in_specs, out_specs=out_specs, out_shape=out_shape,
        scratch_shapes=[pltpu.VMEM(s, d) for s, d in scratch_defs],
        compiler_params=pltpu.CompilerParams(
            dimension_semantics=("arbitrary", "arbitrary"),
            vmem_limit_bytes=_vmem_limit(blocks, sum(_nbytes(s, d) for s, d in scratch_defs))),
        name="ret_grid" if has_init else "ret_seq",
    )(*args)


def _mlstm_kernel(*refs, nc, hp, has_init, emit_states):
    refs = list(refs)
    q_ref, kt_ref, v_ref, mo_ref, g_ref, gn_ref = refs[:6]
    pos = 6
    c0_ref = n0_ref = m0_ref = None
    if has_init:
        c0_ref, n0_ref, m0_ref = refs[pos:pos + 3]
        pos += 3
    o_ref = refs[pos]
    pos += 1
    c_out = n_out = m_out = None
    if emit_states:
        c_out, n_out, m_out = refs[pos:pos + 3]
        pos += 3
    c_hist, n_hist, m_hist, c_scr, n_scr, m_scr = refs[pos:pos + 6]

    b = pl.program_id(0)
    hg = pl.program_id(1)
    ii = lax.broadcasted_iota(jnp.int32, (CHUNK, CHUNK), 0)
    jj = lax.broadcasted_iota(jnp.int32, (CHUNK, CHUNK), 1)
    eye = ii == jj
    lower = ii >= jj
    upper = ii <= jj

    def col(row):
        return jnp.sum(jnp.where(eye, row, 0.0), axis=2, keepdims=True)

    def gate_rows(c, kind):
        base = kind * N_HEADS + hg * hp
        return jnp.stack([g_ref[c, pl.ds(base + hh, 1), :] for hh in range(hp)])

    def direction_out(a, q, qf, v, c_b, n_row, m, b_row, i_row, mask):
        b_col = col(b_row)
        dm = jnp.where(mask, b_col - b_row + i_row, -jnp.inf)
        inter = b_col + m
        mt = jnp.maximum(inter, jnp.max(dm, axis=2, keepdims=True))
        w = jnp.exp(dm - mt)
        sp = jnp.exp(inter - mt)
        s = a * w
        num = _bdot(s.astype(BF16), v, _NN)
        den = jnp.sum(s, axis=2, keepdims=True)
        if c_b is not None:
            num = num + _bdot(q, c_b, _NN) * sp
            den = den + jnp.sum(qf * n_row, axis=2, keepdims=True) * sp
        return num / jnp.maximum(jnp.abs(den), jnp.exp(-mt))

    def state_update(c, v, b_row, i_row, last):
        m = m_scr[...][:, :, :1]
        b_last = b_row[:, :, last:last + 1]
        g = b_last - b_row + i_row
        m_new = jnp.maximum(b_last + m, jnp.max(g, axis=2, keepdims=True))
        wk = jnp.exp(g - m_new)
        sc = jnp.exp(b_last + m - m_new)
        kt = kt_ref[c].reshape(hp, HEAD_DIM, CHUNK)
        kw = (kt.astype(F32) * wk).astype(BF16)
        c_scr[...] = c_scr[...] * sc + _bdot(kw, v, _NN)
        wk8 = jnp.broadcast_to(wk, (hp, 8, CHUNK)).astype(BF16)
        n_scr[...] = n_scr[...] * sc + _bdot(wk8, kt, _NT)[:, :1, :]
        m_scr[...] = jnp.broadcast_to(m_new, (hp, 1, LANES))

    def init_state(direction):
        if has_init:
            c_scr[...] = c0_ref[0, 0, direction]
            n_scr[...] = jnp.stack([n0_ref[0, 0, direction, pl.ds(hg * hp + hh, 1), :]
                                    for hh in range(hp)])
            m_scr[...] = jnp.stack([
                jnp.full((1, LANES), m0_ref[(b * 2 + direction) * N_HEADS + hg * hp + hh], F32)
                for hh in range(hp)])
        else:
            c_scr[...] = jnp.zeros((hp, HEAD_DIM, HEAD_DIM), F32)
            n_scr[...] = jnp.zeros((hp, 1, HEAD_DIM), F32)
            m_scr[...] = jnp.zeros((hp, 1, LANES), F32)

    def emit_state(direction):
        c_out[0, 0, direction] = c_scr[...]
        for hh in range(hp):
            n_out[0, 0, direction, pl.ds(hh, 1), :] = n_scr[hh]
            m_out[0, pl.ds(direction * N_HEADS + hh, 1), :] = m_scr[hh]

    init_state(0)

    def fwd(c):
        c_hist[c] = c_scr[...].astype(BF16)
        n_hist[c] = n_scr[...]
        m_hist[c] = m_scr[...]
        state_update(c, _heads(v_ref, _rows(c), hp), gate_rows(c, 1), gate_rows(c, 0), CHUNK - 1)

    _chunk_loop(nc, fwd)
    if emit_states:
        emit_state(0)

    init_state(1)

    def bwd(c):
        r = _rows(c)
        i_f, b_f, i_b, b_b = (gate_rows(c, kind) for kind in (0, 1, 2, 3))
        q, v = _heads(q_ref, r, hp), _heads(v_ref, r, hp)
        qf = q.astype(F32)
        a = _bdot(q, kt_ref[c].reshape(hp, HEAD_DIM, CHUNK), _NN)
        zero_f = not has_init and _is_chunk(c, 0)
        zero_b = not has_init and _is_chunk(c, nc - 1)
        h_f = direction_out(a, q, qf, v, None if zero_f else c_hist[c], n_hist[c],
                            m_hist[c][:, :, :1], b_f, i_f, lower)
        h_b = direction_out(a, q, qf, v, None if zero_b else c_scr[...].astype(BF16),
                            n_scr[...], m_scr[...][:, :, :1], b_b, i_b, upper)
        y = _ln_rows(h_f + h_b)
        for hh in range(hp):
            cs = slice(hh * HEAD_DIM, (hh + 1) * HEAD_DIM)
            o_ref[r, cs] = (jax.nn.sigmoid(mo_ref[r, cs].astype(F32))
                            * (y[hh] * gn_ref[:, cs])).astype(BF16)
        state_update(c, v, b_b, i_b, 0)

    _chunk_loop(nc, bwd, reverse=True)
    if emit_states:
        emit_state(1)


def _mlstm_call(proj, kt, gates, gn, init, *, batch, seq_len, hp, emit_states):
    nc = seq_len // CHUNK
    has_init = init is not None
    assert N_HEADS % hp == 0 and (hp == N_HEADS or not emit_states)
    c_spec = pl.BlockSpec((1, 1, 2, hp, HEAD_DIM, HEAD_DIM), lambda b, hg: (b, 0, 0, hg, 0, 0))
    n_spec = pl.BlockSpec((1, 1, 2, N_HEADS, HEAD_DIM), lambda b, hg: (b, 0, 0, 0, 0))
    in_specs = [_group_block(seq_len, hp, 3), _kt_block(nc, hp, 1),
                _group_block(seq_len, hp, 4), _group_block(seq_len, hp, 5),
                pl.BlockSpec((nc, N_GATES, LANES), lambda b, hg: (b, 0, 0)),
                pl.BlockSpec((1, hp * HEAD_DIM), lambda b, hg: (0, hg))]
    args = [proj, kt, proj, proj, gates, gn]
    blocks = [_nbytes((seq_len, hp * HEAD_DIM), BF16)] * 5 + [_nbytes((nc, N_GATES, LANES), F32)]
    if has_init:
        in_specs += [c_spec, n_spec, pl.BlockSpec(memory_space=pltpu.SMEM)]
        args += list(init)
        blocks.append(_nbytes((2, hp, HEAD_DIM, HEAD_DIM), F32))
    out_specs = [pl.BlockSpec((seq_len, hp * HEAD_DIM), lambda b, hg: (b, hg))]
    out_shape = [jax.ShapeDtypeStruct((batch * seq_len, GROUP_DIM), BF16)]
    if emit_states:
        out_specs += [c_spec, n_spec, pl.BlockSpec((1, 2 * N_HEADS, LANES), lambda b, hg: (b, 0, 0))]
        out_shape += [jax.ShapeDtypeStruct((batch, 1, 2, N_HEADS, HEAD_DIM, HEAD_DIM), F32),
                      jax.ShapeDtypeStruct((batch, 1, 2, N_HEADS, HEAD_DIM), F32),
                      jax.ShapeDtypeStruct((batch, 2 * N_HEADS, LANES), F32)]
        blocks.append(_nbytes((2, hp, HEAD_DIM, HEAD_DIM), F32))
    scratch_defs = [((nc, hp, HEAD_DIM, HEAD_DIM), BF16), ((nc, hp, 1, HEAD_DIM), F32),
                    ((nc, hp, 1, LANES), F32), ((hp, HEAD_DIM, HEAD_DIM), F32),
                    ((hp, 1, HEAD_DIM), F32), ((hp, 1, LANES), F32)]
    scratch_bytes = sum(_nbytes(s, d) for s, d in scratch_defs) + 16 * nc * hp * HEAD_DIM * 4
    return pl.pallas_call(
        functools.partial(_mlstm_kernel, nc=nc, hp=hp, has_init=has_init, emit_states=emit_states),
        grid=(batch, N_HEADS // hp),
        in_specs=in_specs, out_specs=out_specs, out_shape=out_shape,
        scratch_shapes=[pltpu.VMEM(s, d) for s, d in scratch_defs],
        compiler_params=pltpu.CompilerParams(
            dimension_semantics=("arbitrary", "arbitrary"),
            vmem_limit_bytes=_vmem_limit(blocks, scratch_bytes)),
        name="mlstm_grid" if has_init else "mlstm_seq",
    )(*args)


def _mix_out_kernel(mr_ref, mm_ref, w_ref, x_ref, g1_ref, lg_ref, lb_ref, o_ref):
    for rc in range(o_ref.shape[0] // MIX_ROW_CHUNK):
        rows = slice(rc * MIX_ROW_CHUNK, (rc + 1) * MIX_ROW_CHUNK)
        mix = (jnp.dot(mr_ref[rows, :], w_ref[0:GROUP_DIM, :], preferred_element_type=F32)
               + jnp.dot(mm_ref[rows, :], w_ref[GROUP_DIM:, :], preferred_element_type=F32))
        o_ref[rows, :] = (_ln_rows(ALPHA * x_ref[rows, :] + g1_ref[0] * mix) * lg_ref[...]
                          + lb_ref[...])


def _mix_out_call(mix_r, mix_m, w_out_b, x2d, mod3, mod_row, ln_g, ln_b, *, tm):
    tokens = x2d.shape[0]
    blocks = [_nbytes((tm, GROUP_DIM), BF16)] * 2 + [_nbytes((D_MODEL, D_MODEL), BF16)] \
        + [_nbytes((tm, D_MODEL), F32)] * 2
    return pl.pallas_call(
        _mix_out_kernel,
        grid=(tokens // tm,),
        in_specs=[pl.BlockSpec((tm, GROUP_DIM), lambda m: (m, 0)),
                  pl.BlockSpec((tm, GROUP_DIM), lambda m: (m, 0)),
                  pl.BlockSpec((D_MODEL, D_MODEL), lambda m: (0, 0)),
                  pl.BlockSpec((tm, D_MODEL), lambda m: (m, 0)),
                  pl.BlockSpec((1, 1, D_MODEL), lambda m: (mod_row(m) * 6 + 2, 0, 0)),
                  pl.BlockSpec((1, D_MODEL), lambda m: (0, 0)),
                  pl.BlockSpec((1, D_MODEL), lambda m: (0, 0))],
        out_specs=pl.BlockSpec((tm, D_MODEL), lambda m: (m, 0)),
        out_shape=jax.ShapeDtypeStruct((tokens, D_MODEL), F32),
        compiler_params=pltpu.CompilerParams(
            dimension_semantics=("parallel",), vmem_limit_bytes=_vmem_limit(blocks)),
        name="mix_out",
    )(mix_r, mix_m, w_out_b, x2d, mod3, ln_g, ln_b)


def _ffn_kernel(x_ref, sh_ref, sc_ref, g2_ref, wu_ref, wg_ref, cv_ref, wd_ref, lg_ref, lb_ref,
                o_ref, h_scr, acc_scr, *, period, nj):
    j = pl.program_id(1)

    @pl.when(j == 0)
    def _():
        h = _ln_rows(x_ref[...]) * (1.0 + sc_ref[0]) + sh_ref[0]
        h_scr[...] = h.astype(BF16)
        acc_scr[...] = jnp.zeros_like(acc_scr)

    hb = h_scr[...]
    u = jnp.dot(hb, wu_ref[...], preferred_element_type=F32)
    g = jnp.dot(hb, wg_ref[...], preferred_element_type=F32)
    hid = (_silu(_conv3_rows(u, cv_ref[...], period)) * g).astype(BF16)
    acc_scr[...] += jnp.dot(hid, wd_ref[...], preferred_element_type=F32)

    @pl.when(j == nj - 1)
    def _():
        y = ALPHA * x_ref[...] + g2_ref[0] * acc_scr[...]
        o_ref[...] = _ln_rows(y) * lg_ref[...] + lb_ref[...]


def _ffn_call(x1, mod3, mod_row, w_u, w_g, conv_ff, w_d, ln_g, ln_b, *, period, tm):
    tokens = x1.shape[0]
    nj = D_FF_PAD // FF_TN
    blocks = [_nbytes((tm, D_MODEL), F32)] * 2 + [_nbytes((D_MODEL, FF_TN), BF16)] * 3 \
        + [_nbytes((8, FF_TN), F32)]
    scratch_bytes = _nbytes((tm, D_MODEL), BF16) + _nbytes((tm, D_MODEL), F32)
    return pl.pallas_call(
        functools.partial(_ffn_kernel, period=period, nj=nj),
        grid=(tokens // tm, nj),
        in_specs=[pl.BlockSpec((tm, D_MODEL), lambda m, j: (m, 0)),
                  pl.BlockSpec((1, 1, D_MODEL), lambda m, j: (mod_row(m) * 6 + 3, 0, 0)),
                  pl.BlockSpec((1, 1, D_MODEL), lambda m, j: (mod_row(m) * 6 + 4, 0, 0)),
                  pl.BlockSpec((1, 1, D_MODEL), lambda m, j: (mod_row(m) * 6 + 5, 0, 0)),
                  pl.BlockSpec((D_MODEL, FF_TN), lambda m, j: (0, j)),
                  pl.BlockSpec((D_MODEL, FF_TN), lambda m, j: (0, j)),
                  pl.BlockSpec((3, FF_TN), lambda m, j: (0, j)),
                  pl.BlockSpec((FF_TN, D_MODEL), lambda m, j: (j, 0)),
                  pl.BlockSpec((1, D_MODEL), lambda m, j: (0, 0)),
                  pl.BlockSpec((1, D_MODEL), lambda m, j: (0, 0))],
        out_specs=pl.BlockSpec((tm, D_MODEL), lambda m, j: (m, 0)),
        out_shape=jax.ShapeDtypeStruct((tokens, D_MODEL), F32),
        scratch_shapes=[pltpu.VMEM((tm, D_MODEL), BF16), pltpu.VMEM((tm, D_MODEL), F32)],
        compiler_params=pltpu.CompilerParams(
            dimension_semantics=("parallel", "arbitrary"),
            vmem_limit_bytes=_vmem_limit(blocks, scratch_bytes)),
        name="ffn",
    )(x1, mod3, mod3, mod3, w_u, w_g, conv_ff, w_d, ln_g, ln_b)


def _rope_tables(seq_len):
    quarter = HEAD_DIM // 4
    t = jnp.arange(seq_len)
    row = (t // GRID_W).astype(F32)
    col = (t % GRID_W).astype(F32)
    inv = ROPE_BASE ** (-jnp.arange(quarter, dtype=F32) / quarter)
    ang_r, ang_c = row[:, None] * inv, col[:, None] * inv
    cos_t = jnp.concatenate([jnp.cos(ang_r)] * 2 + [jnp.cos(ang_c)] * 2, axis=1)
    sin_t = jnp.concatenate([-jnp.sin(ang_r), jnp.sin(ang_r), -jnp.sin(ang_c), jnp.sin(ang_c)], axis=1)
    return cos_t, sin_t


def kernel(x_prompt, x_sample, state_ret, state_mlstm_C, state_mlstm_n, state_mlstm_m, c, c_ctx,
           w_mod, b_mod, w_in, b_gate, conv_qk, ret_theta, gn_ret, gn_mlstm, w_out,
           ln1_g, ln1_b, w_up, conv_ff, w_down, ln2_g, ln2_b):
    bp, lp, _ = x_prompt.shape
    bs, ls, _ = x_sample.shape
    layer = 0
    tm_mix, tm = 1024, 512

    cvec = jnp.concatenate([c_ctx[None], c, jnp.zeros((8 - 1 - bs, D_MODEL), F32)], axis=0)
    mod = _mod_call(cvec, w_mod[layer], b_mod[layer][None])
    mod3 = mod.reshape(8 * 6, 1, D_MODEL)

    def row_prompt(tile):
        return lambda m: 0

    def row_sample(tile):
        return lambda m: 1 + (m * tile) // ls

    w_l = w_in[layer].T
    w_gate = w_l[N_PROJ:]
    b_gate_col = b_gate[layer].reshape(N_GATES, 1)
    w_out_b = w_out[layer].astype(BF16)
    pad = D_FF_PAD - D_FF
    w_u = jnp.pad(w_up[layer][:, :D_FF], ((0, 0), (0, pad))).astype(BF16)
    w_g = jnp.pad(w_up[layer][:, D_FF:], ((0, 0), (0, pad))).astype(BF16)
    w_d = jnp.pad(w_down[layer], ((0, pad), (0, 0))).astype(BF16)
    conv_ff_p = jnp.pad(conv_ff[layer], ((0, 0), (0, pad)))
    gn_r = gn_ret[layer].reshape(1, GROUP_DIM)
    gn_m = gn_mlstm[layer].reshape(1, GROUP_DIM)
    theta = ret_theta[layer]
    ln1 = (ln1_g[layer][None], ln1_b[layer][None])
    ln2 = (ln2_g[layer][None], ln2_b[layer][None])

    def run_group(x, mod_row, rope, period, hp, ret_init, mlstm_init, emit_states):
        batch, seq_len, _ = x.shape
        x2d = x.reshape(batch * seq_len, D_MODEL)
        h2d, gates = _ln_mod_call(x2d, mod3, mod_row(tm_mix), w_gate, b_gate_col, tm=tm_mix)
        proj, kt = _mix_in_call(h2d, w_l, conv_qk[layer], rope, seq_len=seq_len, period=period,
                                tm=tm_mix)
        ret = _ret_call(proj, kt, theta, gn_r, ret_init, batch=batch, seq_len=seq_len, hp=hp,
                        emit_states=emit_states)
        mls = _mlstm_call(proj, kt, gates, gn_m, mlstm_init, batch=batch, seq_len=seq_len, hp=hp,
                          emit_states=emit_states)
        x1 = _mix_out_call(ret[0], mls[0], w_out_b, x2d, mod3, mod_row(tm), *ln1, tm=tm)
        y = _ffn_call(x1, mod3, mod_row(tm), w_u, w_g, conv_ff_p, w_d, *ln2, period=period, tm=tm)
        return y.reshape(batch, seq_len, D_MODEL), ret[1:], mls[1:]

    y_prompt, ret_states, mls_states = run_group(x_prompt, row_prompt, None, lp, N_HEADS,
                                                 None, None, True)

    m0 = state_mlstm_m.reshape(-1)
    y_sample, _, _ = run_group(x_sample, row_sample, _rope_tables(ls), GRID_W, 2, state_ret,
                               (state_mlstm_C, state_mlstm_n, m0), False)

    new_ret = ret_states[0]
    new_c, new_n, new_m = mls_states
    new_m = new_m[:, :, 0].reshape(bp, 1, 2, N_HEADS)
    return (y_prompt, y_sample, new_ret, new_c, new_n, new_m)
```

```python
import functools

import jax
import jax.numpy as jnp
from jax import lax
from jax.experimental import pallas as pl
from jax.experimental.pallas import tpu as pltpu

F32 = jnp.float32
BF16 = jnp.bfloat16

D_MODEL = 2048
GRID_W = 64
HEAD_DIM = 256
N_HEADS = 4
GROUP_DIM = N_HEADS * HEAD_DIM
N_PROJ = 8 * GROUP_DIM
N_GATES = 4 * N_HEADS
D_FF = 5504
CHUNK = 128
ROPE_BASE = 10000.0
ALPHA = 2.0 ** 0.25
LN_EPS = 1e-6
K_SCALE = HEAD_DIM ** -0.5

LANES = 128
V7X_VMEM_LIMIT_CAP = 60 * 1024 * 1024
VMEM_TEMP_ALLOWANCE = 16 * 1024 * 1024

MIX_ROW_CHUNK = 512
N_STORED = 6 * GROUP_DIM
FF_TN = 512
FF_NJ = -(-D_FF // FF_TN)
FF_OVERLAP = FF_NJ * FF_TN - D_FF


def _vmem_limit(block_bytes, scratch_bytes=0):
    est = 2 * sum(block_bytes) + scratch_bytes + VMEM_TEMP_ALLOWANCE
    return int(min(V7X_VMEM_LIMIT_CAP, est))


def _nbytes(shape, dtype):
    n = 1
    for s in shape:
        n *= s
    return n * jnp.dtype(dtype).itemsize


def _ln_rows(x):
    mu = jnp.mean(x, axis=-1, keepdims=True)
    xc = x - mu
    var = jnp.mean(xc * xc, axis=-1, keepdims=True)
    return xc * lax.rsqrt(var + LN_EPS)


def _silu(x):
    return x * jax.nn.sigmoid(x)


def _log_sigmoid(x):
    return jnp.minimum(x, 0.0) - jnp.log1p(jnp.exp(-jnp.abs(x)))


def _conv3_rows(u, w, period):
    rows = u.shape[0]
    t = lax.broadcasted_iota(jnp.int32, (rows, 1), 0) & (period - 1)
    prev = jnp.where(t == 0, 0.0, pltpu.roll(u, 1, 0))
    nxt = jnp.where(t == period - 1, 0.0, pltpu.roll(u, rows - 1, 0))
    return prev * w[0:1] + u * w[1:2] + nxt * w[2:3]


def _cumsum_lanes(x, lane, reverse):
    s = 1
    while s < LANES:
        if reverse:
            x = x + jnp.where(lane < LANES - s, pltpu.roll(x, LANES - s, 1), 0.0)
        else:
            x = x + jnp.where(lane >= s, pltpu.roll(x, s, 1), 0.0)
        s *= 2
    return x


def _mod_kernel(c_ref, w_ref, b_ref, o_ref):
    a = _silu(c_ref[...]).astype(BF16)
    o_ref[...] = jnp.dot(a, w_ref[...].astype(BF16), preferred_element_type=F32) + b_ref[...]


def _mod_call(cvec, w_mod, b_mod):
    rows, n = cvec.shape[0], w_mod.shape[1]
    tn = 1024
    blocks = [_nbytes((rows, D_MODEL), F32), _nbytes((D_MODEL, tn), F32),
              _nbytes((8, tn), F32), _nbytes((rows, tn), F32)]
    return pl.pallas_call(
        _mod_kernel,
        grid=(n // tn,),
        in_specs=[pl.BlockSpec((rows, D_MODEL), lambda j: (0, 0)),
                  pl.BlockSpec((D_MODEL, tn), lambda j: (0, j)),
                  pl.BlockSpec((1, tn), lambda j: (0, j))],
        out_specs=pl.BlockSpec((rows, tn), lambda j: (0, j)),
        out_shape=jax.ShapeDtypeStruct((rows, n), F32),
        compiler_params=pltpu.CompilerParams(
            dimension_semantics=("arbitrary",), vmem_limit_bytes=_vmem_limit(blocks)),
        name="mod",
    )(cvec, w_mod, b_mod)


def _ln_mod_kernel(x_ref, sh_ref, sc_ref, wg_ref, bg_ref, h_ref, gates_ref, *, tm):
    h = _ln_rows(x_ref[...]) * (1.0 + sc_ref[0]) + sh_ref[0]
    hb = h.astype(BF16)
    h_ref[...] = hb
    g = lax.dot_general(wg_ref[...].astype(BF16), hb, (((1,), (1,)), ((), ())),
                        preferred_element_type=F32) + bg_ref[...]
    row = lax.broadcasted_iota(jnp.int32, (N_GATES, LANES), 0)
    lane = lax.broadcasted_iota(jnp.int32, (N_GATES, LANES), 1)
    kind = lax.shift_right_logical(row, 2)
    for s in range(tm // LANES):
        gs = g[:, s * LANES:(s + 1) * LANES]
        ls = _log_sigmoid(gs)
        gates_ref[s] = jnp.where(kind == 1, _cumsum_lanes(ls, lane, False),
                                 jnp.where(kind == 3, _cumsum_lanes(ls, lane, True), gs))


def _ln_mod_call(x2d, mod3, mod_row, w_gate, b_gate_col, *, tm):
    tokens = x2d.shape[0]
    slabs = tm // LANES
    blocks = [_nbytes((tm, D_MODEL), F32), _nbytes((tm, D_MODEL), BF16),
              _nbytes((N_GATES, D_MODEL), F32), _nbytes((slabs, N_GATES, LANES), F32)]
    return pl.pallas_call(
        functools.partial(_ln_mod_kernel, tm=tm),
        grid=(tokens // tm,),
        in_specs=[pl.BlockSpec((tm, D_MODEL), lambda m: (m, 0)),
                  pl.BlockSpec((1, 1, D_MODEL), lambda m: (mod_row(m) * 6 + 0, 0, 0)),
                  pl.BlockSpec((1, 1, D_MODEL), lambda m: (mod_row(m) * 6 + 1, 0, 0)),
                  pl.BlockSpec((N_GATES, D_MODEL), lambda m: (0, 0)),
                  pl.BlockSpec((N_GATES, 1), lambda m: (0, 0))],
        out_specs=[pl.BlockSpec((tm, D_MODEL), lambda m: (m, 0)),
                   pl.BlockSpec((slabs, N_GATES, LANES), lambda m: (m, 0, 0))],
        out_shape=[jax.ShapeDtypeStruct((tokens, D_MODEL), BF16),
                   jax.ShapeDtypeStruct((tokens // LANES, N_GATES, LANES), F32)],
        compiler_params=pltpu.CompilerParams(
            dimension_semantics=("parallel",), vmem_limit_bytes=_vmem_limit(blocks)),
        name="ln_mod",
    )(x2d, mod3, mod3, w_gate, b_gate_col)


def _mix_in_kernel(*refs, tm, period, grid_mode):
    if grid_mode:
        h_ref, w_ref, cv_ref, cos_ref, sin_ref, proj_ref, kt_ref, wb_scr = refs
    else:
        h_ref, w_ref, cv_ref, proj_ref, kt_ref, wb_scr = refs
    group = pl.program_id(0)

    @pl.when(pl.program_id(1) == 0)
    def _():
        wb_scr[...] = w_ref[...].astype(BF16)

    def chunks():
        for rc in range(tm // MIX_ROW_CHUNK):
            rows = slice(rc * MIX_ROW_CHUNK, (rc + 1) * MIX_ROW_CHUNK)
            yield rc, rows, lax.dot_general(h_ref[rows, :], wb_scr[...], (((1,), (1,)), ((), ())),
                                            preferred_element_type=F32)

    def rope(a, rows):
        out = []
        for s in range(GROUP_DIM // LANES):
            xs = a[:, s * LANES:(s + 1) * LANES]
            t = (s % 2) * LANES
            out.append(xs * cos_ref[rows, t:t + LANES]
                       + pltpu.roll(xs, LANES // 2, 1) * sin_ref[rows, t:t + LANES])
        return jnp.concatenate(out, axis=1)

    def store_transposed(rc, y):
        yt = y.T.astype(BF16)
        per = MIX_ROW_CHUNK // LANES
        for s in range(per):
            kt_ref[rc * per + s] = yt[:, s * LANES:(s + 1) * LANES]

    plain = (group == 2) | (group == 3) | (group == 6) | (group == 7)
    if not grid_mode:
        plain = plain | (group == 0)

    @pl.when(plain)
    def _():
        for _, rows, acc in chunks():
            proj_ref[rows, :] = acc.astype(BF16)

    if grid_mode:
        @pl.when(group == 0)
        def _():
            for _, rows, acc in chunks():
                proj_ref[rows, :] = rope(acc, rows).astype(BF16)

    @pl.when(group == 1)
    def _():
        for rc, rows, acc in chunks():
            k = acc * K_SCALE
            store_transposed(rc, rope(k, rows) if grid_mode else k)

    @pl.when(group == 4)
    def _():
        for _, rows, acc in chunks():
            proj_ref[rows, :] = _silu(_conv3_rows(acc, cv_ref[...], period)).astype(BF16)

    @pl.when(group == 5)
    def _():
        for rc, _, acc in chunks():
            store_transposed(rc, _silu(_conv3_rows(acc, cv_ref[...], period)) * K_SCALE)


def _mix_in_call(h2d, w_in_t, conv_qk, rope, *, seq_len, period, tm):
    tokens = h2d.shape[0]
    grid_mode = rope is not None
    tiles_per_seq = seq_len // tm
    slabs = tm // LANES
    last = tokens // tm - 1

    def is_key(g):
        return (g == 1) | (g == 5)

    def proj_index(g, m):
        col = g - (g >= 1).astype(jnp.int32) - (g >= 5).astype(jnp.int32)
        return jnp.where(is_key(g), last, m), col

    def kt_index(g, m):
        row = jnp.where(g == 0, 0, jnp.where(is_key(g), m, last))
        return row, (g >= 5).astype(jnp.int32), 0

    in_specs = [
        pl.BlockSpec((tm, D_MODEL), lambda g, m: (m, 0)),
        pl.BlockSpec((GROUP_DIM, D_MODEL), lambda g, m: (g, 0)),
        pl.BlockSpec((3, GROUP_DIM), lambda g, m: (0, jnp.clip(g - 4, 0, 1))),
    ]
    args = [h2d, w_in_t, conv_qk]
    blocks = [_nbytes((tm, D_MODEL), BF16), _nbytes((GROUP_DIM, D_MODEL), F32),
              _nbytes((tm, GROUP_DIM), BF16) * 2, _nbytes((8, GROUP_DIM), F32)]
    if grid_mode:
        in_specs += [pl.BlockSpec((tm, HEAD_DIM),
                                  lambda g, m: (jnp.where(g <= 1, m % tiles_per_seq, 0), 0))] * 2
        args += list(rope)
        blocks += [_nbytes((tm, HEAD_DIM), F32)] * 2
    scratch_defs = [((GROUP_DIM, D_MODEL), BF16)]

    return pl.pallas_call(
        functools.partial(_mix_in_kernel, tm=tm, period=period, grid_mode=grid_mode),
        grid=(N_PROJ // GROUP_DIM, tokens // tm),
        in_specs=in_specs,
        out_specs=[pl.BlockSpec((tm, GROUP_DIM), lambda g, m: proj_index(g, m)),
                   pl.BlockSpec((slabs, GROUP_DIM, LANES), lambda g, m: kt_index(g, m))],
        out_shape=[jax.ShapeDtypeStruct((tokens, N_STORED), BF16),
                   jax.ShapeDtypeStruct((tokens // LANES, 2 * GROUP_DIM, LANES), BF16)],
        scratch_shapes=[pltpu.VMEM(s, d) for s, d in scratch_defs],
        compiler_params=pltpu.CompilerParams(
            dimension_semantics=("arbitrary", "arbitrary"),
            vmem_limit_bytes=_vmem_limit(blocks, sum(_nbytes(s, d) for s, d in scratch_defs))),
        name="mix_in_grid" if grid_mode else "mix_in_seq",
    )(*args)


_NN = (((2,), (1,)), ((0,), (0,)))
_NT = (((2,), (2,)), ((0,), (0,)))


def _bdot(x, y, dims):
    return lax.dot_general(x, y, dims, preferred_element_type=F32)


def _chunk_loop(nc, body, reverse=False):
    if nc <= 2:
        for i in range(nc):
            body(nc - 1 - i if reverse else i)
    else:
        def step(i, carry):
            body(nc - 1 - i if reverse else i)
            return carry
        lax.fori_loop(0, nc, step, 0)


def _rows(c):
    start = c * CHUNK
    if not isinstance(start, int):
        start = pl.multiple_of(start, CHUNK)
    return pl.ds(start, CHUNK)


def _is_chunk(c, value):
    return isinstance(c, int) and c == value


def _heads(ref, r, hp):
    return jnp.stack([ref[r, hh * HEAD_DIM:(hh + 1) * HEAD_DIM] for hh in range(hp)])


def _group_block(seq_len, hp, group):
    per = N_HEADS // hp
    return pl.BlockSpec((seq_len, hp * HEAD_DIM), lambda b, hg: (b, group * per + hg))


def _kt_block(nc, hp, group):
    per = N_HEADS // hp
    return pl.BlockSpec((nc, hp * HEAD_DIM, LANES), lambda b, hg: (b, group * per + hg, 0))


def _ret_kernel(*refs, nc, hp, has_init, emit_states):
    refs = list(refs)
    theta_ref, q_ref, kt_ref, v_ref, rg_ref, gn_ref = refs[:6]
    pos = 6
    s0_ref = None
    if has_init:
        s0_ref = refs[pos]
        pos += 1
    o_ref = refs[pos]
    pos += 1
    st_ref = None
    if emit_states:
        st_ref = refs[pos]
        pos += 1
    hist, s_scr, decay_scr, qdec_scr, kdec_scr, cdec_scr = refs[pos:pos + 6]
    hg = pl.program_id(1)
    hsel = pl.ds(hg * hp, hp)

    @pl.when(pl.program_id(0) == 0)
    def _():
        ii = lax.broadcasted_iota(jnp.int32, (CHUNK, CHUNK), 0)
        jj = lax.broadcasted_iota(jnp.int32, (CHUNK, CHUNK), 1)
        d = (ii - jj).astype(F32)
        p_col = lax.broadcasted_iota(jnp.int32, (CHUNK, HEAD_DIM), 0).astype(F32)
        p_row = lax.broadcasted_iota(jnp.int32, (1, LANES), 1).astype(F32)
        for hh in range(hp):
            h = hg * hp + hh
            lg_f = _log_sigmoid(jnp.full((1, LANES), theta_ref[0, h], F32))
            lg_b = _log_sigmoid(jnp.full((1, LANES), theta_ref[1, h], F32))
            decay_scr[h] = (jnp.where(d >= 0, jnp.exp(lg_f * jnp.maximum(d, 0.0)), 0.0)
                            + jnp.where(d <= 0, jnp.exp(lg_b * jnp.maximum(-d, 0.0)), 0.0))
            qdec_scr[0, h] = jnp.exp(lg_f[:, :1] * (p_col + 1.0))
            qdec_scr[1, h] = jnp.exp(lg_b[:, :1] * (CHUNK - p_col))
            kdec_scr[0, h] = jnp.exp(lg_f * (CHUNK - 1.0 - p_row))
            kdec_scr[1, h] = jnp.exp(lg_b * p_row)
            cdec_scr[0, h] = jnp.exp(lg_f * float(CHUNK))
            cdec_scr[1, h] = jnp.exp(lg_b * float(CHUNK))

    def kv_update(c, v, direction):
        kd = (kt_ref[c].reshape(hp, HEAD_DIM, CHUNK).astype(F32)
              * kdec_scr[direction, hsel]).astype(BF16)
        s_scr[...] = s_scr[...] * cdec_scr[direction, hsel][:, :, :1] + _bdot(kd, v, _NN)

    def init_state(direction):
        s_scr[...] = (s0_ref[0, 0, direction] if has_init
                      else jnp.zeros((hp, HEAD_DIM, HEAD_DIM), F32))

    init_state(0)

    def fwd(c):
        hist[c] = s_scr[...].astype(BF16)
        kv_update(c, _heads(v_ref, _rows(c), hp), 0)

    _chunk_loop(nc, fwd)
    if emit_states:
        st_ref[0, 0, 0] = s_scr[...]

    init_state(1)

    def bwd(c):
        r = _rows(c)
        q, v = _heads(q_ref, r, hp), _heads(v_ref, r, hp)
        kt = kt_ref[c].reshape(hp, HEAD_DIM, CHUNK)
        att = (_bdot(q, kt, _NN) * decay_scr[hsel]).astype(BF16)
        o = _bdot(att, v, _NN)
        if has_init or not _is_chunk(c, 0):
            o = o + _bdot(q, hist[c], _NN) * qdec_scr[0, hsel]
        if has_init or not _is_chunk(c, nc - 1):
            o = o + _bdot(q, s_scr[...].astype(BF16), _NN) * qdec_scr[1, hsel]
        y = _ln_rows(o)
        for hh in range(hp):
            cs = slice(hh * HEAD_DIM, (hh + 1) * HEAD_DIM)
            o_ref[r, cs] = (y[hh] * gn_ref[:, cs] * _silu(rg_ref[r, cs].astype(F32))).astype(BF16)
        kv_update(c, v, 1)

    _chunk_loop(nc, bwd, reverse=True)
    if emit_states:
        st_ref[0, 0, 1] = s_scr[...]


def _ret_call(proj, kt, theta, gn, s0, *, batch, seq_len, hp, emit_states):
    nc = seq_len // CHUNK
    has_init = s0 is not None
    assert N_HEADS % hp == 0
    state_spec = pl.BlockSpec((1, 1, 2, hp, HEAD_DIM, HEAD_DIM), lambda b, hg: (b, 0, 0, hg, 0, 0))
    in_specs = [pl.BlockSpec(memory_space=pltpu.SMEM),
                _group_block(seq_len, hp, 0), _kt_block(nc, hp, 0),
                _group_block(seq_len, hp, 1), _group_block(seq_len, hp, 2),
                pl.BlockSpec((1, hp * HEAD_DIM), lambda b, hg: (0, hg))]
    args = [theta, proj, kt, proj, proj, gn]
    blocks = [_nbytes((seq_len, hp * HEAD_DIM), BF16)] * 5
    if has_init:
        in_specs.append(state_spec)
        args.append(s0)
        blocks.append(_nbytes((2, hp, HEAD_DIM, HEAD_DIM), F32))
    out_specs = [pl.BlockSpec((seq_len, hp * HEAD_DIM), lambda b, hg: (b, hg))]
    out_shape = [jax.ShapeDtypeStruct((batch * seq_len, GROUP_DIM), BF16)]
    if emit_states:
        out_specs.append(state_spec)
        out_shape.append(jax.ShapeDtypeStruct((batch, 1, 2, N_HEADS, HEAD_DIM, HEAD_DIM), F32))
        blocks.append(_nbytes((2, hp, HEAD_DIM, HEAD_DIM), F32))
    scratch_defs = [((nc, hp, HEAD_DIM, HEAD_DIM), BF16), ((hp, HEAD_DIM, HEAD_DIM), F32),
                    ((N_HEADS, CHUNK, CHUNK), F32), ((2, N_HEADS, CHUNK, HEAD_DIM), F32),
                    ((2, N_HEADS, 1, LANES), F32), ((2, N_HEADS, 1, LANES), F32)]
    return pl.pallas_call(
        functools.partial(_ret_kernel, nc=nc, hp=hp, has_init=has_init, emit_states=emit_states),
        grid=(batch, N_HEADS // hp),
        in_specs=in_specs, out_specs=out_specs, out_shape=out_shape,
        scratch_shapes=[pltpu.VMEM(s, d) for s, d in scratch_defs],
        compiler_params=pltpu.CompilerParams(
            dimension_semantics=("arbitrary", "arbitrary"),
            vmem_limit_bytes=_vmem_limit(blocks, sum(_nbytes(s, d) for s, d in scratch_defs))),
        name="ret_grid" if has_init else "ret_seq",
    )(*args)


def _mlstm_kernel(*refs, nc, hp, has_init, emit_states):
    refs = list(refs)
    q_ref, kt_ref, v_ref, mo_ref, g_ref, gn_ref = refs[:6]
    pos = 6
    c0_ref = n0_ref = m0_ref = None
    if has_init:
        c0_ref, n0_ref, m0_ref = refs[pos:pos + 3]
        pos += 3
    o_ref = refs[pos]
    pos += 1
    c_out = n_out = m_out = None
    if emit_states:
        c_out, n_out, m_out = refs[pos:pos + 3]
        pos += 3
    c_hist, n_hist, m_hist, c_scr, n_scr, m_scr = refs[pos:pos + 6]

    b = pl.program_id(0)
    hg = pl.program_id(1)
    ii = lax.broadcasted_iota(jnp.int32, (CHUNK, CHUNK), 0)
    jj = lax.broadcasted_iota(jnp.int32, (CHUNK, CHUNK), 1)
    eye = ii == jj
    lower = ii >= jj
    upper = ii <= jj

    def col(row):
        return jnp.sum(jnp.where(eye, row, 0.0), axis=2, keepdims=True)

    def gate_rows(c, kind):
        base = kind * N_HEADS + hg * hp
        return jnp.stack([g_ref[c, pl.ds(base + hh, 1), :] for hh in range(hp)])

    def direction_out(a, q, qf, v, c_b, n_row, m, b_row, i_row, mask):
        b_col = col(b_row)
        dm = jnp.where(mask, b_col - b_row + i_row, -jnp.inf)
        inter = b_col + m
        mt = jnp.maximum(inter, jnp.max(dm, axis=2, keepdims=True))
        w = jnp.exp(dm - mt)
        sp = jnp.exp(inter - mt)
        s = a * w
        num = _bdot(s.astype(BF16), v, _NN)
        den = jnp.sum(s, axis=2, keepdims=True)
        if c_b is not None:
            num = num + _bdot(q, c_b, _NN) * sp
            den = den + jnp.sum(qf * n_row, axis=2, keepdims=True) * sp
        return num / jnp.maximum(jnp.abs(den), jnp.exp(-mt))

    def state_update(c, v, b_row, i_row, last):
        m = m_scr[...][:, :, :1]
        b_last = b_row[:, :, last:last + 1]
        g = b_last - b_row + i_row
        m_new = jnp.maximum(b_last + m, jnp.max(g, axis=2, keepdims=True))
        wk = jnp.exp(g - m_new)
        sc = jnp.exp(b_last + m - m_new)
        kt = kt_ref[c].reshape(hp, HEAD_DIM, CHUNK)
        kw = (kt.astype(F32) * wk).astype(BF16)
        c_scr[...] = c_scr[...] * sc + _bdot(kw, v, _NN)
        wk8 = jnp.broadcast_to(wk, (hp, 8, CHUNK)).astype(BF16)
        n_scr[...] = n_scr[...] * sc + _bdot(wk8, kt, _NT)[:, :1, :]
        m_scr[...] = jnp.broadcast_to(m_new, (hp, 1, LANES))

    def init_state(direction):
        if has_init:
            c_scr[...] = c0_ref[0, 0, direction]
            n_scr[...] = jnp.stack([n0_ref[0, 0, direction, pl.ds(hg * hp + hh, 1), :]
                                    for hh in range(hp)])
            m_scr[...] = jnp.stack([
                jnp.full((1, LANES), m0_ref[(b * 2 + direction) * N_HEADS + hg * hp + hh], F32)
                for hh in range(hp)])
        else:
            c_scr[...] = jnp.zeros((hp, HEAD_DIM, HEAD_DIM), F32)
            n_scr[...] = jnp.zeros((hp, 1, HEAD_DIM), F32)
            m_scr[...] = jnp.zeros((hp, 1, LANES), F32)

    def emit_state(direction):
        c_out[0, 0, direction] = c_scr[...]
        for hh in range(hp):
            n_out[0, 0, direction, pl.ds(hh, 1), :] = n_scr[hh]
            m_out[0, pl.ds(direction * N_HEADS + hh, 1), :] = m_scr[hh]

    init_state(0)

    def fwd(c):
        c_hist[c] = c_scr[...].astype(BF16)
        n_hist[c] = n_scr[...]
        m_hist[c] = m_scr[...]
        state_update(c, _heads(v_ref, _rows(c), hp), gate_rows(c, 1), gate_rows(c, 0), CHUNK - 1)

    _chunk_loop(nc, fwd)
    if emit_states:
        emit_state(0)

    init_state(1)

    def bwd(c):
        r = _rows(c)
        i_f, b_f, i_b, b_b = (gate_rows(c, kind) for kind in (0, 1, 2, 3))
        q, v = _heads(q_ref, r, hp), _heads(v_ref, r, hp)
        qf = q.astype(F32)
        a = _bdot(q, kt_ref[c].reshape(hp, HEAD_DIM, CHUNK), _NN)
        zero_f = not has_init and _is_chunk(c, 0)
        zero_b = not has_init and _is_chunk(c, nc - 1)
        h_f = direction_out(a, q, qf, v, None if zero_f else c_hist[c], n_hist[c],
                            m_hist[c][:, :, :1], b_f, i_f, lower)
        h_b = direction_out(a, q, qf, v, None if zero_b else c_scr[...].astype(BF16),
                            n_scr[...], m_scr[...][:, :, :1], b_b, i_b, upper)
        y = _ln_rows(h_f + h_b)
        for hh in range(hp):
            cs = slice(hh * HEAD_DIM, (hh + 1) * HEAD_DIM)
            o_ref[r, cs] = (jax.nn.sigmoid(mo_ref[r, cs].astype(F32))
                            * (y[hh] * gn_ref[:, cs])).astype(BF16)
        state_update(c, v, b_b, i_b, 0)

    _chunk_loop(nc, bwd, reverse=True)
    if emit_states:
        emit_state(1)


def _mlstm_call(proj, kt, gates, gn, init, *, batch, seq_len, hp, emit_states):
    nc = seq_len // CHUNK
    has_init = init is not None
    assert N_HEADS % hp == 0 and (hp == N_HEADS or not emit_states)
    c_spec = pl.BlockSpec((1, 1, 2, hp, HEAD_DIM, HEAD_DIM), lambda b, hg: (b, 0, 0, hg, 0, 0))
    n_spec = pl.BlockSpec((1, 1, 2, N_HEADS, HEAD_DIM), lambda b, hg: (b, 0, 0, 0, 0))
    in_specs = [_group_block(seq_len, hp, 3), _kt_block(nc, hp, 1),
                _group_block(seq_len, hp, 4), _group_block(seq_len, hp, 5),
                pl.BlockSpec((nc, N_GATES, LANES), lambda b, hg: (b, 0, 0)),
                pl.BlockSpec((1, hp * HEAD_DIM), lambda b, hg: (0, hg))]
    args = [proj, kt, proj, proj, gates, gn]
    blocks = [_nbytes((seq_len, hp * HEAD_DIM), BF16)] * 5 + [_nbytes((nc, N_GATES, LANES), F32)]
    if has_init:
        in_specs += [c_spec, n_spec, pl.BlockSpec(memory_space=pltpu.SMEM)]
        args += list(init)
        blocks.append(_nbytes((2, hp, HEAD_DIM, HEAD_DIM), F32))
    out_specs = [pl.BlockSpec((seq_len, hp * HEAD_DIM), lambda b, hg: (b, hg))]
    out_shape = [jax.ShapeDtypeStruct((batch * seq_len, GROUP_DIM), BF16)]
    if emit_states:
        out_specs += [c_spec, n_spec, pl.BlockSpec((1, 2 * N_HEADS, LANES), lambda b, hg: (b, 0, 0))]
        out_shape += [jax.ShapeDtypeStruct((batch, 1, 2, N_HEADS, HEAD_DIM, HEAD_DIM), F32),
                      jax.ShapeDtypeStruct((batch, 1, 2, N_HEADS, HEAD_DIM), F32),
                      jax.ShapeDtypeStruct((batch, 2 * N_HEADS, LANES), F32)]
        blocks.append(_nbytes((2, hp, HEAD_DIM, HEAD_DIM), F32))
    scratch_defs = [((nc, hp, HEAD_DIM, HEAD_DIM), BF16), ((nc, hp, 1, HEAD_DIM), F32),
                    ((nc, hp, 1, LANES), F32), ((hp, HEAD_DIM, HEAD_DIM), F32),
                    ((hp, 1, HEAD_DIM), F32), ((hp, 1, LANES), F32)]
    scratch_bytes = sum(_nbytes(s, d) for s, d in scratch_defs) + 16 * nc * hp * HEAD_DIM * 4
    return pl.pallas_call(
        functools.partial(_mlstm_kernel, nc=nc, hp=hp, has_init=has_init, emit_states=emit_states),
        grid=(batch, N_HEADS // hp),
        in_specs=in_specs, out_specs=out_specs, out_shape=out_shape,
        scratch_shapes=[pltpu.VMEM(s, d) for s, d in scratch_defs],
        compiler_params=pltpu.CompilerParams(
            dimension_semantics=("arbitrary", "arbitrary"),
            vmem_limit_bytes=_vmem_limit(blocks, scratch_bytes)),
        name="mlstm_grid" if has_init else "mlstm_seq",
    )(*args)


def _mix_out_kernel(mr_ref, mm_ref, w_ref, x_ref, g1_ref, lg_ref, lb_ref, o_ref):
    for rc in range(o_ref.shape[0] // MIX_ROW_CHUNK):
        rows = slice(rc * MIX_ROW_CHUNK, (rc + 1) * MIX_ROW_CHUNK)
        mix = (jnp.dot(mr_ref[rows, :], w_ref[0:GROUP_DIM, :], preferred_element_type=F32)
               + jnp.dot(mm_ref[rows, :], w_ref[GROUP_DIM:, :], preferred_element_type=F32))
        o_ref[rows, :] = (_ln_rows(ALPHA * x_ref[rows, :] + g1_ref[0] * mix) * lg_ref[...]
                          + lb_ref[...])


def _mix_out_call(mix_r, mix_m, w_out_b, x2d, mod3, mod_row, ln_g, ln_b, *, tm):
    tokens = x2d.shape[0]
    blocks = [_nbytes((tm, GROUP_DIM), BF16)] * 2 + [_nbytes((D_MODEL, D_MODEL), BF16)] \
        + [_nbytes((tm, D_MODEL), F32)] * 2
    return pl.pallas_call(
        _mix_out_kernel,
        grid=(tokens // tm,),
        in_specs=[pl.BlockSpec((tm, GROUP_DIM), lambda m: (m, 0)),
                  pl.BlockSpec((tm, GROUP_DIM), lambda m: (m, 0)),
                  pl.BlockSpec((D_MODEL, D_MODEL), lambda m: (0, 0)),
                  pl.BlockSpec((tm, D_MODEL), lambda m: (m, 0)),
                  pl.BlockSpec((1, 1, D_MODEL), lambda m: (mod_row(m) * 6 + 2, 0, 0)),
                  pl.BlockSpec((1, D_MODEL), lambda m: (0, 0)),
                  pl.BlockSpec((1, D_MODEL), lambda m: (0, 0))],
        out_specs=pl.BlockSpec((tm, D_MODEL), lambda m: (m, 0)),
        out_shape=jax.ShapeDtypeStruct((tokens, D_MODEL), F32),
        compiler_params=pltpu.CompilerParams(
            dimension_semantics=("parallel",), vmem_limit_bytes=_vmem_limit(blocks)),
        name="mix_out",
    )(mix_r, mix_m, w_out_b, x2d, mod3, ln_g, ln_b)


def _ffn_kernel(x_ref, sh_ref, sc_ref, g2_ref, wu_ref, wg_ref, cv_ref, wd_ref, lg_ref, lb_ref,
                o_ref, h_scr, acc_scr, *, period):
    j = pl.program_id(1)

    @pl.when(j == 0)
    def _():
        h = _ln_rows(x_ref[...]) * (1.0 + sc_ref[0]) + sh_ref[0]
        h_scr[...] = h.astype(BF16)
        acc_scr[...] = jnp.zeros_like(acc_scr)

    hb = h_scr[...]
    u = jnp.dot(hb, wu_ref[...], preferred_element_type=F32)
    g = jnp.dot(hb, wg_ref[...], preferred_element_type=F32)
    hid = _silu(_conv3_rows(u, cv_ref[0], period)) * g
    lane = lax.broadcasted_iota(jnp.int32, (1, FF_TN), 1)
    hid = jnp.where((j == FF_NJ - 1) & (lane < FF_OVERLAP), 0.0, hid).astype(BF16)
    acc_scr[...] += jnp.dot(hid, wd_ref[...], preferred_element_type=F32)

    @pl.when(j == FF_NJ - 1)
    def _():
        y = ALPHA * x_ref[...] + g2_ref[0] * acc_scr[...]
        o_ref[...] = _ln_rows(y) * lg_ref[...] + lb_ref[...]


def _ff_offset(j):
    return pl.multiple_of(jnp.minimum(j * FF_TN, D_FF - FF_TN), LANES)


def _ffn_call(x1, mod3, mod_row, w_up_b, conv_tiles, w_down_b, ln_g, ln_b, *, period, tm):
    tokens = x1.shape[0]
    up_tile = (pl.Element(D_MODEL), pl.Element(FF_TN))
    blocks = [_nbytes((tm, D_MODEL), F32)] * 2 + [_nbytes((D_MODEL, FF_TN), BF16)] * 3 \
        + [_nbytes((8, FF_TN), F32)]
    scratch_bytes = _nbytes((tm, D_MODEL), BF16) + _nbytes((tm, D_MODEL), F32)
    return pl.pallas_call(
        functools.partial(_ffn_kernel, period=period),
        grid=(tokens // tm, FF_NJ),
        in_specs=[pl.BlockSpec((tm, D_MODEL), lambda m, j: (m, 0)),
                  pl.BlockSpec((1, 1, D_MODEL), lambda m, j: (mod_row(m) * 6 + 3, 0, 0)),
                  pl.BlockSpec((1, 1, D_MODEL), lambda m, j: (mod_row(m) * 6 + 4, 0, 0)),
                  pl.BlockSpec((1, 1, D_MODEL), lambda m, j: (mod_row(m) * 6 + 5, 0, 0)),
                  pl.BlockSpec(up_tile, lambda m, j: (0, _ff_offset(j))),
                  pl.BlockSpec(up_tile, lambda m, j: (0, pl.multiple_of(D_FF + _ff_offset(j), LANES))),
                  pl.BlockSpec((1, 3, FF_TN), lambda m, j: (j, 0, 0)),
                  pl.BlockSpec((pl.Element(FF_TN), pl.Element(D_MODEL)),
                               lambda m, j: (_ff_offset(j), 0)),
                  pl.BlockSpec((1, D_MODEL), lambda m, j: (0, 0)),
                  pl.BlockSpec((1, D_MODEL), lambda m, j: (0, 0))],
        out_specs=pl.BlockSpec((tm, D_MODEL), lambda m, j: (m, 0)),
        out_shape=jax.ShapeDtypeStruct((tokens, D_MODEL), F32),
        scratch_shapes=[pltpu.VMEM((tm, D_MODEL), BF16), pltpu.VMEM((tm, D_MODEL), F32)],
        compiler_params=pltpu.CompilerParams(
            dimension_semantics=("parallel", "arbitrary"),
            vmem_limit_bytes=_vmem_limit(blocks, scratch_bytes)),
        name="ffn",
    )(x1, mod3, mod3, mod3, w_up_b, w_up_b, conv_tiles, w_down_b, ln_g, ln_b)


def _rope_tables(seq_len):
    quarter = HEAD_DIM // 4
    t = jnp.arange(seq_len)
    row = (t // GRID_W).astype(F32)
    col = (t % GRID_W).astype(F32)
    inv = ROPE_BASE ** (-jnp.arange(quarter, dtype=F32) / quarter)
    ang_r, ang_c = row[:, None] * inv, col[:, None] * inv
    cos_t = jnp.concatenate([jnp.cos(ang_r)] * 2 + [jnp.cos(ang_c)] * 2, axis=1)
    sin_t = jnp.concatenate([-jnp.sin(ang_r), jnp.sin(ang_r), -jnp.sin(ang_c), jnp.sin(ang_c)], axis=1)
    return cos_t, sin_t


def kernel(x_prompt, x_sample, state_ret, state_mlstm_C, state_mlstm_n, state_mlstm_m, c, c_ctx,
           w_mod, b_mod, w_in, b_gate, conv_qk, ret_theta, gn_ret, gn_mlstm, w_out,
           ln1_g, ln1_b, w_up, conv_ff, w_down, ln2_g, ln2_b):
    bp, lp, _ = x_prompt.shape
    bs, ls, _ = x_sample.shape
    layer = 0
    tm_mix, tm = 1024, 512

    cvec = jnp.concatenate([c_ctx[None], c, jnp.zeros((8 - 1 - bs, D_MODEL), F32)], axis=0)
    mod = _mod_call(cvec, w_mod[layer], b_mod[layer][None])
    mod3 = mod.reshape(8 * 6, 1, D_MODEL)

    def row_prompt(tile):
        return lambda m: 0

    def row_sample(tile):
        return lambda m: 1 + (m * tile) // ls

    w_l = w_in[layer].T
    w_gate = w_l[N_PROJ:]
    b_gate_col = b_gate[layer].reshape(N_GATES, 1)
    w_out_b = w_out[layer].astype(BF16)
    w_up_b = w_up[layer].astype(BF16)
    w_down_b = w_down[layer].astype(BF16)
    ff_offsets = [min(j * FF_TN, D_FF - FF_TN) for j in range(FF_NJ)]
    conv_tiles = jnp.stack([conv_ff[layer][:, o:o + FF_TN] for o in ff_offsets])
    gn_r = gn_ret[layer].reshape(1, GROUP_DIM)
    gn_m = gn_mlstm[layer].reshape(1, GROUP_DIM)
    theta = ret_theta[layer]
    ln1 = (ln1_g[layer][None], ln1_b[layer][None])
    ln2 = (ln2_g[layer][None], ln2_b[layer][None])

    def run_group(x, mod_row, rope, period, hp, ret_init, mlstm_init, emit_states):
        batch, seq_len, _ = x.shape
        x2d = x.reshape(batch * seq_len, D_MODEL)
        h2d, gates = _ln_mod_call(x2d, mod3, mod_row(tm_mix), w_gate, b_gate_col, tm=tm_mix)
        proj, kt = _mix_in_call(h2d, w_l, conv_qk[layer], rope, seq_len=seq_len, period=period,
                                tm=tm_mix)
        ret = _ret_call(proj, kt, theta, gn_r, ret_init, batch=batch, seq_len=seq_len, hp=hp,
                        emit_states=emit_states)
        mls = _mlstm_call(proj, kt, gates, gn_m, mlstm_init, batch=batch, seq_len=seq_len, hp=hp,
                          emit_states=emit_states)
        x1 = _mix_out_call(ret[0], mls[0], w_out_b, x2d, mod3, mod_row(tm), *ln1, tm=tm)
        y = _ffn_call(x1, mod3, mod_row(tm), w_up_b, conv_tiles, w_down_b, *ln2, period=period,
                      tm=tm)
        return y.reshape(batch, seq_len, D_MODEL), ret[1:], mls[1:]

    y_prompt, ret_states, mls_states = run_group(x_prompt, row_prompt, None, lp, N_HEADS,
                                                 None, None, True)

    m0 = state_mlstm_m.reshape(-1)
    y_sample, _, _ = run_group(x_sample, row_sample, _rope_tables(ls), GRID_W, 2, state_ret,
                               (state_mlstm_C, state_mlstm_n, m0), False)

    new_ret = ret_states[0]
    new_c, new_n, new_m = mls_states
    new_m = new_m[:, :, 0].reshape(bp, 1, 2, N_HEADS)
    return (y_prompt, y_sample, new_ret, new_c, new_n, new_m)
```

```python
import functools

import jax
import jax.numpy as jnp
from jax import lax
from jax.experimental import pallas as pl
from jax.experimental.pallas import tpu as pltpu

F32 = jnp.float32
BF16 = jnp.bfloat16

D_MODEL = 2048
GRID_W = 64
HEAD_DIM = 256
N_HEADS = 4
GROUP_DIM = N_HEADS * HEAD_DIM
N_PROJ = 8 * GROUP_DIM
N_GATES = 4 * N_HEADS
D_FF = 5504
CHUNK = 128
ROPE_BASE = 10000.0
ALPHA = 2.0 ** 0.25
LN_EPS = 1e-6
K_SCALE = HEAD_DIM ** -0.5

LANES = 128
BF16_ROWS = 16
V7X_VMEM_LIMIT_CAP = 60 * 1024 * 1024
VMEM_TEMP_ALLOWANCE = 16 * 1024 * 1024

MIX_ROW_CHUNK = 512
N_STORED = 6 * GROUP_DIM
FF_TN = 512
FF_NJ = -(-D_FF // FF_TN)
FF_OVERLAP = FF_NJ * FF_TN - D_FF


def _vmem_limit(block_bytes, scratch_bytes=0):
    est = 2 * sum(block_bytes) + scratch_bytes + VMEM_TEMP_ALLOWANCE
    return int(min(V7X_VMEM_LIMIT_CAP, est))


def _nbytes(shape, dtype):
    n = 1
    for s in shape:
        n *= s
    return n * jnp.dtype(dtype).itemsize


def _ln_rows(x):
    mu = jnp.mean(x, axis=-1, keepdims=True)
    xc = x - mu
    var = jnp.mean(xc * xc, axis=-1, keepdims=True)
    return xc * lax.rsqrt(var + LN_EPS)


def _silu(x):
    return x * jax.nn.sigmoid(x)


def _log_sigmoid(x):
    return jnp.minimum(x, 0.0) - jnp.log1p(jnp.exp(-jnp.abs(x)))


def _conv3_rows(u, w, period):
    rows = u.shape[0]
    t = lax.broadcasted_iota(jnp.int32, (rows, 1), 0) & (period - 1)
    prev = jnp.where(t == 0, 0.0, pltpu.roll(u, 1, 0))
    nxt = jnp.where(t == period - 1, 0.0, pltpu.roll(u, rows - 1, 0))
    return prev * w[0:1] + u * w[1:2] + nxt * w[2:3]


def _cumsum_lanes(x, lane, reverse):
    s = 1
    while s < LANES:
        if reverse:
            x = x + jnp.where(lane < LANES - s, pltpu.roll(x, LANES - s, 1), 0.0)
        else:
            x = x + jnp.where(lane >= s, pltpu.roll(x, s, 1), 0.0)
        s *= 2
    return x


def _mod_kernel(c_ref, w_ref, b_ref, o_ref):
    a = _silu(c_ref[...]).astype(BF16)
    o_ref[...] = jnp.dot(a, w_ref[...].astype(BF16), preferred_element_type=F32) + b_ref[...]


def _mod_call(cvec, w_mod, b_mod):
    rows, n = cvec.shape[0], w_mod.shape[1]
    tn = 1024
    blocks = [_nbytes((rows, D_MODEL), F32), _nbytes((D_MODEL, tn), F32),
              _nbytes((8, tn), F32), _nbytes((rows, tn), F32)]
    return pl.pallas_call(
        _mod_kernel,
        grid=(n // tn,),
        in_specs=[pl.BlockSpec((rows, D_MODEL), lambda j: (0, 0)),
                  pl.BlockSpec((D_MODEL, tn), lambda j: (0, j)),
                  pl.BlockSpec((1, tn), lambda j: (0, j))],
        out_specs=pl.BlockSpec((rows, tn), lambda j: (0, j)),
        out_shape=jax.ShapeDtypeStruct((rows, n), F32),
        compiler_params=pltpu.CompilerParams(
            dimension_semantics=("arbitrary",), vmem_limit_bytes=_vmem_limit(blocks)),
        name="mod",
    )(cvec, w_mod, b_mod)


def _ln_mod_kernel(x_ref, sh_ref, sc_ref, wg_ref, bg_ref, h_ref, gates_ref, *, tm):
    h = _ln_rows(x_ref[...]) * (1.0 + sc_ref[0]) + sh_ref[0]
    hb = h.astype(BF16)
    h_ref[...] = hb
    g = lax.dot_general(wg_ref[...].astype(BF16), hb, (((1,), (1,)), ((), ())),
                        preferred_element_type=F32) + bg_ref[...]
    row = lax.broadcasted_iota(jnp.int32, (N_GATES, LANES), 0)
    lane = lax.broadcasted_iota(jnp.int32, (N_GATES, LANES), 1)
    kind = lax.shift_right_logical(row, 2)
    for s in range(tm // LANES):
        gs = g[:, s * LANES:(s + 1) * LANES]
        ls = _log_sigmoid(gs)
        gates_ref[s] = jnp.where(kind == 1, _cumsum_lanes(ls, lane, False),
                                 jnp.where(kind == 3, _cumsum_lanes(ls, lane, True), gs))


def _ln_mod_call(x2d, mod3, mod_row, w_gate, b_gate_col, *, tm):
    tokens = x2d.shape[0]
    slabs = tm // LANES
    blocks = [_nbytes((tm, D_MODEL), F32), _nbytes((tm, D_MODEL), BF16),
              _nbytes((N_GATES, D_MODEL), F32), _nbytes((slabs, N_GATES, LANES), F32)]
    return pl.pallas_call(
        functools.partial(_ln_mod_kernel, tm=tm),
        grid=(tokens // tm,),
        in_specs=[pl.BlockSpec((tm, D_MODEL), lambda m: (m, 0)),
                  pl.BlockSpec((1, 1, D_MODEL), lambda m: (mod_row(m) * 6 + 0, 0, 0)),
                  pl.BlockSpec((1, 1, D_MODEL), lambda m: (mod_row(m) * 6 + 1, 0, 0)),
                  pl.BlockSpec((N_GATES, D_MODEL), lambda m: (0, 0)),
                  pl.BlockSpec((N_GATES, 1), lambda m: (0, 0))],
        out_specs=[pl.BlockSpec((tm, D_MODEL), lambda m: (m, 0)),
                   pl.BlockSpec((slabs, N_GATES, LANES), lambda m: (m, 0, 0))],
        out_shape=[jax.ShapeDtypeStruct((tokens, D_MODEL), BF16),
                   jax.ShapeDtypeStruct((tokens // LANES, N_GATES, LANES), F32)],
        compiler_params=pltpu.CompilerParams(
            dimension_semantics=("parallel",), vmem_limit_bytes=_vmem_limit(blocks)),
        name="ln_mod",
    )(x2d, mod3, mod3, w_gate, b_gate_col)


def _mix_in_kernel(*refs, tm, period, grid_mode, n_staged):
    refs = list(refs)
    h_ref, w_ref, cv_ref = refs[:3]
    pos = 3
    if grid_mode:
        cos_ref, sin_ref = refs[pos:pos + 2]
        pos += 2
    staged_in = refs[pos:pos + n_staged]
    pos += n_staged
    proj_ref, kt_ref = refs[pos:pos + 2]
    pos += 2
    staged_out = refs[pos:pos + n_staged]
    wb_scr = refs[pos + n_staged]
    group = pl.program_id(0)

    @pl.when(pl.program_id(1) == 0)
    def _():
        wb_scr[...] = w_ref[...].astype(BF16)

    def chunks():
        for src, dst in zip(staged_in, staged_out):
            dst[...] = src[...].astype(BF16)
        for rc in range(tm // MIX_ROW_CHUNK):
            rows = slice(rc * MIX_ROW_CHUNK, (rc + 1) * MIX_ROW_CHUNK)
            yield rc, rows, lax.dot_general(h_ref[rows, :], wb_scr[...], (((1,), (1,)), ((), ())),
                                            preferred_element_type=F32)

    def rope(a, rows):
        out = []
        for s in range(GROUP_DIM // LANES):
            xs = a[:, s * LANES:(s + 1) * LANES]
            t = (s % 2) * LANES
            out.append(xs * cos_ref[rows, t:t + LANES]
                       + pltpu.roll(xs, LANES // 2, 1) * sin_ref[rows, t:t + LANES])
        return jnp.concatenate(out, axis=1)

    def store_transposed(rc, y):
        yt = y.T.astype(BF16)
        per = MIX_ROW_CHUNK // LANES
        for s in range(per):
            kt_ref[rc * per + s] = yt[:, s * LANES:(s + 1) * LANES]

    plain = (group == 2) | (group == 3) | (group == 6) | (group == 7)
    if not grid_mode:
        plain = plain | (group == 0)

    @pl.when(plain)
    def _():
        for _, rows, acc in chunks():
            proj_ref[rows, :] = acc.astype(BF16)

    if grid_mode:
        @pl.when(group == 0)
        def _():
            for _, rows, acc in chunks():
                proj_ref[rows, :] = rope(acc, rows).astype(BF16)

    @pl.when(group == 1)
    def _():
        for rc, rows, acc in chunks():
            k = acc * K_SCALE
            store_transposed(rc, rope(k, rows) if grid_mode else k)

    @pl.when(group == 4)
    def _():
        for _, rows, acc in chunks():
            proj_ref[rows, :] = _silu(_conv3_rows(acc, cv_ref[...], period)).astype(BF16)

    @pl.when(group == 5)
    def _():
        for rc, _, acc in chunks():
            store_transposed(rc, _silu(_conv3_rows(acc, cv_ref[...], period)) * K_SCALE)


def _staging_specs(weights, n_groups, n_tiles):
    specs, shapes, block_bytes = [], [], []
    for w in weights:
        rows, cols = w.shape
        slab = BF16_ROWS
        while rows % slab or rows // slab > n_groups * n_tiles:
            slab += BF16_ROWS
        n_slabs = rows // slab
        specs.append(pl.BlockSpec(
            (slab, cols),
            lambda g, m, n_slabs=n_slabs: (jnp.minimum(g * n_tiles + m, n_slabs - 1), 0)))
        shapes.append(jax.ShapeDtypeStruct((rows, cols), BF16))
        block_bytes.append(_nbytes((slab, cols), F32) + _nbytes((slab, cols), BF16))
    return specs, shapes, block_bytes


def _mix_in_call(h2d, w_in_t, conv_qk, rope, *, seq_len, period, tm, stage=()):
    tokens = h2d.shape[0]
    grid_mode = rope is not None
    tiles_per_seq = seq_len // tm
    slabs = tm // LANES
    last = tokens // tm - 1
    n_groups = N_PROJ // GROUP_DIM
    stage_specs, stage_shapes, stage_bytes = _staging_specs(stage, n_groups, tokens // tm)

    def is_key(g):
        return (g == 1) | (g == 5)

    def proj_index(g, m):
        col = g - (g >= 1).astype(jnp.int32) - (g >= 5).astype(jnp.int32)
        return jnp.where(is_key(g), last, m), col

    def kt_index(g, m):
        row = jnp.where(g == 0, 0, jnp.where(is_key(g), m, last))
        return row, (g >= 5).astype(jnp.int32), 0

    in_specs = [
        pl.BlockSpec((tm, D_MODEL), lambda g, m: (m, 0)),
        pl.BlockSpec((GROUP_DIM, D_MODEL), lambda g, m: (g, 0)),
        pl.BlockSpec((3, GROUP_DIM), lambda g, m: (0, jnp.clip(g - 4, 0, 1))),
    ]
    args = [h2d, w_in_t, conv_qk]
    blocks = [_nbytes((tm, D_MODEL), BF16), _nbytes((GROUP_DIM, D_MODEL), F32),
              _nbytes((tm, GROUP_DIM), BF16) * 2, _nbytes((8, GROUP_DIM), F32)]
    if grid_mode:
        in_specs += [pl.BlockSpec((tm, HEAD_DIM),
                                  lambda g, m: (jnp.where(g <= 1, m % tiles_per_seq, 0), 0))] * 2
        args += list(rope)
        blocks += [_nbytes((tm, HEAD_DIM), F32)] * 2
    in_specs += stage_specs
    args += list(stage)
    blocks += stage_bytes
    scratch_defs = [((GROUP_DIM, D_MODEL), BF16)]

    return pl.pallas_call(
        functools.partial(_mix_in_kernel, tm=tm, period=period, grid_mode=grid_mode,
                          n_staged=len(stage)),
        grid=(n_groups, tokens // tm),
        in_specs=in_specs,
        out_specs=[pl.BlockSpec((tm, GROUP_DIM), lambda g, m: proj_index(g, m)),
                   pl.BlockSpec((slabs, GROUP_DIM, LANES), lambda g, m: kt_index(g, m))]
        + stage_specs,
        out_shape=[jax.ShapeDtypeStruct((tokens, N_STORED), BF16),
                   jax.ShapeDtypeStruct((tokens // LANES, 2 * GROUP_DIM, LANES), BF16)]
        + stage_shapes,
        scratch_shapes=[pltpu.VMEM(s, d) for s, d in scratch_defs],
        compiler_params=pltpu.CompilerParams(
            dimension_semantics=("arbitrary", "arbitrary"),
            vmem_limit_bytes=_vmem_limit(blocks, sum(_nbytes(s, d) for s, d in scratch_defs))),
        name="mix_in_grid" if grid_mode else "mix_in_seq",
    )(*args)


_NN = (((2,), (1,)), ((0,), (0,)))
_NT = (((2,), (2,)), ((0,), (0,)))


def _bdot(x, y, dims):
    return lax.dot_general(x, y, dims, preferred_element_type=F32)


def _chunk_loop(nc, body, reverse=False):
    if nc <= 2:
        for i in range(nc):
            body(nc - 1 - i if reverse else i)
    else:
        def step(i, carry):
            body(nc - 1 - i if reverse else i)
            return carry
        lax.fori_loop(0, nc, step, 0)


def _rows(c, base=0):
    start = base + c * CHUNK
    if not isinstance(start, int):
        start = pl.multiple_of(start, CHUNK)
    return pl.ds(start, CHUNK)


def _is_chunk(c, value):
    return isinstance(c, int) and c == value


def _heads(ref, c, nc, bb, hp):
    return jnp.stack([ref[_rows(c, bi * nc * CHUNK), hh * HEAD_DIM:(hh + 1) * HEAD_DIM]
                      for bi in range(bb) for hh in range(hp)])


def _keys_t(kt_ref, c, nc, bb, hp):
    parts = [kt_ref[bi * nc + c].reshape(hp, HEAD_DIM, CHUNK) for bi in range(bb)]
    return parts[0] if bb == 1 else jnp.concatenate(parts, axis=0)


def _per_seq(x, bb):
    return x if bb == 1 else jnp.concatenate([x] * bb, axis=0)


def _group_block(seq_len, hp, group, bb):
    per = N_HEADS // hp
    return pl.BlockSpec((bb * seq_len, hp * HEAD_DIM), lambda b, hg: (b, group * per + hg))


def _kt_block(nc, hp, group, bb):
    per = N_HEADS // hp
    return pl.BlockSpec((bb * nc, hp * HEAD_DIM, LANES), lambda b, hg: (b, group * per + hg, 0))


def _ret_kernel(*refs, nc, bb, hp, has_init, emit_states):
    refs = list(refs)
    theta_ref, q_ref, kt_ref, v_ref, rg_ref, gn_ref = refs[:6]
    pos = 6
    s0_ref = None
    if has_init:
        s0_ref = refs[pos]
        pos += 1
    o_ref = refs[pos]
    pos += 1
    st_ref = None
    if emit_states:
        st_ref = refs[pos]
        pos += 1
    hist, s_scr, decay_scr, qdec_scr, kdec_scr, cdec_scr = refs[pos:pos + 6]
    hg = pl.program_id(1)
    hsel = pl.ds(hg * hp, hp)

    @pl.when(pl.program_id(0) == 0)
    def _():
        ii = lax.broadcasted_iota(jnp.int32, (CHUNK, CHUNK), 0)
        jj = lax.broadcasted_iota(jnp.int32, (CHUNK, CHUNK), 1)
        d = (ii - jj).astype(F32)
        p_col = lax.broadcasted_iota(jnp.int32, (CHUNK, HEAD_DIM), 0).astype(F32)
        p_row = lax.broadcasted_iota(jnp.int32, (1, LANES), 1).astype(F32)
        for hh in range(hp):
            h = hg * hp + hh
            lg_f = _log_sigmoid(jnp.full((1, LANES), theta_ref[0, h], F32))
            lg_b = _log_sigmoid(jnp.full((1, LANES), theta_ref[1, h], F32))
            decay_scr[h] = (jnp.where(d >= 0, jnp.exp(lg_f * jnp.maximum(d, 0.0)), 0.0)
                            + jnp.where(d <= 0, jnp.exp(lg_b * jnp.maximum(-d, 0.0)), 0.0))
            qdec_scr[0, h] = jnp.exp(lg_f[:, :1] * (p_col + 1.0))
            qdec_scr[1, h] = jnp.exp(lg_b[:, :1] * (CHUNK - p_col))
            kdec_scr[0, h] = jnp.exp(lg_f * (CHUNK - 1.0 - p_row))
            kdec_scr[1, h] = jnp.exp(lg_b * p_row)
            cdec_scr[0, h] = jnp.exp(lg_f * float(CHUNK))
            cdec_scr[1, h] = jnp.exp(lg_b * float(CHUNK))

    def table(ref, *lead):
        return _per_seq(ref[(*lead, hsel)], bb)

    def kv_update(c, v, direction):
        kd = (_keys_t(kt_ref, c, nc, bb, hp).astype(F32)
              * table(kdec_scr, direction)).astype(BF16)
        s_scr[...] = s_scr[...] * table(cdec_scr, direction)[:, :, :1] + _bdot(kd, v, _NN)

    def init_state(direction):
        s_scr[...] = (s0_ref[0, 0, direction] if has_init
                      else jnp.zeros((bb * hp, HEAD_DIM, HEAD_DIM), F32))

    def emit_state(direction):
        for bi in range(bb):
            st_ref[bi, 0, direction] = s_scr[bi * hp:(bi + 1) * hp]

    init_state(0)

    def fwd(c):
        hist[c] = s_scr[...].astype(BF16)
        kv_update(c, _heads(v_ref, c, nc, bb, hp), 0)

    _chunk_loop(nc, fwd)
    if emit_states:
        emit_state(0)

    init_state(1)

    def bwd(c):
        q, v = _heads(q_ref, c, nc, bb, hp), _heads(v_ref, c, nc, bb, hp)
        kt = _keys_t(kt_ref, c, nc, bb, hp)
        att = (_bdot(q, kt, _NN) * table(decay_scr)).astype(BF16)
        o = _bdot(att, v, _NN)
        if has_init or not _is_chunk(c, 0):
            o = o + _bdot(q, hist[c], _NN) * table(qdec_scr, 0)
        if has_init or not _is_chunk(c, nc - 1):
            o = o + _bdot(q, s_scr[...].astype(BF16), _NN) * table(qdec_scr, 1)
        y = _ln_rows(o)
        for bi in range(bb):
            r = _rows(c, bi * nc * CHUNK)
            for hh in range(hp):
                cs = slice(hh * HEAD_DIM, (hh + 1) * HEAD_DIM)
                o_ref[r, cs] = (y[bi * hp + hh] * gn_ref[:, cs]
                                * _silu(rg_ref[r, cs].astype(F32))).astype(BF16)
        kv_update(c, v, 1)

    _chunk_loop(nc, bwd, reverse=True)
    if emit_states:
        emit_state(1)


def _ret_call(proj, kt, theta, gn, s0, *, batch, seq_len, bb, hp, emit_states):
    nc = seq_len // CHUNK
    has_init = s0 is not None
    assert N_HEADS % hp == 0 and batch % bb == 0 and (bb == 1 or not has_init)
    state_spec = pl.BlockSpec((bb, 1, 2, hp, HEAD_DIM, HEAD_DIM), lambda b, hg: (b, 0, 0, hg, 0, 0))
    in_specs = [pl.BlockSpec(memory_space=pltpu.SMEM),
                _group_block(seq_len, hp, 0, bb), _kt_block(nc, hp, 0, bb),
                _group_block(seq_len, hp, 1, bb), _group_block(seq_len, hp, 2, bb),
                pl.BlockSpec((1, hp * HEAD_DIM), lambda b, hg: (0, hg))]
    args = [theta, proj, kt, proj, proj, gn]
    blocks = [_nbytes((bb * seq_len, hp * HEAD_DIM), BF16)] * 5
    if has_init:
        in_specs.append(state_spec)
        args.append(s0)
        blocks.append(_nbytes((2, hp, HEAD_DIM, HEAD_DIM), F32))
    out_specs = [pl.BlockSpec((bb * seq_len, hp * HEAD_DIM), lambda b, hg: (b, hg))]
    out_shape = [jax.ShapeDtypeStruct((batch * seq_len, GROUP_DIM), BF16)]
    if emit_states:
        out_specs.append(state_spec)
        out_shape.append(jax.ShapeDtypeStruct((batch, 1, 2, N_HEADS, HEAD_DIM, HEAD_DIM), F32))
        blocks.append(_nbytes((bb, 2, hp, HEAD_DIM, HEAD_DIM), F32))
    scratch_defs = [((nc, bb * hp, HEAD_DIM, HEAD_DIM), BF16), ((bb * hp, HEAD_DIM, HEAD_DIM), F32),
                    ((N_HEADS, CHUNK, CHUNK), F32), ((2, N_HEADS, CHUNK, HEAD_DIM), F32),
                    ((2, N_HEADS, 1, LANES), F32), ((2, N_HEADS, 1, LANES), F32)]
    return pl.pallas_call(
        functools.partial(_ret_kernel, nc=nc, bb=bb, hp=hp, has_init=has_init,
                          emit_states=emit_states),
        grid=(batch // bb, N_HEADS // hp),
        in_specs=in_specs, out_specs=out_specs, out_shape=out_shape,
        scratch_shapes=[pltpu.VMEM(s, d) for s, d in scratch_defs],
        compiler_params=pltpu.CompilerParams(
            dimension_semantics=("arbitrary", "arbitrary"),
            vmem_limit_bytes=_vmem_limit(blocks, sum(_nbytes(s, d) for s, d in scratch_defs))),
        name="ret_grid" if has_init else "ret_seq",
    )(*args)


def _mlstm_kernel(*refs, nc, bb, hp, has_init, emit_states):
    refs = list(refs)
    q_ref, kt_ref, v_ref, mo_ref, g_ref, gn_ref = refs[:6]
    pos = 6
    c0_ref = n0_ref = m0_ref = None
    if has_init:
        c0_ref, n0_ref, m0_ref = refs[pos:pos + 3]
        pos += 3
    o_ref = refs[pos]
    pos += 1
    c_out = n_out = m_out = None
    if emit_states:
        c_out, n_out, m_out = refs[pos:pos + 3]
        pos += 3
    c_hist, n_hist, m_hist, c_scr, n_scr, m_scr = refs[pos:pos + 6]

    b = pl.program_id(0)
    hg = pl.program_id(1)
    ii = lax.broadcasted_iota(jnp.int32, (CHUNK, CHUNK), 0)
    jj = lax.broadcasted_iota(jnp.int32, (CHUNK, CHUNK), 1)
    eye = ii == jj
    lower = ii >= jj
    upper = ii <= jj

    ne = bb * hp

    def col(row):
        return jnp.sum(jnp.where(eye, row, 0.0), axis=2, keepdims=True)

    def gate_rows(c, kind):
        base = kind * N_HEADS + hg * hp
        return jnp.stack([g_ref[bi * nc + c, pl.ds(base + hh, 1), :]
                          for bi in range(bb) for hh in range(hp)])

    def direction_out(a, q, qf, v, c_b, n_row, m, b_row, i_row, mask):
        b_col = col(b_row)
        dm = jnp.where(mask, b_col - b_row + i_row, -jnp.inf)
        inter = b_col + m
        mt = jnp.maximum(inter, jnp.max(dm, axis=2, keepdims=True))
        w = jnp.exp(dm - mt)
        sp = jnp.exp(inter - mt)
        s = a * w
        num = _bdot(s.astype(BF16), v, _NN)
        den = jnp.sum(s, axis=2, keepdims=True)
        if c_b is not None:
            num = num + _bdot(q, c_b, _NN) * sp
            den = den + jnp.sum(qf * n_row, axis=2, keepdims=True) * sp
        return num / jnp.maximum(jnp.abs(den), jnp.exp(-mt))

    def state_update(c, v, b_row, i_row, last):
        m = m_scr[...][:, :, :1]
        b_last = b_row[:, :, last:last + 1]
        g = b_last - b_row + i_row
        m_new = jnp.maximum(b_last + m, jnp.max(g, axis=2, keepdims=True))
        wk = jnp.exp(g - m_new)
        sc = jnp.exp(b_last + m - m_new)
        kt = _keys_t(kt_ref, c, nc, bb, hp)
        kw = (kt.astype(F32) * wk).astype(BF16)
        c_scr[...] = c_scr[...] * sc + _bdot(kw, v, _NN)
        wk8 = jnp.broadcast_to(wk, (ne, 8, CHUNK)).astype(BF16)
        n_scr[...] = n_scr[...] * sc + _bdot(wk8, kt, _NT)[:, :1, :]
        m_scr[...] = jnp.broadcast_to(m_new, (ne, 1, LANES))

    def init_state(direction):
        if has_init:
            c_scr[...] = c0_ref[0, 0, direction]
            n_scr[...] = jnp.stack([n0_ref[0, 0, direction, pl.ds(hg * hp + hh, 1), :]
                                    for hh in range(hp)])
            m_scr[...] = jnp.stack([
                jnp.full((1, LANES), m0_ref[(b * 2 + direction) * N_HEADS + hg * hp + hh], F32)
                for hh in range(hp)])
        else:
            c_scr[...] = jnp.zeros((ne, HEAD_DIM, HEAD_DIM), F32)
            n_scr[...] = jnp.zeros((ne, 1, HEAD_DIM), F32)
            m_scr[...] = jnp.zeros((ne, 1, LANES), F32)

    def emit_state(direction):
        for bi in range(bb):
            c_out[bi, 0, direction] = c_scr[bi * hp:(bi + 1) * hp]
            for hh in range(hp):
                n_out[bi, 0, direction, pl.ds(hh, 1), :] = n_scr[bi * hp + hh]
                m_out[bi, pl.ds(direction * N_HEADS + hh, 1), :] = m_scr[bi * hp + hh]

    init_state(0)

    def fwd(c):
        c_hist[c] = c_scr[...].astype(BF16)
        n_hist[c] = n_scr[...]
        m_hist[c] = m_scr[...]
        state_update(c, _heads(v_ref, c, nc, bb, hp), gate_rows(c, 1), gate_rows(c, 0), CHUNK - 1)

    _chunk_loop(nc, fwd)
    if emit_states:
        emit_state(0)

    init_state(1)

    def bwd(c):
        i_f, b_f, i_b, b_b = (gate_rows(c, kind) for kind in (0, 1, 2, 3))
        q, v = _heads(q_ref, c, nc, bb, hp), _heads(v_ref, c, nc, bb, hp)
        qf = q.astype(F32)
        a = _bdot(q, _keys_t(kt_ref, c, nc, bb, hp), _NN)
        zero_f = not has_init and _is_chunk(c, 0)
        zero_b = not has_init and _is_chunk(c, nc - 1)
        h_f = direction_out(a, q, qf, v, None if zero_f else c_hist[c], n_hist[c],
                            m_hist[c][:, :, :1], b_f, i_f, lower)
        h_b = direction_out(a, q, qf, v, None if zero_b else c_scr[...].astype(BF16),
                            n_scr[...], m_scr[...][:, :, :1], b_b, i_b, upper)
        y = _ln_rows(h_f + h_b)
        for bi in range(bb):
            r = _rows(c, bi * nc * CHUNK)
            for hh in range(hp):
                cs = slice(hh * HEAD_DIM, (hh + 1) * HEAD_DIM)
                o_ref[r, cs] = (jax.nn.sigmoid(mo_ref[r, cs].astype(F32))
                                * (y[bi * hp + hh] * gn_ref[:, cs])).astype(BF16)
        state_update(c, v, b_b, i_b, 0)

    _chunk_loop(nc, bwd, reverse=True)
    if emit_states:
        emit_state(1)


def _mlstm_call(proj, kt, gates, gn, init, *, batch, seq_len, bb, hp, emit_states):
    nc = seq_len // CHUNK
    has_init = init is not None
    assert N_HEADS % hp == 0 and (hp == N_HEADS or not emit_states)
    assert batch % bb == 0 and (bb == 1 or not has_init)
    c_spec = pl.BlockSpec((bb, 1, 2, hp, HEAD_DIM, HEAD_DIM), lambda b, hg: (b, 0, 0, hg, 0, 0))
    n_spec = pl.BlockSpec((bb, 1, 2, N_HEADS, HEAD_DIM), lambda b, hg: (b, 0, 0, 0, 0))
    in_specs = [_group_block(seq_len, hp, 3, bb), _kt_block(nc, hp, 1, bb),
                _group_block(seq_len, hp, 4, bb), _group_block(seq_len, hp, 5, bb),
                pl.BlockSpec((bb * nc, N_GATES, LANES), lambda b, hg: (b, 0, 0)),
                pl.BlockSpec((1, hp * HEAD_DIM), lambda b, hg: (0, hg))]
    args = [proj, kt, proj, proj, gates, gn]
    blocks = ([_nbytes((bb * seq_len, hp * HEAD_DIM), BF16)] * 5
              + [_nbytes((bb * nc, N_GATES, LANES), F32)])
    if has_init:
        in_specs += [c_spec, n_spec, pl.BlockSpec(memory_space=pltpu.SMEM)]
        args += list(init)
        blocks.append(_nbytes((2, hp, HEAD_DIM, HEAD_DIM), F32))
    out_specs = [pl.BlockSpec((bb * seq_len, hp * HEAD_DIM), lambda b, hg: (b, hg))]
    out_shape = [jax.ShapeDtypeStruct((batch * seq_len, GROUP_DIM), BF16)]
    if emit_states:
        out_specs += [c_spec, n_spec, pl.BlockSpec((bb, 2 * N_HEADS, LANES), lambda b, hg: (b, 0, 0))]
        out_shape += [jax.ShapeDtypeStruct((batch, 1, 2, N_HEADS, HEAD_DIM, HEAD_DIM), F32),
                      jax.ShapeDtypeStruct((batch, 1, 2, N_HEADS, HEAD_DIM), F32),
                      jax.ShapeDtypeStruct((batch, 2 * N_HEADS, LANES), F32)]
        blocks.append(_nbytes((bb, 2, hp, HEAD_DIM, HEAD_DIM), F32))
    ne = bb * hp
    scratch_defs = [((nc, ne, HEAD_DIM, HEAD_DIM), BF16), ((nc, ne, 1, HEAD_DIM), F32),
                    ((nc, ne, 1, LANES), F32), ((ne, HEAD_DIM, HEAD_DIM), F32),
                    ((ne, 1, HEAD_DIM), F32), ((ne, 1, LANES), F32)]
    scratch_bytes = sum(_nbytes(s, d) for s, d in scratch_defs) + 16 * nc * ne * HEAD_DIM * 4
    return pl.pallas_call(
        functools.partial(_mlstm_kernel, nc=nc, bb=bb, hp=hp, has_init=has_init,
                          emit_states=emit_states),
        grid=(batch // bb, N_HEADS // hp),
        in_specs=in_specs, out_specs=out_specs, out_shape=out_shape,
        scratch_shapes=[pltpu.VMEM(s, d) for s, d in scratch_defs],
        compiler_params=pltpu.CompilerParams(
            dimension_semantics=("arbitrary", "arbitrary"),
            vmem_limit_bytes=_vmem_limit(blocks, scratch_bytes)),
        name="mlstm_grid" if has_init else "mlstm_seq",
    )(*args)


def _mix_out_kernel(mr_ref, mm_ref, w_ref, x_ref, g1_ref, lg_ref, lb_ref, o_ref):
    for rc in range(o_ref.shape[0] // MIX_ROW_CHUNK):
        rows = slice(rc * MIX_ROW_CHUNK, (rc + 1) * MIX_ROW_CHUNK)
        mix = (jnp.dot(mr_ref[rows, :], w_ref[0:GROUP_DIM, :], preferred_element_type=F32)
               + jnp.dot(mm_ref[rows, :], w_ref[GROUP_DIM:, :], preferred_element_type=F32))
        o_ref[rows, :] = (_ln_rows(ALPHA * x_ref[rows, :] + g1_ref[0] * mix) * lg_ref[...]
                          + lb_ref[...])


def _mix_out_call(mix_r, mix_m, w_out_b, x2d, mod3, mod_row, ln_g, ln_b, *, tm):
    tokens = x2d.shape[0]
    blocks = [_nbytes((tm, GROUP_DIM), BF16)] * 2 + [_nbytes((D_MODEL, D_MODEL), BF16)] \
        + [_nbytes((tm, D_MODEL), F32)] * 2
    return pl.pallas_call(
        _mix_out_kernel,
        grid=(tokens // tm,),
        in_specs=[pl.BlockSpec((tm, GROUP_DIM), lambda m: (m, 0)),
                  pl.BlockSpec((tm, GROUP_DIM), lambda m: (m, 0)),
                  pl.BlockSpec((D_MODEL, D_MODEL), lambda m: (0, 0)),
                  pl.BlockSpec((tm, D_MODEL), lambda m: (m, 0)),
                  pl.BlockSpec((1, 1, D_MODEL), lambda m: (mod_row(m) * 6 + 2, 0, 0)),
                  pl.BlockSpec((1, D_MODEL), lambda m: (0, 0)),
                  pl.BlockSpec((1, D_MODEL), lambda m: (0, 0))],
        out_specs=pl.BlockSpec((tm, D_MODEL), lambda m: (m, 0)),
        out_shape=jax.ShapeDtypeStruct((tokens, D_MODEL), F32),
        compiler_params=pltpu.CompilerParams(
            dimension_semantics=("parallel",), vmem_limit_bytes=_vmem_limit(blocks)),
        name="mix_out",
    )(mix_r, mix_m, w_out_b, x2d, mod3, ln_g, ln_b)


def _ffn_kernel(x_ref, sh_ref, sc_ref, g2_ref, wu_ref, wg_ref, cv_ref, wd_ref, lg_ref, lb_ref,
                o_ref, h_scr, acc_scr, *, period):
    j = pl.program_id(1)

    @pl.when(j == 0)
    def _():
        h = _ln_rows(x_ref[...]) * (1.0 + sc_ref[0]) + sh_ref[0]
        h_scr[...] = h.astype(BF16)
        acc_scr[...] = jnp.zeros_like(acc_scr)

    hb = h_scr[...]
    u = jnp.dot(hb, wu_ref[...], preferred_element_type=F32)
    g = jnp.dot(hb, wg_ref[...], preferred_element_type=F32)
    hid = _silu(_conv3_rows(u, cv_ref[0], period)) * g
    lane = lax.broadcasted_iota(jnp.int32, (1, FF_TN), 1)
    hid = jnp.where((j == FF_NJ - 1) & (lane < FF_OVERLAP), 0.0, hid).astype(BF16)
    acc_scr[...] += jnp.dot(hid, wd_ref[...], preferred_element_type=F32)

    @pl.when(j == FF_NJ - 1)
    def _():
        y = ALPHA * x_ref[...] + g2_ref[0] * acc_scr[...]
        o_ref[...] = _ln_rows(y) * lg_ref[...] + lb_ref[...]


def _ff_offset(j):
    return pl.multiple_of(jnp.minimum(j * FF_TN, D_FF - FF_TN), LANES)


def _ffn_call(x1, mod3, mod_row, w_up_b, conv_tiles, w_down_b, ln_g, ln_b, *, period, tm):
    tokens = x1.shape[0]
    up_tile = (pl.Element(D_MODEL), pl.Element(FF_TN))
    blocks = [_nbytes((tm, D_MODEL), F32)] * 2 + [_nbytes((D_MODEL, FF_TN), BF16)] * 3 \
        + [_nbytes((8, FF_TN), F32)]
    scratch_bytes = _nbytes((tm, D_MODEL), BF16) + _nbytes((tm, D_MODEL), F32)
    return pl.pallas_call(
        functools.partial(_ffn_kernel, period=period),
        grid=(tokens // tm, FF_NJ),
        in_specs=[pl.BlockSpec((tm, D_MODEL), lambda m, j: (m, 0)),
                  pl.BlockSpec((1, 1, D_MODEL), lambda m, j: (mod_row(m) * 6 + 3, 0, 0)),
                  pl.BlockSpec((1, 1, D_MODEL), lambda m, j: (mod_row(m) * 6 + 4, 0, 0)),
                  pl.BlockSpec((1, 1, D_MODEL), lambda m, j: (mod_row(m) * 6 + 5, 0, 0)),
                  pl.BlockSpec(up_tile, lambda m, j: (0, _ff_offset(j))),
                  pl.BlockSpec(up_tile, lambda m, j: (0, pl.multiple_of(D_FF + _ff_offset(j), LANES))),
                  pl.BlockSpec((1, 3, FF_TN), lambda m, j: (j, 0, 0)),
                  pl.BlockSpec((pl.Element(FF_TN), pl.Element(D_MODEL)),
                               lambda m, j: (_ff_offset(j), 0)),
                  pl.BlockSpec((1, D_MODEL), lambda m, j: (0, 0)),
                  pl.BlockSpec((1, D_MODEL), lambda m, j: (0, 0))],
        out_specs=pl.BlockSpec((tm, D_MODEL), lambda m, j: (m, 0)),
        out_shape=jax.ShapeDtypeStruct((tokens, D_MODEL), F32),
        scratch_shapes=[pltpu.VMEM((tm, D_MODEL), BF16), pltpu.VMEM((tm, D_MODEL), F32)],
        compiler_params=pltpu.CompilerParams(
            dimension_semantics=("parallel", "arbitrary"),
            vmem_limit_bytes=_vmem_limit(blocks, scratch_bytes)),
        name="ffn",
    )(x1, mod3, mod3, mod3, w_up_b, w_up_b, conv_tiles, w_down_b, ln_g, ln_b)


def _rope_tables(seq_len):
    quarter = HEAD_DIM // 4
    t = jnp.arange(seq_len)
    row = (t // GRID_W).astype(F32)
    col = (t % GRID_W).astype(F32)
    inv = ROPE_BASE ** (-jnp.arange(quarter, dtype=F32) / quarter)
    ang_r, ang_c = row[:, None] * inv, col[:, None] * inv
    cos_t = jnp.concatenate([jnp.cos(ang_r)] * 2 + [jnp.cos(ang_c)] * 2, axis=1)
    sin_t = jnp.concatenate([-jnp.sin(ang_r), jnp.sin(ang_r), -jnp.sin(ang_c), jnp.sin(ang_c)], axis=1)
    return cos_t, sin_t


def kernel(x_prompt, x_sample, state_ret, state_mlstm_C, state_mlstm_n, state_mlstm_m, c, c_ctx,
           w_mod, b_mod, w_in, b_gate, conv_qk, ret_theta, gn_ret, gn_mlstm, w_out,
           ln1_g, ln1_b, w_up, conv_ff, w_down, ln2_g, ln2_b):
    bp, lp, _ = x_prompt.shape
    bs, ls, _ = x_sample.shape
    layer = 0
    tm_mix, tm = 1024, 512

    cvec = jnp.concatenate([c_ctx[None], c, jnp.zeros((8 - 1 - bs, D_MODEL), F32)], axis=0)
    mod = _mod_call(cvec, w_mod[layer], b_mod[layer][None])
    mod3 = mod.reshape(8 * 6, 1, D_MODEL)

    def row_prompt(tile):
        return lambda m: 0

    def row_sample(tile):
        return lambda m: 1 + (m * tile) // ls

    w_l = w_in[layer].T
    w_gate = w_l[N_PROJ:]
    b_gate_col = b_gate[layer].reshape(N_GATES, 1)
    ff_offsets = [min(j * FF_TN, D_FF - FF_TN) for j in range(FF_NJ)]
    conv_tiles = jnp.stack([conv_ff[layer][:, o:o + FF_TN] for o in ff_offsets])
    gn_r = gn_ret[layer].reshape(1, GROUP_DIM)
    gn_m = gn_mlstm[layer].reshape(1, GROUP_DIM)
    theta = ret_theta[layer]
    ln1 = (ln1_g[layer][None], ln1_b[layer][None])
    ln2 = (ln2_g[layer][None], ln2_b[layer][None])

    def mixer(x, mod_row, rope, period, bb, hp, ret_init, mlstm_init, emit_states, stage=()):
        batch, seq_len, _ = x.shape
        x2d = x.reshape(batch * seq_len, D_MODEL)
        h2d, gates = _ln_mod_call(x2d, mod3, mod_row(tm_mix), w_gate, b_gate_col, tm=tm_mix)
        proj, kt, *staged = _mix_in_call(h2d, w_l, conv_qk[layer], rope, seq_len=seq_len,
                                         period=period, tm=tm_mix, stage=stage)
        ret = _ret_call(proj, kt, theta, gn_r, ret_init, batch=batch, seq_len=seq_len, bb=bb,
                        hp=hp, emit_states=emit_states)
        mls = _mlstm_call(proj, kt, gates, gn_m, mlstm_init, batch=batch, seq_len=seq_len, bb=bb,
                          hp=hp, emit_states=emit_states)
        return x2d, ret, mls, staged

    def channel_mix(x2d, shape, ret, mls, mod_row, period, weights):
        w_up_b, w_down_b, w_out_b = weights
        x1 = _mix_out_call(ret[0], mls[0], w_out_b, x2d, mod3, mod_row(tm), *ln1, tm=tm)
        y = _ffn_call(x1, mod3, mod_row(tm), w_up_b, conv_tiles, w_down_b, *ln2, period=period,
                      tm=tm)
        return y.reshape(shape)

    xp, ret_p, mls_p, weights_b = mixer(x_prompt, row_prompt, None, lp, 2, N_HEADS, None, None,
                                        True, stage=(w_up[layer], w_down[layer], w_out[layer]))
    m0 = state_mlstm_m.reshape(-1)
    xs, ret_s, mls_s, _ = mixer(x_sample, row_sample, _rope_tables(ls), GRID_W, 1, 2, state_ret,
                                (state_mlstm_C, state_mlstm_n, m0), False)
    y_prompt = channel_mix(xp, x_prompt.shape, ret_p, mls_p, row_prompt, lp, weights_b)
    y_sample = channel_mix(xs, x_sample.shape, ret_s, mls_s, row_sample, GRID_W, weights_b)
    ret_states, mls_states = ret_p[1:], mls_p[1:]

    new_ret = ret_states[0]
    new_c, new_n, new_m = mls_states
    new_m = new_m[:, :, 0].reshape(bp, 1, 2, N_HEADS)
    return (y_prompt, y_sample, new_ret, new_c, new_n, new_m)
```

```python
import functools

import jax
import jax.numpy as jnp
from jax import lax
from jax.experimental import pallas as pl
from jax.experimental.pallas import tpu as pltpu

F32 = jnp.float32
BF16 = jnp.bfloat16

D_MODEL = 2048
GRID_W = 64
HEAD_DIM = 256
N_HEADS = 4
GROUP_DIM = N_HEADS * HEAD_DIM
N_PROJ = 8 * GROUP_DIM
N_GATES = 4 * N_HEADS
D_FF = 5504
CHUNK = 128
ROPE_BASE = 10000.0
ALPHA = 2.0 ** 0.25
LN_EPS = 1e-6
K_SCALE = HEAD_DIM ** -0.5

LANES = 128
BF16_ROWS = 16
V7X_VMEM_LIMIT_CAP = 60 * 1024 * 1024
VMEM_TEMP_ALLOWANCE = 16 * 1024 * 1024

MIX_ROW_CHUNK = 512
N_STORED = 6 * GROUP_DIM
FF_TN = 512
FF_NJ = -(-D_FF // FF_TN)
FF_OVERLAP = FF_NJ * FF_TN - D_FF


def _vmem_limit(block_bytes, scratch_bytes=0):
    est = 2 * sum(block_bytes) + scratch_bytes + VMEM_TEMP_ALLOWANCE
    return int(min(V7X_VMEM_LIMIT_CAP, est))


def _nbytes(shape, dtype):
    n = 1
    for s in shape:
        n *= s
    return n * jnp.dtype(dtype).itemsize


def _ln_rows(x):
    mu = jnp.mean(x, axis=-1, keepdims=True)
    xc = x - mu
    var = jnp.mean(xc * xc, axis=-1, keepdims=True)
    return xc * lax.rsqrt(var + LN_EPS)


def _silu(x):
    return x * jax.nn.sigmoid(x)


def _log_sigmoid(x):
    return jnp.minimum(x, 0.0) - jnp.log1p(jnp.exp(-jnp.abs(x)))


def _conv3_rows(u, w, period):
    rows = u.shape[0]
    t = lax.broadcasted_iota(jnp.int32, (rows, 1), 0) & (period - 1)
    prev = jnp.where(t == 0, 0.0, pltpu.roll(u, 1, 0))
    nxt = jnp.where(t == period - 1, 0.0, pltpu.roll(u, rows - 1, 0))
    return prev * w[0:1] + u * w[1:2] + nxt * w[2:3]


def _cumsum_lanes(x, lane, reverse):
    s = 1
    while s < LANES:
        if reverse:
            x = x + jnp.where(lane < LANES - s, pltpu.roll(x, LANES - s, 1), 0.0)
        else:
            x = x + jnp.where(lane >= s, pltpu.roll(x, s, 1), 0.0)
        s *= 2
    return x


def _mod_kernel(c_ref, w_ref, b_ref, o_ref):
    a = _silu(c_ref[...]).astype(BF16)
    o_ref[...] = jnp.dot(a, w_ref[...].astype(BF16), preferred_element_type=F32) + b_ref[...]


def _mod_call(cvec, w_mod, b_mod):
    rows, n = cvec.shape[0], w_mod.shape[1]
    tn = 1024
    blocks = [_nbytes((rows, D_MODEL), F32), _nbytes((D_MODEL, tn), F32),
              _nbytes((8, tn), F32), _nbytes((rows, tn), F32)]
    return pl.pallas_call(
        _mod_kernel,
        grid=(n // tn,),
        in_specs=[pl.BlockSpec((rows, D_MODEL), lambda j: (0, 0)),
                  pl.BlockSpec((D_MODEL, tn), lambda j: (0, j)),
                  pl.BlockSpec((1, tn), lambda j: (0, j))],
        out_specs=pl.BlockSpec((rows, tn), lambda j: (0, j)),
        out_shape=jax.ShapeDtypeStruct((rows, n), F32),
        compiler_params=pltpu.CompilerParams(
            dimension_semantics=("arbitrary",), vmem_limit_bytes=_vmem_limit(blocks)),
        name="mod",
    )(cvec, w_mod, b_mod)


def _ln_mod_kernel(x_ref, sh_ref, sc_ref, wg_ref, bg_ref, h_ref, gates_ref, *, tm):
    h = _ln_rows(x_ref[...]) * (1.0 + sc_ref[0]) + sh_ref[0]
    hb = h.astype(BF16)
    h_ref[...] = hb
    g = lax.dot_general(wg_ref[...].astype(BF16), hb, (((1,), (1,)), ((), ())),
                        preferred_element_type=F32) + bg_ref[...]
    row = lax.broadcasted_iota(jnp.int32, (N_GATES, LANES), 0)
    lane = lax.broadcasted_iota(jnp.int32, (N_GATES, LANES), 1)
    kind = lax.shift_right_logical(row, 2)
    for s in range(tm // LANES):
        gs = g[:, s * LANES:(s + 1) * LANES]
        ls = _log_sigmoid(gs)
        gates_ref[s] = jnp.where(kind == 1, _cumsum_lanes(ls, lane, False),
                                 jnp.where(kind == 3, _cumsum_lanes(ls, lane, True), gs))


def _ln_mod_call(x2d, mod3, mod_row, w_gate, b_gate_col, *, tm):
    tokens = x2d.shape[0]
    slabs = tm // LANES
    blocks = [_nbytes((tm, D_MODEL), F32), _nbytes((tm, D_MODEL), BF16),
              _nbytes((N_GATES, D_MODEL), F32), _nbytes((slabs, N_GATES, LANES), F32)]
    return pl.pallas_call(
        functools.partial(_ln_mod_kernel, tm=tm),
        grid=(tokens // tm,),
        in_specs=[pl.BlockSpec((tm, D_MODEL), lambda m: (m, 0)),
                  pl.BlockSpec((1, 1, D_MODEL), lambda m: (mod_row(m) * 6 + 0, 0, 0)),
                  pl.BlockSpec((1, 1, D_MODEL), lambda m: (mod_row(m) * 6 + 1, 0, 0)),
                  pl.BlockSpec((N_GATES, D_MODEL), lambda m: (0, 0)),
                  pl.BlockSpec((N_GATES, 1), lambda m: (0, 0))],
        out_specs=[pl.BlockSpec((tm, D_MODEL), lambda m: (m, 0)),
                   pl.BlockSpec((slabs, N_GATES, LANES), lambda m: (m, 0, 0))],
        out_shape=[jax.ShapeDtypeStruct((tokens, D_MODEL), BF16),
                   jax.ShapeDtypeStruct((tokens // LANES, N_GATES, LANES), F32)],
        compiler_params=pltpu.CompilerParams(
            dimension_semantics=("parallel",), vmem_limit_bytes=_vmem_limit(blocks)),
        name="ln_mod",
    )(x2d, mod3, mod3, w_gate, b_gate_col)


def _mix_in_kernel(*refs, tm, period, grid_mode, n_staged):
    refs = list(refs)
    h_ref, w_ref, cv_ref = refs[:3]
    pos = 3
    if grid_mode:
        cos_ref, sin_ref = refs[pos:pos + 2]
        pos += 2
    staged_in = refs[pos:pos + n_staged]
    pos += n_staged
    proj_ref, kt_ref = refs[pos:pos + 2]
    pos += 2
    staged_out = refs[pos:pos + n_staged]
    wb_scr = refs[pos + n_staged]
    group = pl.program_id(0)

    @pl.when(pl.program_id(1) == 0)
    def _():
        wb_scr[...] = w_ref[...].astype(BF16)

    def chunks():
        for src, dst in zip(staged_in, staged_out):
            dst[...] = src[...].astype(BF16)
        for rc in range(tm // MIX_ROW_CHUNK):
            rows = slice(rc * MIX_ROW_CHUNK, (rc + 1) * MIX_ROW_CHUNK)
            yield rc, rows, lax.dot_general(h_ref[rows, :], wb_scr[...], (((1,), (1,)), ((), ())),
                                            preferred_element_type=F32)

    def rope(a, rows):
        out = []
        for s in range(GROUP_DIM // LANES):
            xs = a[:, s * LANES:(s + 1) * LANES]
            t = (s % 2) * LANES
            out.append(xs * cos_ref[rows, t:t + LANES]
                       + pltpu.roll(xs, LANES // 2, 1) * sin_ref[rows, t:t + LANES])
        return jnp.concatenate(out, axis=1)

    def store_transposed(rc, y):
        yt = y.T.astype(BF16)
        per = MIX_ROW_CHUNK // LANES
        for s in range(per):
            kt_ref[rc * per + s] = yt[:, s * LANES:(s + 1) * LANES]

    plain = (group == 2) | (group == 3) | (group == 6) | (group == 7)
    if not grid_mode:
        plain = plain | (group == 0)

    @pl.when(plain)
    def _():
        for _, rows, acc in chunks():
            proj_ref[rows, :] = acc.astype(BF16)

    if grid_mode:
        @pl.when(group == 0)
        def _():
            for _, rows, acc in chunks():
                proj_ref[rows, :] = rope(acc, rows).astype(BF16)

    @pl.when(group == 1)
    def _():
        for rc, rows, acc in chunks():
            k = acc * K_SCALE
            store_transposed(rc, rope(k, rows) if grid_mode else k)

    @pl.when(group == 4)
    def _():
        for _, rows, acc in chunks():
            proj_ref[rows, :] = _silu(_conv3_rows(acc, cv_ref[...], period)).astype(BF16)

    @pl.when(group == 5)
    def _():
        for rc, _, acc in chunks():
            store_transposed(rc, _silu(_conv3_rows(acc, cv_ref[...], period)) * K_SCALE)


def _staging_specs(weights, n_groups, n_tiles):
    specs, shapes, block_bytes = [], [], []
    for w in weights:
        rows, cols = w.shape
        slab = BF16_ROWS
        while rows % slab or rows // slab > n_groups * n_tiles:
            slab += BF16_ROWS
        n_slabs = rows // slab
        specs.append(pl.BlockSpec(
            (slab, cols),
            lambda g, m, n_slabs=n_slabs: (jnp.minimum(g * n_tiles + m, n_slabs - 1), 0)))
        shapes.append(jax.ShapeDtypeStruct((rows, cols), BF16))
        block_bytes.append(_nbytes((slab, cols), F32) + _nbytes((slab, cols), BF16))
    return specs, shapes, block_bytes


def _mix_in_call(h2d, w_in_t, conv_qk, rope, *, seq_len, period, tm, stage=()):
    tokens = h2d.shape[0]
    grid_mode = rope is not None
    tiles_per_seq = seq_len // tm
    slabs = tm // LANES
    last = tokens // tm - 1
    n_groups = N_PROJ // GROUP_DIM
    stage_specs, stage_shapes, stage_bytes = _staging_specs(stage, n_groups, tokens // tm)

    def is_key(g):
        return (g == 1) | (g == 5)

    def proj_index(g, m):
        col = g - (g >= 1).astype(jnp.int32) - (g >= 5).astype(jnp.int32)
        return jnp.where(is_key(g), last, m), col

    def kt_index(g, m):
        row = jnp.where(g == 0, 0, jnp.where(is_key(g), m, last))
        return row, (g >= 5).astype(jnp.int32), 0

    in_specs = [
        pl.BlockSpec((tm, D_MODEL), lambda g, m: (m, 0)),
        pl.BlockSpec((GROUP_DIM, D_MODEL), lambda g, m: (g, 0)),
        pl.BlockSpec((3, GROUP_DIM), lambda g, m: (0, jnp.clip(g - 4, 0, 1))),
    ]
    args = [h2d, w_in_t, conv_qk]
    blocks = [_nbytes((tm, D_MODEL), BF16), _nbytes((GROUP_DIM, D_MODEL), F32),
              _nbytes((tm, GROUP_DIM), BF16) * 2, _nbytes((8, GROUP_DIM), F32)]
    if grid_mode:
        in_specs += [pl.BlockSpec((tm, HEAD_DIM),
                                  lambda g, m: (jnp.where(g <= 1, m % tiles_per_seq, 0), 0))] * 2
        args += list(rope)
        blocks += [_nbytes((tm, HEAD_DIM), F32)] * 2
    in_specs += stage_specs
    args += list(stage)
    blocks += stage_bytes
    scratch_defs = [((GROUP_DIM, D_MODEL), BF16)]

    return pl.pallas_call(
        functools.partial(_mix_in_kernel, tm=tm, period=period, grid_mode=grid_mode,
                          n_staged=len(stage)),
        grid=(n_groups, tokens // tm),
        in_specs=in_specs,
        out_specs=[pl.BlockSpec((tm, GROUP_DIM), lambda g, m: proj_index(g, m)),
                   pl.BlockSpec((slabs, GROUP_DIM, LANES), lambda g, m: kt_index(g, m))]
        + stage_specs,
        out_shape=[jax.ShapeDtypeStruct((tokens, N_STORED), BF16),
                   jax.ShapeDtypeStruct((tokens // LANES, 2 * GROUP_DIM, LANES), BF16)]
        + stage_shapes,
        scratch_shapes=[pltpu.VMEM(s, d) for s, d in scratch_defs],
        compiler_params=pltpu.CompilerParams(
            dimension_semantics=("arbitrary", "arbitrary"),
            vmem_limit_bytes=_vmem_limit(blocks, sum(_nbytes(s, d) for s, d in scratch_defs))),
        name="mix_in_grid" if grid_mode else "mix_in_seq",
    )(*args)


_NN = (((2,), (1,)), ((0,), (0,)))
_NT = (((2,), (2,)), ((0,), (0,)))


def _bdot(x, y, dims):
    return lax.dot_general(x, y, dims, preferred_element_type=F32)


def _chunk_loop(nc, body, reverse=False, update_last=True):
    def chunk(i):
        return nc - 1 - i if reverse else i

    if nc <= 2:
        for i in range(nc - 1):
            body(chunk(i), True)
    else:
        def step(i, carry):
            body(chunk(i), True)
            return carry
        lax.fori_loop(0, nc - 1, step, 0)
    body(chunk(nc - 1), update_last)


def _rows(c, base=0):
    start = base + c * CHUNK
    if not isinstance(start, int):
        start = pl.multiple_of(start, CHUNK)
    return pl.ds(start, CHUNK)


def _is_chunk(c, value):
    return isinstance(c, int) and c == value


def _heads(ref, c, nc, bb, hp):
    return jnp.stack([ref[_rows(c, bi * nc * CHUNK), hh * HEAD_DIM:(hh + 1) * HEAD_DIM]
                      for bi in range(bb) for hh in range(hp)])


def _keys_t(kt_ref, c, nc, bb, hp):
    parts = [kt_ref[bi * nc + c].reshape(hp, HEAD_DIM, CHUNK) for bi in range(bb)]
    return parts[0] if bb == 1 else jnp.concatenate(parts, axis=0)


def _per_seq(x, bb):
    return x if bb == 1 else jnp.concatenate([x] * bb, axis=0)


def _group_block(seq_len, hp, group, bb):
    per = N_HEADS // hp
    return pl.BlockSpec((bb * seq_len, hp * HEAD_DIM), lambda b, hg: (b, group * per + hg))


def _kt_block(nc, hp, group, bb):
    per = N_HEADS // hp
    return pl.BlockSpec((bb * nc, hp * HEAD_DIM, LANES), lambda b, hg: (b, group * per + hg, 0))


def _ret_kernel(*refs, nc, bb, hp, has_init, emit_states):
    refs = list(refs)
    theta_ref, q_ref, kt_ref, v_ref, rg_ref, gn_ref = refs[:6]
    pos = 6
    s0_ref = None
    if has_init:
        s0_ref = refs[pos]
        pos += 1
    o_ref = refs[pos]
    pos += 1
    st_ref = None
    if emit_states:
        st_ref = refs[pos]
        pos += 1
    hist, s_scr, decay_scr, qdec_scr, kdec_scr, cdec_scr = refs[pos:pos + 6]
    hg = pl.program_id(1)
    hsel = pl.ds(hg * hp, hp)

    @pl.when(pl.program_id(0) == 0)
    def _():
        ii = lax.broadcasted_iota(jnp.int32, (CHUNK, CHUNK), 0)
        jj = lax.broadcasted_iota(jnp.int32, (CHUNK, CHUNK), 1)
        d = (ii - jj).astype(F32)
        p_col = lax.broadcasted_iota(jnp.int32, (CHUNK, HEAD_DIM), 0).astype(F32)
        p_row = lax.broadcasted_iota(jnp.int32, (1, LANES), 1).astype(F32)
        for hh in range(hp):
            h = hg * hp + hh
            lg_f = _log_sigmoid(jnp.full((1, LANES), theta_ref[0, h], F32))
            lg_b = _log_sigmoid(jnp.full((1, LANES), theta_ref[1, h], F32))
            decay_scr[h] = (jnp.where(d >= 0, jnp.exp(lg_f * jnp.maximum(d, 0.0)), 0.0)
                            + jnp.where(d <= 0, jnp.exp(lg_b * jnp.maximum(-d, 0.0)), 0.0))
            qdec_scr[0, h] = jnp.exp(lg_f[:, :1] * (p_col + 1.0))
            qdec_scr[1, h] = jnp.exp(lg_b[:, :1] * (CHUNK - p_col))
            kdec_scr[0, h] = jnp.exp(lg_f * (CHUNK - 1.0 - p_row))
            kdec_scr[1, h] = jnp.exp(lg_b * p_row)
            cdec_scr[0, h] = jnp.exp(lg_f * float(CHUNK))
            cdec_scr[1, h] = jnp.exp(lg_b * float(CHUNK))

    def table(ref, *lead):
        return _per_seq(ref[(*lead, hsel)], bb)

    def kv_update(c, v, direction):
        kd = (_keys_t(kt_ref, c, nc, bb, hp).astype(F32)
              * table(kdec_scr, direction)).astype(BF16)
        s_scr[...] = s_scr[...] * table(cdec_scr, direction)[:, :, :1] + _bdot(kd, v, _NN)

    def init_state(direction):
        s_scr[...] = (s0_ref[0, 0, direction] if has_init
                      else jnp.zeros((bb * hp, HEAD_DIM, HEAD_DIM), F32))

    def emit_state(direction):
        for bi in range(bb):
            st_ref[bi, 0, direction] = s_scr[bi * hp:(bi + 1) * hp]

    init_state(0)

    def fwd(c, update):
        hist[c] = s_scr[...].astype(BF16)
        if update:
            kv_update(c, _heads(v_ref, c, nc, bb, hp), 0)

    _chunk_loop(nc, fwd, update_last=emit_states)
    if emit_states:
        emit_state(0)

    init_state(1)

    def bwd(c, update):
        q, v = _heads(q_ref, c, nc, bb, hp), _heads(v_ref, c, nc, bb, hp)
        kt = _keys_t(kt_ref, c, nc, bb, hp)
        att = (_bdot(q, kt, _NN) * table(decay_scr)).astype(BF16)
        o = _bdot(att, v, _NN)
        if has_init or not _is_chunk(c, 0):
            o = o + _bdot(q, hist[c], _NN) * table(qdec_scr, 0)
        if has_init or not _is_chunk(c, nc - 1):
            o = o + _bdot(q, s_scr[...].astype(BF16), _NN) * table(qdec_scr, 1)
        y = _ln_rows(o)
        for bi in range(bb):
            r = _rows(c, bi * nc * CHUNK)
            for hh in range(hp):
                cs = slice(hh * HEAD_DIM, (hh + 1) * HEAD_DIM)
                o_ref[r, cs] = (y[bi * hp + hh] * gn_ref[:, cs]
                                * _silu(rg_ref[r, cs].astype(F32))).astype(BF16)
        if update:
            kv_update(c, v, 1)

    _chunk_loop(nc, bwd, reverse=True, update_last=emit_states)
    if emit_states:
        emit_state(1)


def _ret_call(proj, kt, theta, gn, s0, *, batch, seq_len, bb, hp, emit_states):
    nc = seq_len // CHUNK
    has_init = s0 is not None
    assert N_HEADS % hp == 0 and batch % bb == 0 and (bb == 1 or not has_init)
    state_spec = pl.BlockSpec((bb, 1, 2, hp, HEAD_DIM, HEAD_DIM), lambda b, hg: (b, 0, 0, hg, 0, 0))
    in_specs = [pl.BlockSpec(memory_space=pltpu.SMEM),
                _group_block(seq_len, hp, 0, bb), _kt_block(nc, hp, 0, bb),
                _group_block(seq_len, hp, 1, bb), _group_block(seq_len, hp, 2, bb),
                pl.BlockSpec((1, hp * HEAD_DIM), lambda b, hg: (0, hg))]
    args = [theta, proj, kt, proj, proj, gn]
    blocks = [_nbytes((bb * seq_len, hp * HEAD_DIM), BF16)] * 5
    if has_init:
        in_specs.append(state_spec)
        args.append(s0)
        blocks.append(_nbytes((2, hp, HEAD_DIM, HEAD_DIM), F32))
    out_specs = [pl.BlockSpec((bb * seq_len, hp * HEAD_DIM), lambda b, hg: (b, hg))]
    out_shape = [jax.ShapeDtypeStruct((batch * seq_len, GROUP_DIM), BF16)]
    if emit_states:
        out_specs.append(state_spec)
        out_shape.append(jax.ShapeDtypeStruct((batch, 1, 2, N_HEADS, HEAD_DIM, HEAD_DIM), F32))
        blocks.append(_nbytes((bb, 2, hp, HEAD_DIM, HEAD_DIM), F32))
    scratch_defs = [((nc, bb * hp, HEAD_DIM, HEAD_DIM), BF16), ((bb * hp, HEAD_DIM, HEAD_DIM), F32),
                    ((N_HEADS, CHUNK, CHUNK), F32), ((2, N_HEADS, CHUNK, HEAD_DIM), F32),
                    ((2, N_HEADS, 1, LANES), F32), ((2, N_HEADS, 1, LANES), F32)]
    return pl.pallas_call(
        functools.partial(_ret_kernel, nc=nc, bb=bb, hp=hp, has_init=has_init,
                          emit_states=emit_states),
        grid=(batch // bb, N_HEADS // hp),
        in_specs=in_specs, out_specs=out_specs, out_shape=out_shape,
        scratch_shapes=[pltpu.VMEM(s, d) for s, d in scratch_defs],
        compiler_params=pltpu.CompilerParams(
            dimension_semantics=("arbitrary", "arbitrary"),
            vmem_limit_bytes=_vmem_limit(blocks, sum(_nbytes(s, d) for s, d in scratch_defs))),
        name="ret_grid" if has_init else "ret_seq",
    )(*args)


def _mlstm_kernel(*refs, nc, bb, hp, has_init, emit_states):
    refs = list(refs)
    q_ref, kt_ref, v_ref, mo_ref, g_ref, gn_ref = refs[:6]
    pos = 6
    c0_ref = n0_ref = m0_ref = None
    if has_init:
        c0_ref, n0_ref, m0_ref = refs[pos:pos + 3]
        pos += 3
    o_ref = refs[pos]
    pos += 1
    c_out = n_out = m_out = None
    if emit_states:
        c_out, n_out, m_out = refs[pos:pos + 3]
        pos += 3
    c_hist, n_hist, m_hist, c_scr, n_scr, m_scr = refs[pos:pos + 6]

    b = pl.program_id(0)
    hg = pl.program_id(1)
    ii = lax.broadcasted_iota(jnp.int32, (CHUNK, CHUNK), 0)
    jj = lax.broadcasted_iota(jnp.int32, (CHUNK, CHUNK), 1)
    eye = ii == jj
    lower = ii >= jj
    upper = ii <= jj

    ne = bb * hp

    def col(row):
        return jnp.sum(jnp.where(eye, row, 0.0), axis=2, keepdims=True)

    def gate_rows(c, kind):
        base = kind * N_HEADS + hg * hp
        return jnp.stack([g_ref[bi * nc + c, pl.ds(base + hh, 1), :]
                          for bi in range(bb) for hh in range(hp)])

    def direction_out(a, q, qf, v, c_b, n_row, m, b_row, i_row, mask):
        b_col = col(b_row)
        dm = jnp.where(mask, b_col - b_row + i_row, -jnp.inf)
        inter = b_col + m
        mt = jnp.maximum(inter, jnp.max(dm, axis=2, keepdims=True))
        w = jnp.exp(dm - mt)
        sp = jnp.exp(inter - mt)
        s = a * w
        num = _bdot(s.astype(BF16), v, _NN)
        den = jnp.sum(s, axis=2, keepdims=True)
        if c_b is not None:
            num = num + _bdot(q, c_b, _NN) * sp
            den = den + jnp.sum(qf * n_row, axis=2, keepdims=True) * sp
        return num / jnp.maximum(jnp.abs(den), jnp.exp(-mt))

    def state_update(c, v, b_row, i_row, last):
        m = m_scr[...][:, :, :1]
        b_last = b_row[:, :, last:last + 1]
        g = b_last - b_row + i_row
        m_new = jnp.maximum(b_last + m, jnp.max(g, axis=2, keepdims=True))
        wk = jnp.exp(g - m_new)
        sc = jnp.exp(b_last + m - m_new)
        kt = _keys_t(kt_ref, c, nc, bb, hp)
        kw = (kt.astype(F32) * wk).astype(BF16)
        c_scr[...] = c_scr[...] * sc + _bdot(kw, v, _NN)
        wk8 = jnp.broadcast_to(wk, (ne, 8, CHUNK)).astype(BF16)
        n_scr[...] = n_scr[...] * sc + _bdot(wk8, kt, _NT)[:, :1, :]
        m_scr[...] = jnp.broadcast_to(m_new, (ne, 1, LANES))

    def init_state(direction):
        if has_init:
            c_scr[...] = c0_ref[0, 0, direction]
            n_scr[...] = jnp.stack([n0_ref[0, 0, direction, pl.ds(hg * hp + hh, 1), :]
                                    for hh in range(hp)])
            m_scr[...] = jnp.stack([
                jnp.full((1, LANES), m0_ref[(b * 2 + direction) * N_HEADS + hg * hp + hh], F32)
                for hh in range(hp)])
        else:
            c_scr[...] = jnp.zeros((ne, HEAD_DIM, HEAD_DIM), F32)
            n_scr[...] = jnp.zeros((ne, 1, HEAD_DIM), F32)
            m_scr[...] = jnp.zeros((ne, 1, LANES), F32)

    def emit_state(direction):
        for bi in range(bb):
            c_out[bi, 0, direction] = c_scr[bi * hp:(bi + 1) * hp]
            for hh in range(hp):
                n_out[bi, 0, direction, pl.ds(hh, 1), :] = n_scr[bi * hp + hh]
                m_out[bi, pl.ds(direction * N_HEADS + hh, 1), :] = m_scr[bi * hp + hh]

    init_state(0)

    def fwd(c, update):
        c_hist[c] = c_scr[...].astype(BF16)
        n_hist[c] = n_scr[...]
        m_hist[c] = m_scr[...]
        if update:
            state_update(c, _heads(v_ref, c, nc, bb, hp), gate_rows(c, 1), gate_rows(c, 0),
                         CHUNK - 1)

    _chunk_loop(nc, fwd, update_last=emit_states)
    if emit_states:
        emit_state(0)

    init_state(1)

    def bwd(c, update):
        i_f, b_f, i_b, b_b = (gate_rows(c, kind) for kind in (0, 1, 2, 3))
        q, v = _heads(q_ref, c, nc, bb, hp), _heads(v_ref, c, nc, bb, hp)
        qf = q.astype(F32)
        a = _bdot(q, _keys_t(kt_ref, c, nc, bb, hp), _NN)
        zero_f = not has_init and _is_chunk(c, 0)
        zero_b = not has_init and _is_chunk(c, nc - 1)
        h_f = direction_out(a, q, qf, v, None if zero_f else c_hist[c], n_hist[c],
                            m_hist[c][:, :, :1], b_f, i_f, lower)
        h_b = direction_out(a, q, qf, v, None if zero_b else c_scr[...].astype(BF16),
                            n_scr[...], m_scr[...][:, :, :1], b_b, i_b, upper)
        y = _ln_rows(h_f + h_b)
        for bi in range(bb):
            r = _rows(c, bi * nc * CHUNK)
            for hh in range(hp):
                cs = slice(hh * HEAD_DIM, (hh + 1) * HEAD_DIM)
                o_ref[r, cs] = (jax.nn.sigmoid(mo_ref[r, cs].astype(F32))
                                * (y[bi * hp + hh] * gn_ref[:, cs])).astype(BF16)
        if update:
            state_update(c, v, b_b, i_b, 0)

    _chunk_loop(nc, bwd, reverse=True, update_last=emit_states)
    if emit_states:
        emit_state(1)


def _mlstm_call(proj, kt, gates, gn, init, *, batch, seq_len, bb, hp, emit_states):
    nc = seq_len // CHUNK
    has_init = init is not None
    assert N_HEADS % hp == 0 and (hp == N_HEADS or not emit_states)
    assert batch % bb == 0 and (bb == 1 or not has_init)
    c_spec = pl.BlockSpec((bb, 1, 2, hp, HEAD_DIM, HEAD_DIM), lambda b, hg: (b, 0, 0, hg, 0, 0))
    n_spec = pl.BlockSpec((bb, 1, 2, N_HEADS, HEAD_DIM), lambda b, hg: (b, 0, 0, 0, 0))
    in_specs = [_group_block(seq_len, hp, 3, bb), _kt_block(nc, hp, 1, bb),
                _group_block(seq_len, hp, 4, bb), _group_block(seq_len, hp, 5, bb),
                pl.BlockSpec((bb * nc, N_GATES, LANES), lambda b, hg: (b, 0, 0)),
                pl.BlockSpec((1, hp * HEAD_DIM), lambda b, hg: (0, hg))]
    args = [proj, kt, proj, proj, gates, gn]
    blocks = ([_nbytes((bb * seq_len, hp * HEAD_DIM), BF16)] * 5
              + [_nbytes((bb * nc, N_GATES, LANES), F32)])
    if has_init:
        in_specs += [c_spec, n_spec, pl.BlockSpec(memory_space=pltpu.SMEM)]
        args += list(init)
        blocks.append(_nbytes((2, hp, HEAD_DIM, HEAD_DIM), F32))
    out_specs = [pl.BlockSpec((bb * seq_len, hp * HEAD_DIM), lambda b, hg: (b, hg))]
    out_shape = [jax.ShapeDtypeStruct((batch * seq_len, GROUP_DIM), BF16)]
    if emit_states:
        out_specs += [c_spec, n_spec, pl.BlockSpec((bb, 2 * N_HEADS, LANES), lambda b, hg: (b, 0, 0))]
        out_shape += [jax.ShapeDtypeStruct((batch, 1, 2, N_HEADS, HEAD_DIM, HEAD_DIM), F32),
                      jax.ShapeDtypeStruct((batch, 1, 2, N_HEADS, HEAD_DIM), F32),
                      jax.ShapeDtypeStruct((batch, 2 * N_HEADS, LANES), F32)]
        blocks.append(_nbytes((bb, 2, hp, HEAD_DIM, HEAD_DIM), F32))
    ne = bb * hp
    scratch_defs = [((nc, ne, HEAD_DIM, HEAD_DIM), BF16), ((nc, ne, 1, HEAD_DIM), F32),
                    ((nc, ne, 1, LANES), F32), ((ne, HEAD_DIM, HEAD_DIM), F32),
                    ((ne, 1, HEAD_DIM), F32), ((ne, 1, LANES), F32)]
    scratch_bytes = sum(_nbytes(s, d) for s, d in scratch_defs) + 16 * nc * ne * HEAD_DIM * 4
    return pl.pallas_call(
        functools.partial(_mlstm_kernel, nc=nc, bb=bb, hp=hp, has_init=has_init,
                          emit_states=emit_states),
        grid=(batch // bb, N_HEADS // hp),
        in_specs=in_specs, out_specs=out_specs, out_shape=out_shape,
        scratch_shapes=[pltpu.VMEM(s, d) for s, d in scratch_defs],
        compiler_params=pltpu.CompilerParams(
            dimension_semantics=("arbitrary", "arbitrary"),
            vmem_limit_bytes=_vmem_limit(blocks, scratch_bytes)),
        name="mlstm_grid" if has_init else "mlstm_seq",
    )(*args)


def _mix_out_kernel(mr_ref, mm_ref, w_ref, x_ref, g1_ref, lg_ref, lb_ref, o_ref):
    for rc in range(o_ref.shape[0] // MIX_ROW_CHUNK):
        rows = slice(rc * MIX_ROW_CHUNK, (rc + 1) * MIX_ROW_CHUNK)
        mix = (jnp.dot(mr_ref[rows, :], w_ref[0:GROUP_DIM, :], preferred_element_type=F32)
               + jnp.dot(mm_ref[rows, :], w_ref[GROUP_DIM:, :], preferred_element_type=F32))
        o_ref[rows, :] = (_ln_rows(ALPHA * x_ref[rows, :] + g1_ref[0] * mix) * lg_ref[...]
                          + lb_ref[...])


def _mix_out_call(mix_r, mix_m, w_out_b, x2d, mod3, mod_row, ln_g, ln_b, *, tm):
    tokens = x2d.shape[0]
    blocks = [_nbytes((tm, GROUP_DIM), BF16)] * 2 + [_nbytes((D_MODEL, D_MODEL), BF16)] \
        + [_nbytes((tm, D_MODEL), F32)] * 2
    return pl.pallas_call(
        _mix_out_kernel,
        grid=(tokens // tm,),
        in_specs=[pl.BlockSpec((tm, GROUP_DIM), lambda m: (m, 0)),
                  pl.BlockSpec((tm, GROUP_DIM), lambda m: (m, 0)),
                  pl.BlockSpec((D_MODEL, D_MODEL), lambda m: (0, 0)),
                  pl.BlockSpec((tm, D_MODEL), lambda m: (m, 0)),
                  pl.BlockSpec((1, 1, D_MODEL), lambda m: (mod_row(m) * 6 + 2, 0, 0)),
                  pl.BlockSpec((1, D_MODEL), lambda m: (0, 0)),
                  pl.BlockSpec((1, D_MODEL), lambda m: (0, 0))],
        out_specs=pl.BlockSpec((tm, D_MODEL), lambda m: (m, 0)),
        out_shape=jax.ShapeDtypeStruct((tokens, D_MODEL), F32),
        compiler_params=pltpu.CompilerParams(
            dimension_semantics=("parallel",), vmem_limit_bytes=_vmem_limit(blocks)),
        name="mix_out",
    )(mix_r, mix_m, w_out_b, x2d, mod3, ln_g, ln_b)


def _ffn_kernel(x_ref, sh_ref, sc_ref, g2_ref, wu_ref, wg_ref, cv_ref, wd_ref, lg_ref, lb_ref,
                o_ref, h_scr, *, period):
    j = pl.program_id(1)

    @pl.when(j == 0)
    def _():
        h = _ln_rows(x_ref[...]) * (1.0 + sc_ref[0]) + sh_ref[0]
        h_scr[...] = h.astype(BF16)
        o_ref[...] = jnp.zeros_like(o_ref)

    hb = h_scr[...]
    u = jnp.dot(hb, wu_ref[...], preferred_element_type=F32)
    g = jnp.dot(hb, wg_ref[...], preferred_element_type=F32)
    hid = _silu(_conv3_rows(u, cv_ref[0], period)) * g
    lane = lax.broadcasted_iota(jnp.int32, (1, FF_TN), 1)
    hid = jnp.where((j == FF_NJ - 1) & (lane < FF_OVERLAP), 0.0, hid).astype(BF16)
    o_ref[...] += jnp.dot(hid, wd_ref[...], preferred_element_type=F32)

    @pl.when(j == FF_NJ - 1)
    def _():
        y = ALPHA * x_ref[...] + g2_ref[0] * o_ref[...]
        o_ref[...] = _ln_rows(y) * lg_ref[...] + lb_ref[...]


def _ff_offset(j):
    return pl.multiple_of(jnp.minimum(j * FF_TN, D_FF - FF_TN), LANES)


def _ffn_call(x1, mod3, mod_row, w_up_b, conv_tiles, w_down_b, ln_g, ln_b, *, period, tm):
    tokens = x1.shape[0]
    up_tile = (pl.Element(D_MODEL), pl.Element(FF_TN))
    blocks = [_nbytes((tm, D_MODEL), F32)] * 2 + [_nbytes((D_MODEL, FF_TN), BF16)] * 3 \
        + [_nbytes((8, FF_TN), F32)]
    scratch_bytes = _nbytes((tm, D_MODEL), BF16)
    return pl.pallas_call(
        functools.partial(_ffn_kernel, period=period),
        grid=(tokens // tm, FF_NJ),
        in_specs=[pl.BlockSpec((tm, D_MODEL), lambda m, j: (m, 0)),
                  pl.BlockSpec((1, 1, D_MODEL), lambda m, j: (mod_row(m) * 6 + 3, 0, 0)),
                  pl.BlockSpec((1, 1, D_MODEL), lambda m, j: (mod_row(m) * 6 + 4, 0, 0)),
                  pl.BlockSpec((1, 1, D_MODEL), lambda m, j: (mod_row(m) * 6 + 5, 0, 0)),
                  pl.BlockSpec(up_tile, lambda m, j: (0, _ff_offset(j))),
                  pl.BlockSpec(up_tile, lambda m, j: (0, pl.multiple_of(D_FF + _ff_offset(j), LANES))),
                  pl.BlockSpec((1, 3, FF_TN), lambda m, j: (j, 0, 0)),
                  pl.BlockSpec((pl.Element(FF_TN), pl.Element(D_MODEL)),
                               lambda m, j: (_ff_offset(j), 0)),
                  pl.BlockSpec((1, D_MODEL), lambda m, j: (0, 0)),
                  pl.BlockSpec((1, D_MODEL), lambda m, j: (0, 0))],
        out_specs=pl.BlockSpec((tm, D_MODEL), lambda m, j: (m, 0)),
        out_shape=jax.ShapeDtypeStruct((tokens, D_MODEL), F32),
        scratch_shapes=[pltpu.VMEM((tm, D_MODEL), BF16)],
        compiler_params=pltpu.CompilerParams(
            dimension_semantics=("parallel", "arbitrary"),
            vmem_limit_bytes=_vmem_limit(blocks, scratch_bytes)),
        name="ffn",
    )(x1, mod3, mod3, mod3, w_up_b, w_up_b, conv_tiles, w_down_b, ln_g, ln_b)


def _rope_tables(seq_len):
    quarter = HEAD_DIM // 4
    t = jnp.arange(seq_len)
    row = (t // GRID_W).astype(F32)
    col = (t % GRID_W).astype(F32)
    inv = ROPE_BASE ** (-jnp.arange(quarter, dtype=F32) / quarter)
    ang_r, ang_c = row[:, None] * inv, col[:, None] * inv
    cos_t = jnp.concatenate([jnp.cos(ang_r)] * 2 + [jnp.cos(ang_c)] * 2, axis=1)
    sin_t = jnp.concatenate([-jnp.sin(ang_r), jnp.sin(ang_r), -jnp.sin(ang_c), jnp.sin(ang_c)], axis=1)
    return cos_t, sin_t


def kernel(x_prompt, x_sample, state_ret, state_mlstm_C, state_mlstm_n, state_mlstm_m, c, c_ctx,
           w_mod, b_mod, w_in, b_gate, conv_qk, ret_theta, gn_ret, gn_mlstm, w_out,
           ln1_g, ln1_b, w_up, conv_ff, w_down, ln2_g, ln2_b):
    bp, lp, _ = x_prompt.shape
    bs, ls, _ = x_sample.shape
    layer = 0
    tm_mix, tm = 1024, 512

    cvec = jnp.concatenate([c_ctx[None], c, jnp.zeros((8 - 1 - bs, D_MODEL), F32)], axis=0)
    mod = _mod_call(cvec, w_mod[layer], b_mod[layer][None])
    mod3 = mod.reshape(8 * 6, 1, D_MODEL)

    def row_prompt(tile):
        return lambda m: 0

    def row_sample(tile):
        return lambda m: 1 + (m * tile) // ls

    w_l = w_in[layer].T
    w_gate = w_l[N_PROJ:]
    b_gate_col = b_gate[layer].reshape(N_GATES, 1)
    ff_offsets = [min(j * FF_TN, D_FF - FF_TN) for j in range(FF_NJ)]
    conv_tiles = jnp.stack([conv_ff[layer][:, o:o + FF_TN] for o in ff_offsets])
    gn_r = gn_ret[layer].reshape(1, GROUP_DIM)
    gn_m = gn_mlstm[layer].reshape(1, GROUP_DIM)
    theta = ret_theta[layer]
    ln1 = (ln1_g[layer][None], ln1_b[layer][None])
    ln2 = (ln2_g[layer][None], ln2_b[layer][None])

    def mixer(x, mod_row, rope, period, bb, hp, ret_init, mlstm_init, emit_states, stage=()):
        batch, seq_len, _ = x.shape
        x2d = x.reshape(batch * seq_len, D_MODEL)
        h2d, gates = _ln_mod_call(x2d, mod3, mod_row(tm_mix), w_gate, b_gate_col, tm=tm_mix)
        proj, kt, *staged = _mix_in_call(h2d, w_l, conv_qk[layer], rope, seq_len=seq_len,
                                         period=period, tm=tm_mix, stage=stage)
        ret = _ret_call(proj, kt, theta, gn_r, ret_init, batch=batch, seq_len=seq_len, bb=bb,
                        hp=hp, emit_states=emit_states)
        mls = _mlstm_call(proj, kt, gates, gn_m, mlstm_init, batch=batch, seq_len=seq_len, bb=bb,
                          hp=hp, emit_states=emit_states)
        return x2d, ret, mls, staged

    def channel_mix(x2d, shape, ret, mls, mod_row, period, weights):
        w_up_b, w_down_b, w_out_b = weights
        x1 = _mix_out_call(ret[0], mls[0], w_out_b, x2d, mod3, mod_row(tm), *ln1, tm=tm)
        y = _ffn_call(x1, mod3, mod_row(tm), w_up_b, conv_tiles, w_down_b, *ln2, period=period,
                      tm=tm)
        return y.reshape(shape)

    xp, ret_p, mls_p, weights_b = mixer(x_prompt, row_prompt, None, lp, 2, N_HEADS, None, None,
                                        True, stage=(w_up[layer], w_down[layer], w_out[layer]))
    m0 = state_mlstm_m.reshape(-1)
    xs, ret_s, mls_s, _ = mixer(x_sample, row_sample, _rope_tables(ls), GRID_W, 1, 2, state_ret,
                                (state_mlstm_C, state_mlstm_n, m0), False)
    y_prompt = channel_mix(xp, x_prompt.shape, ret_p, mls_p, row_prompt, lp, weights_b)
    y_sample = channel_mix(xs, x_sample.shape, ret_s, mls_s, row_sample, GRID_W, weights_b)
    ret_states, mls_states = ret_p[1:], mls_p[1:]

    new_ret = ret_states[0]
    new_c, new_n, new_m = mls_states
    new_m = new_m[:, :, 0].reshape(bp, 1, 2, N_HEADS)
    return (y_prompt, y_sample, new_ret, new_c, new_n, new_m)
```

```python
import functools

import jax
import jax.numpy as jnp
from jax import lax
from jax.experimental import pallas as pl
from jax.experimental.pallas import tpu as pltpu

F32 = jnp.float32
BF16 = jnp.bfloat16

D_MODEL = 2048
GRID_W = 64
HEAD_DIM = 256
N_HEADS = 4
GROUP_DIM = N_HEADS * HEAD_DIM
N_PROJ = 8 * GROUP_DIM
N_GATES = 4 * N_HEADS
D_FF = 5504
CHUNK = 128
ROPE_BASE = 10000.0
ALPHA = 2.0 ** 0.25
LN_EPS = 1e-6
K_SCALE = HEAD_DIM ** -0.5

LANES = 128
BF16_ROWS = 16
V7X_VMEM_LIMIT_CAP = 60 * 1024 * 1024
VMEM_TEMP_ALLOWANCE = 16 * 1024 * 1024

MIX_ROW_CHUNK = 512
N_STORED = 6 * GROUP_DIM
FF_TN = 512
FF_NJ = -(-D_FF // FF_TN)
FF_OVERLAP = FF_NJ * FF_TN - D_FF
FF_PRO_STEPS = 8


def _vmem_limit(block_bytes, scratch_bytes=0):
    est = 2 * sum(block_bytes) + scratch_bytes + VMEM_TEMP_ALLOWANCE
    return int(min(V7X_VMEM_LIMIT_CAP, est))


def _nbytes(shape, dtype):
    n = 1
    for s in shape:
        n *= s
    return n * jnp.dtype(dtype).itemsize


def _ln_rows(x):
    mu = jnp.mean(x, axis=-1, keepdims=True)
    xc = x - mu
    var = jnp.mean(xc * xc, axis=-1, keepdims=True)
    return xc * lax.rsqrt(var + LN_EPS)


def _silu(x):
    return x * jax.nn.sigmoid(x)


def _log_sigmoid(x):
    return jnp.minimum(x, 0.0) - jnp.log1p(jnp.exp(-jnp.abs(x)))


def _conv3_rows(u, w, period):
    rows = u.shape[0]
    t = lax.broadcasted_iota(jnp.int32, (rows, 1), 0) & (period - 1)
    prev = jnp.where(t == 0, 0.0, pltpu.roll(u, 1, 0))
    nxt = jnp.where(t == period - 1, 0.0, pltpu.roll(u, rows - 1, 0))
    return prev * w[0:1] + u * w[1:2] + nxt * w[2:3]


def _cumsum_lanes(x, lane, reverse):
    s = 1
    while s < LANES:
        if reverse:
            x = x + jnp.where(lane < LANES - s, pltpu.roll(x, LANES - s, 1), 0.0)
        else:
            x = x + jnp.where(lane >= s, pltpu.roll(x, s, 1), 0.0)
        s *= 2
    return x


def _mod_kernel(c_ref, w_ref, b_ref, o_ref):
    a = _silu(c_ref[...]).astype(BF16)
    o_ref[...] = jnp.dot(a, w_ref[...].astype(BF16), preferred_element_type=F32) + b_ref[...]


def _mod_call(cvec, w_mod, b_mod):
    rows, n = cvec.shape[0], w_mod.shape[1]
    tn = 1024
    blocks = [_nbytes((rows, D_MODEL), F32), _nbytes((D_MODEL, tn), F32),
              _nbytes((8, tn), F32), _nbytes((rows, tn), F32)]
    return pl.pallas_call(
        _mod_kernel,
        grid=(n // tn,),
        in_specs=[pl.BlockSpec((rows, D_MODEL), lambda j: (0, 0)),
                  pl.BlockSpec((D_MODEL, tn), lambda j: (0, j)),
                  pl.BlockSpec((1, tn), lambda j: (0, j))],
        out_specs=pl.BlockSpec((rows, tn), lambda j: (0, j)),
        out_shape=jax.ShapeDtypeStruct((rows, n), F32),
        compiler_params=pltpu.CompilerParams(
            dimension_semantics=("arbitrary",), vmem_limit_bytes=_vmem_limit(blocks)),
        name="mod",
    )(cvec, w_mod, b_mod)


def _ln_mod_kernel(x_ref, sh_ref, sc_ref, wg_ref, bg_ref, h_ref, gates_ref, *, tm):
    h = _ln_rows(x_ref[...]) * (1.0 + sc_ref[0]) + sh_ref[0]
    hb = h.astype(BF16)
    h_ref[...] = hb
    g = lax.dot_general(wg_ref[...].astype(BF16), hb, (((1,), (1,)), ((), ())),
                        preferred_element_type=F32) + bg_ref[...]
    row = lax.broadcasted_iota(jnp.int32, (N_GATES, LANES), 0)
    lane = lax.broadcasted_iota(jnp.int32, (N_GATES, LANES), 1)
    kind = lax.shift_right_logical(row, 2)
    for s in range(tm // LANES):
        gs = g[:, s * LANES:(s + 1) * LANES]
        ls = _log_sigmoid(gs)
        gates_ref[s] = jnp.where(kind == 1, _cumsum_lanes(ls, lane, False),
                                 jnp.where(kind == 3, _cumsum_lanes(ls, lane, True), gs))


def _ln_mod_call(x2d, mod3, mod_row, w_gate, b_gate_col, *, tm):
    tokens = x2d.shape[0]
    slabs = tm // LANES
    blocks = [_nbytes((tm, D_MODEL), F32), _nbytes((tm, D_MODEL), BF16),
              _nbytes((N_GATES, D_MODEL), F32), _nbytes((slabs, N_GATES, LANES), F32)]
    return pl.pallas_call(
        functools.partial(_ln_mod_kernel, tm=tm),
        grid=(tokens // tm,),
        in_specs=[pl.BlockSpec((tm, D_MODEL), lambda m: (m, 0)),
                  pl.BlockSpec((1, 1, D_MODEL), lambda m: (mod_row(m) * 6 + 0, 0, 0)),
                  pl.BlockSpec((1, 1, D_MODEL), lambda m: (mod_row(m) * 6 + 1, 0, 0)),
                  pl.BlockSpec((N_GATES, D_MODEL), lambda m: (0, 0)),
                  pl.BlockSpec((N_GATES, 1), lambda m: (0, 0))],
        out_specs=[pl.BlockSpec((tm, D_MODEL), lambda m: (m, 0)),
                   pl.BlockSpec((slabs, N_GATES, LANES), lambda m: (m, 0, 0))],
        out_shape=[jax.ShapeDtypeStruct((tokens, D_MODEL), BF16),
                   jax.ShapeDtypeStruct((tokens // LANES, N_GATES, LANES), F32)],
        compiler_params=pltpu.CompilerParams(
            dimension_semantics=("parallel",), vmem_limit_bytes=_vmem_limit(blocks)),
        name="ln_mod",
    )(x2d, mod3, mod3, w_gate, b_gate_col)


def _mix_in_kernel(*refs, tm, period, grid_mode, n_staged):
    refs = list(refs)
    h_ref, w_ref, cv_ref = refs[:3]
    pos = 3
    if grid_mode:
        cos_ref, sin_ref = refs[pos:pos + 2]
        pos += 2
    staged_in = refs[pos:pos + n_staged]
    pos += n_staged
    proj_ref, kt_ref = refs[pos:pos + 2]
    pos += 2
    staged_out = refs[pos:pos + n_staged]
    wb_scr = refs[pos + n_staged]
    group = pl.program_id(0)

    @pl.when(pl.program_id(1) == 0)
    def _():
        wb_scr[...] = w_ref[...].astype(BF16)

    def chunks():
        for src, dst in zip(staged_in, staged_out):
            dst[...] = src[...].astype(BF16)
        for rc in range(tm // MIX_ROW_CHUNK):
            rows = slice(rc * MIX_ROW_CHUNK, (rc + 1) * MIX_ROW_CHUNK)
            yield rc, rows, lax.dot_general(h_ref[rows, :], wb_scr[...], (((1,), (1,)), ((), ())),
                                            preferred_element_type=F32)

    def rope(a, rows):
        out = []
        for s in range(GROUP_DIM // LANES):
            xs = a[:, s * LANES:(s + 1) * LANES]
            t = (s % 2) * LANES
            out.append(xs * cos_ref[rows, t:t + LANES]
                       + pltpu.roll(xs, LANES // 2, 1) * sin_ref[rows, t:t + LANES])
        return jnp.concatenate(out, axis=1)

    def store_transposed(rc, y):
        yt = y.T.astype(BF16)
        per = MIX_ROW_CHUNK // LANES
        for s in range(per):
            kt_ref[rc * per + s] = yt[:, s * LANES:(s + 1) * LANES]

    plain = (group == 2) | (group == 3) | (group == 6) | (group == 7)
    if not grid_mode:
        plain = plain | (group == 0)

    @pl.when(plain)
    def _():
        for _, rows, acc in chunks():
            proj_ref[rows, :] = acc.astype(BF16)

    if grid_mode:
        @pl.when(group == 0)
        def _():
            for _, rows, acc in chunks():
                proj_ref[rows, :] = rope(acc, rows).astype(BF16)

    @pl.when(group == 1)
    def _():
        for rc, rows, acc in chunks():
            k = acc * K_SCALE
            store_transposed(rc, rope(k, rows) if grid_mode else k)

    @pl.when(group == 4)
    def _():
        for _, rows, acc in chunks():
            proj_ref[rows, :] = _silu(_conv3_rows(acc, cv_ref[...], period)).astype(BF16)

    @pl.when(group == 5)
    def _():
        for rc, _, acc in chunks():
            store_transposed(rc, _silu(_conv3_rows(acc, cv_ref[...], period)) * K_SCALE)


def _staging_specs(weights, n_groups, n_tiles):
    specs, shapes, block_bytes = [], [], []
    for w in weights:
        rows, cols = w.shape
        slab = BF16_ROWS
        while rows % slab or rows // slab > n_groups * n_tiles:
            slab += BF16_ROWS
        n_slabs = rows // slab
        specs.append(pl.BlockSpec(
            (slab, cols),
            lambda g, m, n_slabs=n_slabs: (jnp.minimum(g * n_tiles + m, n_slabs - 1), 0)))
        shapes.append(jax.ShapeDtypeStruct((rows, cols), BF16))
        block_bytes.append(_nbytes((slab, cols), F32) + _nbytes((slab, cols), BF16))
    return specs, shapes, block_bytes


def _mix_in_call(h2d, w_in_t, conv_qk, rope, *, seq_len, period, tm, stage=()):
    tokens = h2d.shape[0]
    grid_mode = rope is not None
    tiles_per_seq = seq_len // tm
    slabs = tm // LANES
    last = tokens // tm - 1
    n_groups = N_PROJ // GROUP_DIM
    stage_specs, stage_shapes, stage_bytes = _staging_specs(stage, n_groups, tokens // tm)

    def is_key(g):
        return (g == 1) | (g == 5)

    def proj_index(g, m):
        col = g - (g >= 1).astype(jnp.int32) - (g >= 5).astype(jnp.int32)
        return jnp.where(is_key(g), last, m), col

    def kt_index(g, m):
        row = jnp.where(g == 0, 0, jnp.where(is_key(g), m, last))
        return row, (g >= 5).astype(jnp.int32), 0

    in_specs = [
        pl.BlockSpec((tm, D_MODEL), lambda g, m: (m, 0)),
        pl.BlockSpec((GROUP_DIM, D_MODEL), lambda g, m: (g, 0)),
        pl.BlockSpec((3, GROUP_DIM), lambda g, m: (0, jnp.clip(g - 4, 0, 1))),
    ]
    args = [h2d, w_in_t, conv_qk]
    blocks = [_nbytes((tm, D_MODEL), BF16), _nbytes((GROUP_DIM, D_MODEL), F32),
              _nbytes((tm, GROUP_DIM), BF16) * 2, _nbytes((8, GROUP_DIM), F32)]
    if grid_mode:
        in_specs += [pl.BlockSpec((tm, HEAD_DIM),
                                  lambda g, m: (jnp.where(g <= 1, m % tiles_per_seq, 0), 0))] * 2
        args += list(rope)
        blocks += [_nbytes((tm, HEAD_DIM), F32)] * 2
    in_specs += stage_specs
    args += list(stage)
    blocks += stage_bytes
    scratch_defs = [((GROUP_DIM, D_MODEL), BF16)]

    return pl.pallas_call(
        functools.partial(_mix_in_kernel, tm=tm, period=period, grid_mode=grid_mode,
                          n_staged=len(stage)),
        grid=(n_groups, tokens // tm),
        in_specs=in_specs,
        out_specs=[pl.BlockSpec((tm, GROUP_DIM), lambda g, m: proj_index(g, m)),
                   pl.BlockSpec((slabs, GROUP_DIM, LANES), lambda g, m: kt_index(g, m))]
        + stage_specs,
        out_shape=[jax.ShapeDtypeStruct((tokens, N_STORED), BF16),
                   jax.ShapeDtypeStruct((tokens // LANES, 2 * GROUP_DIM, LANES), BF16)]
        + stage_shapes,
        scratch_shapes=[pltpu.VMEM(s, d) for s, d in scratch_defs],
        compiler_params=pltpu.CompilerParams(
            dimension_semantics=("arbitrary", "arbitrary"),
            vmem_limit_bytes=_vmem_limit(blocks, sum(_nbytes(s, d) for s, d in scratch_defs))),
        name="mix_in_grid" if grid_mode else "mix_in_seq",
    )(*args)


_NN = (((2,), (1,)), ((0,), (0,)))
_NT = (((2,), (2,)), ((0,), (0,)))


def _bdot(x, y, dims):
    return lax.dot_general(x, y, dims, preferred_element_type=F32)


def _chunk_loop(nc, body, reverse=False, update_last=True):
    def chunk(i):
        return nc - 1 - i if reverse else i

    if nc <= 2:
        for i in range(nc - 1):
            body(chunk(i), True)
    else:
        def step(i, carry):
            body(chunk(i), True)
            return carry
        lax.fori_loop(0, nc - 1, step, 0)
    body(chunk(nc - 1), update_last)


def _rows(c, base=0):
    start = base + c * CHUNK
    if not isinstance(start, int):
        start = pl.multiple_of(start, CHUNK)
    return pl.ds(start, CHUNK)


def _is_chunk(c, value):
    return isinstance(c, int) and c == value


def _heads(ref, c, nc, bb, hp):
    return jnp.stack([ref[_rows(c, bi * nc * CHUNK), hh * HEAD_DIM:(hh + 1) * HEAD_DIM]
                      for bi in range(bb) for hh in range(hp)])


def _keys_t(kt_ref, c, nc, bb, hp):
    parts = [kt_ref[bi * nc + c].reshape(hp, HEAD_DIM, CHUNK) for bi in range(bb)]
    return parts[0] if bb == 1 else jnp.concatenate(parts, axis=0)


def _per_seq(x, bb):
    return x if bb == 1 else jnp.concatenate([x] * bb, axis=0)


def _group_block(seq_len, hp, group, bb):
    per = N_HEADS // hp
    return pl.BlockSpec((bb * seq_len, hp * HEAD_DIM), lambda b, hg: (b, group * per + hg))


def _kt_block(nc, hp, group, bb):
    per = N_HEADS // hp
    return pl.BlockSpec((bb * nc, hp * HEAD_DIM, LANES), lambda b, hg: (b, group * per + hg, 0))


def _ret_kernel(*refs, nc, bb, hp, has_init, emit_states):
    refs = list(refs)
    theta_ref, q_ref, kt_ref, v_ref, rg_ref, gn_ref = refs[:6]
    pos = 6
    s0_ref = None
    if has_init:
        s0_ref = refs[pos]
        pos += 1
    o_ref = refs[pos]
    pos += 1
    st_ref = None
    if emit_states:
        st_ref = refs[pos]
        pos += 1
    hist, s_scr, decay_scr, qdec_scr, kdec_scr, cdec_scr = refs[pos:pos + 6]
    hg = pl.program_id(1)
    hsel = pl.ds(hg * hp, hp)

    @pl.when(pl.program_id(0) == 0)
    def _():
        ii = lax.broadcasted_iota(jnp.int32, (CHUNK, CHUNK), 0)
        jj = lax.broadcasted_iota(jnp.int32, (CHUNK, CHUNK), 1)
        d = (ii - jj).astype(F32)
        p_col = lax.broadcasted_iota(jnp.int32, (CHUNK, HEAD_DIM), 0).astype(F32)
        p_row = lax.broadcasted_iota(jnp.int32, (1, LANES), 1).astype(F32)
        for hh in range(hp):
            h = hg * hp + hh
            lg_f = _log_sigmoid(jnp.full((1, LANES), theta_ref[0, h], F32))
            lg_b = _log_sigmoid(jnp.full((1, LANES), theta_ref[1, h], F32))
            decay_scr[h] = (jnp.where(d >= 0, jnp.exp(lg_f * jnp.maximum(d, 0.0)), 0.0)
                            + jnp.where(d <= 0, jnp.exp(lg_b * jnp.maximum(-d, 0.0)), 0.0))
            qdec_scr[0, h] = jnp.exp(lg_f[:, :1] * (p_col + 1.0))
            qdec_scr[1, h] = jnp.exp(lg_b[:, :1] * (CHUNK - p_col))
            kdec_scr[0, h] = jnp.exp(lg_f * (CHUNK - 1.0 - p_row))
            kdec_scr[1, h] = jnp.exp(lg_b * p_row)
            cdec_scr[0, h] = jnp.exp(lg_f * float(CHUNK))
            cdec_scr[1, h] = jnp.exp(lg_b * float(CHUNK))

    def table(ref, *lead):
        return _per_seq(ref[(*lead, hsel)], bb)

    def kv_update(c, v, direction):
        kd = (_keys_t(kt_ref, c, nc, bb, hp).astype(F32)
              * table(kdec_scr, direction)).astype(BF16)
        s_scr[...] = s_scr[...] * table(cdec_scr, direction)[:, :, :1] + _bdot(kd, v, _NN)

    def init_state(direction):
        s_scr[...] = (s0_ref[0, 0, direction] if has_init
                      else jnp.zeros((bb * hp, HEAD_DIM, HEAD_DIM), F32))

    def emit_state(direction):
        for bi in range(bb):
            st_ref[bi, 0, direction] = s_scr[bi * hp:(bi + 1) * hp]

    init_state(0)

    def fwd(c, update):
        hist[c] = s_scr[...].astype(BF16)
        if update:
            kv_update(c, _heads(v_ref, c, nc, bb, hp), 0)

    _chunk_loop(nc, fwd, update_last=emit_states)
    if emit_states:
        emit_state(0)

    init_state(1)

    def bwd(c, update):
        q, v = _heads(q_ref, c, nc, bb, hp), _heads(v_ref, c, nc, bb, hp)
        kt = _keys_t(kt_ref, c, nc, bb, hp)
        att = (_bdot(q, kt, _NN) * table(decay_scr)).astype(BF16)
        o = _bdot(att, v, _NN)
        if has_init or not _is_chunk(c, 0):
            o = o + _bdot(q, hist[c], _NN) * table(qdec_scr, 0)
        if has_init or not _is_chunk(c, nc - 1):
            o = o + _bdot(q, s_scr[...].astype(BF16), _NN) * table(qdec_scr, 1)
        y = _ln_rows(o)
        for bi in range(bb):
            r = _rows(c, bi * nc * CHUNK)
            for hh in range(hp):
                cs = slice(hh * HEAD_DIM, (hh + 1) * HEAD_DIM)
                o_ref[r, cs] = (y[bi * hp + hh] * gn_ref[:, cs]
                                * _silu(rg_ref[r, cs].astype(F32))).astype(BF16)
        if update:
            kv_update(c, v, 1)

    _chunk_loop(nc, bwd, reverse=True, update_last=emit_states)
    if emit_states:
        emit_state(1)


def _ret_call(proj, kt, theta, gn, s0, *, batch, seq_len, bb, hp, emit_states):
    nc = seq_len // CHUNK
    has_init = s0 is not None
    assert N_HEADS % hp == 0 and batch % bb == 0 and (bb == 1 or not has_init)
    state_spec = pl.BlockSpec((bb, 1, 2, hp, HEAD_DIM, HEAD_DIM), lambda b, hg: (b, 0, 0, hg, 0, 0))
    in_specs = [pl.BlockSpec(memory_space=pltpu.SMEM),
                _group_block(seq_len, hp, 0, bb), _kt_block(nc, hp, 0, bb),
                _group_block(seq_len, hp, 1, bb), _group_block(seq_len, hp, 2, bb),
                pl.BlockSpec((1, hp * HEAD_DIM), lambda b, hg: (0, hg))]
    args = [theta, proj, kt, proj, proj, gn]
    blocks = [_nbytes((bb * seq_len, hp * HEAD_DIM), BF16)] * 5
    if has_init:
        in_specs.append(state_spec)
        args.append(s0)
        blocks.append(_nbytes((2, hp, HEAD_DIM, HEAD_DIM), F32))
    out_specs = [pl.BlockSpec((bb * seq_len, hp * HEAD_DIM), lambda b, hg: (b, hg))]
    out_shape = [jax.ShapeDtypeStruct((batch * seq_len, GROUP_DIM), BF16)]
    if emit_states:
        out_specs.append(state_spec)
        out_shape.append(jax.ShapeDtypeStruct((batch, 1, 2, N_HEADS, HEAD_DIM, HEAD_DIM), F32))
        blocks.append(_nbytes((bb, 2, hp, HEAD_DIM, HEAD_DIM), F32))
    scratch_defs = [((nc, bb * hp, HEAD_DIM, HEAD_DIM), BF16), ((bb * hp, HEAD_DIM, HEAD_DIM), F32),
                    ((N_HEADS, CHUNK, CHUNK), F32), ((2, N_HEADS, CHUNK, HEAD_DIM), F32),
                    ((2, N_HEADS, 1, LANES), F32), ((2, N_HEADS, 1, LANES), F32)]
    return pl.pallas_call(
        functools.partial(_ret_kernel, nc=nc, bb=bb, hp=hp, has_init=has_init,
                          emit_states=emit_states),
        grid=(batch // bb, N_HEADS // hp),
        in_specs=in_specs, out_specs=out_specs, out_shape=out_shape,
        scratch_shapes=[pltpu.VMEM(s, d) for s, d in scratch_defs],
        compiler_params=pltpu.CompilerParams(
            dimension_semantics=("arbitrary", "arbitrary"),
            vmem_limit_bytes=_vmem_limit(blocks, sum(_nbytes(s, d) for s, d in scratch_defs))),
        name="ret_grid" if has_init else "ret_seq",
    )(*args)


def _mlstm_kernel(*refs, nc, bb, hp, has_init, emit_states):
    refs = list(refs)
    q_ref, kt_ref, v_ref, mo_ref, g_ref, gn_ref = refs[:6]
    pos = 6
    c0_ref = n0_ref = m0_ref = None
    if has_init:
        c0_ref, n0_ref, m0_ref = refs[pos:pos + 3]
        pos += 3
    o_ref = refs[pos]
    pos += 1
    c_out = n_out = m_out = None
    if emit_states:
        c_out, n_out, m_out = refs[pos:pos + 3]
        pos += 3
    c_hist, n_hist, m_hist, c_scr, n_scr, m_scr = refs[pos:pos + 6]

    b = pl.program_id(0)
    hg = pl.program_id(1)
    ii = lax.broadcasted_iota(jnp.int32, (CHUNK, CHUNK), 0)
    jj = lax.broadcasted_iota(jnp.int32, (CHUNK, CHUNK), 1)
    eye = ii == jj
    lower = ii >= jj
    upper = ii <= jj

    ne = bb * hp

    def col(row):
        return jnp.sum(jnp.where(eye, row, 0.0), axis=2, keepdims=True)

    def gate_rows(c, kind):
        base = kind * N_HEADS + hg * hp
        return jnp.stack([g_ref[bi * nc + c, pl.ds(base + hh, 1), :]
                          for bi in range(bb) for hh in range(hp)])

    def direction_out(a, q, qf, v, c_b, n_row, m, b_row, i_row, mask):
        b_col = col(b_row)
        dm = jnp.where(mask, b_col - b_row + i_row, -jnp.inf)
        inter = b_col + m
        mt = jnp.maximum(inter, jnp.max(dm, axis=2, keepdims=True))
        w = jnp.exp(dm - mt)
        sp = jnp.exp(inter - mt)
        s = a * w
        num = _bdot(s.astype(BF16), v, _NN)
        den = jnp.sum(s, axis=2, keepdims=True)
        if c_b is not None:
            num = num + _bdot(q, c_b, _NN) * sp
            den = den + jnp.sum(qf * n_row, axis=2, keepdims=True) * sp
        return num / jnp.maximum(jnp.abs(den), jnp.exp(-mt))

    def state_update(c, v, b_row, i_row, last):
        m = m_scr[...][:, :, :1]
        b_last = b_row[:, :, last:last + 1]
        g = b_last - b_row + i_row
        m_new = jnp.maximum(b_last + m, jnp.max(g, axis=2, keepdims=True))
        wk = jnp.exp(g - m_new)
        sc = jnp.exp(b_last + m - m_new)
        kt = _keys_t(kt_ref, c, nc, bb, hp)
        kw = (kt.astype(F32) * wk).astype(BF16)
        c_scr[...] = c_scr[...] * sc + _bdot(kw, v, _NN)
        wk8 = jnp.broadcast_to(wk, (ne, 8, CHUNK)).astype(BF16)
        n_scr[...] = n_scr[...] * sc + _bdot(wk8, kt, _NT)[:, :1, :]
        m_scr[...] = jnp.broadcast_to(m_new, (ne, 1, LANES))

    def init_state(direction):
        if has_init:
            c_scr[...] = c0_ref[0, 0, direction]
            n_scr[...] = jnp.stack([n0_ref[0, 0, direction, pl.ds(hg * hp + hh, 1), :]
                                    for hh in range(hp)])
            m_scr[...] = jnp.stack([
                jnp.full((1, LANES), m0_ref[(b * 2 + direction) * N_HEADS + hg * hp + hh], F32)
                for hh in range(hp)])
        else:
            c_scr[...] = jnp.zeros((ne, HEAD_DIM, HEAD_DIM), F32)
            n_scr[...] = jnp.zeros((ne, 1, HEAD_DIM), F32)
            m_scr[...] = jnp.zeros((ne, 1, LANES), F32)

    def emit_state(direction):
        for bi in range(bb):
            c_out[bi, 0, direction] = c_scr[bi * hp:(bi + 1) * hp]
            for hh in range(hp):
                n_out[bi, 0, direction, pl.ds(hh, 1), :] = n_scr[bi * hp + hh]
                m_out[bi, pl.ds(direction * N_HEADS + hh, 1), :] = m_scr[bi * hp + hh]

    init_state(0)

    def fwd(c, update):
        c_hist[c] = c_scr[...].astype(BF16)
        n_hist[c] = n_scr[...]
        m_hist[c] = m_scr[...]
        if update:
            state_update(c, _heads(v_ref, c, nc, bb, hp), gate_rows(c, 1), gate_rows(c, 0),
                         CHUNK - 1)

    _chunk_loop(nc, fwd, update_last=emit_states)
    if emit_states:
        emit_state(0)

    init_state(1)

    def bwd(c, update):
        i_f, b_f, i_b, b_b = (gate_rows(c, kind) for kind in (0, 1, 2, 3))
        q, v = _heads(q_ref, c, nc, bb, hp), _heads(v_ref, c, nc, bb, hp)
        qf = q.astype(F32)
        a = _bdot(q, _keys_t(kt_ref, c, nc, bb, hp), _NN)
        zero_f = not has_init and _is_chunk(c, 0)
        zero_b = not has_init and _is_chunk(c, nc - 1)
        h_f = direction_out(a, q, qf, v, None if zero_f else c_hist[c], n_hist[c],
                            m_hist[c][:, :, :1], b_f, i_f, lower)
        h_b = direction_out(a, q, qf, v, None if zero_b else c_scr[...].astype(BF16),
                            n_scr[...], m_scr[...][:, :, :1], b_b, i_b, upper)
        y = _ln_rows(h_f + h_b)
        for bi in range(bb):
            r = _rows(c, bi * nc * CHUNK)
            for hh in range(hp):
                cs = slice(hh * HEAD_DIM, (hh + 1) * HEAD_DIM)
                o_ref[r, cs] = (jax.nn.sigmoid(mo_ref[r, cs].astype(F32))
                                * (y[bi * hp + hh] * gn_ref[:, cs])).astype(BF16)
        if update:
            state_update(c, v, b_b, i_b, 0)

    _chunk_loop(nc, bwd, reverse=True, update_last=emit_states)
    if emit_states:
        emit_state(1)


def _mlstm_call(proj, kt, gates, gn, init, *, batch, seq_len, bb, hp, emit_states):
    nc = seq_len // CHUNK
    has_init = init is not None
    assert N_HEADS % hp == 0 and (hp == N_HEADS or not emit_states)
    assert batch % bb == 0 and (bb == 1 or not has_init)
    c_spec = pl.BlockSpec((bb, 1, 2, hp, HEAD_DIM, HEAD_DIM), lambda b, hg: (b, 0, 0, hg, 0, 0))
    n_spec = pl.BlockSpec((bb, 1, 2, N_HEADS, HEAD_DIM), lambda b, hg: (b, 0, 0, 0, 0))
    in_specs = [_group_block(seq_len, hp, 3, bb), _kt_block(nc, hp, 1, bb),
                _group_block(seq_len, hp, 4, bb), _group_block(seq_len, hp, 5, bb),
                pl.BlockSpec((bb * nc, N_GATES, LANES), lambda b, hg: (b, 0, 0)),
                pl.BlockSpec((1, hp * HEAD_DIM), lambda b, hg: (0, hg))]
    args = [proj, kt, proj, proj, gates, gn]
    blocks = ([_nbytes((bb * seq_len, hp * HEAD_DIM), BF16)] * 5
              + [_nbytes((bb * nc, N_GATES, LANES), F32)])
    if has_init:
        in_specs += [c_spec, n_spec, pl.BlockSpec(memory_space=pltpu.SMEM)]
        args += list(init)
        blocks.append(_nbytes((2, hp, HEAD_DIM, HEAD_DIM), F32))
    out_specs = [pl.BlockSpec((bb * seq_len, hp * HEAD_DIM), lambda b, hg: (b, hg))]
    out_shape = [jax.ShapeDtypeStruct((batch * seq_len, GROUP_DIM), BF16)]
    if emit_states:
        out_specs += [c_spec, n_spec, pl.BlockSpec((bb, 2 * N_HEADS, LANES), lambda b, hg: (b, 0, 0))]
        out_shape += [jax.ShapeDtypeStruct((batch, 1, 2, N_HEADS, HEAD_DIM, HEAD_DIM), F32),
                      jax.ShapeDtypeStruct((batch, 1, 2, N_HEADS, HEAD_DIM), F32),
                      jax.ShapeDtypeStruct((batch, 2 * N_HEADS, LANES), F32)]
        blocks.append(_nbytes((bb, 2, hp, HEAD_DIM, HEAD_DIM), F32))
    ne = bb * hp
    scratch_defs = [((nc, ne, HEAD_DIM, HEAD_DIM), BF16), ((nc, ne, 1, HEAD_DIM), F32),
                    ((nc, ne, 1, LANES), F32), ((ne, HEAD_DIM, HEAD_DIM), F32),
                    ((ne, 1, HEAD_DIM), F32), ((ne, 1, LANES), F32)]
    scratch_bytes = sum(_nbytes(s, d) for s, d in scratch_defs) + 16 * nc * ne * HEAD_DIM * 4
    return pl.pallas_call(
        functools.partial(_mlstm_kernel, nc=nc, bb=bb, hp=hp, has_init=has_init,
                          emit_states=emit_states),
        grid=(batch // bb, N_HEADS // hp),
        in_specs=in_specs, out_specs=out_specs, out_shape=out_shape,
        scratch_shapes=[pltpu.VMEM(s, d) for s, d in scratch_defs],
        compiler_params=pltpu.CompilerParams(
            dimension_semantics=("arbitrary", "arbitrary"),
            vmem_limit_bytes=_vmem_limit(blocks, scratch_bytes)),
        name="mlstm_grid" if has_init else "mlstm_seq",
    )(*args)


def _mix_out_kernel(mr_ref, mm_ref, w_ref, x_ref, g1_ref, lg_ref, lb_ref, o_ref):
    for rc in range(o_ref.shape[0] // MIX_ROW_CHUNK):
        rows = slice(rc * MIX_ROW_CHUNK, (rc + 1) * MIX_ROW_CHUNK)
        mix = (jnp.dot(mr_ref[rows, :], w_ref[0:GROUP_DIM, :], preferred_element_type=F32)
               + jnp.dot(mm_ref[rows, :], w_ref[GROUP_DIM:, :], preferred_element_type=F32))
        o_ref[rows, :] = (_ln_rows(ALPHA * x_ref[rows, :] + g1_ref[0] * mix) * lg_ref[...]
                          + lb_ref[...])


def _mix_out_call(mix_r, mix_m, w_out_b, x2d, mod3, mod_row, ln_g, ln_b, *, tm):
    tokens = x2d.shape[0]
    blocks = [_nbytes((tm, GROUP_DIM), BF16)] * 2 + [_nbytes((D_MODEL, D_MODEL), BF16)] \
        + [_nbytes((tm, D_MODEL), F32)] * 2
    return pl.pallas_call(
        _mix_out_kernel,
        grid=(tokens // tm,),
        in_specs=[pl.BlockSpec((tm, GROUP_DIM), lambda m: (m, 0)),
                  pl.BlockSpec((tm, GROUP_DIM), lambda m: (m, 0)),
                  pl.BlockSpec((D_MODEL, D_MODEL), lambda m: (0, 0)),
                  pl.BlockSpec((tm, D_MODEL), lambda m: (m, 0)),
                  pl.BlockSpec((1, 1, D_MODEL), lambda m: (mod_row(m) * 6 + 2, 0, 0)),
                  pl.BlockSpec((1, D_MODEL), lambda m: (0, 0)),
                  pl.BlockSpec((1, D_MODEL), lambda m: (0, 0))],
        out_specs=pl.BlockSpec((tm, D_MODEL), lambda m: (m, 0)),
        out_shape=jax.ShapeDtypeStruct((tokens, D_MODEL), F32),
        compiler_params=pltpu.CompilerParams(
            dimension_semantics=("parallel",), vmem_limit_bytes=_vmem_limit(blocks)),
        name="mix_out",
    )(mix_r, mix_m, w_out_b, x2d, mod3, ln_g, ln_b)


def _ffn_kernel(x_ref, xn_ref, sh_ref, sc_ref, shn_ref, scn_ref, g2_ref, wu_ref, wg_ref, cv_ref,
                wd_ref, lg_ref, lb_ref, o_ref, h_buf, *, period, tm):
    m = pl.program_id(0)
    j = pl.program_id(1)
    slot = lax.rem(m, 2)

    def modulated(x, sc, sh):
        return (_ln_rows(x) * (1.0 + sc) + sh).astype(BF16)

    @pl.when((m == 0) & (j == 0))
    def _():
        h_buf[0] = modulated(x_ref[...], sc_ref[0], sh_ref[0])

    @pl.when(j == 0)
    def _():
        o_ref[...] = jnp.zeros_like(o_ref)

    n_rows = tm // FF_PRO_STEPS
    start = pl.multiple_of(jnp.minimum(j, FF_PRO_STEPS - 1) * n_rows, n_rows)
    rows = pl.ds(start, n_rows)
    h_next = modulated(xn_ref[rows, :], scn_ref[0], shn_ref[0])

    hb = h_buf[slot]
    u = jnp.dot(hb, wu_ref[...], preferred_element_type=F32)
    g = jnp.dot(hb, wg_ref[...], preferred_element_type=F32)
    hid = _silu(_conv3_rows(u, cv_ref[0], period)) * g
    lane = lax.broadcasted_iota(jnp.int32, (1, FF_TN), 1)
    hid = jnp.where((j == FF_NJ - 1) & (lane < FF_OVERLAP), 0.0, hid).astype(BF16)
    o_ref[...] += jnp.dot(hid, wd_ref[...], preferred_element_type=F32)
    h_buf[1 - slot, rows, :] = h_next

    @pl.when(j == FF_NJ - 1)
    def _():
        y = ALPHA * x_ref[...] + g2_ref[0] * o_ref[...]
        o_ref[...] = _ln_rows(y) * lg_ref[...] + lb_ref[...]


def _ff_offset(j):
    return pl.multiple_of(jnp.minimum(j * FF_TN, D_FF - FF_TN), LANES)


def _ffn_call(x1, mod3, mod_row, w_up_b, conv_tiles, w_down_b, ln_g, ln_b, *, period, tm):
    tokens = x1.shape[0]
    up_tile = (pl.Element(D_MODEL), pl.Element(FF_TN))
    blocks = [_nbytes((tm, D_MODEL), F32)] * 3 + [_nbytes((D_MODEL, FF_TN), BF16)] * 3 \
        + [_nbytes((8, FF_TN), F32)]
    scratch_bytes = _nbytes((2, tm, D_MODEL), BF16)
    last = tokens // tm - 1

    def nxt(m):
        return jnp.minimum(m + 1, last)

    return pl.pallas_call(
        functools.partial(_ffn_kernel, period=period, tm=tm),
        grid=(tokens // tm, FF_NJ),
        in_specs=[pl.BlockSpec((tm, D_MODEL), lambda m, j: (m, 0)),
                  pl.BlockSpec((tm, D_MODEL), lambda m, j: (nxt(m), 0)),
                  pl.BlockSpec((1, 1, D_MODEL), lambda m, j: (mod_row(m) * 6 + 3, 0, 0)),
                  pl.BlockSpec((1, 1, D_MODEL), lambda m, j: (mod_row(m) * 6 + 4, 0, 0)),
                  pl.BlockSpec((1, 1, D_MODEL), lambda m, j: (mod_row(nxt(m)) * 6 + 3, 0, 0)),
                  pl.BlockSpec((1, 1, D_MODEL), lambda m, j: (mod_row(nxt(m)) * 6 + 4, 0, 0)),
                  pl.BlockSpec((1, 1, D_MODEL), lambda m, j: (mod_row(m) * 6 + 5, 0, 0)),
                  pl.BlockSpec(up_tile, lambda m, j: (0, _ff_offset(j))),
                  pl.BlockSpec(up_tile, lambda m, j: (0, pl.multiple_of(D_FF + _ff_offset(j), LANES))),
                  pl.BlockSpec((1, 3, FF_TN), lambda m, j: (j, 0, 0)),
                  pl.BlockSpec((pl.Element(FF_TN), pl.Element(D_MODEL)),
                               lambda m, j: (_ff_offset(j), 0)),
                  pl.BlockSpec((1, D_MODEL), lambda m, j: (0, 0)),
                  pl.BlockSpec((1, D_MODEL), lambda m, j: (0, 0))],
        out_specs=pl.BlockSpec((tm, D_MODEL), lambda m, j: (m, 0)),
        out_shape=jax.ShapeDtypeStruct((tokens, D_MODEL), F32),
        scratch_shapes=[pltpu.VMEM((2, tm, D_MODEL), BF16)],
        compiler_params=pltpu.CompilerParams(
            dimension_semantics=("arbitrary", "arbitrary"),
            vmem_limit_bytes=_vmem_limit(blocks, scratch_bytes)),
        name="ffn",
    )(x1, x1, mod3, mod3, mod3, mod3, mod3, w_up_b, w_up_b, conv_tiles, w_down_b, ln_g, ln_b)


def _rope_tables(seq_len):
    quarter = HEAD_DIM // 4
    t = jnp.arange(seq_len)
    row = (t // GRID_W).astype(F32)
    col = (t % GRID_W).astype(F32)
    inv = ROPE_BASE ** (-jnp.arange(quarter, dtype=F32) / quarter)
    ang_r, ang_c = row[:, None] * inv, col[:, None] * inv
    cos_t = jnp.concatenate([jnp.cos(ang_r)] * 2 + [jnp.cos(ang_c)] * 2, axis=1)
    sin_t = jnp.concatenate([-jnp.sin(ang_r), jnp.sin(ang_r), -jnp.sin(ang_c), jnp.sin(ang_c)], axis=1)
    return cos_t, sin_t


def kernel(x_prompt, x_sample, state_ret, state_mlstm_C, state_mlstm_n, state_mlstm_m, c, c_ctx,
           w_mod, b_mod, w_in, b_gate, conv_qk, ret_theta, gn_ret, gn_mlstm, w_out,
           ln1_g, ln1_b, w_up, conv_ff, w_down, ln2_g, ln2_b):
    bp, lp, _ = x_prompt.shape
    bs, ls, _ = x_sample.shape
    layer = 0
    tm_mix, tm = 1024, 512

    cvec = jnp.concatenate([c_ctx[None], c, jnp.zeros((8 - 1 - bs, D_MODEL), F32)], axis=0)
    mod = _mod_call(cvec, w_mod[layer], b_mod[layer][None])
    mod3 = mod.reshape(8 * 6, 1, D_MODEL)

    def row_prompt(tile):
        return lambda m: 0

    def row_sample(tile):
        return lambda m: 1 + (m * tile) // ls

    w_l = w_in[layer].T
    w_gate = w_l[N_PROJ:]
    b_gate_col = b_gate[layer].reshape(N_GATES, 1)
    ff_offsets = [min(j * FF_TN, D_FF - FF_TN) for j in range(FF_NJ)]
    conv_tiles = jnp.stack([conv_ff[layer][:, o:o + FF_TN] for o in ff_offsets])
    gn_r = gn_ret[layer].reshape(1, GROUP_DIM)
    gn_m = gn_mlstm[layer].reshape(1, GROUP_DIM)
    theta = ret_theta[layer]
    ln1 = (ln1_g[layer][None], ln1_b[layer][None])
    ln2 = (ln2_g[layer][None], ln2_b[layer][None])

    def mixer(x, mod_row, rope, period, bb, hp, ret_init, mlstm_init, emit_states, stage=()):
        batch, seq_len, _ = x.shape
        x2d = x.reshape(batch * seq_len, D_MODEL)
        h2d, gates = _ln_mod_call(x2d, mod3, mod_row(tm_mix), w_gate, b_gate_col, tm=tm_mix)
        proj, kt, *staged = _mix_in_call(h2d, w_l, conv_qk[layer], rope, seq_len=seq_len,
                                         period=period, tm=tm_mix, stage=stage)
        ret = _ret_call(proj, kt, theta, gn_r, ret_init, batch=batch, seq_len=seq_len, bb=bb,
                        hp=hp, emit_states=emit_states)
        mls = _mlstm_call(proj, kt, gates, gn_m, mlstm_init, batch=batch, seq_len=seq_len, bb=bb,
                          hp=hp, emit_states=emit_states)
        return x2d, ret, mls, staged

    def channel_mix(x2d, shape, ret, mls, mod_row, period, weights):
        w_up_b, w_down_b, w_out_b = weights
        x1 = _mix_out_call(ret[0], mls[0], w_out_b, x2d, mod3, mod_row(tm), *ln1, tm=tm)
        y = _ffn_call(x1, mod3, mod_row(tm), w_up_b, conv_tiles, w_down_b, *ln2, period=period,
                      tm=tm)
        return y.reshape(shape)

    xp, ret_p, mls_p, weights_b = mixer(x_prompt, row_prompt, None, lp, 2, N_HEADS, None, None,
                                        True, stage=(w_up[layer], w_down[layer], w_out[layer]))
    m0 = state_mlstm_m.reshape(-1)
    xs, ret_s, mls_s, _ = mixer(x_sample, row_sample, _rope_tables(ls), GRID_W, 1, 2, state_ret,
                                (state_mlstm_C, state_mlstm_n, m0), False)
    y_prompt = channel_mix(xp, x_prompt.shape, ret_p, mls_p, row_prompt, lp, weights_b)
    y_sample = channel_mix(xs, x_sample.shape, ret_s, mls_s, row_sample, GRID_W, weights_b)
    ret_states, mls_states = ret_p[1:], mls_p[1:]

    new_ret = ret_states[0]
    new_c, new_n, new_m = mls_states
    new_m = new_m[:, :, 0].reshape(bp, 1, 2, N_HEADS)
    return (y_prompt, y_sample, new_ret, new_c, new_n, new_m)
```

```python
import functools

import jax
import jax.numpy as jnp
from jax import lax
from jax.experimental import pallas as pl
from jax.experimental.pallas import tpu as pltpu

F32 = jnp.float32
BF16 = jnp.bfloat16

D_MODEL = 2048
GRID_W = 64
HEAD_DIM = 256
N_HEADS = 4
GROUP_DIM = N_HEADS * HEAD_DIM
N_PROJ = 8 * GROUP_DIM
N_GATES = 4 * N_HEADS
D_FF = 5504
CHUNK = 128
ROPE_BASE = 10000.0
ALPHA = 2.0 ** 0.25
LN_EPS = 1e-6
K_SCALE = HEAD_DIM ** -0.5

LANES = 128
BF16_ROWS = 16
V7X_VMEM_LIMIT_CAP = 60 * 1024 * 1024
VMEM_TEMP_ALLOWANCE = 16 * 1024 * 1024

MIX_ROW_CHUNK = 512
N_STORED = 6 * GROUP_DIM
FF_TN = 512
FF_NJ = -(-D_FF // FF_TN)
FF_OVERLAP = FF_NJ * FF_TN - D_FF


def _vmem_limit(block_bytes, scratch_bytes=0):
    est = 2 * sum(block_bytes) + scratch_bytes + VMEM_TEMP_ALLOWANCE
    return int(min(V7X_VMEM_LIMIT_CAP, est))


def _nbytes(shape, dtype):
    n = 1
    for s in shape:
        n *= s
    return n * jnp.dtype(dtype).itemsize


def _ln_rows(x):
    mu = jnp.mean(x, axis=-1, keepdims=True)
    xc = x - mu
    var = jnp.mean(xc * xc, axis=-1, keepdims=True)
    return xc * lax.rsqrt(var + LN_EPS)


def _silu(x):
    return x * jax.nn.sigmoid(x)


def _log_sigmoid(x):
    return jnp.minimum(x, 0.0) - jnp.log1p(jnp.exp(-jnp.abs(x)))


def _conv3_rows(u, w, period):
    rows = u.shape[0]
    t = lax.broadcasted_iota(jnp.int32, (rows, 1), 0) & (period - 1)
    prev = jnp.where(t == 0, 0.0, pltpu.roll(u, 1, 0))
    nxt = jnp.where(t == period - 1, 0.0, pltpu.roll(u, rows - 1, 0))
    return prev * w[0:1] + u * w[1:2] + nxt * w[2:3]


def _cumsum_lanes(x, lane, reverse):
    s = 1
    while s < LANES:
        if reverse:
            x = x + jnp.where(lane < LANES - s, pltpu.roll(x, LANES - s, 1), 0.0)
        else:
            x = x + jnp.where(lane >= s, pltpu.roll(x, s, 1), 0.0)
        s *= 2
    return x


def _mod_kernel(c_ref, w_ref, b_ref, o_ref):
    a = _silu(c_ref[...]).astype(BF16)
    o_ref[...] = jnp.dot(a, w_ref[...].astype(BF16), preferred_element_type=F32) + b_ref[...]


def _mod_call(cvec, w_mod, b_mod):
    rows, n = cvec.shape[0], w_mod.shape[1]
    tn = 1024
    blocks = [_nbytes((rows, D_MODEL), F32), _nbytes((D_MODEL, tn), F32),
              _nbytes((8, tn), F32), _nbytes((rows, tn), F32)]
    return pl.pallas_call(
        _mod_kernel,
        grid=(n // tn,),
        in_specs=[pl.BlockSpec((rows, D_MODEL), lambda j: (0, 0)),
                  pl.BlockSpec((D_MODEL, tn), lambda j: (0, j)),
                  pl.BlockSpec((1, tn), lambda j: (0, j))],
        out_specs=pl.BlockSpec((rows, tn), lambda j: (0, j)),
        out_shape=jax.ShapeDtypeStruct((rows, n), F32),
        compiler_params=pltpu.CompilerParams(
            dimension_semantics=("arbitrary",), vmem_limit_bytes=_vmem_limit(blocks)),
        name="mod",
    )(cvec, w_mod, b_mod)


def _ln_mod_kernel(x_ref, sh_ref, sc_ref, wg_ref, bg_ref, h_ref, gates_ref, *, tm):
    h = _ln_rows(x_ref[...]) * (1.0 + sc_ref[0]) + sh_ref[0]
    hb = h.astype(BF16)
    h_ref[...] = hb
    g = lax.dot_general(wg_ref[...].astype(BF16), hb, (((1,), (1,)), ((), ())),
                        preferred_element_type=F32) + bg_ref[...]
    row = lax.broadcasted_iota(jnp.int32, (N_GATES, LANES), 0)
    lane = lax.broadcasted_iota(jnp.int32, (N_GATES, LANES), 1)
    kind = lax.shift_right_logical(row, 2)
    for s in range(tm // LANES):
        gs = g[:, s * LANES:(s + 1) * LANES]
        ls = _log_sigmoid(gs)
        gates_ref[s] = jnp.where(kind == 1, _cumsum_lanes(ls, lane, False),
                                 jnp.where(kind == 3, _cumsum_lanes(ls, lane, True), gs))


def _ln_mod_call(x2d, mod3, mod_row, w_gate, b_gate_col, *, tm):
    tokens = x2d.shape[0]
    slabs = tm // LANES
    blocks = [_nbytes((tm, D_MODEL), F32), _nbytes((tm, D_MODEL), BF16),
              _nbytes((N_GATES, D_MODEL), F32), _nbytes((slabs, N_GATES, LANES), F32)]
    return pl.pallas_call(
        functools.partial(_ln_mod_kernel, tm=tm),
        grid=(tokens // tm,),
        in_specs=[pl.BlockSpec((tm, D_MODEL), lambda m: (m, 0)),
                  pl.BlockSpec((1, 1, D_MODEL), lambda m: (mod_row(m) * 6 + 0, 0, 0)),
                  pl.BlockSpec((1, 1, D_MODEL), lambda m: (mod_row(m) * 6 + 1, 0, 0)),
                  pl.BlockSpec((N_GATES, D_MODEL), lambda m: (0, 0)),
                  pl.BlockSpec((N_GATES, 1), lambda m: (0, 0))],
        out_specs=[pl.BlockSpec((tm, D_MODEL), lambda m: (m, 0)),
                   pl.BlockSpec((slabs, N_GATES, LANES), lambda m: (m, 0, 0))],
        out_shape=[jax.ShapeDtypeStruct((tokens, D_MODEL), BF16),
                   jax.ShapeDtypeStruct((tokens // LANES, N_GATES, LANES), F32)],
        compiler_params=pltpu.CompilerParams(
            dimension_semantics=("parallel",), vmem_limit_bytes=_vmem_limit(blocks)),
        name="ln_mod",
    )(x2d, mod3, mod3, w_gate, b_gate_col)


def _mix_in_kernel(*refs, tm, period, grid_mode, n_staged):
    refs = list(refs)
    h_ref, w_ref, cv_ref = refs[:3]
    pos = 3
    if grid_mode:
        cos_ref, sin_ref = refs[pos:pos + 2]
        pos += 2
    staged_in = refs[pos:pos + n_staged]
    pos += n_staged
    proj_ref, kt_ref = refs[pos:pos + 2]
    pos += 2
    staged_out = refs[pos:pos + n_staged]
    wb_scr = refs[pos + n_staged]
    group = pl.program_id(0)

    @pl.when(pl.program_id(1) == 0)
    def _():
        wb_scr[...] = w_ref[...].astype(BF16)

    def chunks(n_rows=MIX_ROW_CHUNK):
        for src, dst in zip(staged_in, staged_out):
            dst[...] = src[...].astype(BF16)
        for rc in range(tm // n_rows):
            rows = slice(rc * n_rows, (rc + 1) * n_rows)
            yield rc, rows, lax.dot_general(h_ref[rows, :], wb_scr[...], (((1,), (1,)), ((), ())),
                                            preferred_element_type=F32)

    def rope(a, rows):
        out = []
        for s in range(GROUP_DIM // LANES):
            xs = a[:, s * LANES:(s + 1) * LANES]
            t = (s % 2) * LANES
            out.append(xs * cos_ref[rows, t:t + LANES]
                       + pltpu.roll(xs, LANES // 2, 1) * sin_ref[rows, t:t + LANES])
        return jnp.concatenate(out, axis=1)

    def store_transposed(rc, y):
        yt = y.T.astype(BF16)
        per = y.shape[0] // LANES
        for s in range(per):
            kt_ref[rc * per + s] = yt[:, s * LANES:(s + 1) * LANES]

    plain = (group == 2) | (group == 3) | (group == 6) | (group == 7)
    if not grid_mode:
        plain = plain | (group == 0)

    @pl.when(plain)
    def _():
        for _, rows, acc in chunks(tm):
            proj_ref[rows, :] = acc.astype(BF16)

    if grid_mode:
        @pl.when(group == 0)
        def _():
            for _, rows, acc in chunks():
                proj_ref[rows, :] = rope(acc, rows).astype(BF16)

    @pl.when(group == 1)
    def _():
        for rc, rows, acc in chunks():
            k = acc * K_SCALE
            store_transposed(rc, rope(k, rows) if grid_mode else k)

    @pl.when(group == 4)
    def _():
        for _, rows, acc in chunks():
            proj_ref[rows, :] = _silu(_conv3_rows(acc, cv_ref[...], period)).astype(BF16)

    @pl.when(group == 5)
    def _():
        for rc, _, acc in chunks():
            store_transposed(rc, _silu(_conv3_rows(acc, cv_ref[...], period)) * K_SCALE)


def _staging_specs(weights, n_groups, n_tiles):
    specs, shapes, block_bytes = [], [], []
    for w in weights:
        rows, cols = w.shape
        slab = BF16_ROWS
        while rows % slab or rows // slab > n_groups * n_tiles:
            slab += BF16_ROWS
        n_slabs = rows // slab
        specs.append(pl.BlockSpec(
            (slab, cols),
            lambda g, m, n_slabs=n_slabs: (jnp.minimum(g * n_tiles + m, n_slabs - 1), 0)))
        shapes.append(jax.ShapeDtypeStruct((rows, cols), BF16))
        block_bytes.append(_nbytes((slab, cols), F32) + _nbytes((slab, cols), BF16))
    return specs, shapes, block_bytes


def _mix_in_call(h2d, w_in_t, conv_qk, rope, *, seq_len, period, tm, stage=()):
    tokens = h2d.shape[0]
    grid_mode = rope is not None
    tiles_per_seq = seq_len // tm
    slabs = tm // LANES
    last = tokens // tm - 1
    n_groups = N_PROJ // GROUP_DIM
    stage_specs, stage_shapes, stage_bytes = _staging_specs(stage, n_groups, tokens // tm)

    def is_key(g):
        return (g == 1) | (g == 5)

    def proj_index(g, m):
        col = g - (g >= 1).astype(jnp.int32) - (g >= 5).astype(jnp.int32)
        return jnp.where(is_key(g), last, m), col

    def kt_index(g, m):
        row = jnp.where(g == 0, 0, jnp.where(is_key(g), m, last))
        return row, (g >= 5).astype(jnp.int32), 0

    in_specs = [
        pl.BlockSpec((tm, D_MODEL), lambda g, m: (m, 0)),
        pl.BlockSpec((GROUP_DIM, D_MODEL), lambda g, m: (g, 0)),
        pl.BlockSpec((3, GROUP_DIM), lambda g, m: (0, jnp.clip(g - 4, 0, 1))),
    ]
    args = [h2d, w_in_t, conv_qk]
    blocks = [_nbytes((tm, D_MODEL), BF16), _nbytes((GROUP_DIM, D_MODEL), F32),
              _nbytes((tm, GROUP_DIM), BF16) * 2, _nbytes((8, GROUP_DIM), F32)]
    if grid_mode:
        in_specs += [pl.BlockSpec((tm, HEAD_DIM),
                                  lambda g, m: (jnp.where(g <= 1, m % tiles_per_seq, 0), 0))] * 2
        args += list(rope)
        blocks += [_nbytes((tm, HEAD_DIM), F32)] * 2
    in_specs += stage_specs
    args += list(stage)
    blocks += stage_bytes
    scratch_defs = [((GROUP_DIM, D_MODEL), BF16)]

    return pl.pallas_call(
        functools.partial(_mix_in_kernel, tm=tm, period=period, grid_mode=grid_mode,
                          n_staged=len(stage)),
        grid=(n_groups, tokens // tm),
        in_specs=in_specs,
        out_specs=[pl.BlockSpec((tm, GROUP_DIM), lambda g, m: proj_index(g, m)),
                   pl.BlockSpec((slabs, GROUP_DIM, LANES), lambda g, m: kt_index(g, m))]
        + stage_specs,
        out_shape=[jax.ShapeDtypeStruct((tokens, N_STORED), BF16),
                   jax.ShapeDtypeStruct((tokens // LANES, 2 * GROUP_DIM, LANES), BF16)]
        + stage_shapes,
        scratch_shapes=[pltpu.VMEM(s, d) for s, d in scratch_defs],
        compiler_params=pltpu.CompilerParams(
            dimension_semantics=("arbitrary", "arbitrary"),
            vmem_limit_bytes=_vmem_limit(blocks, sum(_nbytes(s, d) for s, d in scratch_defs))),
        name="mix_in_grid" if grid_mode else "mix_in_seq",
    )(*args)


_NN = (((2,), (1,)), ((0,), (0,)))
_NT = (((2,), (2,)), ((0,), (0,)))


def _bdot(x, y, dims):
    return lax.dot_general(x, y, dims, preferred_element_type=F32)


def _chunk_loop(nc, body, reverse=False):
    if nc <= 2:
        for i in range(nc):
            body(nc - 1 - i if reverse else i)
    else:
        def step(i, carry):
            body(nc - 1 - i if reverse else i)
            return carry
        lax.fori_loop(0, nc, step, 0)


def _rows(c, base=0):
    start = base + c * CHUNK
    if not isinstance(start, int):
        start = pl.multiple_of(start, CHUNK)
    return pl.ds(start, CHUNK)


def _is_chunk(c, value):
    return isinstance(c, int) and c == value


def _heads(ref, c, nc, bb, hp):
    return jnp.stack([ref[_rows(c, bi * nc * CHUNK), hh * HEAD_DIM:(hh + 1) * HEAD_DIM]
                      for bi in range(bb) for hh in range(hp)])


def _keys_t(kt_ref, c, nc, bb, hp):
    parts = [kt_ref[bi * nc + c].reshape(hp, HEAD_DIM, CHUNK) for bi in range(bb)]
    return parts[0] if bb == 1 else jnp.concatenate(parts, axis=0)


def _per_seq(x, bb):
    return x if bb == 1 else jnp.concatenate([x] * bb, axis=0)


def _group_block(seq_len, hp, group, bb):
    per = N_HEADS // hp
    return pl.BlockSpec((bb * seq_len, hp * HEAD_DIM), lambda b, hg: (b, group * per + hg))


def _kt_block(nc, hp, group, bb):
    per = N_HEADS // hp
    return pl.BlockSpec((bb * nc, hp * HEAD_DIM, LANES), lambda b, hg: (b, group * per + hg, 0))


def _ret_kernel(*refs, nc, bb, hp, has_init, emit_states):
    refs = list(refs)
    theta_ref, q_ref, kt_ref, v_ref, rg_ref, gn_ref = refs[:6]
    pos = 6
    s0_ref = None
    if has_init:
        s0_ref = refs[pos]
        pos += 1
    o_ref = refs[pos]
    pos += 1
    st_ref = None
    if emit_states:
        st_ref = refs[pos]
        pos += 1
    hist, s_scr, decay_scr, qdec_scr, kdec_scr, cdec_scr = refs[pos:pos + 6]
    hg = pl.program_id(1)
    hsel = pl.ds(hg * hp, hp)

    @pl.when(pl.program_id(0) == 0)
    def _():
        ii = lax.broadcasted_iota(jnp.int32, (CHUNK, CHUNK), 0)
        jj = lax.broadcasted_iota(jnp.int32, (CHUNK, CHUNK), 1)
        d = (ii - jj).astype(F32)
        p_col = lax.broadcasted_iota(jnp.int32, (CHUNK, HEAD_DIM), 0).astype(F32)
        p_row = lax.broadcasted_iota(jnp.int32, (1, LANES), 1).astype(F32)
        for hh in range(hp):
            h = hg * hp + hh
            lg_f = _log_sigmoid(jnp.full((1, LANES), theta_ref[0, h], F32))
            lg_b = _log_sigmoid(jnp.full((1, LANES), theta_ref[1, h], F32))
            decay_scr[h] = (jnp.where(d >= 0, jnp.exp(lg_f * jnp.maximum(d, 0.0)), 0.0)
                            + jnp.where(d <= 0, jnp.exp(lg_b * jnp.maximum(-d, 0.0)), 0.0))
            qdec_scr[0, h] = jnp.exp(lg_f[:, :1] * (p_col + 1.0))
            qdec_scr[1, h] = jnp.exp(lg_b[:, :1] * (CHUNK - p_col))
            kdec_scr[0, h] = jnp.exp(lg_f * (CHUNK - 1.0 - p_row))
            kdec_scr[1, h] = jnp.exp(lg_b * p_row)
            cdec_scr[0, h] = jnp.exp(lg_f * float(CHUNK))
            cdec_scr[1, h] = jnp.exp(lg_b * float(CHUNK))

    def table(ref, *lead):
        return _per_seq(ref[(*lead, hsel)], bb)

    def kv_update(c, v, direction):
        kd = (_keys_t(kt_ref, c, nc, bb, hp).astype(F32)
              * table(kdec_scr, direction)).astype(BF16)
        s_scr[...] = s_scr[...] * table(cdec_scr, direction)[:, :, :1] + _bdot(kd, v, _NN)

    def init_state(direction):
        s_scr[...] = (s0_ref[0, 0, direction] if has_init
                      else jnp.zeros((bb * hp, HEAD_DIM, HEAD_DIM), F32))

    def emit_state(direction):
        for bi in range(bb):
            st_ref[bi, 0, direction] = s_scr[bi * hp:(bi + 1) * hp]

    init_state(0)

    def fwd(c):
        hist[c] = s_scr[...].astype(BF16)
        kv_update(c, _heads(v_ref, c, nc, bb, hp), 0)

    _chunk_loop(nc, fwd)
    if emit_states:
        emit_state(0)

    init_state(1)

    def bwd(c):
        q, v = _heads(q_ref, c, nc, bb, hp), _heads(v_ref, c, nc, bb, hp)
        kt = _keys_t(kt_ref, c, nc, bb, hp)
        att = (_bdot(q, kt, _NN) * table(decay_scr)).astype(BF16)
        o = _bdot(att, v, _NN)
        if has_init or not _is_chunk(c, 0):
            o = o + _bdot(q, hist[c], _NN) * table(qdec_scr, 0)
        if has_init or not _is_chunk(c, nc - 1):
            o = o + _bdot(q, s_scr[...].astype(BF16), _NN) * table(qdec_scr, 1)
        y = _ln_rows(o)
        for bi in range(bb):
            r = _rows(c, bi * nc * CHUNK)
            for hh in range(hp):
                cs = slice(hh * HEAD_DIM, (hh + 1) * HEAD_DIM)
                o_ref[r, cs] = (y[bi * hp + hh] * gn_ref[:, cs]
                                * _silu(rg_ref[r, cs].astype(F32))).astype(BF16)
        kv_update(c, v, 1)

    _chunk_loop(nc, bwd, reverse=True)
    if emit_states:
        emit_state(1)


def _ret_call(proj, kt, theta, gn, s0, *, batch, seq_len, bb, hp, emit_states):
    nc = seq_len // CHUNK
    has_init = s0 is not None
    assert N_HEADS % hp == 0 and batch % bb == 0 and (bb == 1 or not has_init)
    state_spec = pl.BlockSpec((bb, 1, 2, hp, HEAD_DIM, HEAD_DIM), lambda b, hg: (b, 0, 0, hg, 0, 0))
    in_specs = [pl.BlockSpec(memory_space=pltpu.SMEM),
                _group_block(seq_len, hp, 0, bb), _kt_block(nc, hp, 0, bb),
                _group_block(seq_len, hp, 1, bb), _group_block(seq_len, hp, 2, bb),
                pl.BlockSpec((1, hp * HEAD_DIM), lambda b, hg: (0, hg))]
    args = [theta, proj, kt, proj, proj, gn]
    blocks = [_nbytes((bb * seq_len, hp * HEAD_DIM), BF16)] * 5
    if has_init:
        in_specs.append(state_spec)
        args.append(s0)
        blocks.append(_nbytes((2, hp, HEAD_DIM, HEAD_DIM), F32))
    out_specs = [pl.BlockSpec((bb * seq_len, hp * HEAD_DIM), lambda b, hg: (b, hg))]
    out_shape = [jax.ShapeDtypeStruct((batch * seq_len, GROUP_DIM), BF16)]
    if emit_states:
        out_specs.append(state_spec)
        out_shape.append(jax.ShapeDtypeStruct((batch, 1, 2, N_HEADS, HEAD_DIM, HEAD_DIM), F32))
        blocks.append(_nbytes((bb, 2, hp, HEAD_DIM, HEAD_DIM), F32))
    scratch_defs = [((nc, bb * hp, HEAD_DIM, HEAD_DIM), BF16), ((bb * hp, HEAD_DIM, HEAD_DIM), F32),
                    ((N_HEADS, CHUNK, CHUNK), F32), ((2, N_HEADS, CHUNK, HEAD_DIM), F32),
                    ((2, N_HEADS, 1, LANES), F32), ((2, N_HEADS, 1, LANES), F32)]
    return pl.pallas_call(
        functools.partial(_ret_kernel, nc=nc, bb=bb, hp=hp, has_init=has_init,
                          emit_states=emit_states),
        grid=(batch // bb, N_HEADS // hp),
        in_specs=in_specs, out_specs=out_specs, out_shape=out_shape,
        scratch_shapes=[pltpu.VMEM(s, d) for s, d in scratch_defs],
        compiler_params=pltpu.CompilerParams(
            dimension_semantics=("arbitrary", "arbitrary"),
            vmem_limit_bytes=_vmem_limit(blocks, sum(_nbytes(s, d) for s, d in scratch_defs))),
        name="ret_grid" if has_init else "ret_seq",
    )(*args)


def _mlstm_kernel(*refs, nc, bb, hp, has_init, emit_states):
    refs = list(refs)
    q_ref, kt_ref, v_ref, mo_ref, g_ref, gn_ref = refs[:6]
    pos = 6
    c0_ref = n0_ref = m0_ref = None
    if has_init:
        c0_ref, n0_ref, m0_ref = refs[pos:pos + 3]
        pos += 3
    o_ref = refs[pos]
    pos += 1
    c_out = n_out = m_out = None
    if emit_states:
        c_out, n_out, m_out = refs[pos:pos + 3]
        pos += 3
    c_hist, n_hist, m_hist, c_scr, n_scr, m_scr = refs[pos:pos + 6]

    b = pl.program_id(0)
    hg = pl.program_id(1)
    ii = lax.broadcasted_iota(jnp.int32, (CHUNK, CHUNK), 0)
    jj = lax.broadcasted_iota(jnp.int32, (CHUNK, CHUNK), 1)
    eye = ii == jj
    lower = ii >= jj
    upper = ii <= jj

    ne = bb * hp

    def col(row):
        return jnp.sum(jnp.where(eye, row, 0.0), axis=2, keepdims=True)

    def gate_rows(c, kind):
        base = kind * N_HEADS + hg * hp
        return jnp.stack([g_ref[bi * nc + c, pl.ds(base + hh, 1), :]
                          for bi in range(bb) for hh in range(hp)])

    def direction_out(a, q, qf, v, c_b, n_row, m, b_row, i_row, mask):
        b_col = col(b_row)
        dm = jnp.where(mask, b_col - b_row + i_row, -jnp.inf)
        inter = b_col + m
        mt = jnp.maximum(inter, jnp.max(dm, axis=2, keepdims=True))
        w = jnp.exp(dm - mt)
        sp = jnp.exp(inter - mt)
        s = a * w
        num = _bdot(s.astype(BF16), v, _NN)
        den = jnp.sum(s, axis=2, keepdims=True)
        if c_b is not None:
            num = num + _bdot(q, c_b, _NN) * sp
            den = den + jnp.sum(qf * n_row, axis=2, keepdims=True) * sp
        return num / jnp.maximum(jnp.abs(den), jnp.exp(-mt))

    def state_update(c, v, b_row, i_row, last):
        m = m_scr[...][:, :, :1]
        b_last = b_row[:, :, last:last + 1]
        g = b_last - b_row + i_row
        m_new = jnp.maximum(b_last + m, jnp.max(g, axis=2, keepdims=True))
        wk = jnp.exp(g - m_new)
        sc = jnp.exp(b_last + m - m_new)
        kt = _keys_t(kt_ref, c, nc, bb, hp)
        kw = (kt.astype(F32) * wk).astype(BF16)
        c_scr[...] = c_scr[...] * sc + _bdot(kw, v, _NN)
        wk8 = jnp.broadcast_to(wk, (ne, 8, CHUNK)).astype(BF16)
        n_scr[...] = n_scr[...] * sc + _bdot(wk8, kt, _NT)[:, :1, :]
        m_scr[...] = jnp.broadcast_to(m_new, (ne, 1, LANES))

    def init_state(direction):
        if has_init:
            c_scr[...] = c0_ref[0, 0, direction]
            n_scr[...] = jnp.stack([n0_ref[0, 0, direction, pl.ds(hg * hp + hh, 1), :]
                                    for hh in range(hp)])
            m_scr[...] = jnp.stack([
                jnp.full((1, LANES), m0_ref[(b * 2 + direction) * N_HEADS + hg * hp + hh], F32)
                for hh in range(hp)])
        else:
            c_scr[...] = jnp.zeros((ne, HEAD_DIM, HEAD_DIM), F32)
            n_scr[...] = jnp.zeros((ne, 1, HEAD_DIM), F32)
            m_scr[...] = jnp.zeros((ne, 1, LANES), F32)

    def emit_state(direction):
        for bi in range(bb):
            c_out[bi, 0, direction] = c_scr[bi * hp:(bi + 1) * hp]
            for hh in range(hp):
                n_out[bi, 0, direction, pl.ds(hh, 1), :] = n_scr[bi * hp + hh]
                m_out[bi, pl.ds(direction * N_HEADS + hh, 1), :] = m_scr[bi * hp + hh]

    init_state(0)

    def fwd(c):
        c_hist[c] = c_scr[...].astype(BF16)
        n_hist[c] = n_scr[...]
        m_hist[c] = m_scr[...]
        state_update(c, _heads(v_ref, c, nc, bb, hp), gate_rows(c, 1), gate_rows(c, 0), CHUNK - 1)

    _chunk_loop(nc, fwd)
    if emit_states:
        emit_state(0)

    init_state(1)

    def bwd(c):
        i_f, b_f, i_b, b_b = (gate_rows(c, kind) for kind in (0, 1, 2, 3))
        q, v = _heads(q_ref, c, nc, bb, hp), _heads(v_ref, c, nc, bb, hp)
        qf = q.astype(F32)
        a = _bdot(q, _keys_t(kt_ref, c, nc, bb, hp), _NN)
        zero_f = not has_init and _is_chunk(c, 0)
        zero_b = not has_init and _is_chunk(c, nc - 1)
        h_f = direction_out(a, q, qf, v, None if zero_f else c_hist[c], n_hist[c],
                            m_hist[c][:, :, :1], b_f, i_f, lower)
        h_b = direction_out(a, q, qf, v, None if zero_b else c_scr[...].astype(BF16),
                            n_scr[...], m_scr[...][:, :, :1], b_b, i_b, upper)
        y = _ln_rows(h_f + h_b)
        for bi in range(bb):
            r = _rows(c, bi * nc * CHUNK)
            for hh in range(hp):
                cs = slice(hh * HEAD_DIM, (hh + 1) * HEAD_DIM)
                o_ref[r, cs] = (jax.nn.sigmoid(mo_ref[r, cs].astype(F32))
                                * (y[bi * hp + hh] * gn_ref[:, cs])).astype(BF16)
        state_update(c, v, b_b, i_b, 0)

    _chunk_loop(nc, bwd, reverse=True)
    if emit_states:
        emit_state(1)


def _mlstm_call(proj, kt, gates, gn, init, *, batch, seq_len, bb, hp, emit_states):
    nc = seq_len // CHUNK
    has_init = init is not None
    assert N_HEADS % hp == 0 and (hp == N_HEADS or not emit_states)
    assert batch % bb == 0 and (bb == 1 or not has_init)
    c_spec = pl.BlockSpec((bb, 1, 2, hp, HEAD_DIM, HEAD_DIM), lambda b, hg: (b, 0, 0, hg, 0, 0))
    n_spec = pl.BlockSpec((bb, 1, 2, N_HEADS, HEAD_DIM), lambda b, hg: (b, 0, 0, 0, 0))
    in_specs = [_group_block(seq_len, hp, 3, bb), _kt_block(nc, hp, 1, bb),
                _group_block(seq_len, hp, 4, bb), _group_block(seq_len, hp, 5, bb),
                pl.BlockSpec((bb * nc, N_GATES, LANES), lambda b, hg: (b, 0, 0)),
                pl.BlockSpec((1, hp * HEAD_DIM), lambda b, hg: (0, hg))]
    args = [proj, kt, proj, proj, gates, gn]
    blocks = ([_nbytes((bb * seq_len, hp * HEAD_DIM), BF16)] * 5
              + [_nbytes((bb * nc, N_GATES, LANES), F32)])
    if has_init:
        in_specs += [c_spec, n_spec, pl.BlockSpec(memory_space=pltpu.SMEM)]
        args += list(init)
        blocks.append(_nbytes((2, hp, HEAD_DIM, HEAD_DIM), F32))
    out_specs = [pl.BlockSpec((bb * seq_len, hp * HEAD_DIM), lambda b, hg: (b, hg))]
    out_shape = [jax.ShapeDtypeStruct((batch * seq_len, GROUP_DIM), BF16)]
    if emit_states:
        out_specs += [c_spec, n_spec, pl.BlockSpec((bb, 2 * N_HEADS, LANES), lambda b, hg: (b, 0, 0))]
        out_shape += [jax.ShapeDtypeStruct((batch, 1, 2, N_HEADS, HEAD_DIM, HEAD_DIM), F32),
                      jax.ShapeDtypeStruct((batch, 1, 2, N_HEADS, HEAD_DIM), F32),
                      jax.ShapeDtypeStruct((batch, 2 * N_HEADS, LANES), F32)]
        blocks.append(_nbytes((bb, 2, hp, HEAD_DIM, HEAD_DIM), F32))
    ne = bb * hp
    scratch_defs = [((nc, ne, HEAD_DIM, HEAD_DIM), BF16), ((nc, ne, 1, HEAD_DIM), F32),
                    ((nc, ne, 1, LANES), F32), ((ne, HEAD_DIM, HEAD_DIM), F32),
                    ((ne, 1, HEAD_DIM), F32), ((ne, 1, LANES), F32)]
    scratch_bytes = sum(_nbytes(s, d) for s, d in scratch_defs) + 16 * nc * ne * HEAD_DIM * 4
    return pl.pallas_call(
        functools.partial(_mlstm_kernel, nc=nc, bb=bb, hp=hp, has_init=has_init,
                          emit_states=emit_states),
        grid=(batch // bb, N_HEADS // hp),
        in_specs=in_specs, out_specs=out_specs, out_shape=out_shape,
        scratch_shapes=[pltpu.VMEM(s, d) for s, d in scratch_defs],
        compiler_params=pltpu.CompilerParams(
            dimension_semantics=("arbitrary", "arbitrary"),
            vmem_limit_bytes=_vmem_limit(blocks, scratch_bytes)),
        name="mlstm_grid" if has_init else "mlstm_seq",
    )(*args)


def _mix_out_kernel(mr_ref, mm_ref, w_ref, x_ref, g1_ref, lg_ref, lb_ref, o_ref):
    for rc in range(o_ref.shape[0] // MIX_ROW_CHUNK):
        rows = slice(rc * MIX_ROW_CHUNK, (rc + 1) * MIX_ROW_CHUNK)
        mix = (jnp.dot(mr_ref[rows, :], w_ref[0:GROUP_DIM, :], preferred_element_type=F32)
               + jnp.dot(mm_ref[rows, :], w_ref[GROUP_DIM:, :], preferred_element_type=F32))
        o_ref[rows, :] = (_ln_rows(ALPHA * x_ref[rows, :] + g1_ref[0] * mix) * lg_ref[...]
                          + lb_ref[...])


def _mix_out_call(mix_r, mix_m, w_out_b, x2d, mod3, mod_row, ln_g, ln_b, *, tm):
    tokens = x2d.shape[0]
    blocks = [_nbytes((tm, GROUP_DIM), BF16)] * 2 + [_nbytes((D_MODEL, D_MODEL), BF16)] \
        + [_nbytes((tm, D_MODEL), F32)] * 2
    return pl.pallas_call(
        _mix_out_kernel,
        grid=(tokens // tm,),
        in_specs=[pl.BlockSpec((tm, GROUP_DIM), lambda m: (m, 0)),
                  pl.BlockSpec((tm, GROUP_DIM), lambda m: (m, 0)),
                  pl.BlockSpec((D_MODEL, D_MODEL), lambda m: (0, 0)),
                  pl.BlockSpec((tm, D_MODEL), lambda m: (m, 0)),
                  pl.BlockSpec((1, 1, D_MODEL), lambda m: (mod_row(m) * 6 + 2, 0, 0)),
                  pl.BlockSpec((1, D_MODEL), lambda m: (0, 0)),
                  pl.BlockSpec((1, D_MODEL), lambda m: (0, 0))],
        out_specs=pl.BlockSpec((tm, D_MODEL), lambda m: (m, 0)),
        out_shape=jax.ShapeDtypeStruct((tokens, D_MODEL), F32),
        compiler_params=pltpu.CompilerParams(
            dimension_semantics=("parallel",), vmem_limit_bytes=_vmem_limit(blocks)),
        name="mix_out",
    )(mix_r, mix_m, w_out_b, x2d, mod3, ln_g, ln_b)


def _ffn_kernel(x_ref, sh_ref, sc_ref, g2_ref, wu_ref, wg_ref, cv_ref, wd_ref, lg_ref, lb_ref,
                o_ref, h_scr, acc_scr, *, period):
    j = pl.program_id(1)

    @pl.when(j == 0)
    def _():
        h = _ln_rows(x_ref[...]) * (1.0 + sc_ref[0]) + sh_ref[0]
        h_scr[...] = h.astype(BF16)
        acc_scr[...] = jnp.zeros_like(acc_scr)

    hb = h_scr[...]
    u = jnp.dot(hb, wu_ref[...], preferred_element_type=F32)
    g = jnp.dot(hb, wg_ref[...], preferred_element_type=F32)
    hid = _silu(_conv3_rows(u, cv_ref[0], period)) * g
    lane = lax.broadcasted_iota(jnp.int32, (1, FF_TN), 1)
    hid = jnp.where((j == FF_NJ - 1) & (lane < FF_OVERLAP), 0.0, hid).astype(BF16)
    acc_scr[...] += jnp.dot(hid, wd_ref[...], preferred_element_type=F32)

    @pl.when(j == FF_NJ - 1)
    def _():
        y = ALPHA * x_ref[...] + g2_ref[0] * acc_scr[...]
        o_ref[...] = _ln_rows(y) * lg_ref[...] + lb_ref[...]


def _ff_offset(j):
    return pl.multiple_of(jnp.minimum(j * FF_TN, D_FF - FF_TN), LANES)


def _ffn_call(x1, mod3, mod_row, w_up_b, conv_tiles, w_down_b, ln_g, ln_b, *, period, tm):
    tokens = x1.shape[0]
    up_tile = (pl.Element(D_MODEL), pl.Element(FF_TN))
    blocks = [_nbytes((tm, D_MODEL), F32)] * 2 + [_nbytes((D_MODEL, FF_TN), BF16)] * 3 \
        + [_nbytes((8, FF_TN), F32)]
    scratch_bytes = _nbytes((tm, D_MODEL), BF16) + _nbytes((tm, D_MODEL), F32)
    return pl.pallas_call(
        functools.partial(_ffn_kernel, period=period),
        grid=(tokens // tm, FF_NJ),
        in_specs=[pl.BlockSpec((tm, D_MODEL), lambda m, j: (m, 0)),
                  pl.BlockSpec((1, 1, D_MODEL), lambda m, j: (mod_row(m) * 6 + 3, 0, 0)),
                  pl.BlockSpec((1, 1, D_MODEL), lambda m, j: (mod_row(m) * 6 + 4, 0, 0)),
                  pl.BlockSpec((1, 1, D_MODEL), lambda m, j: (mod_row(m) * 6 + 5, 0, 0)),
                  pl.BlockSpec(up_tile, lambda m, j: (0, _ff_offset(j))),
                  pl.BlockSpec(up_tile, lambda m, j: (0, pl.multiple_of(D_FF + _ff_offset(j), LANES))),
                  pl.BlockSpec((1, 3, FF_TN), lambda m, j: (j, 0, 0)),
                  pl.BlockSpec((pl.Element(FF_TN), pl.Element(D_MODEL)),
                               lambda m, j: (_ff_offset(j), 0)),
                  pl.BlockSpec((1, D_MODEL), lambda m, j: (0, 0)),
                  pl.BlockSpec((1, D_MODEL), lambda m, j: (0, 0))],
        out_specs=pl.BlockSpec((tm, D_MODEL), lambda m, j: (m, 0)),
        out_shape=jax.ShapeDtypeStruct((tokens, D_MODEL), F32),
        scratch_shapes=[pltpu.VMEM((tm, D_MODEL), BF16), pltpu.VMEM((tm, D_MODEL), F32)],
        compiler_params=pltpu.CompilerParams(
            dimension_semantics=("parallel", "arbitrary"),
            vmem_limit_bytes=_vmem_limit(blocks, scratch_bytes)),
        name="ffn",
    )(x1, mod3, mod3, mod3, w_up_b, w_up_b, conv_tiles, w_down_b, ln_g, ln_b)


def _rope_tables(seq_len):
    quarter = HEAD_DIM // 4
    t = jnp.arange(seq_len)
    row = (t // GRID_W).astype(F32)
    col = (t % GRID_W).astype(F32)
    inv = ROPE_BASE ** (-jnp.arange(quarter, dtype=F32) / quarter)
    ang_r, ang_c = row[:, None] * inv, col[:, None] * inv
    cos_t = jnp.concatenate([jnp.cos(ang_r)] * 2 + [jnp.cos(ang_c)] * 2, axis=1)
    sin_t = jnp.concatenate([-jnp.sin(ang_r), jnp.sin(ang_r), -jnp.sin(ang_c), jnp.sin(ang_c)], axis=1)
    return cos_t, sin_t


def kernel(x_prompt, x_sample, state_ret, state_mlstm_C, state_mlstm_n, state_mlstm_m, c, c_ctx,
           w_mod, b_mod, w_in, b_gate, conv_qk, ret_theta, gn_ret, gn_mlstm, w_out,
           ln1_g, ln1_b, w_up, conv_ff, w_down, ln2_g, ln2_b):
    bp, lp, _ = x_prompt.shape
    bs, ls, _ = x_sample.shape
    layer = 0
    tm_mix, tm = 1024, 512

    cvec = jnp.concatenate([c_ctx[None], c, jnp.zeros((8 - 1 - bs, D_MODEL), F32)], axis=0)
    mod = _mod_call(cvec, w_mod[layer], b_mod[layer][None])
    mod3 = mod.reshape(8 * 6, 1, D_MODEL)

    def row_prompt(tile):
        return lambda m: 0

    def row_sample(tile):
        return lambda m: 1 + (m * tile) // ls

    w_l = w_in[layer].T
    w_gate = w_l[N_PROJ:]
    b_gate_col = b_gate[layer].reshape(N_GATES, 1)
    ff_offsets = [min(j * FF_TN, D_FF - FF_TN) for j in range(FF_NJ)]
    conv_tiles = jnp.stack([conv_ff[layer][:, o:o + FF_TN] for o in ff_offsets])
    gn_r = gn_ret[layer].reshape(1, GROUP_DIM)
    gn_m = gn_mlstm[layer].reshape(1, GROUP_DIM)
    theta = ret_theta[layer]
    ln1 = (ln1_g[layer][None], ln1_b[layer][None])
    ln2 = (ln2_g[layer][None], ln2_b[layer][None])

    def mixer(x, mod_row, rope, period, bb, hp, ret_init, mlstm_init, emit_states, stage=()):
        batch, seq_len, _ = x.shape
        x2d = x.reshape(batch * seq_len, D_MODEL)
        h2d, gates = _ln_mod_call(x2d, mod3, mod_row(tm_mix), w_gate, b_gate_col, tm=tm_mix)
        proj, kt, *staged = _mix_in_call(h2d, w_l, conv_qk[layer], rope, seq_len=seq_len,
                                         period=period, tm=tm_mix, stage=stage)
        ret = _ret_call(proj, kt, theta, gn_r, ret_init, batch=batch, seq_len=seq_len, bb=bb,
                        hp=hp, emit_states=emit_states)
        mls = _mlstm_call(proj, kt, gates, gn_m, mlstm_init, batch=batch, seq_len=seq_len, bb=bb,
                          hp=hp, emit_states=emit_states)
        return x2d, ret, mls, staged

    def channel_mix(x2d, shape, ret, mls, mod_row, period, weights):
        w_up_b, w_down_b, w_out_b = weights
        x1 = _mix_out_call(ret[0], mls[0], w_out_b, x2d, mod3, mod_row(tm), *ln1, tm=tm)
        y = _ffn_call(x1, mod3, mod_row(tm), w_up_b, conv_tiles, w_down_b, *ln2, period=period,
                      tm=tm)
        return y.reshape(shape)

    xp, ret_p, mls_p, weights_b = mixer(x_prompt, row_prompt, None, lp, 2, N_HEADS, None, None,
                                        True, stage=(w_up[layer], w_down[layer], w_out[layer]))
    m0 = state_mlstm_m.reshape(-1)
    xs, ret_s, mls_s, _ = mixer(x_sample, row_sample, _rope_tables(ls), GRID_W, 1, 2, state_ret,
                                (state_mlstm_C, state_mlstm_n, m0), False)
    y_prompt = channel_mix(xp, x_prompt.shape, ret_p, mls_p, row_prompt, lp, weights_b)
    y_sample = channel_mix(xs, x_sample.shape, ret_s, mls_s, row_sample, GRID_W, weights_b)
    ret_states, mls_states = ret_p[1:], mls_p[1:]

    new_ret = ret_states[0]
    new_c, new_n, new_m = mls_states
    new_m = new_m[:, :, 0].reshape(bp, 1, 2, N_HEADS)
    return (y_prompt, y_sample, new_ret, new_c, new_n, new_m)
```

```python
import functools

import jax
import jax.numpy as jnp
from jax import lax
from jax.experimental import pallas as pl
from jax.experimental.pallas import tpu as pltpu

F32 = jnp.float32
BF16 = jnp.bfloat16

D_MODEL = 2048
GRID_W = 64
HEAD_DIM = 256
N_HEADS = 4
GROUP_DIM = N_HEADS * HEAD_DIM
N_PROJ = 8 * GROUP_DIM
N_GATES = 4 * N_HEADS
D_FF = 5504
CHUNK = 128
ROPE_BASE = 10000.0
ALPHA = 2.0 ** 0.25
LN_EPS = 1e-6
K_SCALE = HEAD_DIM ** -0.5

LANES = 128
BF16_ROWS = 16
V7X_VMEM_LIMIT_CAP = 60 * 1024 * 1024
VMEM_TEMP_ALLOWANCE = 16 * 1024 * 1024

MIX_ROW_CHUNK = 512
N_STORED = 6 * GROUP_DIM
FF_TN = 512
FF_NJ = -(-D_FF // FF_TN)
FF_OVERLAP = FF_NJ * FF_TN - D_FF


def _vmem_limit(block_bytes, scratch_bytes=0):
    est = 2 * sum(block_bytes) + scratch_bytes + VMEM_TEMP_ALLOWANCE
    return int(min(V7X_VMEM_LIMIT_CAP, est))


def _nbytes(shape, dtype):
    n = 1
    for s in shape:
        n *= s
    return n * jnp.dtype(dtype).itemsize


def _ln_rows(x):
    mu = jnp.mean(x, axis=-1, keepdims=True)
    xc = x - mu
    var = jnp.mean(xc * xc, axis=-1, keepdims=True)
    return xc * lax.rsqrt(var + LN_EPS)


def _silu(x):
    return x * jax.nn.sigmoid(x)


def _log_sigmoid(x):
    return jnp.minimum(x, 0.0) - jnp.log1p(jnp.exp(-jnp.abs(x)))


def _conv3_rows(u, w, period):
    rows = u.shape[0]
    t = lax.broadcasted_iota(jnp.int32, (rows, 1), 0) & (period - 1)
    prev = jnp.where(t == 0, 0.0, pltpu.roll(u, 1, 0))
    nxt = jnp.where(t == period - 1, 0.0, pltpu.roll(u, rows - 1, 0))
    return prev * w[0:1] + u * w[1:2] + nxt * w[2:3]


def _cumsum_lanes(x, lane, reverse):
    s = 1
    while s < LANES:
        if reverse:
            x = x + jnp.where(lane < LANES - s, pltpu.roll(x, LANES - s, 1), 0.0)
        else:
            x = x + jnp.where(lane >= s, pltpu.roll(x, s, 1), 0.0)
        s *= 2
    return x


def _mod_kernel(c_ref, w_ref, b_ref, o_ref):
    a = _silu(c_ref[...]).astype(BF16)
    o_ref[...] = jnp.dot(a, w_ref[...].astype(BF16), preferred_element_type=F32) + b_ref[...]


def _mod_call(cvec, w_mod, b_mod):
    rows, n = cvec.shape[0], w_mod.shape[1]
    tn = 1024
    blocks = [_nbytes((rows, D_MODEL), F32), _nbytes((D_MODEL, tn), F32),
              _nbytes((8, tn), F32), _nbytes((rows, tn), F32)]
    return pl.pallas_call(
        _mod_kernel,
        grid=(n // tn,),
        in_specs=[pl.BlockSpec((rows, D_MODEL), lambda j: (0, 0)),
                  pl.BlockSpec((D_MODEL, tn), lambda j: (0, j)),
                  pl.BlockSpec((1, tn), lambda j: (0, j))],
        out_specs=pl.BlockSpec((rows, tn), lambda j: (0, j)),
        out_shape=jax.ShapeDtypeStruct((rows, n), F32),
        compiler_params=pltpu.CompilerParams(
            dimension_semantics=("arbitrary",), vmem_limit_bytes=_vmem_limit(blocks)),
        name="mod",
    )(cvec, w_mod, b_mod)


def _ln_mod_kernel(x_ref, sh_ref, sc_ref, wg_ref, bg_ref, h_ref, gates_ref, *, tm):
    h = _ln_rows(x_ref[...]) * (1.0 + sc_ref[0]) + sh_ref[0]
    hb = h.astype(BF16)
    h_ref[...] = hb
    g = lax.dot_general(wg_ref[...].astype(BF16), hb, (((1,), (1,)), ((), ())),
                        preferred_element_type=F32) + bg_ref[...]
    row = lax.broadcasted_iota(jnp.int32, (N_GATES, LANES), 0)
    lane = lax.broadcasted_iota(jnp.int32, (N_GATES, LANES), 1)
    kind = lax.shift_right_logical(row, 2)
    for s in range(tm // LANES):
        gs = g[:, s * LANES:(s + 1) * LANES]
        ls = _log_sigmoid(gs)
        gates_ref[s] = jnp.where(kind == 1, _cumsum_lanes(ls, lane, False),
                                 jnp.where(kind == 3, _cumsum_lanes(ls, lane, True), gs))


def _ln_mod_call(x2d, mod3, mod_row, w_gate, b_gate_col, *, tm):
    tokens = x2d.shape[0]
    slabs = tm // LANES
    blocks = [_nbytes((tm, D_MODEL), F32), _nbytes((tm, D_MODEL), BF16),
              _nbytes((N_GATES, D_MODEL), F32), _nbytes((slabs, N_GATES, LANES), F32)]
    return pl.pallas_call(
        functools.partial(_ln_mod_kernel, tm=tm),
        grid=(tokens // tm,),
        in_specs=[pl.BlockSpec((tm, D_MODEL), lambda m: (m, 0)),
                  pl.BlockSpec((1, 1, D_MODEL), lambda m: (mod_row(m) * 6 + 0, 0, 0)),
                  pl.BlockSpec((1, 1, D_MODEL), lambda m: (mod_row(m) * 6 + 1, 0, 0)),
                  pl.BlockSpec((N_GATES, D_MODEL), lambda m: (0, 0)),
                  pl.BlockSpec((N_GATES, 1), lambda m: (0, 0))],
        out_specs=[pl.BlockSpec((tm, D_MODEL), lambda m: (m, 0)),
                   pl.BlockSpec((slabs, N_GATES, LANES), lambda m: (m, 0, 0))],
        out_shape=[jax.ShapeDtypeStruct((tokens, D_MODEL), BF16),
                   jax.ShapeDtypeStruct((tokens // LANES, N_GATES, LANES), F32)],
        compiler_params=pltpu.CompilerParams(
            dimension_semantics=("parallel",), vmem_limit_bytes=_vmem_limit(blocks)),
        name="ln_mod",
    )(x2d, mod3, mod3, w_gate, b_gate_col)


def _mix_in_kernel(*refs, tm, period, grid_mode, n_staged):
    refs = list(refs)
    h_ref, w_ref, cv_ref = refs[:3]
    pos = 3
    if grid_mode:
        cos_ref, sin_ref = refs[pos:pos + 2]
        pos += 2
    staged_in = refs[pos:pos + n_staged]
    pos += n_staged
    proj_ref, kt_ref = refs[pos:pos + 2]
    pos += 2
    staged_out = refs[pos:pos + n_staged]
    wb_scr = refs[pos + n_staged]
    group = pl.program_id(0)

    @pl.when(pl.program_id(1) == 0)
    def _():
        wb_scr[...] = w_ref[...].astype(BF16)

    def chunks(n_rows=MIX_ROW_CHUNK):
        for src, dst in zip(staged_in, staged_out):
            dst[...] = src[...].astype(BF16)
        for rc in range(tm // n_rows):
            rows = slice(rc * n_rows, (rc + 1) * n_rows)
            yield rc, rows, lax.dot_general(h_ref[rows, :], wb_scr[...], (((1,), (1,)), ((), ())),
                                            preferred_element_type=F32)

    def rope(a, rows):
        out = []
        for s in range(GROUP_DIM // LANES):
            xs = a[:, s * LANES:(s + 1) * LANES]
            t = (s % 2) * LANES
            out.append(xs * cos_ref[rows, t:t + LANES]
                       + pltpu.roll(xs, LANES // 2, 1) * sin_ref[rows, t:t + LANES])
        return jnp.concatenate(out, axis=1)

    def store_transposed(rc, y):
        yt = y.T.astype(BF16)
        per = y.shape[0] // LANES
        for s in range(per):
            kt_ref[rc * per + s] = yt[:, s * LANES:(s + 1) * LANES]

    plain = (group == 2) | (group == 3) | (group == 6) | (group == 7)
    if not grid_mode:
        plain = plain | (group == 0)

    @pl.when(plain)
    def _():
        for _, rows, acc in chunks(tm):
            proj_ref[rows, :] = acc.astype(BF16)

    if grid_mode:
        @pl.when(group == 0)
        def _():
            for _, rows, acc in chunks():
                proj_ref[rows, :] = rope(acc, rows).astype(BF16)

    @pl.when(group == 1)
    def _():
        for rc, rows, acc in chunks():
            k = acc * K_SCALE
            store_transposed(rc, rope(k, rows) if grid_mode else k)

    @pl.when(group == 4)
    def _():
        for _, rows, acc in chunks():
            proj_ref[rows, :] = _silu(_conv3_rows(acc, cv_ref[...], period)).astype(BF16)

    @pl.when(group == 5)
    def _():
        for rc, _, acc in chunks():
            store_transposed(rc, _silu(_conv3_rows(acc, cv_ref[...], period)) * K_SCALE)


def _staging_specs(weights, n_groups, n_tiles):
    specs, shapes, block_bytes = [], [], []
    for w in weights:
        rows, cols = w.shape
        slab = BF16_ROWS
        while rows % slab or rows // slab > n_groups * n_tiles:
            slab += BF16_ROWS
        n_slabs = rows // slab
        specs.append(pl.BlockSpec(
            (slab, cols),
            lambda g, m, n_slabs=n_slabs: (jnp.minimum(g * n_tiles + m, n_slabs - 1), 0)))
        shapes.append(jax.ShapeDtypeStruct((rows, cols), BF16))
        block_bytes.append(_nbytes((slab, cols), F32) + _nbytes((slab, cols), BF16))
    return specs, shapes, block_bytes


def _mix_in_call(h2d, w_in_t, conv_qk, rope, *, seq_len, period, tm, stage=()):
    tokens = h2d.shape[0]
    grid_mode = rope is not None
    tiles_per_seq = seq_len // tm
    slabs = tm // LANES
    last = tokens // tm - 1
    n_groups = N_PROJ // GROUP_DIM
    stage_specs, stage_shapes, stage_bytes = _staging_specs(stage, n_groups, tokens // tm)

    def is_key(g):
        return (g == 1) | (g == 5)

    def proj_index(g, m):
        col = g - (g >= 1).astype(jnp.int32) - (g >= 5).astype(jnp.int32)
        return jnp.where(is_key(g), last, m), col

    def kt_index(g, m):
        row = jnp.where(g == 0, 0, jnp.where(is_key(g), m, last))
        return row, (g >= 5).astype(jnp.int32), 0

    in_specs = [
        pl.BlockSpec((tm, D_MODEL), lambda g, m: (m, 0)),
        pl.BlockSpec((GROUP_DIM, D_MODEL), lambda g, m: (g, 0)),
        pl.BlockSpec((3, GROUP_DIM), lambda g, m: (0, jnp.clip(g - 4, 0, 1))),
    ]
    args = [h2d, w_in_t, conv_qk]
    blocks = [_nbytes((tm, D_MODEL), BF16), _nbytes((GROUP_DIM, D_MODEL), F32),
              _nbytes((tm, GROUP_DIM), BF16) * 2, _nbytes((8, GROUP_DIM), F32)]
    if grid_mode:
        in_specs += [pl.BlockSpec((tm, HEAD_DIM),
                                  lambda g, m: (jnp.where(g <= 1, m % tiles_per_seq, 0), 0))] * 2
        args += list(rope)
        blocks += [_nbytes((tm, HEAD_DIM), F32)] * 2
    in_specs += stage_specs
    args += list(stage)
    blocks += stage_bytes
    scratch_defs = [((GROUP_DIM, D_MODEL), BF16)]

    return pl.pallas_call(
        functools.partial(_mix_in_kernel, tm=tm, period=period, grid_mode=grid_mode,
                          n_staged=len(stage)),
        grid=(n_groups, tokens // tm),
        in_specs=in_specs,
        out_specs=[pl.BlockSpec((tm, GROUP_DIM), lambda g, m: proj_index(g, m)),
                   pl.BlockSpec((slabs, GROUP_DIM, LANES), lambda g, m: kt_index(g, m))]
        + stage_specs,
        out_shape=[jax.ShapeDtypeStruct((tokens, N_STORED), BF16),
                   jax.ShapeDtypeStruct((tokens // LANES, 2 * GROUP_DIM, LANES), BF16)]
        + stage_shapes,
        scratch_shapes=[pltpu.VMEM(s, d) for s, d in scratch_defs],
        compiler_params=pltpu.CompilerParams(
            dimension_semantics=("arbitrary", "arbitrary"),
            vmem_limit_bytes=_vmem_limit(blocks, sum(_nbytes(s, d) for s, d in scratch_defs))),
        name="mix_in_grid" if grid_mode else "mix_in_seq",
    )(*args)


_NN = (((2,), (1,)), ((0,), (0,)))
_NT = (((2,), (2,)), ((0,), (0,)))


def _bdot(x, y, dims):
    return lax.dot_general(x, y, dims, preferred_element_type=F32)


def _chunk_loop(nc, body, reverse=False):
    if nc <= 2:
        for i in range(nc):
            body(nc - 1 - i if reverse else i)
    else:
        def step(i, carry):
            body(nc - 1 - i if reverse else i)
            return carry
        lax.fori_loop(0, nc, step, 0)


def _rows(c, base=0):
    start = base + c * CHUNK
    if not isinstance(start, int):
        start = pl.multiple_of(start, CHUNK)
    return pl.ds(start, CHUNK)


def _is_chunk(c, value):
    return isinstance(c, int) and c == value


class _Chunks:
    def __init__(self, nc, bb, hp, streamed):
        self.nc, self.bb, self.hp, self.streamed = nc, bb, hp, streamed

    def rows(self, ref, bi, c, cols):
        return ref[bi, :, cols] if self.streamed else ref[_rows(c, bi * self.nc * CHUNK), cols]

    def store_rows(self, ref, bi, c, cols, value):
        if self.streamed:
            ref[bi, :, cols] = value
        else:
            ref[_rows(c, bi * self.nc * CHUNK), cols] = value

    def per_chunk(self, ref, bi, c):
        return ref[bi, 0] if self.streamed else ref[bi * self.nc + c]

    def heads(self, ref, c):
        return jnp.stack([self.rows(ref, bi, c, slice(hh * HEAD_DIM, (hh + 1) * HEAD_DIM))
                          for bi in range(self.bb) for hh in range(self.hp)])

    def keys_t(self, kt_ref, c):
        parts = [self.per_chunk(kt_ref, bi, c).reshape(self.hp, HEAD_DIM, CHUNK)
                 for bi in range(self.bb)]
        return parts[0] if self.bb == 1 else jnp.concatenate(parts, axis=0)

    def gate_row(self, g_ref, bi, c, row):
        if self.streamed:
            return g_ref[bi, 0, pl.ds(row, 1), :]
        return g_ref[bi * self.nc + c, pl.ds(row, 1), :]


def _run_scan(nc, streamed, init_state, fwd, bwd, emit_state):
    if not streamed:
        init_state(0)
        _chunk_loop(nc, fwd)
        if emit_state is not None:
            emit_state(0)
        init_state(1)
        _chunk_loop(nc, bwd, reverse=True)
        if emit_state is not None:
            emit_state(1)
        return
    assert emit_state is None
    s = pl.program_id(0)

    @pl.when(s == 0)
    def _():
        init_state(0)

    @pl.when(s == nc)
    def _():
        init_state(1)

    @pl.when(s < nc)
    def _():
        fwd(s)

    @pl.when(s >= nc)
    def _():
        bwd(2 * nc - 1 - s)


def _per_seq(x, bb):
    return x if bb == 1 else jnp.concatenate([x] * bb, axis=0)


def _group_block(seq_len, hp, group, bb):
    per = N_HEADS // hp
    return pl.BlockSpec((bb * seq_len, hp * HEAD_DIM), lambda b, hg: (b, group * per + hg))


def _kt_block(nc, hp, group, bb):
    per = N_HEADS // hp
    return pl.BlockSpec((bb * nc, hp * HEAD_DIM, LANES), lambda b, hg: (b, group * per + hg, 0))


def _step_chunk(s, nc):
    return jnp.where(s < nc, s, 2 * nc - 1 - s)


def _stream_rows_block(bb, nc, group):
    return pl.BlockSpec((bb, CHUNK, GROUP_DIM), lambda s: (0, _step_chunk(s, nc), group))


def _stream_out_block(bb, nc):
    return pl.BlockSpec((bb, CHUNK, GROUP_DIM), lambda s: (0, jnp.minimum(2 * nc - 1 - s, nc - 1), 0))


def _stream_chunk_block(bb, nc, tail, tail_index):
    return pl.BlockSpec((bb, 1) + tail, lambda s: (0, _step_chunk(s, nc)) + tail_index)


def _ret_kernel(*refs, nc, bb, hp, has_init, emit_states, streamed):
    refs = list(refs)
    theta_ref, q_ref, kt_ref, v_ref, rg_ref, gn_ref = refs[:6]
    pos = 6
    s0_ref = None
    if has_init:
        s0_ref = refs[pos]
        pos += 1
    o_ref = refs[pos]
    pos += 1
    st_ref = None
    if emit_states:
        st_ref = refs[pos]
        pos += 1
    hist, s_scr, decay_scr, qdec_scr, kdec_scr, cdec_scr = refs[pos:pos + 6]
    ck = _Chunks(nc, bb, hp, streamed)
    hg = 0 if streamed else pl.program_id(1)
    hsel = pl.ds(hg * hp, hp)

    @pl.when(pl.program_id(0) == 0)
    def _():
        ii = lax.broadcasted_iota(jnp.int32, (CHUNK, CHUNK), 0)
        jj = lax.broadcasted_iota(jnp.int32, (CHUNK, CHUNK), 1)
        d = (ii - jj).astype(F32)
        p_col = lax.broadcasted_iota(jnp.int32, (CHUNK, HEAD_DIM), 0).astype(F32)
        p_row = lax.broadcasted_iota(jnp.int32, (1, LANES), 1).astype(F32)
        for hh in range(hp):
            h = hg * hp + hh
            lg_f = _log_sigmoid(jnp.full((1, LANES), theta_ref[0, h], F32))
            lg_b = _log_sigmoid(jnp.full((1, LANES), theta_ref[1, h], F32))
            decay_scr[h] = (jnp.where(d >= 0, jnp.exp(lg_f * jnp.maximum(d, 0.0)), 0.0)
                            + jnp.where(d <= 0, jnp.exp(lg_b * jnp.maximum(-d, 0.0)), 0.0))
            qdec_scr[0, h] = jnp.exp(lg_f[:, :1] * (p_col + 1.0))
            qdec_scr[1, h] = jnp.exp(lg_b[:, :1] * (CHUNK - p_col))
            kdec_scr[0, h] = jnp.exp(lg_f * (CHUNK - 1.0 - p_row))
            kdec_scr[1, h] = jnp.exp(lg_b * p_row)
            cdec_scr[0, h] = jnp.exp(lg_f * float(CHUNK))
            cdec_scr[1, h] = jnp.exp(lg_b * float(CHUNK))

    def table(ref, *lead):
        return _per_seq(ref[(*lead, hsel)], bb)

    def kv_update(c, v, direction):
        kd = (ck.keys_t(kt_ref, c).astype(F32)
              * table(kdec_scr, direction)).astype(BF16)
        s_scr[...] = s_scr[...] * table(cdec_scr, direction)[:, :, :1] + _bdot(kd, v, _NN)

    def init_state(direction):
        if has_init:
            for bi in range(bb):
                s_scr[bi * hp:(bi + 1) * hp] = s0_ref[bi, 0, direction]
        else:
            s_scr[...] = jnp.zeros((bb * hp, HEAD_DIM, HEAD_DIM), F32)

    def emit_state(direction):
        for bi in range(bb):
            st_ref[bi, 0, direction] = s_scr[bi * hp:(bi + 1) * hp]

    def fwd(c):
        hist[c] = s_scr[...].astype(BF16)
        kv_update(c, ck.heads(v_ref, c), 0)

    def bwd(c):
        q, v = ck.heads(q_ref, c), ck.heads(v_ref, c)
        kt = ck.keys_t(kt_ref, c)
        att = (_bdot(q, kt, _NN) * table(decay_scr)).astype(BF16)
        o = _bdot(att, v, _NN)
        if has_init or not _is_chunk(c, 0):
            o = o + _bdot(q, hist[c], _NN) * table(qdec_scr, 0)
        if has_init or not _is_chunk(c, nc - 1):
            o = o + _bdot(q, s_scr[...].astype(BF16), _NN) * table(qdec_scr, 1)
        y = _ln_rows(o)
        for bi in range(bb):
            for hh in range(hp):
                cs = slice(hh * HEAD_DIM, (hh + 1) * HEAD_DIM)
                gate = _silu(ck.rows(rg_ref, bi, c, cs).astype(F32))
                ck.store_rows(o_ref, bi, c, cs,
                              (y[bi * hp + hh] * gn_ref[:, cs] * gate).astype(BF16))
        kv_update(c, v, 1)

    _run_scan(nc, streamed, init_state, fwd, bwd, emit_state if emit_states else None)


def _ret_call(proj, kt, theta, gn, s0, *, batch, seq_len, bb, hp, emit_states, streamed=False):
    nc = seq_len // CHUNK
    has_init = s0 is not None
    assert N_HEADS % hp == 0 and batch % bb == 0
    if streamed:
        assert bb == batch and hp == N_HEADS and not emit_states
        proj = proj.reshape(batch, seq_len, N_STORED)
        kt = kt.reshape(batch, nc, 2 * GROUP_DIM, LANES)
        state_spec = pl.BlockSpec((bb, 1, 2, hp, HEAD_DIM, HEAD_DIM), lambda s: (0,) * 6)
        in_specs = [pl.BlockSpec(memory_space=pltpu.SMEM),
                    _stream_rows_block(bb, nc, 0),
                    _stream_chunk_block(bb, nc, (GROUP_DIM, LANES), (0, 0)),
                    _stream_rows_block(bb, nc, 1), _stream_rows_block(bb, nc, 2),
                    pl.BlockSpec((1, GROUP_DIM), lambda s: (0, 0))]
        blocks = [_nbytes((bb, CHUNK, GROUP_DIM), BF16)] * 5
        out_specs = [_stream_out_block(bb, nc)]
        out_shape = [jax.ShapeDtypeStruct((batch, seq_len, GROUP_DIM), BF16)]
        grid, semantics = (2 * nc,), ("arbitrary",)
    else:
        state_spec = pl.BlockSpec((bb, 1, 2, hp, HEAD_DIM, HEAD_DIM),
                                  lambda b, hg: (b, 0, 0, hg, 0, 0))
        in_specs = [pl.BlockSpec(memory_space=pltpu.SMEM),
                    _group_block(seq_len, hp, 0, bb), _kt_block(nc, hp, 0, bb),
                    _group_block(seq_len, hp, 1, bb), _group_block(seq_len, hp, 2, bb),
                    pl.BlockSpec((1, hp * HEAD_DIM), lambda b, hg: (0, hg))]
        blocks = [_nbytes((bb * seq_len, hp * HEAD_DIM), BF16)] * 5
        out_specs = [pl.BlockSpec((bb * seq_len, hp * HEAD_DIM), lambda b, hg: (b, hg))]
        out_shape = [jax.ShapeDtypeStruct((batch * seq_len, GROUP_DIM), BF16)]
        grid, semantics = (batch // bb, N_HEADS // hp), ("arbitrary", "arbitrary")
    args = [theta, proj, kt, proj, proj, gn]
    if has_init:
        in_specs.append(state_spec)
        args.append(s0)
        blocks.append(_nbytes((bb, 2, hp, HEAD_DIM, HEAD_DIM), F32))
    if emit_states:
        out_specs.append(state_spec)
        out_shape.append(jax.ShapeDtypeStruct((batch, 1, 2, N_HEADS, HEAD_DIM, HEAD_DIM), F32))
        blocks.append(_nbytes((bb, 2, hp, HEAD_DIM, HEAD_DIM), F32))
    scratch_defs = [((nc, bb * hp, HEAD_DIM, HEAD_DIM), BF16), ((bb * hp, HEAD_DIM, HEAD_DIM), F32),
                    ((N_HEADS, CHUNK, CHUNK), F32), ((2, N_HEADS, CHUNK, HEAD_DIM), F32),
                    ((2, N_HEADS, 1, LANES), F32), ((2, N_HEADS, 1, LANES), F32)]
    outs = pl.pallas_call(
        functools.partial(_ret_kernel, nc=nc, bb=bb, hp=hp, has_init=has_init,
                          emit_states=emit_states, streamed=streamed),
        grid=grid,
        in_specs=in_specs, out_specs=out_specs, out_shape=out_shape,
        scratch_shapes=[pltpu.VMEM(s, d) for s, d in scratch_defs],
        compiler_params=pltpu.CompilerParams(
            dimension_semantics=semantics,
            vmem_limit_bytes=_vmem_limit(blocks, sum(_nbytes(s, d) for s, d in scratch_defs))),
        name="ret_grid" if has_init else "ret_seq",
    )(*args)
    if streamed:
        outs = [outs[0].reshape(batch * seq_len, GROUP_DIM)] + list(outs[1:])
    return outs


def _mlstm_kernel(*refs, nc, bb, hp, has_init, emit_states, streamed):
    refs = list(refs)
    q_ref, kt_ref, v_ref, mo_ref, g_ref, gn_ref = refs[:6]
    pos = 6
    c0_ref = n0_ref = m0_ref = None
    if has_init:
        c0_ref, n0_ref, m0_ref = refs[pos:pos + 3]
        pos += 3
    o_ref = refs[pos]
    pos += 1
    c_out = n_out = m_out = None
    if emit_states:
        c_out, n_out, m_out = refs[pos:pos + 3]
        pos += 3
    c_hist, n_hist, m_hist, c_scr, n_scr, m_scr = refs[pos:pos + 6]

    ck = _Chunks(nc, bb, hp, streamed)
    b = 0 if streamed else pl.program_id(0)
    hg = 0 if streamed else pl.program_id(1)
    ii = lax.broadcasted_iota(jnp.int32, (CHUNK, CHUNK), 0)
    jj = lax.broadcasted_iota(jnp.int32, (CHUNK, CHUNK), 1)
    eye = ii == jj
    lower = ii >= jj
    upper = ii <= jj

    ne = bb * hp

    def col(row):
        return jnp.sum(jnp.where(eye, row, 0.0), axis=2, keepdims=True)

    def gate_rows(c, kind):
        base = kind * N_HEADS + hg * hp
        return jnp.stack([ck.gate_row(g_ref, bi, c, base + hh)
                          for bi in range(bb) for hh in range(hp)])

    def direction_out(a, q, qf, v, c_b, n_row, m, b_row, i_row, mask):
        b_col = col(b_row)
        dm = jnp.where(mask, b_col - b_row + i_row, -jnp.inf)
        inter = b_col + m
        mt = jnp.maximum(inter, jnp.max(dm, axis=2, keepdims=True))
        w = jnp.exp(dm - mt)
        sp = jnp.exp(inter - mt)
        s = a * w
        num = _bdot(s.astype(BF16), v, _NN)
        den = jnp.sum(s, axis=2, keepdims=True)
        if c_b is not None:
            num = num + _bdot(q, c_b, _NN) * sp
            den = den + jnp.sum(qf * n_row, axis=2, keepdims=True) * sp
        return num / jnp.maximum(jnp.abs(den), jnp.exp(-mt))

    def state_update(c, v, b_row, i_row, last):
        m = m_scr[...][:, :, :1]
        b_last = b_row[:, :, last:last + 1]
        g = b_last - b_row + i_row
        m_new = jnp.maximum(b_last + m, jnp.max(g, axis=2, keepdims=True))
        wk = jnp.exp(g - m_new)
        sc = jnp.exp(b_last + m - m_new)
        kt = ck.keys_t(kt_ref, c)
        kw = (kt.astype(F32) * wk).astype(BF16)
        c_scr[...] = c_scr[...] * sc + _bdot(kw, v, _NN)
        wk8 = jnp.broadcast_to(wk, (ne, 8, CHUNK)).astype(BF16)
        n_scr[...] = n_scr[...] * sc + _bdot(wk8, kt, _NT)[:, :1, :]
        m_scr[...] = jnp.broadcast_to(m_new, (ne, 1, LANES))

    def init_state(direction):
        if has_init:
            for bi in range(bb):
                c_scr[bi * hp:(bi + 1) * hp] = c0_ref[bi, 0, direction]
            n_scr[...] = jnp.stack([n0_ref[bi, 0, direction, pl.ds(hg * hp + hh, 1), :]
                                    for bi in range(bb) for hh in range(hp)])
            m_scr[...] = jnp.stack([
                jnp.full((1, LANES),
                         m0_ref[((b * bb + bi) * 2 + direction) * N_HEADS + hg * hp + hh], F32)
                for bi in range(bb) for hh in range(hp)])
        else:
            c_scr[...] = jnp.zeros((ne, HEAD_DIM, HEAD_DIM), F32)
            n_scr[...] = jnp.zeros((ne, 1, HEAD_DIM), F32)
            m_scr[...] = jnp.zeros((ne, 1, LANES), F32)

    def emit_state(direction):
        for bi in range(bb):
            c_out[bi, 0, direction] = c_scr[bi * hp:(bi + 1) * hp]
            for hh in range(hp):
                n_out[bi, 0, direction, pl.ds(hh, 1), :] = n_scr[bi * hp + hh]
                m_out[bi, pl.ds(direction * N_HEADS + hh, 1), :] = m_scr[bi * hp + hh]

    def fwd(c):
        c_hist[c] = c_scr[...].astype(BF16)
        n_hist[c] = n_scr[...]
        m_hist[c] = m_scr[...]
        state_update(c, ck.heads(v_ref, c), gate_rows(c, 1), gate_rows(c, 0), CHUNK - 1)

    def bwd(c):
        i_f, b_f, i_b, b_b = (gate_rows(c, kind) for kind in (0, 1, 2, 3))
        q, v = ck.heads(q_ref, c), ck.heads(v_ref, c)
        qf = q.astype(F32)
        a = _bdot(q, ck.keys_t(kt_ref, c), _NN)
        zero_f = not has_init and _is_chunk(c, 0)
        zero_b = not has_init and _is_chunk(c, nc - 1)
        h_f = direction_out(a, q, qf, v, None if zero_f else c_hist[c], n_hist[c],
                            m_hist[c][:, :, :1], b_f, i_f, lower)
        h_b = direction_out(a, q, qf, v, None if zero_b else c_scr[...].astype(BF16),
                            n_scr[...], m_scr[...][:, :, :1], b_b, i_b, upper)
        y = _ln_rows(h_f + h_b)
        for bi in range(bb):
            for hh in range(hp):
                cs = slice(hh * HEAD_DIM, (hh + 1) * HEAD_DIM)
                gate = jax.nn.sigmoid(ck.rows(mo_ref, bi, c, cs).astype(F32))
                ck.store_rows(o_ref, bi, c, cs,
                              (gate * (y[bi * hp + hh] * gn_ref[:, cs])).astype(BF16))
        state_update(c, v, b_b, i_b, 0)

    _run_scan(nc, streamed, init_state, fwd, bwd, emit_state if emit_states else None)


def _mlstm_call(proj, kt, gates, gn, init, *, batch, seq_len, bb, hp, emit_states,
                streamed=False):
    nc = seq_len // CHUNK
    has_init = init is not None
    assert N_HEADS % hp == 0 and (hp == N_HEADS or not emit_states) and batch % bb == 0
    if streamed:
        assert bb == batch and hp == N_HEADS and not emit_states
        proj = proj.reshape(batch, seq_len, N_STORED)
        kt = kt.reshape(batch, nc, 2 * GROUP_DIM, LANES)
        gates = gates.reshape(batch, nc, N_GATES, LANES)
        c_spec = pl.BlockSpec((bb, 1, 2, hp, HEAD_DIM, HEAD_DIM), lambda s: (0,) * 6)
        n_spec = pl.BlockSpec((bb, 1, 2, N_HEADS, HEAD_DIM), lambda s: (0,) * 5)
        in_specs = [_stream_rows_block(bb, nc, 3),
                    _stream_chunk_block(bb, nc, (GROUP_DIM, LANES), (1, 0)),
                    _stream_rows_block(bb, nc, 4), _stream_rows_block(bb, nc, 5),
                    _stream_chunk_block(bb, nc, (N_GATES, LANES), (0, 0)),
                    pl.BlockSpec((1, GROUP_DIM), lambda s: (0, 0))]
        blocks = ([_nbytes((bb, CHUNK, GROUP_DIM), BF16)] * 5
                  + [_nbytes((bb, N_GATES, LANES), F32)])
        out_specs = [_stream_out_block(bb, nc)]
        out_shape = [jax.ShapeDtypeStruct((batch, seq_len, GROUP_DIM), BF16)]
        grid, semantics = (2 * nc,), ("arbitrary",)
    else:
        c_spec = pl.BlockSpec((bb, 1, 2, hp, HEAD_DIM, HEAD_DIM),
                              lambda b, hg: (b, 0, 0, hg, 0, 0))
        n_spec = pl.BlockSpec((bb, 1, 2, N_HEADS, HEAD_DIM), lambda b, hg: (b, 0, 0, 0, 0))
        in_specs = [_group_block(seq_len, hp, 3, bb), _kt_block(nc, hp, 1, bb),
                    _group_block(seq_len, hp, 4, bb), _group_block(seq_len, hp, 5, bb),
                    pl.BlockSpec((bb * nc, N_GATES, LANES), lambda b, hg: (b, 0, 0)),
                    pl.BlockSpec((1, hp * HEAD_DIM), lambda b, hg: (0, hg))]
        blocks = ([_nbytes((bb * seq_len, hp * HEAD_DIM), BF16)] * 5
                  + [_nbytes((bb * nc, N_GATES, LANES), F32)])
        out_specs = [pl.BlockSpec((bb * seq_len, hp * HEAD_DIM), lambda b, hg: (b, hg))]
        out_shape = [jax.ShapeDtypeStruct((batch * seq_len, GROUP_DIM), BF16)]
        grid, semantics = (batch // bb, N_HEADS // hp), ("arbitrary", "arbitrary")
    args = [proj, kt, proj, proj, gates, gn]
    if has_init:
        in_specs += [c_spec, n_spec, pl.BlockSpec(memory_space=pltpu.SMEM)]
        args += list(init)
        blocks.append(_nbytes((bb, 2, hp, HEAD_DIM, HEAD_DIM), F32))
    if emit_states:
        out_specs += [c_spec, n_spec, pl.BlockSpec((bb, 2 * N_HEADS, LANES), lambda b, hg: (b, 0, 0))]
        out_shape += [jax.ShapeDtypeStruct((batch, 1, 2, N_HEADS, HEAD_DIM, HEAD_DIM), F32),
                      jax.ShapeDtypeStruct((batch, 1, 2, N_HEADS, HEAD_DIM), F32),
                      jax.ShapeDtypeStruct((batch, 2 * N_HEADS, LANES), F32)]
        blocks.append(_nbytes((bb, 2, hp, HEAD_DIM, HEAD_DIM), F32))
    ne = bb * hp
    scratch_defs = [((nc, ne, HEAD_DIM, HEAD_DIM), BF16), ((nc, ne, 1, HEAD_DIM), F32),
                    ((nc, ne, 1, LANES), F32), ((ne, HEAD_DIM, HEAD_DIM), F32),
                    ((ne, 1, HEAD_DIM), F32), ((ne, 1, LANES), F32)]
    scratch_bytes = sum(_nbytes(s, d) for s, d in scratch_defs) + 16 * nc * ne * HEAD_DIM * 4
    outs = pl.pallas_call(
        functools.partial(_mlstm_kernel, nc=nc, bb=bb, hp=hp, has_init=has_init,
                          emit_states=emit_states, streamed=streamed),
        grid=grid,
        in_specs=in_specs, out_specs=out_specs, out_shape=out_shape,
        scratch_shapes=[pltpu.VMEM(s, d) for s, d in scratch_defs],
        compiler_params=pltpu.CompilerParams(
            dimension_semantics=semantics,
            vmem_limit_bytes=_vmem_limit(blocks, scratch_bytes)),
        name="mlstm_grid" if has_init else "mlstm_seq",
    )(*args)
    if streamed:
        outs = [outs[0].reshape(batch * seq_len, GROUP_DIM)] + list(outs[1:])
    return outs


def _mix_out_kernel(mr_ref, mm_ref, w_ref, x_ref, g1_ref, lg_ref, lb_ref, o_ref):
    for rc in range(o_ref.shape[0] // MIX_ROW_CHUNK):
        rows = slice(rc * MIX_ROW_CHUNK, (rc + 1) * MIX_ROW_CHUNK)
        mix = (jnp.dot(mr_ref[rows, :], w_ref[0:GROUP_DIM, :], preferred_element_type=F32)
               + jnp.dot(mm_ref[rows, :], w_ref[GROUP_DIM:, :], preferred_element_type=F32))
        o_ref[rows, :] = (_ln_rows(ALPHA * x_ref[rows, :] + g1_ref[0] * mix) * lg_ref[...]
                          + lb_ref[...])


def _mix_out_call(mix_r, mix_m, w_out_b, x2d, mod3, mod_row, ln_g, ln_b, *, tm):
    tokens = x2d.shape[0]
    blocks = [_nbytes((tm, GROUP_DIM), BF16)] * 2 + [_nbytes((D_MODEL, D_MODEL), BF16)] \
        + [_nbytes((tm, D_MODEL), F32)] * 2
    return pl.pallas_call(
        _mix_out_kernel,
        grid=(tokens // tm,),
        in_specs=[pl.BlockSpec((tm, GROUP_DIM), lambda m: (m, 0)),
                  pl.BlockSpec((tm, GROUP_DIM), lambda m: (m, 0)),
                  pl.BlockSpec((D_MODEL, D_MODEL), lambda m: (0, 0)),
                  pl.BlockSpec((tm, D_MODEL), lambda m: (m, 0)),
                  pl.BlockSpec((1, 1, D_MODEL), lambda m: (mod_row(m) * 6 + 2, 0, 0)),
                  pl.BlockSpec((1, D_MODEL), lambda m: (0, 0)),
                  pl.BlockSpec((1, D_MODEL), lambda m: (0, 0))],
        out_specs=pl.BlockSpec((tm, D_MODEL), lambda m: (m, 0)),
        out_shape=jax.ShapeDtypeStruct((tokens, D_MODEL), F32),
        compiler_params=pltpu.CompilerParams(
            dimension_semantics=("parallel",), vmem_limit_bytes=_vmem_limit(blocks)),
        name="mix_out",
    )(mix_r, mix_m, w_out_b, x2d, mod3, ln_g, ln_b)


def _ffn_kernel(x_ref, sh_ref, sc_ref, g2_ref, wu_ref, wg_ref, cv_ref, wd_ref, lg_ref, lb_ref,
                o_ref, h_scr, acc_scr, *, period):
    j = pl.program_id(1)

    @pl.when(j == 0)
    def _():
        h = _ln_rows(x_ref[...]) * (1.0 + sc_ref[0]) + sh_ref[0]
        h_scr[...] = h.astype(BF16)
        acc_scr[...] = jnp.zeros_like(acc_scr)

    hb = h_scr[...]
    u = jnp.dot(hb, wu_ref[...], preferred_element_type=F32)
    g = jnp.dot(hb, wg_ref[...], preferred_element_type=F32)
    hid = _silu(_conv3_rows(u, cv_ref[0], period)) * g
    lane = lax.broadcasted_iota(jnp.int32, (1, FF_TN), 1)
    hid = jnp.where((j == FF_NJ - 1) & (lane < FF_OVERLAP), 0.0, hid).astype(BF16)
    acc_scr[...] += jnp.dot(hid, wd_ref[...], preferred_element_type=F32)

    @pl.when(j == FF_NJ - 1)
    def _():
        y = ALPHA * x_ref[...] + g2_ref[0] * acc_scr[...]
        o_ref[...] = _ln_rows(y) * lg_ref[...] + lb_ref[...]


def _ff_offset(j):
    return pl.multiple_of(jnp.minimum(j * FF_TN, D_FF - FF_TN), LANES)


def _ffn_call(x1, mod3, mod_row, w_up_b, conv_tiles, w_down_b, ln_g, ln_b, *, period, tm):
    tokens = x1.shape[0]
    up_tile = (pl.Element(D_MODEL), pl.Element(FF_TN))
    blocks = [_nbytes((tm, D_MODEL), F32)] * 2 + [_nbytes((D_MODEL, FF_TN), BF16)] * 3 \
        + [_nbytes((8, FF_TN), F32)]
    scratch_bytes = _nbytes((tm, D_MODEL), BF16) + _nbytes((tm, D_MODEL), F32)
    return pl.pallas_call(
        functools.partial(_ffn_kernel, period=period),
        grid=(tokens // tm, FF_NJ),
        in_specs=[pl.BlockSpec((tm, D_MODEL), lambda m, j: (m, 0)),
                  pl.BlockSpec((1, 1, D_MODEL), lambda m, j: (mod_row(m) * 6 + 3, 0, 0)),
                  pl.BlockSpec((1, 1, D_MODEL), lambda m, j: (mod_row(m) * 6 + 4, 0, 0)),
                  pl.BlockSpec((1, 1, D_MODEL), lambda m, j: (mod_row(m) * 6 + 5, 0, 0)),
                  pl.BlockSpec(up_tile, lambda m, j: (0, _ff_offset(j))),
                  pl.BlockSpec(up_tile, lambda m, j: (0, pl.multiple_of(D_FF + _ff_offset(j), LANES))),
                  pl.BlockSpec((1, 3, FF_TN), lambda m, j: (j, 0, 0)),
                  pl.BlockSpec((pl.Element(FF_TN), pl.Element(D_MODEL)),
                               lambda m, j: (_ff_offset(j), 0)),
                  pl.BlockSpec((1, D_MODEL), lambda m, j: (0, 0)),
                  pl.BlockSpec((1, D_MODEL), lambda m, j: (0, 0))],
        out_specs=pl.BlockSpec((tm, D_MODEL), lambda m, j: (m, 0)),
        out_shape=jax.ShapeDtypeStruct((tokens, D_MODEL), F32),
        scratch_shapes=[pltpu.VMEM((tm, D_MODEL), BF16), pltpu.VMEM((tm, D_MODEL), F32)],
        compiler_params=pltpu.CompilerParams(
            dimension_semantics=("parallel", "arbitrary"),
            vmem_limit_bytes=_vmem_limit(blocks, scratch_bytes)),
        name="ffn",
    )(x1, mod3, mod3, mod3, w_up_b, w_up_b, conv_tiles, w_down_b, ln_g, ln_b)


def _rope_tables(seq_len):
    quarter = HEAD_DIM // 4
    t = jnp.arange(seq_len)
    row = (t // GRID_W).astype(F32)
    col = (t % GRID_W).astype(F32)
    inv = ROPE_BASE ** (-jnp.arange(quarter, dtype=F32) / quarter)
    ang_r, ang_c = row[:, None] * inv, col[:, None] * inv
    cos_t = jnp.concatenate([jnp.cos(ang_r)] * 2 + [jnp.cos(ang_c)] * 2, axis=1)
    sin_t = jnp.concatenate([-jnp.sin(ang_r), jnp.sin(ang_r), -jnp.sin(ang_c), jnp.sin(ang_c)], axis=1)
    return cos_t, sin_t


def kernel(x_prompt, x_sample, state_ret, state_mlstm_C, state_mlstm_n, state_mlstm_m, c, c_ctx,
           w_mod, b_mod, w_in, b_gate, conv_qk, ret_theta, gn_ret, gn_mlstm, w_out,
           ln1_g, ln1_b, w_up, conv_ff, w_down, ln2_g, ln2_b):
    bp, lp, _ = x_prompt.shape
    bs, ls, _ = x_sample.shape
    layer = 0
    tm_mix, tm = 1024, 512

    cvec = jnp.concatenate([c_ctx[None], c, jnp.zeros((8 - 1 - bs, D_MODEL), F32)], axis=0)
    mod = _mod_call(cvec, w_mod[layer], b_mod[layer][None])
    mod3 = mod.reshape(8 * 6, 1, D_MODEL)

    def row_prompt(tile):
        return lambda m: 0

    def row_sample(tile):
        return lambda m: 1 + (m * tile) // ls

    w_l = w_in[layer].T
    w_gate = w_l[N_PROJ:]
    b_gate_col = b_gate[layer].reshape(N_GATES, 1)
    ff_offsets = [min(j * FF_TN, D_FF - FF_TN) for j in range(FF_NJ)]
    conv_tiles = jnp.stack([conv_ff[layer][:, o:o + FF_TN] for o in ff_offsets])
    gn_r = gn_ret[layer].reshape(1, GROUP_DIM)
    gn_m = gn_mlstm[layer].reshape(1, GROUP_DIM)
    theta = ret_theta[layer]
    ln1 = (ln1_g[layer][None], ln1_b[layer][None])
    ln2 = (ln2_g[layer][None], ln2_b[layer][None])

    def mixer(x, mod_row, rope, period, bb, hp, ret_init, mlstm_init, emit_states, stage=(),
              streamed=False):
        batch, seq_len, _ = x.shape
        x2d = x.reshape(batch * seq_len, D_MODEL)
        h2d, gates = _ln_mod_call(x2d, mod3, mod_row(tm_mix), w_gate, b_gate_col, tm=tm_mix)
        proj, kt, *staged = _mix_in_call(h2d, w_l, conv_qk[layer], rope, seq_len=seq_len,
                                         period=period, tm=tm_mix, stage=stage)
        ret = _ret_call(proj, kt, theta, gn_r, ret_init, batch=batch, seq_len=seq_len, bb=bb,
                        hp=hp, emit_states=emit_states, streamed=streamed)
        mls = _mlstm_call(proj, kt, gates, gn_m, mlstm_init, batch=batch, seq_len=seq_len, bb=bb,
                          hp=hp, emit_states=emit_states, streamed=streamed)
        return x2d, ret, mls, staged

    def channel_mix(x2d, shape, ret, mls, mod_row, period, weights):
        w_up_b, w_down_b, w_out_b = weights
        x1 = _mix_out_call(ret[0], mls[0], w_out_b, x2d, mod3, mod_row(tm), *ln1, tm=tm)
        y = _ffn_call(x1, mod3, mod_row(tm), w_up_b, conv_tiles, w_down_b, *ln2, period=period,
                      tm=tm)
        return y.reshape(shape)

    xp, ret_p, mls_p, weights_b = mixer(x_prompt, row_prompt, None, lp, 2, N_HEADS, None, None,
                                        True, stage=(w_up[layer], w_down[layer], w_out[layer]))
    m0 = state_mlstm_m.reshape(-1)
    xs, ret_s, mls_s, _ = mixer(x_sample, row_sample, _rope_tables(ls), GRID_W, bs, N_HEADS,
                                state_ret, (state_mlstm_C, state_mlstm_n, m0), False,
                                streamed=True)
    y_prompt = channel_mix(xp, x_prompt.shape, ret_p, mls_p, row_prompt, lp, weights_b)
    y_sample = channel_mix(xs, x_sample.shape, ret_s, mls_s, row_sample, GRID_W, weights_b)
    ret_states, mls_states = ret_p[1:], mls_p[1:]

    new_ret = ret_states[0]
    new_c, new_n, new_m = mls_states
    new_m = new_m[:, :, 0].reshape(bp, 1, 2, N_HEADS)
    return (y_prompt, y_sample, new_ret, new_c, new_n, new_m)
```

```python
import functools

import jax
import jax.numpy as jnp
from jax import lax
from jax.experimental import pallas as pl
from jax.experimental.pallas import tpu as pltpu

F32 = jnp.float32
BF16 = jnp.bfloat16

D_MODEL = 2048
GRID_W = 64
HEAD_DIM = 256
N_HEADS = 4
GROUP_DIM = N_HEADS * HEAD_DIM
N_PROJ = 8 * GROUP_DIM
N_GATES = 4 * N_HEADS
D_FF = 5504
CHUNK = 128
ROPE_BASE = 10000.0
ALPHA = 2.0 ** 0.25
LN_EPS = 1e-6
K_SCALE = HEAD_DIM ** -0.5

LANES = 128
BF16_ROWS = 16
V7X_VMEM_LIMIT_CAP = 60 * 1024 * 1024
VMEM_TEMP_ALLOWANCE = 16 * 1024 * 1024

MIX_ROW_CHUNK = 512
N_STORED = 6 * GROUP_DIM
FF_TN = 512
FF_NJ = -(-D_FF // FF_TN)
FF_OVERLAP = FF_NJ * FF_TN - D_FF


def _vmem_limit(block_bytes, scratch_bytes=0):
    est = 2 * sum(block_bytes) + scratch_bytes + VMEM_TEMP_ALLOWANCE
    return int(min(V7X_VMEM_LIMIT_CAP, est))


def _nbytes(shape, dtype):
    n = 1
    for s in shape:
        n *= s
    return n * jnp.dtype(dtype).itemsize


def _ln_rows(x):
    mu = jnp.mean(x, axis=-1, keepdims=True)
    xc = x - mu
    var = jnp.mean(xc * xc, axis=-1, keepdims=True)
    return xc * lax.rsqrt(var + LN_EPS)


def _silu(x):
    return x * jax.nn.sigmoid(x)


def _log_sigmoid(x):
    return jnp.minimum(x, 0.0) - jnp.log1p(jnp.exp(-jnp.abs(x)))


def _conv3_rows(u, w, period):
    rows = u.shape[0]
    t = lax.broadcasted_iota(jnp.int32, (rows, 1), 0) & (period - 1)
    prev = jnp.where(t == 0, 0.0, pltpu.roll(u, 1, 0))
    nxt = jnp.where(t == period - 1, 0.0, pltpu.roll(u, rows - 1, 0))
    return prev * w[0:1] + u * w[1:2] + nxt * w[2:3]


def _cumsum_lanes(x, lane, reverse):
    s = 1
    while s < LANES:
        if reverse:
            x = x + jnp.where(lane < LANES - s, pltpu.roll(x, LANES - s, 1), 0.0)
        else:
            x = x + jnp.where(lane >= s, pltpu.roll(x, s, 1), 0.0)
        s *= 2
    return x


def _mod_kernel(c_ref, w_ref, b_ref, o_ref):
    a = _silu(c_ref[...]).astype(BF16)
    o_ref[...] = jnp.dot(a, w_ref[...].astype(BF16), preferred_element_type=F32) + b_ref[...]


def _mod_call(cvec, w_mod, b_mod):
    rows, n = cvec.shape[0], w_mod.shape[1]
    tn = 1024
    blocks = [_nbytes((rows, D_MODEL), F32), _nbytes((D_MODEL, tn), F32),
              _nbytes((8, tn), F32), _nbytes((rows, tn), F32)]
    return pl.pallas_call(
        _mod_kernel,
        grid=(n // tn,),
        in_specs=[pl.BlockSpec((rows, D_MODEL), lambda j: (0, 0)),
                  pl.BlockSpec((D_MODEL, tn), lambda j: (0, j)),
                  pl.BlockSpec((1, tn), lambda j: (0, j))],
        out_specs=pl.BlockSpec((rows, tn), lambda j: (0, j)),
        out_shape=jax.ShapeDtypeStruct((rows, n), F32),
        compiler_params=pltpu.CompilerParams(
            dimension_semantics=("arbitrary",), vmem_limit_bytes=_vmem_limit(blocks)),
        name="mod",
    )(cvec, w_mod, b_mod)


def _ln_mod_kernel(x_ref, sh_ref, sc_ref, wg_ref, bg_ref, h_ref, gates_ref, *, tm):
    h = _ln_rows(x_ref[...]) * (1.0 + sc_ref[0]) + sh_ref[0]
    hb = h.astype(BF16)
    h_ref[...] = hb
    g = lax.dot_general(wg_ref[...].astype(BF16), hb, (((1,), (1,)), ((), ())),
                        preferred_element_type=F32) + bg_ref[...]
    row = lax.broadcasted_iota(jnp.int32, (N_GATES, LANES), 0)
    lane = lax.broadcasted_iota(jnp.int32, (N_GATES, LANES), 1)
    kind = lax.shift_right_logical(row, 2)
    for s in range(tm // LANES):
        gs = g[:, s * LANES:(s + 1) * LANES]
        ls = _log_sigmoid(gs)
        gates_ref[s] = jnp.where(kind == 1, _cumsum_lanes(ls, lane, False),
                                 jnp.where(kind == 3, _cumsum_lanes(ls, lane, True), gs))


def _ln_mod_call(x2d, mod3, mod_row, w_gate, b_gate_col, *, tm):
    tokens = x2d.shape[0]
    slabs = tm // LANES
    blocks = [_nbytes((tm, D_MODEL), F32), _nbytes((tm, D_MODEL), BF16),
              _nbytes((N_GATES, D_MODEL), F32), _nbytes((slabs, N_GATES, LANES), F32)]
    return pl.pallas_call(
        functools.partial(_ln_mod_kernel, tm=tm),
        grid=(tokens // tm,),
        in_specs=[pl.BlockSpec((tm, D_MODEL), lambda m: (m, 0)),
                  pl.BlockSpec((1, 1, D_MODEL), lambda m: (mod_row(m) * 6 + 0, 0, 0)),
                  pl.BlockSpec((1, 1, D_MODEL), lambda m: (mod_row(m) * 6 + 1, 0, 0)),
                  pl.BlockSpec((N_GATES, D_MODEL), lambda m: (0, 0)),
                  pl.BlockSpec((N_GATES, 1), lambda m: (0, 0))],
        out_specs=[pl.BlockSpec((tm, D_MODEL), lambda m: (m, 0)),
                   pl.BlockSpec((slabs, N_GATES, LANES), lambda m: (m, 0, 0))],
        out_shape=[jax.ShapeDtypeStruct((tokens, D_MODEL), BF16),
                   jax.ShapeDtypeStruct((tokens // LANES, N_GATES, LANES), F32)],
        compiler_params=pltpu.CompilerParams(
            dimension_semantics=("parallel",), vmem_limit_bytes=_vmem_limit(blocks)),
        name="ln_mod",
    )(x2d, mod3, mod3, w_gate, b_gate_col)


def _mix_in_kernel(*refs, tm, period, grid_mode, n_staged):
    refs = list(refs)
    h_ref, w_ref, cv_ref = refs[:3]
    pos = 3
    if grid_mode:
        cos_ref, sin_ref = refs[pos:pos + 2]
        pos += 2
    staged_in = refs[pos:pos + n_staged]
    pos += n_staged
    proj_ref, kt_ref = refs[pos:pos + 2]
    pos += 2
    staged_out = refs[pos:pos + n_staged]
    wb_scr = refs[pos + n_staged]
    group = pl.program_id(0)

    @pl.when(pl.program_id(1) == 0)
    def _():
        wb_scr[...] = w_ref[...].astype(BF16)

    def chunks(sizes=(tm // 2, tm // 2)):
        for src, dst in zip(staged_in, staged_out):
            dst[...] = src[...].astype(BF16)
        start = 0
        for n_rows in sizes:
            rows = slice(start, start + n_rows)
            yield start, rows, lax.dot_general(h_ref[rows, :], wb_scr[...],
                                               (((1,), (1,)), ((), ())), preferred_element_type=F32)
            start += n_rows
        assert start == tm

    def rope(a, rows):
        out = []
        for s in range(GROUP_DIM // LANES):
            xs = a[:, s * LANES:(s + 1) * LANES]
            t = (s % 2) * LANES
            out.append(xs * cos_ref[rows, t:t + LANES]
                       + pltpu.roll(xs, LANES // 2, 1) * sin_ref[rows, t:t + LANES])
        return jnp.concatenate(out, axis=1)

    def store_transposed(start, y):
        yt = y.T.astype(BF16)
        for s in range(y.shape[0] // LANES):
            kt_ref[start // LANES + s] = yt[:, s * LANES:(s + 1) * LANES]

    plain = (group == 2) | (group == 3) | (group == 6) | (group == 7)
    if not grid_mode:
        plain = plain | (group == 0)

    @pl.when(plain)
    def _():
        for _, rows, acc in chunks((tm,)):
            proj_ref[rows, :] = acc.astype(BF16)

    if grid_mode:
        @pl.when(group == 0)
        def _():
            for _, rows, acc in chunks():
                proj_ref[rows, :] = rope(acc, rows).astype(BF16)

    @pl.when(group == 1)
    def _():
        for rc, rows, acc in chunks():
            k = acc * K_SCALE
            store_transposed(rc, rope(k, rows) if grid_mode else k)

    @pl.when(group == 4)
    def _():
        for _, rows, acc in chunks():
            proj_ref[rows, :] = _silu(_conv3_rows(acc, cv_ref[...], period)).astype(BF16)

    @pl.when(group == 5)
    def _():
        for rc, _, acc in chunks():
            store_transposed(rc, _silu(_conv3_rows(acc, cv_ref[...], period)) * K_SCALE)


def _staging_specs(weights, n_groups, n_tiles):
    specs, shapes, block_bytes = [], [], []
    for w in weights:
        rows, cols = w.shape
        slab = BF16_ROWS
        while rows % slab or rows // slab > n_groups * n_tiles:
            slab += BF16_ROWS
        n_slabs = rows // slab
        specs.append(pl.BlockSpec(
            (slab, cols),
            lambda g, m, n_slabs=n_slabs: (jnp.minimum(g * n_tiles + m, n_slabs - 1), 0)))
        shapes.append(jax.ShapeDtypeStruct((rows, cols), BF16))
        block_bytes.append(_nbytes((slab, cols), F32) + _nbytes((slab, cols), BF16))
    return specs, shapes, block_bytes


def _mix_in_call(h2d, w_in_t, conv_qk, rope, *, seq_len, period, tm, stage=()):
    tokens = h2d.shape[0]
    grid_mode = rope is not None
    tiles_per_seq = seq_len // tm
    slabs = tm // LANES
    last = tokens // tm - 1
    n_groups = N_PROJ // GROUP_DIM
    stage_specs, stage_shapes, stage_bytes = _staging_specs(stage, n_groups, tokens // tm)

    def tile(g, m):
        return jnp.where(g % 2 == 1, last - m, m)

    def is_key(g):
        return (g == 1) | (g == 5)

    def proj_index(g, m):
        col = g - (g >= 1).astype(jnp.int32) - (g >= 5).astype(jnp.int32)
        return jnp.where(is_key(g), last, tile(g, m)), col

    def kt_index(g, m):
        row = jnp.where(g == 0, last, jnp.where(is_key(g), tile(g, m), 0))
        return row, (g >= 5).astype(jnp.int32), 0

    in_specs = [
        pl.BlockSpec((tm, D_MODEL), lambda g, m: (tile(g, m), 0)),
        pl.BlockSpec((GROUP_DIM, D_MODEL), lambda g, m: (g, 0)),
        pl.BlockSpec((3, GROUP_DIM), lambda g, m: (0, jnp.clip(g - 4, 0, 1))),
    ]
    args = [h2d, w_in_t, conv_qk]
    blocks = [_nbytes((tm, D_MODEL), BF16), _nbytes((GROUP_DIM, D_MODEL), F32),
              _nbytes((tm, GROUP_DIM), BF16) * 2, _nbytes((8, GROUP_DIM), F32)]
    if grid_mode:
        in_specs += [pl.BlockSpec((tm, HEAD_DIM),
                                  lambda g, m: (jnp.where(g <= 1, tile(g, m) % tiles_per_seq, 0),
                                                0))] * 2
        args += list(rope)
        blocks += [_nbytes((tm, HEAD_DIM), F32)] * 2
    in_specs += stage_specs
    args += list(stage)
    blocks += stage_bytes
    scratch_defs = [((GROUP_DIM, D_MODEL), BF16)]

    return pl.pallas_call(
        functools.partial(_mix_in_kernel, tm=tm, period=period, grid_mode=grid_mode,
                          n_staged=len(stage)),
        grid=(n_groups, tokens // tm),
        in_specs=in_specs,
        out_specs=[pl.BlockSpec((tm, GROUP_DIM), lambda g, m: proj_index(g, m)),
                   pl.BlockSpec((slabs, GROUP_DIM, LANES), lambda g, m: kt_index(g, m))]
        + stage_specs,
        out_shape=[jax.ShapeDtypeStruct((tokens, N_STORED), BF16),
                   jax.ShapeDtypeStruct((tokens // LANES, 2 * GROUP_DIM, LANES), BF16)]
        + stage_shapes,
        scratch_shapes=[pltpu.VMEM(s, d) for s, d in scratch_defs],
        compiler_params=pltpu.CompilerParams(
            dimension_semantics=("arbitrary", "arbitrary"),
            vmem_limit_bytes=_vmem_limit(blocks, sum(_nbytes(s, d) for s, d in scratch_defs))),
        name="mix_in_grid" if grid_mode else "mix_in_seq",
    )(*args)


_NN = (((2,), (1,)), ((0,), (0,)))
_NT = (((2,), (2,)), ((0,), (0,)))


def _bdot(x, y, dims):
    return lax.dot_general(x, y, dims, preferred_element_type=F32)


def _chunk_loop(nc, body, reverse=False):
    if nc <= 2:
        for i in range(nc):
            body(nc - 1 - i if reverse else i)
    else:
        def step(i, carry):
            body(nc - 1 - i if reverse else i)
            return carry
        lax.fori_loop(0, nc, step, 0)


def _rows(c, base=0):
    start = base + c * CHUNK
    if not isinstance(start, int):
        start = pl.multiple_of(start, CHUNK)
    return pl.ds(start, CHUNK)


def _is_chunk(c, value):
    return isinstance(c, int) and c == value


class _Chunks:
    def __init__(self, nc, bb, hp, streamed):
        self.nc, self.bb, self.hp, self.streamed = nc, bb, hp, streamed

    def rows(self, ref, bi, c, cols):
        return ref[bi, :, cols] if self.streamed else ref[_rows(c, bi * self.nc * CHUNK), cols]

    def store_rows(self, ref, bi, c, cols, value):
        if self.streamed:
            ref[bi, :, cols] = value
        else:
            ref[_rows(c, bi * self.nc * CHUNK), cols] = value

    def per_chunk(self, ref, bi, c):
        return ref[bi, 0] if self.streamed else ref[bi * self.nc + c]

    def heads(self, ref, c):
        return jnp.stack([self.rows(ref, bi, c, slice(hh * HEAD_DIM, (hh + 1) * HEAD_DIM))
                          for bi in range(self.bb) for hh in range(self.hp)])

    def keys_t(self, kt_ref, c):
        parts = [self.per_chunk(kt_ref, bi, c).reshape(self.hp, HEAD_DIM, CHUNK)
                 for bi in range(self.bb)]
        return parts[0] if self.bb == 1 else jnp.concatenate(parts, axis=0)

    def gate_row(self, g_ref, bi, c, row):
        if self.streamed:
            return g_ref[bi, 0, pl.ds(row, 1), :]
        return g_ref[bi * self.nc + c, pl.ds(row, 1), :]


def _run_scan(nc, streamed, init_state, fwd, bwd, emit_state):
    if not streamed:
        init_state(0)
        _chunk_loop(nc, fwd)
        if emit_state is not None:
            emit_state(0)
        init_state(1)
        _chunk_loop(nc, bwd, reverse=True)
        if emit_state is not None:
            emit_state(1)
        return
    assert emit_state is None
    s = pl.program_id(0)

    @pl.when(s == 0)
    def _():
        init_state(0)

    @pl.when(s == nc)
    def _():
        init_state(1)

    @pl.when(s < nc)
    def _():
        fwd(s)

    @pl.when(s >= nc)
    def _():
        bwd(2 * nc - 1 - s)


def _per_seq(x, bb):
    return x if bb == 1 else jnp.concatenate([x] * bb, axis=0)


def _group_block(seq_len, hp, group, bb):
    per = N_HEADS // hp
    return pl.BlockSpec((bb * seq_len, hp * HEAD_DIM), lambda b, hg: (b, group * per + hg))


def _kt_block(nc, hp, group, bb):
    per = N_HEADS // hp
    return pl.BlockSpec((bb * nc, hp * HEAD_DIM, LANES), lambda b, hg: (b, group * per + hg, 0))


def _step_chunk(s, nc):
    return jnp.where(s < nc, s, 2 * nc - 1 - s)


def _stream_rows_block(bb, nc, group):
    return pl.BlockSpec((bb, CHUNK, GROUP_DIM), lambda s: (0, _step_chunk(s, nc), group))


def _stream_out_block(bb, nc):
    return pl.BlockSpec((bb, CHUNK, GROUP_DIM), lambda s: (0, jnp.minimum(2 * nc - 1 - s, nc - 1), 0))


def _stream_chunk_block(bb, nc, tail, tail_index):
    return pl.BlockSpec((bb, 1) + tail, lambda s: (0, _step_chunk(s, nc)) + tail_index)


def _ret_kernel(*refs, nc, bb, hp, has_init, emit_states, streamed):
    refs = list(refs)
    theta_ref, q_ref, kt_ref, v_ref, rg_ref, gn_ref = refs[:6]
    pos = 6
    s0_ref = None
    if has_init:
        s0_ref = refs[pos]
        pos += 1
    o_ref = refs[pos]
    pos += 1
    st_ref = None
    if emit_states:
        st_ref = refs[pos]
        pos += 1
    hist, s_scr, decay_scr, qdec_scr, kdec_scr, cdec_scr = refs[pos:pos + 6]
    ck = _Chunks(nc, bb, hp, streamed)
    hg = 0 if streamed else pl.program_id(1)
    hsel = pl.ds(hg * hp, hp)

    @pl.when(pl.program_id(0) == 0)
    def _():
        ii = lax.broadcasted_iota(jnp.int32, (CHUNK, CHUNK), 0)
        jj = lax.broadcasted_iota(jnp.int32, (CHUNK, CHUNK), 1)
        d = (ii - jj).astype(F32)
        p_col = lax.broadcasted_iota(jnp.int32, (CHUNK, HEAD_DIM), 0).astype(F32)
        p_row = lax.broadcasted_iota(jnp.int32, (1, LANES), 1).astype(F32)
        for hh in range(hp):
            h = hg * hp + hh
            lg_f = _log_sigmoid(jnp.full((1, LANES), theta_ref[0, h], F32))
            lg_b = _log_sigmoid(jnp.full((1, LANES), theta_ref[1, h], F32))
            decay_scr[h] = (jnp.where(d >= 0, jnp.exp(lg_f * jnp.maximum(d, 0.0)), 0.0)
                            + jnp.where(d <= 0, jnp.exp(lg_b * jnp.maximum(-d, 0.0)), 0.0))
            qdec_scr[0, h] = jnp.exp(lg_f[:, :1] * (p_col + 1.0))
            qdec_scr[1, h] = jnp.exp(lg_b[:, :1] * (CHUNK - p_col))
            kdec_scr[0, h] = jnp.exp(lg_f * (CHUNK - 1.0 - p_row))
            kdec_scr[1, h] = jnp.exp(lg_b * p_row)
            cdec_scr[0, h] = jnp.exp(lg_f * float(CHUNK))
            cdec_scr[1, h] = jnp.exp(lg_b * float(CHUNK))

    def table(ref, *lead):
        return _per_seq(ref[(*lead, hsel)], bb)

    def kv_update(c, v, direction):
        kd = (ck.keys_t(kt_ref, c).astype(F32)
              * table(kdec_scr, direction)).astype(BF16)
        s_scr[...] = s_scr[...] * table(cdec_scr, direction)[:, :, :1] + _bdot(kd, v, _NN)

    def init_state(direction):
        if has_init:
            for bi in range(bb):
                s_scr[bi * hp:(bi + 1) * hp] = s0_ref[bi, 0, direction]
        else:
            s_scr[...] = jnp.zeros((bb * hp, HEAD_DIM, HEAD_DIM), F32)

    def emit_state(direction):
        for bi in range(bb):
            st_ref[bi, 0, direction] = s_scr[bi * hp:(bi + 1) * hp]

    def fwd(c):
        hist[c] = s_scr[...].astype(BF16)
        kv_update(c, ck.heads(v_ref, c), 0)

    def bwd(c):
        q, v = ck.heads(q_ref, c), ck.heads(v_ref, c)
        kt = ck.keys_t(kt_ref, c)
        att = (_bdot(q, kt, _NN) * table(decay_scr)).astype(BF16)
        o = _bdot(att, v, _NN)
        if has_init or not _is_chunk(c, 0):
            o = o + _bdot(q, hist[c], _NN) * table(qdec_scr, 0)
        if has_init or not _is_chunk(c, nc - 1):
            o = o + _bdot(q, s_scr[...].astype(BF16), _NN) * table(qdec_scr, 1)
        y = _ln_rows(o)
        for bi in range(bb):
            for hh in range(hp):
                cs = slice(hh * HEAD_DIM, (hh + 1) * HEAD_DIM)
                gate = _silu(ck.rows(rg_ref, bi, c, cs).astype(F32))
                ck.store_rows(o_ref, bi, c, cs,
                              (y[bi * hp + hh] * gn_ref[:, cs] * gate).astype(BF16))
        kv_update(c, v, 1)

    _run_scan(nc, streamed, init_state, fwd, bwd, emit_state if emit_states else None)


def _ret_call(proj, kt, theta, gn, s0, *, batch, seq_len, bb, hp, emit_states, streamed=False):
    nc = seq_len // CHUNK
    has_init = s0 is not None
    assert N_HEADS % hp == 0 and batch % bb == 0
    if streamed:
        assert bb == batch and hp == N_HEADS and not emit_states
        proj = proj.reshape(batch, seq_len, N_STORED)
        kt = kt.reshape(batch, nc, 2 * GROUP_DIM, LANES)
        state_spec = pl.BlockSpec((bb, 1, 2, hp, HEAD_DIM, HEAD_DIM), lambda s: (0,) * 6)
        in_specs = [pl.BlockSpec(memory_space=pltpu.SMEM),
                    _stream_rows_block(bb, nc, 0),
                    _stream_chunk_block(bb, nc, (GROUP_DIM, LANES), (0, 0)),
                    _stream_rows_block(bb, nc, 1), _stream_rows_block(bb, nc, 2),
                    pl.BlockSpec((1, GROUP_DIM), lambda s: (0, 0))]
        blocks = [_nbytes((bb, CHUNK, GROUP_DIM), BF16)] * 5
        out_specs = [_stream_out_block(bb, nc)]
        out_shape = [jax.ShapeDtypeStruct((batch, seq_len, GROUP_DIM), BF16)]
        grid, semantics = (2 * nc,), ("arbitrary",)
    else:
        state_spec = pl.BlockSpec((bb, 1, 2, hp, HEAD_DIM, HEAD_DIM),
                                  lambda b, hg: (b, 0, 0, hg, 0, 0))
        in_specs = [pl.BlockSpec(memory_space=pltpu.SMEM),
                    _group_block(seq_len, hp, 0, bb), _kt_block(nc, hp, 0, bb),
                    _group_block(seq_len, hp, 1, bb), _group_block(seq_len, hp, 2, bb),
                    pl.BlockSpec((1, hp * HEAD_DIM), lambda b, hg: (0, hg))]
        blocks = [_nbytes((bb * seq_len, hp * HEAD_DIM), BF16)] * 5
        out_specs = [pl.BlockSpec((bb * seq_len, hp * HEAD_DIM), lambda b, hg: (b, hg))]
        out_shape = [jax.ShapeDtypeStruct((batch * seq_len, GROUP_DIM), BF16)]
        grid, semantics = (batch // bb, N_HEADS // hp), ("arbitrary", "arbitrary")
    args = [theta, proj, kt, proj, proj, gn]
    if has_init:
        in_specs.append(state_spec)
        args.append(s0)
        blocks.append(_nbytes((bb, 2, hp, HEAD_DIM, HEAD_DIM), F32))
    if emit_states:
        out_specs.append(state_spec)
        out_shape.append(jax.ShapeDtypeStruct((batch, 1, 2, N_HEADS, HEAD_DIM, HEAD_DIM), F32))
        blocks.append(_nbytes((bb, 2, hp, HEAD_DIM, HEAD_DIM), F32))
    scratch_defs = [((nc, bb * hp, HEAD_DIM, HEAD_DIM), BF16), ((bb * hp, HEAD_DIM, HEAD_DIM), F32),
                    ((N_HEADS, CHUNK, CHUNK), F32), ((2, N_HEADS, CHUNK, HEAD_DIM), F32),
                    ((2, N_HEADS, 1, LANES), F32), ((2, N_HEADS, 1, LANES), F32)]
    outs = pl.pallas_call(
        functools.partial(_ret_kernel, nc=nc, bb=bb, hp=hp, has_init=has_init,
                          emit_states=emit_states, streamed=streamed),
        grid=grid,
        in_specs=in_specs, out_specs=out_specs, out_shape=out_shape,
        scratch_shapes=[pltpu.VMEM(s, d) for s, d in scratch_defs],
        compiler_params=pltpu.CompilerParams(
            dimension_semantics=semantics,
            vmem_limit_bytes=_vmem_limit(blocks, sum(_nbytes(s, d) for s, d in scratch_defs))),
        name="ret_grid" if has_init else "ret_seq",
    )(*args)
    if streamed:
        outs = [outs[0].reshape(batch * seq_len, GROUP_DIM)] + list(outs[1:])
    return outs


def _mlstm_kernel(*refs, nc, bb, hp, has_init, emit_states, streamed):
    refs = list(refs)
    q_ref, kt_ref, v_ref, mo_ref, g_ref, gn_ref = refs[:6]
    pos = 6
    c0_ref = n0_ref = m0_ref = None
    if has_init:
        c0_ref, n0_ref, m0_ref = refs[pos:pos + 3]
        pos += 3
    o_ref = refs[pos]
    pos += 1
    c_out = n_out = m_out = None
    if emit_states:
        c_out, n_out, m_out = refs[pos:pos + 3]
        pos += 3
    c_hist, n_hist, m_hist, c_scr, n_scr, m_scr = refs[pos:pos + 6]

    ck = _Chunks(nc, bb, hp, streamed)
    b = 0 if streamed else pl.program_id(0)
    hg = 0 if streamed else pl.program_id(1)
    ii = lax.broadcasted_iota(jnp.int32, (CHUNK, CHUNK), 0)
    jj = lax.broadcasted_iota(jnp.int32, (CHUNK, CHUNK), 1)
    eye = ii == jj
    lower = ii >= jj
    upper = ii <= jj

    ne = bb * hp

    def col(row):
        return jnp.sum(jnp.where(eye, row, 0.0), axis=2, keepdims=True)

    def gate_rows(c, kind):
        base = kind * N_HEADS + hg * hp
        return jnp.stack([ck.gate_row(g_ref, bi, c, base + hh)
                          for bi in range(bb) for hh in range(hp)])

    def direction_out(a, q, qf, v, c_b, n_row, m, b_row, i_row, mask):
        b_col = col(b_row)
        dm = jnp.where(mask, b_col - b_row + i_row, -jnp.inf)
        inter = b_col + m
        mt = jnp.maximum(inter, jnp.max(dm, axis=2, keepdims=True))
        w = jnp.exp(dm - mt)
        sp = jnp.exp(inter - mt)
        s = a * w
        num = _bdot(s.astype(BF16), v, _NN)
        den = jnp.sum(s, axis=2, keepdims=True)
        if c_b is not None:
            num = num + _bdot(q, c_b, _NN) * sp
            den = den + jnp.sum(qf * n_row, axis=2, keepdims=True) * sp
        return num / jnp.maximum(jnp.abs(den), jnp.exp(-mt))

    def state_update(c, v, b_row, i_row, last):
        m = m_scr[...][:, :, :1]
        b_last = b_row[:, :, last:last + 1]
        g = b_last - b_row + i_row
        m_new = jnp.maximum(b_last + m, jnp.max(g, axis=2, keepdims=True))
        wk = jnp.exp(g - m_new)
        sc = jnp.exp(b_last + m - m_new)
        kt = ck.keys_t(kt_ref, c)
        kw = (kt.astype(F32) * wk).astype(BF16)
        c_scr[...] = c_scr[...] * sc + _bdot(kw, v, _NN)
        wk8 = jnp.broadcast_to(wk, (ne, 8, CHUNK)).astype(BF16)
        n_scr[...] = n_scr[...] * sc + _bdot(wk8, kt, _NT)[:, :1, :]
        m_scr[...] = jnp.broadcast_to(m_new, (ne, 1, LANES))

    def init_state(direction):
        if has_init:
            for bi in range(bb):
                c_scr[bi * hp:(bi + 1) * hp] = c0_ref[bi, 0, direction]
            n_scr[...] = jnp.stack([n0_ref[bi, 0, direction, pl.ds(hg * hp + hh, 1), :]
                                    for bi in range(bb) for hh in range(hp)])
            m_scr[...] = jnp.stack([
                jnp.full((1, LANES),
                         m0_ref[((b * bb + bi) * 2 + direction) * N_HEADS + hg * hp + hh], F32)
                for bi in range(bb) for hh in range(hp)])
        else:
            c_scr[...] = jnp.zeros((ne, HEAD_DIM, HEAD_DIM), F32)
            n_scr[...] = jnp.zeros((ne, 1, HEAD_DIM), F32)
            m_scr[...] = jnp.zeros((ne, 1, LANES), F32)

    def emit_state(direction):
        for bi in range(bb):
            c_out[bi, 0, direction] = c_scr[bi * hp:(bi + 1) * hp]
            for hh in range(hp):
                n_out[bi, 0, direction, pl.ds(hh, 1), :] = n_scr[bi * hp + hh]
                m_out[bi, pl.ds(direction * N_HEADS + hh, 1), :] = m_scr[bi * hp + hh]

    def fwd(c):
        c_hist[c] = c_scr[...].astype(BF16)
        n_hist[c] = n_scr[...]
        m_hist[c] = m_scr[...]
        state_update(c, ck.heads(v_ref, c), gate_rows(c, 1), gate_rows(c, 0), CHUNK - 1)

    def bwd(c):
        i_f, b_f, i_b, b_b = (gate_rows(c, kind) for kind in (0, 1, 2, 3))
        q, v = ck.heads(q_ref, c), ck.heads(v_ref, c)
        qf = q.astype(F32)
        a = _bdot(q, ck.keys_t(kt_ref, c), _NN)
        zero_f = not has_init and _is_chunk(c, 0)
        zero_b = not has_init and _is_chunk(c, nc - 1)
        h_f = direction_out(a, q, qf, v, None if zero_f else c_hist[c], n_hist[c],
                            m_hist[c][:, :, :1], b_f, i_f, lower)
        h_b = direction_out(a, q, qf, v, None if zero_b else c_scr[...].astype(BF16),
                            n_scr[...], m_scr[...][:, :, :1], b_b, i_b, upper)
        y = _ln_rows(h_f + h_b)
        for bi in range(bb):
            for hh in range(hp):
                cs = slice(hh * HEAD_DIM, (hh + 1) * HEAD_DIM)
                gate = jax.nn.sigmoid(ck.rows(mo_ref, bi, c, cs).astype(F32))
                ck.store_rows(o_ref, bi, c, cs,
                              (gate * (y[bi * hp + hh] * gn_ref[:, cs])).astype(BF16))
        state_update(c, v, b_b, i_b, 0)

    _run_scan(nc, streamed, init_state, fwd, bwd, emit_state if emit_states else None)


def _mlstm_call(proj, kt, gates, gn, init, *, batch, seq_len, bb, hp, emit_states,
                streamed=False):
    nc = seq_len // CHUNK
    has_init = init is not None
    assert N_HEADS % hp == 0 and (hp == N_HEADS or not emit_states) and batch % bb == 0
    if streamed:
        assert bb == batch and hp == N_HEADS and not emit_states
        proj = proj.reshape(batch, seq_len, N_STORED)
        kt = kt.reshape(batch, nc, 2 * GROUP_DIM, LANES)
        gates = gates.reshape(batch, nc, N_GATES, LANES)
        c_spec = pl.BlockSpec((bb, 1, 2, hp, HEAD_DIM, HEAD_DIM), lambda s: (0,) * 6)
        n_spec = pl.BlockSpec((bb, 1, 2, N_HEADS, HEAD_DIM), lambda s: (0,) * 5)
        in_specs = [_stream_rows_block(bb, nc, 3),
                    _stream_chunk_block(bb, nc, (GROUP_DIM, LANES), (1, 0)),
                    _stream_rows_block(bb, nc, 4), _stream_rows_block(bb, nc, 5),
                    _stream_chunk_block(bb, nc, (N_GATES, LANES), (0, 0)),
                    pl.BlockSpec((1, GROUP_DIM), lambda s: (0, 0))]
        blocks = ([_nbytes((bb, CHUNK, GROUP_DIM), BF16)] * 5
                  + [_nbytes((bb, N_GATES, LANES), F32)])
        out_specs = [_stream_out_block(bb, nc)]
        out_shape = [jax.ShapeDtypeStruct((batch, seq_len, GROUP_DIM), BF16)]
        grid, semantics = (2 * nc,), ("arbitrary",)
    else:
        c_spec = pl.BlockSpec((bb, 1, 2, hp, HEAD_DIM, HEAD_DIM),
                              lambda b, hg: (b, 0, 0, hg, 0, 0))
        n_spec = pl.BlockSpec((bb, 1, 2, N_HEADS, HEAD_DIM), lambda b, hg: (b, 0, 0, 0, 0))
        in_specs = [_group_block(seq_len, hp, 3, bb), _kt_block(nc, hp, 1, bb),
                    _group_block(seq_len, hp, 4, bb), _group_block(seq_len, hp, 5, bb),
                    pl.BlockSpec((bb * nc, N_GATES, LANES), lambda b, hg: (b, 0, 0)),
                    pl.BlockSpec((1, hp * HEAD_DIM), lambda b, hg: (0, hg))]
        blocks = ([_nbytes((bb * seq_len, hp * HEAD_DIM), BF16)] * 5
                  + [_nbytes((bb * nc, N_GATES, LANES), F32)])
        out_specs = [pl.BlockSpec((bb * seq_len, hp * HEAD_DIM), lambda b, hg: (b, hg))]
        out_shape = [jax.ShapeDtypeStruct((batch * seq_len, GROUP_DIM), BF16)]
        grid, semantics = (batch // bb, N_HEADS // hp), ("arbitrary", "arbitrary")
    args = [proj, kt, proj, proj, gates, gn]
    if has_init:
        in_specs += [c_spec, n_spec, pl.BlockSpec(memory_space=pltpu.SMEM)]
        args += list(init)
        blocks.append(_nbytes((bb, 2, hp, HEAD_DIM, HEAD_DIM), F32))
    if emit_states:
        out_specs += [c_spec, n_spec, pl.BlockSpec((bb, 2 * N_HEADS, LANES), lambda b, hg: (b, 0, 0))]
        out_shape += [jax.ShapeDtypeStruct((batch, 1, 2, N_HEADS, HEAD_DIM, HEAD_DIM), F32),
                      jax.ShapeDtypeStruct((batch, 1, 2, N_HEADS, HEAD_DIM), F32),
                      jax.ShapeDtypeStruct((batch, 2 * N_HEADS, LANES), F32)]
        blocks.append(_nbytes((bb, 2, hp, HEAD_DIM, HEAD_DIM), F32))
    ne = bb * hp
    scratch_defs = [((nc, ne, HEAD_DIM, HEAD_DIM), BF16), ((nc, ne, 1, HEAD_DIM), F32),
                    ((nc, ne, 1, LANES), F32), ((ne, HEAD_DIM, HEAD_DIM), F32),
                    ((ne, 1, HEAD_DIM), F32), ((ne, 1, LANES), F32)]
    scratch_bytes = sum(_nbytes(s, d) for s, d in scratch_defs) + 16 * nc * ne * HEAD_DIM * 4
    outs = pl.pallas_call(
        functools.partial(_mlstm_kernel, nc=nc, bb=bb, hp=hp, has_init=has_init,
                          emit_states=emit_states, streamed=streamed),
        grid=grid,
        in_specs=in_specs, out_specs=out_specs, out_shape=out_shape,
        scratch_shapes=[pltpu.VMEM(s, d) for s, d in scratch_defs],
        compiler_params=pltpu.CompilerParams(
            dimension_semantics=semantics,
            vmem_limit_bytes=_vmem_limit(blocks, scratch_bytes)),
        name="mlstm_grid" if has_init else "mlstm_seq",
    )(*args)
    if streamed:
        outs = [outs[0].reshape(batch * seq_len, GROUP_DIM)] + list(outs[1:])
    return outs


def _mix_out_kernel(mr_ref, mm_ref, w_ref, x_ref, g1_ref, lg_ref, lb_ref, o_ref):
    for rc in range(o_ref.shape[0] // MIX_ROW_CHUNK):
        rows = slice(rc * MIX_ROW_CHUNK, (rc + 1) * MIX_ROW_CHUNK)
        mix = (jnp.dot(mr_ref[rows, :], w_ref[0:GROUP_DIM, :], preferred_element_type=F32)
               + jnp.dot(mm_ref[rows, :], w_ref[GROUP_DIM:, :], preferred_element_type=F32))
        o_ref[rows, :] = (_ln_rows(ALPHA * x_ref[rows, :] + g1_ref[0] * mix) * lg_ref[...]
                          + lb_ref[...])


def _mix_out_call(mix_r, mix_m, w_out_b, x2d, mod3, mod_row, ln_g, ln_b, *, tm):
    tokens = x2d.shape[0]
    blocks = [_nbytes((tm, GROUP_DIM), BF16)] * 2 + [_nbytes((D_MODEL, D_MODEL), BF16)] \
        + [_nbytes((tm, D_MODEL), F32)] * 2
    return pl.pallas_call(
        _mix_out_kernel,
        grid=(tokens // tm,),
        in_specs=[pl.BlockSpec((tm, GROUP_DIM), lambda m: (m, 0)),
                  pl.BlockSpec((tm, GROUP_DIM), lambda m: (m, 0)),
                  pl.BlockSpec((D_MODEL, D_MODEL), lambda m: (0, 0)),
                  pl.BlockSpec((tm, D_MODEL), lambda m: (m, 0)),
                  pl.BlockSpec((1, 1, D_MODEL), lambda m: (mod_row(m) * 6 + 2, 0, 0)),
                  pl.BlockSpec((1, D_MODEL), lambda m: (0, 0)),
                  pl.BlockSpec((1, D_MODEL), lambda m: (0, 0))],
        out_specs=pl.BlockSpec((tm, D_MODEL), lambda m: (m, 0)),
        out_shape=jax.ShapeDtypeStruct((tokens, D_MODEL), F32),
        compiler_params=pltpu.CompilerParams(
            dimension_semantics=("parallel",), vmem_limit_bytes=_vmem_limit(blocks)),
        name="mix_out",
    )(mix_r, mix_m, w_out_b, x2d, mod3, ln_g, ln_b)


def _ffn_kernel(x_ref, sh_ref, sc_ref, g2_ref, wu_ref, wg_ref, cv_ref, wd_ref, lg_ref, lb_ref,
                o_ref, h_scr, acc_scr, *, period):
    j = pl.program_id(1)

    @pl.when(j == 0)
    def _():
        h = _ln_rows(x_ref[...]) * (1.0 + sc_ref[0]) + sh_ref[0]
        h_scr[...] = h.astype(BF16)
        acc_scr[...] = jnp.zeros_like(acc_scr)

    hb = h_scr[...]
    u = jnp.dot(hb, wu_ref[...], preferred_element_type=F32)
    g = jnp.dot(hb, wg_ref[...], preferred_element_type=F32)
    hid = _silu(_conv3_rows(u, cv_ref[0], period)) * g
    lane = lax.broadcasted_iota(jnp.int32, (1, FF_TN), 1)
    hid = jnp.where((j == FF_NJ - 1) & (lane < FF_OVERLAP), 0.0, hid).astype(BF16)
    acc_scr[...] += jnp.dot(hid, wd_ref[...], preferred_element_type=F32)

    @pl.when(j == FF_NJ - 1)
    def _():
        y = ALPHA * x_ref[...] + g2_ref[0] * acc_scr[...]
        o_ref[...] = _ln_rows(y) * lg_ref[...] + lb_ref[...]


def _ff_offset(j):
    return pl.multiple_of(jnp.minimum(j * FF_TN, D_FF - FF_TN), LANES)


def _ffn_call(x1, mod3, mod_row, w_up_b, conv_tiles, w_down_b, ln_g, ln_b, *, period, tm):
    tokens = x1.shape[0]
    up_tile = (pl.Element(D_MODEL), pl.Element(FF_TN))
    blocks = [_nbytes((tm, D_MODEL), F32)] * 2 + [_nbytes((D_MODEL, FF_TN), BF16)] * 3 \
        + [_nbytes((8, FF_TN), F32)]
    scratch_bytes = _nbytes((tm, D_MODEL), BF16) + _nbytes((tm, D_MODEL), F32)
    return pl.pallas_call(
        functools.partial(_ffn_kernel, period=period),
        grid=(tokens // tm, FF_NJ),
        in_specs=[pl.BlockSpec((tm, D_MODEL), lambda m, j: (m, 0)),
                  pl.BlockSpec((1, 1, D_MODEL), lambda m, j: (mod_row(m) * 6 + 3, 0, 0)),
                  pl.BlockSpec((1, 1, D_MODEL), lambda m, j: (mod_row(m) * 6 + 4, 0, 0)),
                  pl.BlockSpec((1, 1, D_MODEL), lambda m, j: (mod_row(m) * 6 + 5, 0, 0)),
                  pl.BlockSpec(up_tile, lambda m, j: (0, _ff_offset(j))),
                  pl.BlockSpec(up_tile, lambda m, j: (0, pl.multiple_of(D_FF + _ff_offset(j), LANES))),
                  pl.BlockSpec((1, 3, FF_TN), lambda m, j: (j, 0, 0)),
                  pl.BlockSpec((pl.Element(FF_TN), pl.Element(D_MODEL)),
                               lambda m, j: (_ff_offset(j), 0)),
                  pl.BlockSpec((1, D_MODEL), lambda m, j: (0, 0)),
                  pl.BlockSpec((1, D_MODEL), lambda m, j: (0, 0))],
        out_specs=pl.BlockSpec((tm, D_MODEL), lambda m, j: (m, 0)),
        out_shape=jax.ShapeDtypeStruct((tokens, D_MODEL), F32),
        scratch_shapes=[pltpu.VMEM((tm, D_MODEL), BF16), pltpu.VMEM((tm, D_MODEL), F32)],
        compiler_params=pltpu.CompilerParams(
            dimension_semantics=("parallel", "arbitrary"),
            vmem_limit_bytes=_vmem_limit(blocks, scratch_bytes)),
        name="ffn",
    )(x1, mod3, mod3, mod3, w_up_b, w_up_b, conv_tiles, w_down_b, ln_g, ln_b)


def _rope_tables(seq_len):
    quarter = HEAD_DIM // 4
    t = jnp.arange(seq_len)
    row = (t // GRID_W).astype(F32)
    col = (t % GRID_W).astype(F32)
    inv = ROPE_BASE ** (-jnp.arange(quarter, dtype=F32) / quarter)
    ang_r, ang_c = row[:, None] * inv, col[:, None] * inv
    cos_t = jnp.concatenate([jnp.cos(ang_r)] * 2 + [jnp.cos(ang_c)] * 2, axis=1)
    sin_t = jnp.concatenate([-jnp.sin(ang_r), jnp.sin(ang_r), -jnp.sin(ang_c), jnp.sin(ang_c)], axis=1)
    return cos_t, sin_t


def kernel(x_prompt, x_sample, state_ret, state_mlstm_C, state_mlstm_n, state_mlstm_m, c, c_ctx,
           w_mod, b_mod, w_in, b_gate, conv_qk, ret_theta, gn_ret, gn_mlstm, w_out,
           ln1_g, ln1_b, w_up, conv_ff, w_down, ln2_g, ln2_b):
    bp, lp, _ = x_prompt.shape
    bs, ls, _ = x_sample.shape
    layer = 0
    tm_mix, tm = 1024, 512

    cvec = jnp.concatenate([c_ctx[None], c, jnp.zeros((8 - 1 - bs, D_MODEL), F32)], axis=0)
    mod = _mod_call(cvec, w_mod[layer], b_mod[layer][None])
    mod3 = mod.reshape(8 * 6, 1, D_MODEL)

    def row_prompt(tile):
        return lambda m: 0

    def row_sample(tile):
        return lambda m: 1 + (m * tile) // ls

    w_l = w_in[layer].T
    w_gate = w_l[N_PROJ:]
    b_gate_col = b_gate[layer].reshape(N_GATES, 1)
    ff_offsets = [min(j * FF_TN, D_FF - FF_TN) for j in range(FF_NJ)]
    conv_tiles = jnp.stack([conv_ff[layer][:, o:o + FF_TN] for o in ff_offsets])
    gn_r = gn_ret[layer].reshape(1, GROUP_DIM)
    gn_m = gn_mlstm[layer].reshape(1, GROUP_DIM)
    theta = ret_theta[layer]
    ln1 = (ln1_g[layer][None], ln1_b[layer][None])
    ln2 = (ln2_g[layer][None], ln2_b[layer][None])

    def mixer(x, mod_row, rope, period, bb, hp, ret_init, mlstm_init, emit_states, stage=(),
              streamed=False):
        batch, seq_len, _ = x.shape
        x2d = x.reshape(batch * seq_len, D_MODEL)
        h2d, gates = _ln_mod_call(x2d, mod3, mod_row(tm_mix), w_gate, b_gate_col, tm=tm_mix)
        proj, kt, *staged = _mix_in_call(h2d, w_l, conv_qk[layer], rope, seq_len=seq_len,
                                         period=period, tm=tm_mix, stage=stage)
        ret = _ret_call(proj, kt, theta, gn_r, ret_init, batch=batch, seq_len=seq_len, bb=bb,
                        hp=hp, emit_states=emit_states, streamed=streamed)
        mls = _mlstm_call(proj, kt, gates, gn_m, mlstm_init, batch=batch, seq_len=seq_len, bb=bb,
                          hp=hp, emit_states=emit_states, streamed=streamed)
        return x2d, ret, mls, staged

    def channel_mix(x2d, shape, ret, mls, mod_row, period, weights):
        w_up_b, w_down_b, w_out_b = weights
        x1 = _mix_out_call(ret[0], mls[0], w_out_b, x2d, mod3, mod_row(tm), *ln1, tm=tm)
        y = _ffn_call(x1, mod3, mod_row(tm), w_up_b, conv_tiles, w_down_b, *ln2, period=period,
                      tm=tm)
        return y.reshape(shape)

    xp, ret_p, mls_p, weights_b = mixer(x_prompt, row_prompt, None, lp, 2, N_HEADS, None, None,
                                        True, stage=(w_up[layer], w_down[layer], w_out[layer]))
    m0 = state_mlstm_m.reshape(-1)
    xs, ret_s, mls_s, _ = mixer(x_sample, row_sample, _rope_tables(ls), GRID_W, bs, N_HEADS,
                                state_ret, (state_mlstm_C, state_mlstm_n, m0), False,
                                streamed=True)
    y_prompt = channel_mix(xp, x_prompt.shape, ret_p, mls_p, row_prompt, lp, weights_b)
    y_sample = channel_mix(xs, x_sample.shape, ret_s, mls_s, row_sample, GRID_W, weights_b)
    ret_states, mls_states = ret_p[1:], mls_p[1:]

    new_ret = ret_states[0]
    new_c, new_n, new_m = mls_states
    new_m = new_m[:, :, 0].reshape(bp, 1, 2, N_HEADS)
    return (y_prompt, y_sample, new_ret, new_c, new_n, new_m)
```

```python
import functools

import jax
import jax.numpy as jnp
from jax import lax
from jax.experimental import pallas as pl
from jax.experimental.pallas import tpu as pltpu

F32 = jnp.float32
BF16 = jnp.bfloat16

D_MODEL = 2048
GRID_W = 64
HEAD_DIM = 256
N_HEADS = 4
GROUP_DIM = N_HEADS * HEAD_DIM
N_PROJ = 8 * GROUP_DIM
N_GATES = 4 * N_HEADS
D_FF = 5504
CHUNK = 128
ROPE_BASE = 10000.0
ALPHA = 2.0 ** 0.25
LN_EPS = 1e-6
K_SCALE = HEAD_DIM ** -0.5

LANES = 128
BF16_ROWS = 16
V7X_VMEM_LIMIT_CAP = 60 * 1024 * 1024
VMEM_TEMP_ALLOWANCE = 16 * 1024 * 1024

MIX_ROW_CHUNK = 512
N_STORED = 6 * GROUP_DIM
FF_TN = 512
FF_NJ = -(-D_FF // FF_TN)
FF_OVERLAP = FF_NJ * FF_TN - D_FF


def _vmem_limit(block_bytes, scratch_bytes=0):
    est = 2 * sum(block_bytes) + scratch_bytes + VMEM_TEMP_ALLOWANCE
    return int(min(V7X_VMEM_LIMIT_CAP, est))


def _nbytes(shape, dtype):
    n = 1
    for s in shape:
        n *= s
    return n * jnp.dtype(dtype).itemsize


def _ln_rows(x):
    mu = jnp.mean(x, axis=-1, keepdims=True)
    xc = x - mu
    var = jnp.mean(xc * xc, axis=-1, keepdims=True)
    return xc * lax.rsqrt(var + LN_EPS)


def _silu(x):
    return x * jax.nn.sigmoid(x)


def _log_sigmoid(x):
    return jnp.minimum(x, 0.0) - jnp.log1p(jnp.exp(-jnp.abs(x)))


def _conv3_rows(u, w, period):
    rows = u.shape[0]
    t = lax.broadcasted_iota(jnp.int32, (rows, 1), 0) & (period - 1)
    prev = jnp.where(t == 0, 0.0, pltpu.roll(u, 1, 0))
    nxt = jnp.where(t == period - 1, 0.0, pltpu.roll(u, rows - 1, 0))
    return prev * w[0:1] + u * w[1:2] + nxt * w[2:3]


def _cumsum_lanes(x, lane, reverse):
    s = 1
    while s < LANES:
        if reverse:
            x = x + jnp.where(lane < LANES - s, pltpu.roll(x, LANES - s, 1), 0.0)
        else:
            x = x + jnp.where(lane >= s, pltpu.roll(x, s, 1), 0.0)
        s *= 2
    return x


def _mod_kernel(c_ref, w_ref, b_ref, o_ref):
    a = _silu(c_ref[...]).astype(BF16)
    o_ref[...] = jnp.dot(a, w_ref[...].astype(BF16), preferred_element_type=F32) + b_ref[...]


def _mod_call(cvec, w_mod, b_mod):
    rows, n = cvec.shape[0], w_mod.shape[1]
    tn = 1024
    blocks = [_nbytes((rows, D_MODEL), F32), _nbytes((D_MODEL, tn), F32),
              _nbytes((8, tn), F32), _nbytes((rows, tn), F32)]
    return pl.pallas_call(
        _mod_kernel,
        grid=(n // tn,),
        in_specs=[pl.BlockSpec((rows, D_MODEL), lambda j: (0, 0)),
                  pl.BlockSpec((D_MODEL, tn), lambda j: (0, j)),
                  pl.BlockSpec((1, tn), lambda j: (0, j))],
        out_specs=pl.BlockSpec((rows, tn), lambda j: (0, j)),
        out_shape=jax.ShapeDtypeStruct((rows, n), F32),
        compiler_params=pltpu.CompilerParams(
            dimension_semantics=("arbitrary",), vmem_limit_bytes=_vmem_limit(blocks)),
        name="mod",
    )(cvec, w_mod, b_mod)


def _ln_mod_kernel(x_ref, sh_ref, sc_ref, wg_ref, bg_ref, h_ref, gates_ref, *, tm):
    h = _ln_rows(x_ref[...]) * (1.0 + sc_ref[0]) + sh_ref[0]
    hb = h.astype(BF16)
    h_ref[...] = hb
    g = lax.dot_general(wg_ref[...].astype(BF16), hb, (((1,), (1,)), ((), ())),
                        preferred_element_type=F32) + bg_ref[...]
    row = lax.broadcasted_iota(jnp.int32, (N_GATES, LANES), 0)
    lane = lax.broadcasted_iota(jnp.int32, (N_GATES, LANES), 1)
    kind = lax.shift_right_logical(row, 2)
    for s in range(tm // LANES):
        gs = g[:, s * LANES:(s + 1) * LANES]
        ls = _log_sigmoid(gs)
        gates_ref[s] = jnp.where(kind == 1, _cumsum_lanes(ls, lane, False),
                                 jnp.where(kind == 3, _cumsum_lanes(ls, lane, True), gs))


def _ln_mod_call(x2d, mod3, mod_row, w_gate, b_gate_col, *, tm):
    tokens = x2d.shape[0]
    slabs = tm // LANES
    blocks = [_nbytes((tm, D_MODEL), F32), _nbytes((tm, D_MODEL), BF16),
              _nbytes((N_GATES, D_MODEL), F32), _nbytes((slabs, N_GATES, LANES), F32)]
    return pl.pallas_call(
        functools.partial(_ln_mod_kernel, tm=tm),
        grid=(tokens // tm,),
        in_specs=[pl.BlockSpec((tm, D_MODEL), lambda m: (m, 0)),
                  pl.BlockSpec((1, 1, D_MODEL), lambda m: (mod_row(m) * 6 + 0, 0, 0)),
                  pl.BlockSpec((1, 1, D_MODEL), lambda m: (mod_row(m) * 6 + 1, 0, 0)),
                  pl.BlockSpec((N_GATES, D_MODEL), lambda m: (0, 0)),
                  pl.BlockSpec((N_GATES, 1), lambda m: (0, 0))],
        out_specs=[pl.BlockSpec((tm, D_MODEL), lambda m: (m, 0)),
                   pl.BlockSpec((slabs, N_GATES, LANES), lambda m: (m, 0, 0))],
        out_shape=[jax.ShapeDtypeStruct((tokens, D_MODEL), BF16),
                   jax.ShapeDtypeStruct((tokens // LANES, N_GATES, LANES), F32)],
        compiler_params=pltpu.CompilerParams(
            dimension_semantics=("parallel",), vmem_limit_bytes=_vmem_limit(blocks)),
        name="ln_mod",
    )(x2d, mod3, mod3, w_gate, b_gate_col)


def _mix_in_kernel(*refs, tm, period, grid_mode, n_staged):
    refs = list(refs)
    h_ref, w_ref, cv_ref = refs[:3]
    pos = 3
    if grid_mode:
        cos_ref, sin_ref = refs[pos:pos + 2]
        pos += 2
    staged_in = refs[pos:pos + n_staged]
    pos += n_staged
    proj_ref, kt_ref = refs[pos:pos + 2]
    pos += 2
    staged_out = refs[pos:pos + n_staged]
    wb_scr = refs[pos + n_staged]
    group = pl.program_id(0)

    @pl.when(pl.program_id(1) == 0)
    def _():
        wb_scr[...] = w_ref[...].astype(BF16)

    def chunks(sizes=(tm // 2, tm // 2)):
        for src, dst in zip(staged_in, staged_out):
            dst[...] = src[...].astype(BF16)
        start = 0
        for n_rows in sizes:
            rows = slice(start, start + n_rows)
            yield start, rows, lax.dot_general(h_ref[rows, :], wb_scr[...],
                                               (((1,), (1,)), ((), ())), preferred_element_type=F32)
            start += n_rows
        assert start == tm

    def rope(a, rows):
        out = []
        for s in range(GROUP_DIM // LANES):
            xs = a[:, s * LANES:(s + 1) * LANES]
            t = (s % 2) * LANES
            out.append(xs * cos_ref[rows, t:t + LANES]
                       + pltpu.roll(xs, LANES // 2, 1) * sin_ref[rows, t:t + LANES])
        return jnp.concatenate(out, axis=1)

    def store_transposed(start, y):
        yt = y.T.astype(BF16)
        for s in range(y.shape[0] // LANES):
            kt_ref[start // LANES + s] = yt[:, s * LANES:(s + 1) * LANES]

    plain = (group == 2) | (group == 3) | (group == 6) | (group == 7)
    if not grid_mode:
        plain = plain | (group == 0)

    @pl.when(plain)
    def _():
        for _, rows, acc in chunks((tm,)):
            proj_ref[rows, :] = acc.astype(BF16)

    if grid_mode:
        @pl.when(group == 0)
        def _():
            for _, rows, acc in chunks():
                proj_ref[rows, :] = rope(acc, rows).astype(BF16)

    @pl.when(group == 1)
    def _():
        for rc, rows, acc in chunks():
            k = acc * K_SCALE
            store_transposed(rc, rope(k, rows) if grid_mode else k)

    @pl.when(group == 4)
    def _():
        for _, rows, acc in chunks():
            proj_ref[rows, :] = _silu(_conv3_rows(acc, cv_ref[...], period)).astype(BF16)

    @pl.when(group == 5)
    def _():
        for rc, _, acc in chunks():
            store_transposed(rc, _silu(_conv3_rows(acc, cv_ref[...], period)) * K_SCALE)


def _staging_specs(weights, n_groups, n_tiles):
    specs, shapes, block_bytes = [], [], []
    for w in weights:
        rows, cols = w.shape
        slab = BF16_ROWS
        while rows % slab or rows // slab > n_groups * n_tiles:
            slab += BF16_ROWS
        n_slabs = rows // slab
        specs.append(pl.BlockSpec(
            (slab, cols),
            lambda g, m, n_slabs=n_slabs: (jnp.minimum(g * n_tiles + m, n_slabs - 1), 0)))
        shapes.append(jax.ShapeDtypeStruct((rows, cols), BF16))
        block_bytes.append(_nbytes((slab, cols), F32) + _nbytes((slab, cols), BF16))
    return specs, shapes, block_bytes


def _mix_in_call(h2d, w_in_t, conv_qk, rope, *, seq_len, period, tm, stage=()):
    tokens = h2d.shape[0]
    grid_mode = rope is not None
    tiles_per_seq = seq_len // tm
    slabs = tm // LANES
    last = tokens // tm - 1
    n_groups = N_PROJ // GROUP_DIM
    stage_specs, stage_shapes, stage_bytes = _staging_specs(stage, n_groups, tokens // tm)

    def tile(g, m):
        return jnp.where(g % 2 == 1, last - m, m)

    def is_key(g):
        return (g == 1) | (g == 5)

    def proj_index(g, m):
        col = g - (g >= 1).astype(jnp.int32) - (g >= 5).astype(jnp.int32)
        return jnp.where(is_key(g), last, tile(g, m)), col

    def kt_index(g, m):
        row = jnp.where(g == 0, last, jnp.where(is_key(g), tile(g, m), 0))
        return row, (g >= 5).astype(jnp.int32), 0

    in_specs = [
        pl.BlockSpec((tm, D_MODEL), lambda g, m: (tile(g, m), 0)),
        pl.BlockSpec((GROUP_DIM, D_MODEL), lambda g, m: (g, 0)),
        pl.BlockSpec((3, GROUP_DIM), lambda g, m: (0, jnp.clip(g - 4, 0, 1))),
    ]
    args = [h2d, w_in_t, conv_qk]
    blocks = [_nbytes((tm, D_MODEL), BF16), _nbytes((GROUP_DIM, D_MODEL), F32),
              _nbytes((tm, GROUP_DIM), BF16) * 2, _nbytes((8, GROUP_DIM), F32)]
    if grid_mode:
        in_specs += [pl.BlockSpec((tm, HEAD_DIM),
                                  lambda g, m: (jnp.where(g <= 1, tile(g, m) % tiles_per_seq, 0),
                                                0))] * 2
        args += list(rope)
        blocks += [_nbytes((tm, HEAD_DIM), F32)] * 2
    in_specs += stage_specs
    args += list(stage)
    blocks += stage_bytes
    scratch_defs = [((GROUP_DIM, D_MODEL), BF16)]

    return pl.pallas_call(
        functools.partial(_mix_in_kernel, tm=tm, period=period, grid_mode=grid_mode,
                          n_staged=len(stage)),
        grid=(n_groups, tokens // tm),
        in_specs=in_specs,
        out_specs=[pl.BlockSpec((tm, GROUP_DIM), lambda g, m: proj_index(g, m)),
                   pl.BlockSpec((slabs, GROUP_DIM, LANES), lambda g, m: kt_index(g, m))]
        + stage_specs,
        out_shape=[jax.ShapeDtypeStruct((tokens, N_STORED), BF16),
                   jax.ShapeDtypeStruct((tokens // LANES, 2 * GROUP_DIM, LANES), BF16)]
        + stage_shapes,
        scratch_shapes=[pltpu.VMEM(s, d) for s, d in scratch_defs],
        compiler_params=pltpu.CompilerParams(
            dimension_semantics=("arbitrary", "arbitrary"),
            vmem_limit_bytes=_vmem_limit(blocks, sum(_nbytes(s, d) for s, d in scratch_defs))),
        name="mix_in_grid" if grid_mode else "mix_in_seq",
    )(*args)


_NN = (((2,), (1,)), ((0,), (0,)))
_NT = (((2,), (2,)), ((0,), (0,)))


def _bdot(x, y, dims):
    return lax.dot_general(x, y, dims, preferred_element_type=F32)


def _chunk_loop(nc, body, reverse=False):
    if nc <= 2:
        for i in range(nc):
            body(nc - 1 - i if reverse else i)
    else:
        def step(i, carry):
            body(nc - 1 - i if reverse else i)
            return carry
        lax.fori_loop(0, nc, step, 0)


def _rows(c, base=0):
    start = base + c * CHUNK
    if not isinstance(start, int):
        start = pl.multiple_of(start, CHUNK)
    return pl.ds(start, CHUNK)


def _is_chunk(c, value):
    return isinstance(c, int) and c == value


class _Chunks:
    def __init__(self, nc, bb, hp, streamed):
        self.nc, self.bb, self.hp, self.streamed = nc, bb, hp, streamed

    def rows(self, ref, bi, c, cols):
        return ref[bi, :, cols] if self.streamed else ref[_rows(c, bi * self.nc * CHUNK), cols]

    def store_rows(self, ref, bi, c, cols, value):
        if self.streamed:
            ref[bi, :, cols] = value
        else:
            ref[_rows(c, bi * self.nc * CHUNK), cols] = value

    def per_chunk(self, ref, bi, c):
        return ref[bi, 0] if self.streamed else ref[bi * self.nc + c]

    def heads(self, ref, c):
        return jnp.stack([self.rows(ref, bi, c, slice(hh * HEAD_DIM, (hh + 1) * HEAD_DIM))
                          for bi in range(self.bb) for hh in range(self.hp)])

    def keys_t(self, kt_ref, c):
        parts = [self.per_chunk(kt_ref, bi, c).reshape(self.hp, HEAD_DIM, CHUNK)
                 for bi in range(self.bb)]
        return parts[0] if self.bb == 1 else jnp.concatenate(parts, axis=0)

    def gate_row(self, g_ref, bi, c, row):
        if self.streamed:
            return g_ref[bi, 0, pl.ds(row, 1), :]
        return g_ref[bi * self.nc + c, pl.ds(row, 1), :]


def _run_scan(nc, streamed, init_state, fwd, bwd, emit_state):
    if not streamed:
        init_state(0)
        _chunk_loop(nc, fwd)
        if emit_state is not None:
            emit_state(0)
        init_state(1)
        _chunk_loop(nc, bwd, reverse=True)
        if emit_state is not None:
            emit_state(1)
        return
    assert emit_state is None
    s = pl.program_id(0)

    @pl.when(s == 0)
    def _():
        init_state(0)

    @pl.when(s == nc)
    def _():
        init_state(1)

    @pl.when(s < nc)
    def _():
        fwd(s)

    @pl.when(s >= nc)
    def _():
        bwd(2 * nc - 1 - s)


def _per_seq(x, bb):
    return x if bb == 1 else jnp.concatenate([x] * bb, axis=0)


def _group_block(seq_len, hp, group, bb):
    per = N_HEADS // hp
    return pl.BlockSpec((bb * seq_len, hp * HEAD_DIM), lambda b, hg: (b, group * per + hg))


def _kt_block(nc, hp, group, bb):
    per = N_HEADS // hp
    return pl.BlockSpec((bb * nc, hp * HEAD_DIM, LANES), lambda b, hg: (b, group * per + hg, 0))


def _step_chunk(s, nc):
    return jnp.where(s < nc, s, 2 * nc - 1 - s)


def _stream_rows_block(bb, nc, group):
    return pl.BlockSpec((bb, CHUNK, GROUP_DIM), lambda s: (0, _step_chunk(s, nc), group))


def _stream_out_block(bb, nc):
    return pl.BlockSpec((bb, CHUNK, GROUP_DIM), lambda s: (0, jnp.minimum(2 * nc - 1 - s, nc - 1), 0))


def _stream_chunk_block(bb, nc, tail, tail_index):
    return pl.BlockSpec((bb, 1) + tail, lambda s: (0, _step_chunk(s, nc)) + tail_index)


def _ret_kernel(*refs, nc, bb, hp, has_init, emit_states, streamed):
    refs = list(refs)
    theta_ref, q_ref, kt_ref, v_ref, rg_ref, gn_ref = refs[:6]
    pos = 6
    s0_ref = None
    if has_init:
        s0_ref = refs[pos]
        pos += 1
    o_ref = refs[pos]
    pos += 1
    st_ref = None
    if emit_states:
        st_ref = refs[pos]
        pos += 1
    hist, s_scr, decay_scr, qdec_scr, kdec_scr, cdec_scr = refs[pos:pos + 6]
    ck = _Chunks(nc, bb, hp, streamed)
    hg = 0 if streamed else pl.program_id(1)
    hsel = pl.ds(hg * hp, hp)

    @pl.when(pl.program_id(0) == 0)
    def _():
        ii = lax.broadcasted_iota(jnp.int32, (CHUNK, CHUNK), 0)
        jj = lax.broadcasted_iota(jnp.int32, (CHUNK, CHUNK), 1)
        d = (ii - jj).astype(F32)
        p_col = lax.broadcasted_iota(jnp.int32, (CHUNK, HEAD_DIM), 0).astype(F32)
        p_row = lax.broadcasted_iota(jnp.int32, (1, LANES), 1).astype(F32)
        for hh in range(hp):
            h = hg * hp + hh
            lg_f = _log_sigmoid(jnp.full((1, LANES), theta_ref[0, h], F32))
            lg_b = _log_sigmoid(jnp.full((1, LANES), theta_ref[1, h], F32))
            decay_scr[h] = (jnp.where(d >= 0, jnp.exp(lg_f * jnp.maximum(d, 0.0)), 0.0)
                            + jnp.where(d <= 0, jnp.exp(lg_b * jnp.maximum(-d, 0.0)), 0.0))
            qdec_scr[0, h] = jnp.exp(lg_f[:, :1] * (p_col + 1.0))
            qdec_scr[1, h] = jnp.exp(lg_b[:, :1] * (CHUNK - p_col))
            kdec_scr[0, h] = jnp.exp(lg_f * (CHUNK - 1.0 - p_row))
            kdec_scr[1, h] = jnp.exp(lg_b * p_row)
            cdec_scr[0, h] = jnp.exp(lg_f * float(CHUNK))
            cdec_scr[1, h] = jnp.exp(lg_b * float(CHUNK))

    def table(ref, *lead):
        return _per_seq(ref[(*lead, hsel)], bb)

    def kv_update(c, v, direction):
        kd = (ck.keys_t(kt_ref, c).astype(F32)
              * table(kdec_scr, direction)).astype(BF16)
        s_scr[...] = s_scr[...] * table(cdec_scr, direction)[:, :, :1] + _bdot(kd, v, _NN)

    def init_state(direction):
        if has_init:
            for bi in range(bb):
                s_scr[bi * hp:(bi + 1) * hp] = s0_ref[bi, 0, direction]
        else:
            s_scr[...] = jnp.zeros((bb * hp, HEAD_DIM, HEAD_DIM), F32)

    def emit_state(direction):
        for bi in range(bb):
            st_ref[bi, 0, direction] = s_scr[bi * hp:(bi + 1) * hp]

    def fwd(c):
        hist[c] = s_scr[...].astype(BF16)
        kv_update(c, ck.heads(v_ref, c), 0)

    def bwd(c):
        q, v = ck.heads(q_ref, c), ck.heads(v_ref, c)
        kt = ck.keys_t(kt_ref, c)
        att = (_bdot(q, kt, _NN) * table(decay_scr)).astype(BF16)
        o = _bdot(att, v, _NN)
        if has_init or not _is_chunk(c, 0):
            o = o + _bdot(q, hist[c], _NN) * table(qdec_scr, 0)
        if has_init or not _is_chunk(c, nc - 1):
            o = o + _bdot(q, s_scr[...].astype(BF16), _NN) * table(qdec_scr, 1)
        y = _ln_rows(o)
        for bi in range(bb):
            for hh in range(hp):
                cs = slice(hh * HEAD_DIM, (hh + 1) * HEAD_DIM)
                gate = _silu(ck.rows(rg_ref, bi, c, cs).astype(F32))
                ck.store_rows(o_ref, bi, c, cs,
                              (y[bi * hp + hh] * gn_ref[:, cs] * gate).astype(BF16))
        kv_update(c, v, 1)

    _run_scan(nc, streamed, init_state, fwd, bwd, emit_state if emit_states else None)


def _ret_call(proj, kt, theta, gn, s0, *, batch, seq_len, bb, hp, emit_states, streamed=False):
    nc = seq_len // CHUNK
    has_init = s0 is not None
    assert N_HEADS % hp == 0 and batch % bb == 0
    if streamed:
        assert bb == batch and hp == N_HEADS and not emit_states
        proj = proj.reshape(batch, seq_len, N_STORED)
        kt = kt.reshape(batch, nc, 2 * GROUP_DIM, LANES)
        state_spec = pl.BlockSpec((bb, 1, 2, hp, HEAD_DIM, HEAD_DIM), lambda s: (0,) * 6)
        in_specs = [pl.BlockSpec(memory_space=pltpu.SMEM),
                    _stream_rows_block(bb, nc, 0),
                    _stream_chunk_block(bb, nc, (GROUP_DIM, LANES), (0, 0)),
                    _stream_rows_block(bb, nc, 1), _stream_rows_block(bb, nc, 2),
                    pl.BlockSpec((1, GROUP_DIM), lambda s: (0, 0))]
        blocks = [_nbytes((bb, CHUNK, GROUP_DIM), BF16)] * 5
        out_specs = [_stream_out_block(bb, nc)]
        out_shape = [jax.ShapeDtypeStruct((batch, seq_len, GROUP_DIM), BF16)]
        grid, semantics = (2 * nc,), ("arbitrary",)
    else:
        state_spec = pl.BlockSpec((bb, 1, 2, hp, HEAD_DIM, HEAD_DIM),
                                  lambda b, hg: (b, 0, 0, hg, 0, 0))
        in_specs = [pl.BlockSpec(memory_space=pltpu.SMEM),
                    _group_block(seq_len, hp, 0, bb), _kt_block(nc, hp, 0, bb),
                    _group_block(seq_len, hp, 1, bb), _group_block(seq_len, hp, 2, bb),
                    pl.BlockSpec((1, hp * HEAD_DIM), lambda b, hg: (0, hg))]
        blocks = [_nbytes((bb * seq_len, hp * HEAD_DIM), BF16)] * 5
        out_specs = [pl.BlockSpec((bb * seq_len, hp * HEAD_DIM), lambda b, hg: (b, hg))]
        out_shape = [jax.ShapeDtypeStruct((batch * seq_len, GROUP_DIM), BF16)]
        grid, semantics = (batch // bb, N_HEADS // hp), ("arbitrary", "arbitrary")
    args = [theta, proj, kt, proj, proj, gn]
    if has_init:
        in_specs.append(state_spec)
        args.append(s0)
        blocks.append(_nbytes((bb, 2, hp, HEAD_DIM, HEAD_DIM), F32))
    if emit_states:
        out_specs.append(state_spec)
        out_shape.append(jax.ShapeDtypeStruct((batch, 1, 2, N_HEADS, HEAD_DIM, HEAD_DIM), F32))
        blocks.append(_nbytes((bb, 2, hp, HEAD_DIM, HEAD_DIM), F32))
    scratch_defs = [((nc, bb * hp, HEAD_DIM, HEAD_DIM), BF16), ((bb * hp, HEAD_DIM, HEAD_DIM), F32),
                    ((N_HEADS, CHUNK, CHUNK), F32), ((2, N_HEADS, CHUNK, HEAD_DIM), F32),
                    ((2, N_HEADS, 1, LANES), F32), ((2, N_HEADS, 1, LANES), F32)]
    outs = pl.pallas_call(
        functools.partial(_ret_kernel, nc=nc, bb=bb, hp=hp, has_init=has_init,
                          emit_states=emit_states, streamed=streamed),
        grid=grid,
        in_specs=in_specs, out_specs=out_specs, out_shape=out_shape,
        scratch_shapes=[pltpu.VMEM(s, d) for s, d in scratch_defs],
        compiler_params=pltpu.CompilerParams(
            dimension_semantics=semantics,
            vmem_limit_bytes=_vmem_limit(blocks, sum(_nbytes(s, d) for s, d in scratch_defs))),
        name="ret_grid" if has_init else "ret_seq",
    )(*args)
    if streamed:
        outs = [outs[0].reshape(batch * seq_len, GROUP_DIM)] + list(outs[1:])
    return outs


def _mlstm_kernel(*refs, nc, bb, hp, has_init, emit_states, streamed):
    refs = list(refs)
    q_ref, kt_ref, v_ref, mo_ref, g_ref, gn_ref = refs[:6]
    pos = 6
    c0_ref = n0_ref = m0_ref = None
    if has_init:
        c0_ref, n0_ref, m0_ref = refs[pos:pos + 3]
        pos += 3
    o_ref = refs[pos]
    pos += 1
    c_out = n_out = m_out = None
    if emit_states:
        c_out, n_out, m_out = refs[pos:pos + 3]
        pos += 3
    c_hist, n_hist, m_hist, c_scr, n_scr, m_scr = refs[pos:pos + 6]

    ck = _Chunks(nc, bb, hp, streamed)
    b = 0 if streamed else pl.program_id(0)
    hg = 0 if streamed else pl.program_id(1)
    ii = lax.broadcasted_iota(jnp.int32, (CHUNK, CHUNK), 0)
    jj = lax.broadcasted_iota(jnp.int32, (CHUNK, CHUNK), 1)
    eye = ii == jj
    lower = ii >= jj
    upper = ii <= jj

    ne = bb * hp

    def col(row):
        return jnp.sum(jnp.where(eye, row, 0.0), axis=2, keepdims=True)

    def gate_rows(c, kind):
        base = kind * N_HEADS + hg * hp
        return jnp.stack([ck.gate_row(g_ref, bi, c, base + hh)
                          for bi in range(bb) for hh in range(hp)])

    def direction_out(a, q, qf, v, c_b, n_row, m, b_row, i_row, mask):
        b_col = col(b_row)
        dm = jnp.where(mask, b_col - b_row + i_row, -jnp.inf)
        inter = b_col + m
        mt = jnp.maximum(inter, jnp.max(dm, axis=2, keepdims=True))
        w = jnp.exp(dm - mt)
        sp = jnp.exp(inter - mt)
        s = a * w
        num = _bdot(s.astype(BF16), v, _NN)
        den = jnp.sum(s, axis=2, keepdims=True)
        if c_b is not None:
            num = num + _bdot(q, c_b, _NN) * sp
            den = den + jnp.sum(qf * n_row, axis=2, keepdims=True) * sp
        return num / jnp.maximum(jnp.abs(den), jnp.exp(-mt))

    def state_update(c, v, b_row, i_row, last):
        m = m_scr[...][:, :, :1]
        b_last = b_row[:, :, last:last + 1]
        g = b_last - b_row + i_row
        m_new = jnp.maximum(b_last + m, jnp.max(g, axis=2, keepdims=True))
        wk = jnp.exp(g - m_new)
        sc = jnp.exp(b_last + m - m_new)
        kt = ck.keys_t(kt_ref, c)
        kw = (kt.astype(F32) * wk).astype(BF16)
        c_scr[...] = c_scr[...] * sc + _bdot(kw, v, _NN)
        wk8 = jnp.broadcast_to(wk, (ne, 8, CHUNK)).astype(BF16)
        n_scr[...] = n_scr[...] * sc + _bdot(wk8, kt, _NT)[:, :1, :]
        m_scr[...] = jnp.broadcast_to(m_new, (ne, 1, LANES))

    def init_state(direction):
        if has_init:
            for bi in range(bb):
                c_scr[bi * hp:(bi + 1) * hp] = c0_ref[bi, 0, direction]
            n_scr[...] = jnp.stack([n0_ref[bi, 0, direction, pl.ds(hg * hp + hh, 1), :]
                                    for bi in range(bb) for hh in range(hp)])
            m_scr[...] = jnp.stack([
                jnp.full((1, LANES),
                         m0_ref[((b * bb + bi) * 2 + direction) * N_HEADS + hg * hp + hh], F32)
                for bi in range(bb) for hh in range(hp)])
        else:
            c_scr[...] = jnp.zeros((ne, HEAD_DIM, HEAD_DIM), F32)
            n_scr[...] = jnp.zeros((ne, 1, HEAD_DIM), F32)
            m_scr[...] = jnp.zeros((ne, 1, LANES), F32)

    def emit_state(direction):
        for bi in range(bb):
            c_out[bi, 0, direction] = c_scr[bi * hp:(bi + 1) * hp]
            for hh in range(hp):
                n_out[bi, 0, direction, pl.ds(hh, 1), :] = n_scr[bi * hp + hh]
                m_out[bi, pl.ds(direction * N_HEADS + hh, 1), :] = m_scr[bi * hp + hh]

    def fwd(c):
        c_hist[c] = c_scr[...].astype(BF16)
        n_hist[c] = n_scr[...]
        m_hist[c] = m_scr[...]
        state_update(c, ck.heads(v_ref, c), gate_rows(c, 1), gate_rows(c, 0), CHUNK - 1)

    def bwd(c):
        i_f, b_f, i_b, b_b = (gate_rows(c, kind) for kind in (0, 1, 2, 3))
        q, v = ck.heads(q_ref, c), ck.heads(v_ref, c)
        qf = q.astype(F32)
        a = _bdot(q, ck.keys_t(kt_ref, c), _NN)
        zero_f = not has_init and _is_chunk(c, 0)
        zero_b = not has_init and _is_chunk(c, nc - 1)
        h_f = direction_out(a, q, qf, v, None if zero_f else c_hist[c], n_hist[c],
                            m_hist[c][:, :, :1], b_f, i_f, lower)
        h_b = direction_out(a, q, qf, v, None if zero_b else c_scr[...].astype(BF16),
                            n_scr[...], m_scr[...][:, :, :1], b_b, i_b, upper)
        y = _ln_rows(h_f + h_b)
        for bi in range(bb):
            for hh in range(hp):
                cs = slice(hh * HEAD_DIM, (hh + 1) * HEAD_DIM)
                gate = jax.nn.sigmoid(ck.rows(mo_ref, bi, c, cs).astype(F32))
                ck.store_rows(o_ref, bi, c, cs,
                              (gate * (y[bi * hp + hh] * gn_ref[:, cs])).astype(BF16))
        state_update(c, v, b_b, i_b, 0)

    _run_scan(nc, streamed, init_state, fwd, bwd, emit_state if emit_states else None)


def _mlstm_call(proj, kt, gates, gn, init, *, batch, seq_len, bb, hp, emit_states,
                streamed=False):
    nc = seq_len // CHUNK
    has_init = init is not None
    assert N_HEADS % hp == 0 and (hp == N_HEADS or not emit_states) and batch % bb == 0
    if streamed:
        assert bb == batch and hp == N_HEADS and not emit_states
        proj = proj.reshape(batch, seq_len, N_STORED)
        kt = kt.reshape(batch, nc, 2 * GROUP_DIM, LANES)
        gates = gates.reshape(batch, nc, N_GATES, LANES)
        c_spec = pl.BlockSpec((bb, 1, 2, hp, HEAD_DIM, HEAD_DIM), lambda s: (0,) * 6)
        n_spec = pl.BlockSpec((bb, 1, 2, N_HEADS, HEAD_DIM), lambda s: (0,) * 5)
        in_specs = [_stream_rows_block(bb, nc, 3),
                    _stream_chunk_block(bb, nc, (GROUP_DIM, LANES), (1, 0)),
                    _stream_rows_block(bb, nc, 4), _stream_rows_block(bb, nc, 5),
                    _stream_chunk_block(bb, nc, (N_GATES, LANES), (0, 0)),
                    pl.BlockSpec((1, GROUP_DIM), lambda s: (0, 0))]
        blocks = ([_nbytes((bb, CHUNK, GROUP_DIM), BF16)] * 5
                  + [_nbytes((bb, N_GATES, LANES), F32)])
        out_specs = [_stream_out_block(bb, nc)]
        out_shape = [jax.ShapeDtypeStruct((batch, seq_len, GROUP_DIM), BF16)]
        grid, semantics = (2 * nc,), ("arbitrary",)
    else:
        c_spec = pl.BlockSpec((bb, 1, 2, hp, HEAD_DIM, HEAD_DIM),
                              lambda b, hg: (b, 0, 0, hg, 0, 0))
        n_spec = pl.BlockSpec((bb, 1, 2, N_HEADS, HEAD_DIM), lambda b, hg: (b, 0, 0, 0, 0))
        in_specs = [_group_block(seq_len, hp, 3, bb), _kt_block(nc, hp, 1, bb),
                    _group_block(seq_len, hp, 4, bb), _group_block(seq_len, hp, 5, bb),
                    pl.BlockSpec((bb * nc, N_GATES, LANES), lambda b, hg: (b, 0, 0)),
                    pl.BlockSpec((1, hp * HEAD_DIM), lambda b, hg: (0, hg))]
        blocks = ([_nbytes((bb * seq_len, hp * HEAD_DIM), BF16)] * 5
                  + [_nbytes((bb * nc, N_GATES, LANES), F32)])
        out_specs = [pl.BlockSpec((bb * seq_len, hp * HEAD_DIM), lambda b, hg: (b, hg))]
        out_shape = [jax.ShapeDtypeStruct((batch * seq_len, GROUP_DIM), BF16)]
        grid, semantics = (batch // bb, N_HEADS // hp), ("arbitrary", "arbitrary")
    args = [proj, kt, proj, proj, gates, gn]
    if has_init:
        in_specs += [c_spec, n_spec, pl.BlockSpec(memory_space=pltpu.SMEM)]
        args += list(init)
        blocks.append(_nbytes((bb, 2, hp, HEAD_DIM, HEAD_DIM), F32))
    if emit_states:
        out_specs += [c_spec, n_spec, pl.BlockSpec((bb, 2 * N_HEADS, LANES), lambda b, hg: (b, 0, 0))]
        out_shape += [jax.ShapeDtypeStruct((batch, 1, 2, N_HEADS, HEAD_DIM, HEAD_DIM), F32),
                      jax.ShapeDtypeStruct((batch, 1, 2, N_HEADS, HEAD_DIM), F32),
                      jax.ShapeDtypeStruct((batch, 2 * N_HEADS, LANES), F32)]
        blocks.append(_nbytes((bb, 2, hp, HEAD_DIM, HEAD_DIM), F32))
    ne = bb * hp
    scratch_defs = [((nc, ne, HEAD_DIM, HEAD_DIM), BF16), ((nc, ne, 1, HEAD_DIM), F32),
                    ((nc, ne, 1, LANES), F32), ((ne, HEAD_DIM, HEAD_DIM), F32),
                    ((ne, 1, HEAD_DIM), F32), ((ne, 1, LANES), F32)]
    scratch_bytes = sum(_nbytes(s, d) for s, d in scratch_defs) + 16 * nc * ne * HEAD_DIM * 4
    outs = pl.pallas_call(
        functools.partial(_mlstm_kernel, nc=nc, bb=bb, hp=hp, has_init=has_init,
                          emit_states=emit_states, streamed=streamed),
        grid=grid,
        in_specs=in_specs, out_specs=out_specs, out_shape=out_shape,
        scratch_shapes=[pltpu.VMEM(s, d) for s, d in scratch_defs],
        compiler_params=pltpu.CompilerParams(
            dimension_semantics=semantics,
            vmem_limit_bytes=_vmem_limit(blocks, scratch_bytes)),
        name="mlstm_grid" if has_init else "mlstm_seq",
    )(*args)
    if streamed:
        outs = [outs[0].reshape(batch * seq_len, GROUP_DIM)] + list(outs[1:])
    return outs


def _mix_out_kernel(mr_ref, mm_ref, w_ref, x_ref, g1_ref, lg_ref, lb_ref, o_ref):
    for rc in range(o_ref.shape[0] // MIX_ROW_CHUNK):
        rows = slice(rc * MIX_ROW_CHUNK, (rc + 1) * MIX_ROW_CHUNK)
        mix = (jnp.dot(mr_ref[rows, :], w_ref[0:GROUP_DIM, :], preferred_element_type=F32)
               + jnp.dot(mm_ref[rows, :], w_ref[GROUP_DIM:, :], preferred_element_type=F32))
        o_ref[rows, :] = (_ln_rows(ALPHA * x_ref[rows, :] + g1_ref[0] * mix) * lg_ref[...]
                          + lb_ref[...])


def _mix_out_call(mix_r, mix_m, w_out_b, x2d, mod3, mod_row, ln_g, ln_b, *, tm):
    tokens = x2d.shape[0]
    blocks = [_nbytes((tm, GROUP_DIM), BF16)] * 2 + [_nbytes((D_MODEL, D_MODEL), BF16)] \
        + [_nbytes((tm, D_MODEL), F32)] * 2
    return pl.pallas_call(
        _mix_out_kernel,
        grid=(tokens // tm,),
        in_specs=[pl.BlockSpec((tm, GROUP_DIM), lambda m: (m, 0)),
                  pl.BlockSpec((tm, GROUP_DIM), lambda m: (m, 0)),
                  pl.BlockSpec((D_MODEL, D_MODEL), lambda m: (0, 0)),
                  pl.BlockSpec((tm, D_MODEL), lambda m: (m, 0)),
                  pl.BlockSpec((1, 1, D_MODEL), lambda m: (mod_row(m) * 6 + 2, 0, 0)),
                  pl.BlockSpec((1, D_MODEL), lambda m: (0, 0)),
                  pl.BlockSpec((1, D_MODEL), lambda m: (0, 0))],
        out_specs=pl.BlockSpec((tm, D_MODEL), lambda m: (m, 0)),
        out_shape=jax.ShapeDtypeStruct((tokens, D_MODEL), F32),
        compiler_params=pltpu.CompilerParams(
            dimension_semantics=("parallel",), vmem_limit_bytes=_vmem_limit(blocks)),
        name="mix_out",
    )(mix_r, mix_m, w_out_b, x2d, mod3, ln_g, ln_b)


def _ffn_kernel(x_ref, sh_ref, sc_ref, g2_ref, wu_ref, wg_ref, cv_ref, wd_ref, lg_ref, lb_ref,
                o_ref, h_scr, *, period):
    j = pl.program_id(1)

    def row_chunks():
        for rc in range(o_ref.shape[0] // MIX_ROW_CHUNK):
            yield slice(rc * MIX_ROW_CHUNK, (rc + 1) * MIX_ROW_CHUNK)

    @pl.when(j == 0)
    def _():
        for rows in row_chunks():
            h = _ln_rows(x_ref[rows, :]) * (1.0 + sc_ref[0]) + sh_ref[0]
            h_scr[rows, :] = h.astype(BF16)
        o_ref[...] = jnp.zeros_like(o_ref)

    lane = lax.broadcasted_iota(jnp.int32, (1, FF_TN), 1)
    duplicate = (j == FF_NJ - 1) & (lane < FF_OVERLAP)
    for rows in row_chunks():
        hb = h_scr[rows, :]
        u = jnp.dot(hb, wu_ref[...], preferred_element_type=F32)
        g = jnp.dot(hb, wg_ref[...], preferred_element_type=F32)
        hid = _silu(_conv3_rows(u, cv_ref[0], period)) * g
        hid = jnp.where(duplicate, 0.0, hid).astype(BF16)
        o_ref[rows, :] += jnp.dot(hid, wd_ref[...], preferred_element_type=F32)

    @pl.when(j == FF_NJ - 1)
    def _():
        for rows in row_chunks():
            y = ALPHA * x_ref[rows, :] + g2_ref[0] * o_ref[rows, :]
            o_ref[rows, :] = _ln_rows(y) * lg_ref[...] + lb_ref[...]


def _ff_offset(j):
    return pl.multiple_of(jnp.minimum(j * FF_TN, D_FF - FF_TN), LANES)


def _ffn_call(x1, mod3, mod_row, w_up_b, conv_tiles, w_down_b, ln_g, ln_b, *, period, tm):
    tokens = x1.shape[0]
    up_tile = (pl.Element(D_MODEL), pl.Element(FF_TN))
    blocks = [_nbytes((tm, D_MODEL), F32)] * 2 + [_nbytes((D_MODEL, FF_TN), BF16)] * 3 \
        + [_nbytes((8, FF_TN), F32)]
    scratch_bytes = _nbytes((tm, D_MODEL), BF16)
    return pl.pallas_call(
        functools.partial(_ffn_kernel, period=period),
        grid=(tokens // tm, FF_NJ),
        in_specs=[pl.BlockSpec((tm, D_MODEL), lambda m, j: (m, 0)),
                  pl.BlockSpec((1, 1, D_MODEL), lambda m, j: (mod_row(m) * 6 + 3, 0, 0)),
                  pl.BlockSpec((1, 1, D_MODEL), lambda m, j: (mod_row(m) * 6 + 4, 0, 0)),
                  pl.BlockSpec((1, 1, D_MODEL), lambda m, j: (mod_row(m) * 6 + 5, 0, 0)),
                  pl.BlockSpec(up_tile, lambda m, j: (0, _ff_offset(j))),
                  pl.BlockSpec(up_tile, lambda m, j: (0, pl.multiple_of(D_FF + _ff_offset(j), LANES))),
                  pl.BlockSpec((1, 3, FF_TN), lambda m, j: (j, 0, 0)),
                  pl.BlockSpec((pl.Element(FF_TN), pl.Element(D_MODEL)),
                               lambda m, j: (_ff_offset(j), 0)),
                  pl.BlockSpec((1, D_MODEL), lambda m, j: (0, 0)),
                  pl.BlockSpec((1, D_MODEL), lambda m, j: (0, 0))],
        out_specs=pl.BlockSpec((tm, D_MODEL), lambda m, j: (m, 0)),
        out_shape=jax.ShapeDtypeStruct((tokens, D_MODEL), F32),
        scratch_shapes=[pltpu.VMEM((tm, D_MODEL), BF16)],
        compiler_params=pltpu.CompilerParams(
            dimension_semantics=("parallel", "arbitrary"),
            vmem_limit_bytes=_vmem_limit(blocks, scratch_bytes)),
        name="ffn",
    )(x1, mod3, mod3, mod3, w_up_b, w_up_b, conv_tiles, w_down_b, ln_g, ln_b)


def _rope_tables(seq_len):
    quarter = HEAD_DIM // 4
    t = jnp.arange(seq_len)
    row = (t // GRID_W).astype(F32)
    col = (t % GRID_W).astype(F32)
    inv = ROPE_BASE ** (-jnp.arange(quarter, dtype=F32) / quarter)
    ang_r, ang_c = row[:, None] * inv, col[:, None] * inv
    cos_t = jnp.concatenate([jnp.cos(ang_r)] * 2 + [jnp.cos(ang_c)] * 2, axis=1)
    sin_t = jnp.concatenate([-jnp.sin(ang_r), jnp.sin(ang_r), -jnp.sin(ang_c), jnp.sin(ang_c)], axis=1)
    return cos_t, sin_t


def kernel(x_prompt, x_sample, state_ret, state_mlstm_C, state_mlstm_n, state_mlstm_m, c, c_ctx,
           w_mod, b_mod, w_in, b_gate, conv_qk, ret_theta, gn_ret, gn_mlstm, w_out,
           ln1_g, ln1_b, w_up, conv_ff, w_down, ln2_g, ln2_b):
    bp, lp, _ = x_prompt.shape
    bs, ls, _ = x_sample.shape
    layer = 0
    tm_mix, tm, tm_ffn = 1024, 512, 1024

    cvec = jnp.concatenate([c_ctx[None], c, jnp.zeros((8 - 1 - bs, D_MODEL), F32)], axis=0)
    mod = _mod_call(cvec, w_mod[layer], b_mod[layer][None])
    mod3 = mod.reshape(8 * 6, 1, D_MODEL)

    def row_prompt(tile):
        return lambda m: 0

    def row_sample(tile):
        return lambda m: 1 + (m * tile) // ls

    w_l = w_in[layer].T
    w_gate = w_l[N_PROJ:]
    b_gate_col = b_gate[layer].reshape(N_GATES, 1)
    ff_offsets = [min(j * FF_TN, D_FF - FF_TN) for j in range(FF_NJ)]
    conv_tiles = jnp.stack([conv_ff[layer][:, o:o + FF_TN] for o in ff_offsets])
    gn_r = gn_ret[layer].reshape(1, GROUP_DIM)
    gn_m = gn_mlstm[layer].reshape(1, GROUP_DIM)
    theta = ret_theta[layer]
    ln1 = (ln1_g[layer][None], ln1_b[layer][None])
    ln2 = (ln2_g[layer][None], ln2_b[layer][None])

    def mixer(x, mod_row, rope, period, bb, hp, ret_init, mlstm_init, emit_states, stage=(),
              streamed=False):
        batch, seq_len, _ = x.shape
        x2d = x.reshape(batch * seq_len, D_MODEL)
        h2d, gates = _ln_mod_call(x2d, mod3, mod_row(tm_mix), w_gate, b_gate_col, tm=tm_mix)
        proj, kt, *staged = _mix_in_call(h2d, w_l, conv_qk[layer], rope, seq_len=seq_len,
                                         period=period, tm=tm_mix, stage=stage)
        ret = _ret_call(proj, kt, theta, gn_r, ret_init, batch=batch, seq_len=seq_len, bb=bb,
                        hp=hp, emit_states=emit_states, streamed=streamed)
        mls = _mlstm_call(proj, kt, gates, gn_m, mlstm_init, batch=batch, seq_len=seq_len, bb=bb,
                          hp=hp, emit_states=emit_states, streamed=streamed)
        return x2d, ret, mls, staged

    def channel_mix(x2d, shape, ret, mls, mod_row, period, weights):
        w_up_b, w_down_b, w_out_b = weights
        x1 = _mix_out_call(ret[0], mls[0], w_out_b, x2d, mod3, mod_row(tm), *ln1, tm=tm)
        y = _ffn_call(x1, mod3, mod_row(tm_ffn), w_up_b, conv_tiles, w_down_b, *ln2,
                      period=period, tm=tm_ffn)
        return y.reshape(shape)

    xp, ret_p, mls_p, weights_b = mixer(x_prompt, row_prompt, None, lp, 2, N_HEADS, None, None,
                                        True, stage=(w_up[layer], w_down[layer], w_out[layer]))
    m0 = state_mlstm_m.reshape(-1)
    xs, ret_s, mls_s, _ = mixer(x_sample, row_sample, _rope_tables(ls), GRID_W, bs, N_HEADS,
                                state_ret, (state_mlstm_C, state_mlstm_n, m0), False,
                                streamed=True)
    y_prompt = channel_mix(xp, x_prompt.shape, ret_p, mls_p, row_prompt, lp, weights_b)
    y_sample = channel_mix(xs, x_sample.shape, ret_s, mls_s, row_sample, GRID_W, weights_b)
    ret_states, mls_states = ret_p[1:], mls_p[1:]

    new_ret = ret_states[0]
    new_c, new_n, new_m = mls_states
    new_m = new_m[:, :, 0].reshape(bp, 1, 2, N_HEADS)
    return (y_prompt, y_sample, new_ret, new_c, new_n, new_m)
```

```python
import functools

import jax
import jax.numpy as jnp
from jax import lax
from jax.experimental import pallas as pl
from jax.experimental.pallas import tpu as pltpu

F32 = jnp.float32
BF16 = jnp.bfloat16

D_MODEL = 2048
GRID_W = 64
HEAD_DIM = 256
N_HEADS = 4
GROUP_DIM = N_HEADS * HEAD_DIM
N_PROJ = 8 * GROUP_DIM
N_GATES = 4 * N_HEADS
D_FF = 5504
CHUNK = 128
ROPE_BASE = 10000.0
ALPHA = 2.0 ** 0.25
LN_EPS = 1e-6
K_SCALE = HEAD_DIM ** -0.5

LANES = 128
BF16_ROWS = 16
V7X_VMEM_LIMIT_CAP = 60 * 1024 * 1024
VMEM_TEMP_ALLOWANCE = 16 * 1024 * 1024

MIX_ROW_CHUNK = 512
N_STORED = 6 * GROUP_DIM
STREAM_CHUNKS = 2
FF_TN = 512
FF_NJ = -(-D_FF // FF_TN)
FF_OVERLAP = FF_NJ * FF_TN - D_FF


def _vmem_limit(block_bytes, scratch_bytes=0):
    est = 2 * sum(block_bytes) + scratch_bytes + VMEM_TEMP_ALLOWANCE
    return int(min(V7X_VMEM_LIMIT_CAP, est))


def _nbytes(shape, dtype):
    n = 1
    for s in shape:
        n *= s
    return n * jnp.dtype(dtype).itemsize


def _ln_rows(x):
    mu = jnp.mean(x, axis=-1, keepdims=True)
    xc = x - mu
    var = jnp.mean(xc * xc, axis=-1, keepdims=True)
    return xc * lax.rsqrt(var + LN_EPS)


def _silu(x):
    return x * jax.nn.sigmoid(x)


def _log_sigmoid(x):
    return jnp.minimum(x, 0.0) - jnp.log1p(jnp.exp(-jnp.abs(x)))


def _conv3_rows(u, w, period):
    rows = u.shape[0]
    t = lax.broadcasted_iota(jnp.int32, (rows, 1), 0) & (period - 1)
    prev = jnp.where(t == 0, 0.0, pltpu.roll(u, 1, 0))
    nxt = jnp.where(t == period - 1, 0.0, pltpu.roll(u, rows - 1, 0))
    return prev * w[0:1] + u * w[1:2] + nxt * w[2:3]


def _cumsum_lanes(x, lane, reverse):
    s = 1
    while s < LANES:
        if reverse:
            x = x + jnp.where(lane < LANES - s, pltpu.roll(x, LANES - s, 1), 0.0)
        else:
            x = x + jnp.where(lane >= s, pltpu.roll(x, s, 1), 0.0)
        s *= 2
    return x


def _mod_kernel(c_ref, w_ref, b_ref, o_ref):
    a = _silu(c_ref[...]).astype(BF16)
    o_ref[...] = jnp.dot(a, w_ref[...].astype(BF16), preferred_element_type=F32) + b_ref[...]


def _mod_call(cvec, w_mod, b_mod):
    rows, n = cvec.shape[0], w_mod.shape[1]
    tn = 1024
    blocks = [_nbytes((rows, D_MODEL), F32), _nbytes((D_MODEL, tn), F32),
              _nbytes((8, tn), F32), _nbytes((rows, tn), F32)]
    return pl.pallas_call(
        _mod_kernel,
        grid=(n // tn,),
        in_specs=[pl.BlockSpec((rows, D_MODEL), lambda j: (0, 0)),
                  pl.BlockSpec((D_MODEL, tn), lambda j: (0, j)),
                  pl.BlockSpec((1, tn), lambda j: (0, j))],
        out_specs=pl.BlockSpec((rows, tn), lambda j: (0, j)),
        out_shape=jax.ShapeDtypeStruct((rows, n), F32),
        compiler_params=pltpu.CompilerParams(
            dimension_semantics=("arbitrary",), vmem_limit_bytes=_vmem_limit(blocks)),
        name="mod",
    )(cvec, w_mod, b_mod)


def _ln_mod_kernel(x_ref, sh_ref, sc_ref, wg_ref, bg_ref, h_ref, gates_ref, *, tm):
    h = _ln_rows(x_ref[...]) * (1.0 + sc_ref[0]) + sh_ref[0]
    hb = h.astype(BF16)
    h_ref[...] = hb
    g = lax.dot_general(wg_ref[...].astype(BF16), hb, (((1,), (1,)), ((), ())),
                        preferred_element_type=F32) + bg_ref[...]
    row = lax.broadcasted_iota(jnp.int32, (N_GATES, LANES), 0)
    lane = lax.broadcasted_iota(jnp.int32, (N_GATES, LANES), 1)
    kind = lax.shift_right_logical(row, 2)
    for s in range(tm // LANES):
        gs = g[:, s * LANES:(s + 1) * LANES]
        ls = _log_sigmoid(gs)
        gates_ref[s] = jnp.where(kind == 1, _cumsum_lanes(ls, lane, False),
                                 jnp.where(kind == 3, _cumsum_lanes(ls, lane, True), gs))


def _ln_mod_call(x2d, mod3, mod_row, w_gate, b_gate_col, *, tm):
    tokens = x2d.shape[0]
    slabs = tm // LANES
    blocks = [_nbytes((tm, D_MODEL), F32), _nbytes((tm, D_MODEL), BF16),
              _nbytes((N_GATES, D_MODEL), F32), _nbytes((slabs, N_GATES, LANES), F32)]
    return pl.pallas_call(
        functools.partial(_ln_mod_kernel, tm=tm),
        grid=(tokens // tm,),
        in_specs=[pl.BlockSpec((tm, D_MODEL), lambda m: (m, 0)),
                  pl.BlockSpec((1, 1, D_MODEL), lambda m: (mod_row(m) * 6 + 0, 0, 0)),
                  pl.BlockSpec((1, 1, D_MODEL), lambda m: (mod_row(m) * 6 + 1, 0, 0)),
                  pl.BlockSpec((N_GATES, D_MODEL), lambda m: (0, 0)),
                  pl.BlockSpec((N_GATES, 1), lambda m: (0, 0))],
        out_specs=[pl.BlockSpec((tm, D_MODEL), lambda m: (m, 0)),
                   pl.BlockSpec((slabs, N_GATES, LANES), lambda m: (m, 0, 0))],
        out_shape=[jax.ShapeDtypeStruct((tokens, D_MODEL), BF16),
                   jax.ShapeDtypeStruct((tokens // LANES, N_GATES, LANES), F32)],
        compiler_params=pltpu.CompilerParams(
            dimension_semantics=("parallel",), vmem_limit_bytes=_vmem_limit(blocks)),
        name="ln_mod",
    )(x2d, mod3, mod3, w_gate, b_gate_col)


def _mix_in_kernel(*refs, tm, period, grid_mode, n_staged):
    refs = list(refs)
    h_ref, w_ref, cv_ref = refs[:3]
    pos = 3
    if grid_mode:
        cos_ref, sin_ref = refs[pos:pos + 2]
        pos += 2
    staged_in = refs[pos:pos + n_staged]
    pos += n_staged
    proj_ref, kt_ref = refs[pos:pos + 2]
    pos += 2
    staged_out = refs[pos:pos + n_staged]
    wb_scr = refs[pos + n_staged]
    group = pl.program_id(0)

    @pl.when(pl.program_id(1) == 0)
    def _():
        wb_scr[...] = w_ref[...].astype(BF16)

    def chunks(sizes=(tm // 2, tm // 2)):
        for src, dst in zip(staged_in, staged_out):
            dst[...] = src[...].astype(BF16)
        start = 0
        for n_rows in sizes:
            rows = slice(start, start + n_rows)
            yield start, rows, lax.dot_general(h_ref[rows, :], wb_scr[...],
                                               (((1,), (1,)), ((), ())), preferred_element_type=F32)
            start += n_rows
        assert start == tm

    def rope(a, rows):
        out = []
        for s in range(GROUP_DIM // LANES):
            xs = a[:, s * LANES:(s + 1) * LANES]
            t = (s % 2) * LANES
            out.append(xs * cos_ref[rows, t:t + LANES]
                       + pltpu.roll(xs, LANES // 2, 1) * sin_ref[rows, t:t + LANES])
        return jnp.concatenate(out, axis=1)

    def store_transposed(start, y):
        yt = y.T.astype(BF16)
        for s in range(y.shape[0] // LANES):
            kt_ref[start // LANES + s] = yt[:, s * LANES:(s + 1) * LANES]

    plain = (group == 2) | (group == 3) | (group == 6) | (group == 7)
    if not grid_mode:
        plain = plain | (group == 0)

    @pl.when(plain)
    def _():
        for _, rows, acc in chunks((tm,)):
            proj_ref[rows, :] = acc.astype(BF16)

    if grid_mode:
        @pl.when(group == 0)
        def _():
            for _, rows, acc in chunks():
                proj_ref[rows, :] = rope(acc, rows).astype(BF16)

    @pl.when(group == 1)
    def _():
        for rc, rows, acc in chunks():
            k = acc * K_SCALE
            store_transposed(rc, rope(k, rows) if grid_mode else k)

    @pl.when(group == 4)
    def _():
        for _, rows, acc in chunks():
            proj_ref[rows, :] = _silu(_conv3_rows(acc, cv_ref[...], period)).astype(BF16)

    @pl.when(group == 5)
    def _():
        for rc, _, acc in chunks():
            store_transposed(rc, _silu(_conv3_rows(acc, cv_ref[...], period)) * K_SCALE)


def _staging_specs(weights, n_groups, n_tiles):
    specs, shapes, block_bytes = [], [], []
    for w in weights:
        rows, cols = w.shape
        slab = BF16_ROWS
        while rows % slab or rows // slab > n_groups * n_tiles:
            slab += BF16_ROWS
        n_slabs = rows // slab
        specs.append(pl.BlockSpec(
            (slab, cols),
            lambda g, m, n_slabs=n_slabs: (jnp.minimum(g * n_tiles + m, n_slabs - 1), 0)))
        shapes.append(jax.ShapeDtypeStruct((rows, cols), BF16))
        block_bytes.append(_nbytes((slab, cols), F32) + _nbytes((slab, cols), BF16))
    return specs, shapes, block_bytes


def _mix_in_call(h2d, w_in_t, conv_qk, rope, *, seq_len, period, tm, stage=()):
    tokens = h2d.shape[0]
    grid_mode = rope is not None
    tiles_per_seq = seq_len // tm
    slabs = tm // LANES
    last = tokens // tm - 1
    n_groups = N_PROJ // GROUP_DIM
    stage_specs, stage_shapes, stage_bytes = _staging_specs(stage, n_groups, tokens // tm)

    def tile(g, m):
        return jnp.where(g % 2 == 1, last - m, m)

    def is_key(g):
        return (g == 1) | (g == 5)

    def proj_index(g, m):
        col = g - (g >= 1).astype(jnp.int32) - (g >= 5).astype(jnp.int32)
        return jnp.where(is_key(g), last, tile(g, m)), col

    def kt_index(g, m):
        row = jnp.where(g == 0, last, jnp.where(is_key(g), tile(g, m), 0))
        return row, (g >= 5).astype(jnp.int32), 0

    in_specs = [
        pl.BlockSpec((tm, D_MODEL), lambda g, m: (tile(g, m), 0)),
        pl.BlockSpec((GROUP_DIM, D_MODEL), lambda g, m: (g, 0)),
        pl.BlockSpec((3, GROUP_DIM), lambda g, m: (0, jnp.clip(g - 4, 0, 1))),
    ]
    args = [h2d, w_in_t, conv_qk]
    blocks = [_nbytes((tm, D_MODEL), BF16), _nbytes((GROUP_DIM, D_MODEL), F32),
              _nbytes((tm, GROUP_DIM), BF16) * 2, _nbytes((8, GROUP_DIM), F32)]
    if grid_mode:
        in_specs += [pl.BlockSpec((tm, HEAD_DIM),
                                  lambda g, m: (jnp.where(g <= 1, tile(g, m) % tiles_per_seq, 0),
                                                0))] * 2
        args += list(rope)
        blocks += [_nbytes((tm, HEAD_DIM), F32)] * 2
    in_specs += stage_specs
    args += list(stage)
    blocks += stage_bytes
    scratch_defs = [((GROUP_DIM, D_MODEL), BF16)]

    return pl.pallas_call(
        functools.partial(_mix_in_kernel, tm=tm, period=period, grid_mode=grid_mode,
                          n_staged=len(stage)),
        grid=(n_groups, tokens // tm),
        in_specs=in_specs,
        out_specs=[pl.BlockSpec((tm, GROUP_DIM), lambda g, m: proj_index(g, m)),
                   pl.BlockSpec((slabs, GROUP_DIM, LANES), lambda g, m: kt_index(g, m))]
        + stage_specs,
        out_shape=[jax.ShapeDtypeStruct((tokens, N_STORED), BF16),
                   jax.ShapeDtypeStruct((tokens // LANES, 2 * GROUP_DIM, LANES), BF16)]
        + stage_shapes,
        scratch_shapes=[pltpu.VMEM(s, d) for s, d in scratch_defs],
        compiler_params=pltpu.CompilerParams(
            dimension_semantics=("arbitrary", "arbitrary"),
            vmem_limit_bytes=_vmem_limit(blocks, sum(_nbytes(s, d) for s, d in scratch_defs))),
        name="mix_in_grid" if grid_mode else "mix_in_seq",
    )(*args)


_NN = (((2,), (1,)), ((0,), (0,)))
_NT = (((2,), (2,)), ((0,), (0,)))


def _bdot(x, y, dims):
    return lax.dot_general(x, y, dims, preferred_element_type=F32)


def _chunk_loop(nc, body, reverse=False):
    if nc <= 2:
        for i in range(nc):
            body(nc - 1 - i if reverse else i)
    else:
        def step(i, carry):
            body(nc - 1 - i if reverse else i)
            return carry
        lax.fori_loop(0, nc, step, 0)


def _rows(c, base=0):
    start = base + c * CHUNK
    if not isinstance(start, int):
        start = pl.multiple_of(start, CHUNK)
    return pl.ds(start, CHUNK)


def _is_chunk(c, value):
    return isinstance(c, int) and c == value


class _StepChunk:
    def __init__(self, index, local):
        self.index, self.local = index, local


def _chunk_index(c):
    return c.index if isinstance(c, _StepChunk) else c


class _Chunks:
    def __init__(self, nc, bb, hp, streamed):
        self.nc, self.bb, self.hp, self.streamed = nc, bb, hp, streamed

    def rows(self, ref, bi, c, cols):
        if self.streamed:
            return ref[bi, c.local * CHUNK:(c.local + 1) * CHUNK, cols]
        return ref[_rows(c, bi * self.nc * CHUNK), cols]

    def store_rows(self, ref, bi, c, cols, value):
        if self.streamed:
            ref[bi, c.local * CHUNK:(c.local + 1) * CHUNK, cols] = value
        else:
            ref[_rows(c, bi * self.nc * CHUNK), cols] = value

    def per_chunk(self, ref, bi, c):
        return ref[bi, c.local] if self.streamed else ref[bi * self.nc + c]

    def heads(self, ref, c):
        return jnp.stack([self.rows(ref, bi, c, slice(hh * HEAD_DIM, (hh + 1) * HEAD_DIM))
                          for bi in range(self.bb) for hh in range(self.hp)])

    def keys_t(self, kt_ref, c):
        parts = [self.per_chunk(kt_ref, bi, c).reshape(self.hp, HEAD_DIM, CHUNK)
                 for bi in range(self.bb)]
        return parts[0] if self.bb == 1 else jnp.concatenate(parts, axis=0)

    def gate_row(self, g_ref, bi, c, row):
        if self.streamed:
            return g_ref[bi, c.local, pl.ds(row, 1), :]
        return g_ref[bi * self.nc + c, pl.ds(row, 1), :]


def _run_scan(nc, streamed, init_state, fwd, bwd, emit_state):
    if not streamed:
        init_state(0)
        _chunk_loop(nc, fwd)
        if emit_state is not None:
            emit_state(0)
        init_state(1)
        _chunk_loop(nc, bwd, reverse=True)
        if emit_state is not None:
            emit_state(1)
        return
    assert emit_state is None
    s = pl.program_id(0)
    steps = _stream_steps(nc)

    @pl.when(s == 0)
    def _():
        init_state(0)

    @pl.when(s == steps)
    def _():
        init_state(1)

    @pl.when(s < steps)
    def _():
        for local in range(STREAM_CHUNKS):
            fwd(_StepChunk(s * STREAM_CHUNKS + local, local))

    @pl.when(s >= steps)
    def _():
        for local in reversed(range(STREAM_CHUNKS)):
            bwd(_StepChunk((2 * steps - 1 - s) * STREAM_CHUNKS + local, local))


def _per_seq(x, bb):
    return x if bb == 1 else jnp.concatenate([x] * bb, axis=0)


def _group_block(seq_len, hp, group, bb):
    per = N_HEADS // hp
    return pl.BlockSpec((bb * seq_len, hp * HEAD_DIM), lambda b, hg: (b, group * per + hg))


def _kt_block(nc, hp, group, bb):
    per = N_HEADS // hp
    return pl.BlockSpec((bb * nc, hp * HEAD_DIM, LANES), lambda b, hg: (b, group * per + hg, 0))


def _stream_steps(nc):
    assert nc % STREAM_CHUNKS == 0
    return nc // STREAM_CHUNKS


def _step_block(s, nc):
    steps = _stream_steps(nc)
    return jnp.where(s < steps, s, 2 * steps - 1 - s)


def _stream_rows_block(bb, nc, group):
    return pl.BlockSpec((bb, STREAM_CHUNKS * CHUNK, GROUP_DIM),
                        lambda s: (0, _step_block(s, nc), group))


def _stream_out_block(bb, nc):
    steps = _stream_steps(nc)
    return pl.BlockSpec((bb, STREAM_CHUNKS * CHUNK, GROUP_DIM),
                        lambda s: (0, jnp.minimum(2 * steps - 1 - s, steps - 1), 0))


def _stream_chunk_block(bb, nc, tail, tail_index):
    return pl.BlockSpec((bb, STREAM_CHUNKS) + tail, lambda s: (0, _step_block(s, nc)) + tail_index)


def _ret_kernel(*refs, nc, bb, hp, has_init, emit_states, streamed):
    refs = list(refs)
    theta_ref, q_ref, kt_ref, v_ref, rg_ref, gn_ref = refs[:6]
    pos = 6
    s0_ref = None
    if has_init:
        s0_ref = refs[pos]
        pos += 1
    o_ref = refs[pos]
    pos += 1
    st_ref = None
    if emit_states:
        st_ref = refs[pos]
        pos += 1
    hist, s_scr, decay_scr, qdec_scr, kdec_scr, cdec_scr = refs[pos:pos + 6]
    ck = _Chunks(nc, bb, hp, streamed)
    hg = 0 if streamed else pl.program_id(1)
    hsel = pl.ds(hg * hp, hp)

    @pl.when(pl.program_id(0) == 0)
    def _():
        ii = lax.broadcasted_iota(jnp.int32, (CHUNK, CHUNK), 0)
        jj = lax.broadcasted_iota(jnp.int32, (CHUNK, CHUNK), 1)
        d = (ii - jj).astype(F32)
        p_col = lax.broadcasted_iota(jnp.int32, (CHUNK, HEAD_DIM), 0).astype(F32)
        p_row = lax.broadcasted_iota(jnp.int32, (1, LANES), 1).astype(F32)
        for hh in range(hp):
            h = hg * hp + hh
            lg_f = _log_sigmoid(jnp.full((1, LANES), theta_ref[0, h], F32))
            lg_b = _log_sigmoid(jnp.full((1, LANES), theta_ref[1, h], F32))
            decay_scr[h] = (jnp.where(d >= 0, jnp.exp(lg_f * jnp.maximum(d, 0.0)), 0.0)
                            + jnp.where(d <= 0, jnp.exp(lg_b * jnp.maximum(-d, 0.0)), 0.0))
            qdec_scr[0, h] = jnp.exp(lg_f[:, :1] * (p_col + 1.0))
            qdec_scr[1, h] = jnp.exp(lg_b[:, :1] * (CHUNK - p_col))
            kdec_scr[0, h] = jnp.exp(lg_f * (CHUNK - 1.0 - p_row))
            kdec_scr[1, h] = jnp.exp(lg_b * p_row)
            cdec_scr[0, h] = jnp.exp(lg_f * float(CHUNK))
            cdec_scr[1, h] = jnp.exp(lg_b * float(CHUNK))

    def table(ref, *lead):
        return _per_seq(ref[(*lead, hsel)], bb)

    def kv_update(c, v, direction):
        kd = (ck.keys_t(kt_ref, c).astype(F32)
              * table(kdec_scr, direction)).astype(BF16)
        s_scr[...] = s_scr[...] * table(cdec_scr, direction)[:, :, :1] + _bdot(kd, v, _NN)

    def init_state(direction):
        if has_init:
            for bi in range(bb):
                s_scr[bi * hp:(bi + 1) * hp] = s0_ref[bi, 0, direction]
        else:
            s_scr[...] = jnp.zeros((bb * hp, HEAD_DIM, HEAD_DIM), F32)

    def emit_state(direction):
        for bi in range(bb):
            st_ref[bi, 0, direction] = s_scr[bi * hp:(bi + 1) * hp]

    def fwd(c):
        hist[_chunk_index(c)] = s_scr[...].astype(BF16)
        kv_update(c, ck.heads(v_ref, c), 0)

    def bwd(c):
        q, v = ck.heads(q_ref, c), ck.heads(v_ref, c)
        kt = ck.keys_t(kt_ref, c)
        att = (_bdot(q, kt, _NN) * table(decay_scr)).astype(BF16)
        o = _bdot(att, v, _NN)
        if has_init or not _is_chunk(c, 0):
            o = o + _bdot(q, hist[_chunk_index(c)], _NN) * table(qdec_scr, 0)
        if has_init or not _is_chunk(c, nc - 1):
            o = o + _bdot(q, s_scr[...].astype(BF16), _NN) * table(qdec_scr, 1)
        y = _ln_rows(o)
        for bi in range(bb):
            for hh in range(hp):
                cs = slice(hh * HEAD_DIM, (hh + 1) * HEAD_DIM)
                gate = _silu(ck.rows(rg_ref, bi, c, cs).astype(F32))
                ck.store_rows(o_ref, bi, c, cs,
                              (y[bi * hp + hh] * gn_ref[:, cs] * gate).astype(BF16))
        kv_update(c, v, 1)

    _run_scan(nc, streamed, init_state, fwd, bwd, emit_state if emit_states else None)


def _ret_call(proj, kt, theta, gn, s0, *, batch, seq_len, bb, hp, emit_states, streamed=False):
    nc = seq_len // CHUNK
    has_init = s0 is not None
    assert N_HEADS % hp == 0 and batch % bb == 0
    if streamed:
        assert bb == batch and hp == N_HEADS and not emit_states
        proj = proj.reshape(batch, seq_len, N_STORED)
        kt = kt.reshape(batch, nc, 2 * GROUP_DIM, LANES)
        state_spec = pl.BlockSpec((bb, 1, 2, hp, HEAD_DIM, HEAD_DIM), lambda s: (0,) * 6)
        in_specs = [pl.BlockSpec(memory_space=pltpu.SMEM),
                    _stream_rows_block(bb, nc, 0),
                    _stream_chunk_block(bb, nc, (GROUP_DIM, LANES), (0, 0)),
                    _stream_rows_block(bb, nc, 1), _stream_rows_block(bb, nc, 2),
                    pl.BlockSpec((1, GROUP_DIM), lambda s: (0, 0))]
        blocks = [_nbytes((bb, STREAM_CHUNKS * CHUNK, GROUP_DIM), BF16)] * 5
        out_specs = [_stream_out_block(bb, nc)]
        out_shape = [jax.ShapeDtypeStruct((batch, seq_len, GROUP_DIM), BF16)]
        grid, semantics = (2 * _stream_steps(nc),), ("arbitrary",)
    else:
        state_spec = pl.BlockSpec((bb, 1, 2, hp, HEAD_DIM, HEAD_DIM),
                                  lambda b, hg: (b, 0, 0, hg, 0, 0))
        in_specs = [pl.BlockSpec(memory_space=pltpu.SMEM),
                    _group_block(seq_len, hp, 0, bb), _kt_block(nc, hp, 0, bb),
                    _group_block(seq_len, hp, 1, bb), _group_block(seq_len, hp, 2, bb),
                    pl.BlockSpec((1, hp * HEAD_DIM), lambda b, hg: (0, hg))]
        blocks = [_nbytes((bb * seq_len, hp * HEAD_DIM), BF16)] * 5
        out_specs = [pl.BlockSpec((bb * seq_len, hp * HEAD_DIM), lambda b, hg: (b, hg))]
        out_shape = [jax.ShapeDtypeStruct((batch * seq_len, GROUP_DIM), BF16)]
        grid, semantics = (batch // bb, N_HEADS // hp), ("arbitrary", "arbitrary")
    args = [theta, proj, kt, proj, proj, gn]
    if has_init:
        in_specs.append(state_spec)
        args.append(s0)
        blocks.append(_nbytes((bb, 2, hp, HEAD_DIM, HEAD_DIM), F32))
    if emit_states:
        out_specs.append(state_spec)
        out_shape.append(jax.ShapeDtypeStruct((batch, 1, 2, N_HEADS, HEAD_DIM, HEAD_DIM), F32))
        blocks.append(_nbytes((bb, 2, hp, HEAD_DIM, HEAD_DIM), F32))
    scratch_defs = [((nc, bb * hp, HEAD_DIM, HEAD_DIM), BF16), ((bb * hp, HEAD_DIM, HEAD_DIM), F32),
                    ((N_HEADS, CHUNK, CHUNK), F32), ((2, N_HEADS, CHUNK, HEAD_DIM), F32),
                    ((2, N_HEADS, 1, LANES), F32), ((2, N_HEADS, 1, LANES), F32)]
    outs = pl.pallas_call(
        functools.partial(_ret_kernel, nc=nc, bb=bb, hp=hp, has_init=has_init,
                          emit_states=emit_states, streamed=streamed),
        grid=grid,
        in_specs=in_specs, out_specs=out_specs, out_shape=out_shape,
        scratch_shapes=[pltpu.VMEM(s, d) for s, d in scratch_defs],
        compiler_params=pltpu.CompilerParams(
            dimension_semantics=semantics,
            vmem_limit_bytes=_vmem_limit(blocks, sum(_nbytes(s, d) for s, d in scratch_defs))),
        name="ret_grid" if has_init else "ret_seq",
    )(*args)
    if streamed:
        outs = [outs[0].reshape(batch * seq_len, GROUP_DIM)] + list(outs[1:])
    return outs


def _mlstm_kernel(*refs, nc, bb, hp, has_init, emit_states, streamed):
    refs = list(refs)
    q_ref, kt_ref, v_ref, mo_ref, g_ref, gn_ref = refs[:6]
    pos = 6
    c0_ref = n0_ref = m0_ref = None
    if has_init:
        c0_ref, n0_ref, m0_ref = refs[pos:pos + 3]
        pos += 3
    o_ref = refs[pos]
    pos += 1
    c_out = n_out = m_out = None
    if emit_states:
        c_out, n_out, m_out = refs[pos:pos + 3]
        pos += 3
    c_hist, n_hist, m_hist, c_scr, n_scr, m_scr = refs[pos:pos + 6]

    ck = _Chunks(nc, bb, hp, streamed)
    b = 0 if streamed else pl.program_id(0)
    hg = 0 if streamed else pl.program_id(1)
    ii = lax.broadcasted_iota(jnp.int32, (CHUNK, CHUNK), 0)
    jj = lax.broadcasted_iota(jnp.int32, (CHUNK, CHUNK), 1)
    eye = ii == jj
    lower = ii >= jj
    upper = ii <= jj

    ne = bb * hp

    def col(row):
        return jnp.sum(jnp.where(eye, row, 0.0), axis=2, keepdims=True)

    def gate_rows(c, kind):
        base = kind * N_HEADS + hg * hp
        return jnp.stack([ck.gate_row(g_ref, bi, c, base + hh)
                          for bi in range(bb) for hh in range(hp)])

    def direction_out(a, q, qf, v, c_b, n_row, m, b_row, i_row, mask):
        b_col = col(b_row)
        dm = jnp.where(mask, b_col - b_row + i_row, -jnp.inf)
        inter = b_col + m
        mt = jnp.maximum(inter, jnp.max(dm, axis=2, keepdims=True))
        w = jnp.exp(dm - mt)
        sp = jnp.exp(inter - mt)
        s = a * w
        num = _bdot(s.astype(BF16), v, _NN)
        den = jnp.sum(s, axis=2, keepdims=True)
        if c_b is not None:
            num = num + _bdot(q, c_b, _NN) * sp
            den = den + jnp.sum(qf * n_row, axis=2, keepdims=True) * sp
        return num / jnp.maximum(jnp.abs(den), jnp.exp(-mt))

    def state_update(c, v, b_row, i_row, last):
        m = m_scr[...][:, :, :1]
        b_last = b_row[:, :, last:last + 1]
        g = b_last - b_row + i_row
        m_new = jnp.maximum(b_last + m, jnp.max(g, axis=2, keepdims=True))
        wk = jnp.exp(g - m_new)
        sc = jnp.exp(b_last + m - m_new)
        kt = ck.keys_t(kt_ref, c)
        kw = (kt.astype(F32) * wk).astype(BF16)
        c_scr[...] = c_scr[...] * sc + _bdot(kw, v, _NN)
        wk8 = jnp.broadcast_to(wk, (ne, 8, CHUNK)).astype(BF16)
        n_scr[...] = n_scr[...] * sc + _bdot(wk8, kt, _NT)[:, :1, :]
        m_scr[...] = jnp.broadcast_to(m_new, (ne, 1, LANES))

    def init_state(direction):
        if has_init:
            for bi in range(bb):
                c_scr[bi * hp:(bi + 1) * hp] = c0_ref[bi, 0, direction]
            n_scr[...] = jnp.stack([n0_ref[bi, 0, direction, pl.ds(hg * hp + hh, 1), :]
                                    for bi in range(bb) for hh in range(hp)])
            m_scr[...] = jnp.stack([
                jnp.full((1, LANES),
                         m0_ref[((b * bb + bi) * 2 + direction) * N_HEADS + hg * hp + hh], F32)
                for bi in range(bb) for hh in range(hp)])
        else:
            c_scr[...] = jnp.zeros((ne, HEAD_DIM, HEAD_DIM), F32)
            n_scr[...] = jnp.zeros((ne, 1, HEAD_DIM), F32)
            m_scr[...] = jnp.zeros((ne, 1, LANES), F32)

    def emit_state(direction):
        for bi in range(bb):
            c_out[bi, 0, direction] = c_scr[bi * hp:(bi + 1) * hp]
            for hh in range(hp):
                n_out[bi, 0, direction, pl.ds(hh, 1), :] = n_scr[bi * hp + hh]
                m_out[bi, pl.ds(direction * N_HEADS + hh, 1), :] = m_scr[bi * hp + hh]

    def fwd(c):
        ci = _chunk_index(c)
        c_hist[ci] = c_scr[...].astype(BF16)
        n_hist[ci] = n_scr[...]
        m_hist[ci] = m_scr[...]
        state_update(c, ck.heads(v_ref, c), gate_rows(c, 1), gate_rows(c, 0), CHUNK - 1)

    def bwd(c):
        i_f, b_f, i_b, b_b = (gate_rows(c, kind) for kind in (0, 1, 2, 3))
        q, v = ck.heads(q_ref, c), ck.heads(v_ref, c)
        qf = q.astype(F32)
        a = _bdot(q, ck.keys_t(kt_ref, c), _NN)
        zero_f = not has_init and _is_chunk(c, 0)
        zero_b = not has_init and _is_chunk(c, nc - 1)
        ci = _chunk_index(c)
        h_f = direction_out(a, q, qf, v, None if zero_f else c_hist[ci], n_hist[ci],
                            m_hist[ci][:, :, :1], b_f, i_f, lower)
        h_b = direction_out(a, q, qf, v, None if zero_b else c_scr[...].astype(BF16),
                            n_scr[...], m_scr[...][:, :, :1], b_b, i_b, upper)
        y = _ln_rows(h_f + h_b)
        for bi in range(bb):
            for hh in range(hp):
                cs = slice(hh * HEAD_DIM, (hh + 1) * HEAD_DIM)
                gate = jax.nn.sigmoid(ck.rows(mo_ref, bi, c, cs).astype(F32))
                ck.store_rows(o_ref, bi, c, cs,
                              (gate * (y[bi * hp + hh] * gn_ref[:, cs])).astype(BF16))
        state_update(c, v, b_b, i_b, 0)

    _run_scan(nc, streamed, init_state, fwd, bwd, emit_state if emit_states else None)


def _mlstm_call(proj, kt, gates, gn, init, *, batch, seq_len, bb, hp, emit_states,
                streamed=False):
    nc = seq_len // CHUNK
    has_init = init is not None
    assert N_HEADS % hp == 0 and (hp == N_HEADS or not emit_states) and batch % bb == 0
    if streamed:
        assert bb == batch and hp == N_HEADS and not emit_states
        proj = proj.reshape(batch, seq_len, N_STORED)
        kt = kt.reshape(batch, nc, 2 * GROUP_DIM, LANES)
        gates = gates.reshape(batch, nc, N_GATES, LANES)
        c_spec = pl.BlockSpec((bb, 1, 2, hp, HEAD_DIM, HEAD_DIM), lambda s: (0,) * 6)
        n_spec = pl.BlockSpec((bb, 1, 2, N_HEADS, HEAD_DIM), lambda s: (0,) * 5)
        in_specs = [_stream_rows_block(bb, nc, 3),
                    _stream_chunk_block(bb, nc, (GROUP_DIM, LANES), (1, 0)),
                    _stream_rows_block(bb, nc, 4), _stream_rows_block(bb, nc, 5),
                    _stream_chunk_block(bb, nc, (N_GATES, LANES), (0, 0)),
                    pl.BlockSpec((1, GROUP_DIM), lambda s: (0, 0))]
        blocks = ([_nbytes((bb, STREAM_CHUNKS * CHUNK, GROUP_DIM), BF16)] * 5
                  + [_nbytes((bb, STREAM_CHUNKS * N_GATES, LANES), F32)])
        out_specs = [_stream_out_block(bb, nc)]
        out_shape = [jax.ShapeDtypeStruct((batch, seq_len, GROUP_DIM), BF16)]
        grid, semantics = (2 * _stream_steps(nc),), ("arbitrary",)
    else:
        c_spec = pl.BlockSpec((bb, 1, 2, hp, HEAD_DIM, HEAD_DIM),
                              lambda b, hg: (b, 0, 0, hg, 0, 0))
        n_spec = pl.BlockSpec((bb, 1, 2, N_HEADS, HEAD_DIM), lambda b, hg: (b, 0, 0, 0, 0))
        in_specs = [_group_block(seq_len, hp, 3, bb), _kt_block(nc, hp, 1, bb),
                    _group_block(seq_len, hp, 4, bb), _group_block(seq_len, hp, 5, bb),
                    pl.BlockSpec((bb * nc, N_GATES, LANES), lambda b, hg: (b, 0, 0)),
                    pl.BlockSpec((1, hp * HEAD_DIM), lambda b, hg: (0, hg))]
        blocks = ([_nbytes((bb * seq_len, hp * HEAD_DIM), BF16)] * 5
                  + [_nbytes((bb * nc, N_GATES, LANES), F32)])
        out_specs = [pl.BlockSpec((bb * seq_len, hp * HEAD_DIM), lambda b, hg: (b, hg))]
        out_shape = [jax.ShapeDtypeStruct((batch * seq_len, GROUP_DIM), BF16)]
        grid, semantics = (batch // bb, N_HEADS // hp), ("arbitrary", "arbitrary")
    args = [proj, kt, proj, proj, gates, gn]
    if has_init:
        in_specs += [c_spec, n_spec, pl.BlockSpec(memory_space=pltpu.SMEM)]
        args += list(init)
        blocks.append(_nbytes((bb, 2, hp, HEAD_DIM, HEAD_DIM), F32))
    if emit_states:
        out_specs += [c_spec, n_spec, pl.BlockSpec((bb, 2 * N_HEADS, LANES), lambda b, hg: (b, 0, 0))]
        out_shape += [jax.ShapeDtypeStruct((batch, 1, 2, N_HEADS, HEAD_DIM, HEAD_DIM), F32),
                      jax.ShapeDtypeStruct((batch, 1, 2, N_HEADS, HEAD_DIM), F32),
                      jax.ShapeDtypeStruct((batch, 2 * N_HEADS, LANES), F32)]
        blocks.append(_nbytes((bb, 2, hp, HEAD_DIM, HEAD_DIM), F32))
    ne = bb * hp
    scratch_defs = [((nc, ne, HEAD_DIM, HEAD_DIM), BF16), ((nc, ne, 1, HEAD_DIM), F32),
                    ((nc, ne, 1, LANES), F32), ((ne, HEAD_DIM, HEAD_DIM), F32),
                    ((ne, 1, HEAD_DIM), F32), ((ne, 1, LANES), F32)]
    scratch_bytes = sum(_nbytes(s, d) for s, d in scratch_defs) + 16 * nc * ne * HEAD_DIM * 4
    outs = pl.pallas_call(
        functools.partial(_mlstm_kernel, nc=nc, bb=bb, hp=hp, has_init=has_init,
                          emit_states=emit_states, streamed=streamed),
        grid=grid,
        in_specs=in_specs, out_specs=out_specs, out_shape=out_shape,
        scratch_shapes=[pltpu.VMEM(s, d) for s, d in scratch_defs],
        compiler_params=pltpu.CompilerParams(
            dimension_semantics=semantics,
            vmem_limit_bytes=_vmem_limit(blocks, scratch_bytes)),
        name="mlstm_grid" if has_init else "mlstm_seq",
    )(*args)
    if streamed:
        outs = [outs[0].reshape(batch * seq_len, GROUP_DIM)] + list(outs[1:])
    return outs


def _mix_out_kernel(mr_ref, mm_ref, w_ref, x_ref, g1_ref, lg_ref, lb_ref, o_ref):
    for rc in range(o_ref.shape[0] // MIX_ROW_CHUNK):
        rows = slice(rc * MIX_ROW_CHUNK, (rc + 1) * MIX_ROW_CHUNK)
        mix = (jnp.dot(mr_ref[rows, :], w_ref[0:GROUP_DIM, :], preferred_element_type=F32)
               + jnp.dot(mm_ref[rows, :], w_ref[GROUP_DIM:, :], preferred_element_type=F32))
        o_ref[rows, :] = (_ln_rows(ALPHA * x_ref[rows, :] + g1_ref[0] * mix) * lg_ref[...]
                          + lb_ref[...])


def _mix_out_call(mix_r, mix_m, w_out_b, x2d, mod3, mod_row, ln_g, ln_b, *, tm):
    tokens = x2d.shape[0]
    blocks = [_nbytes((tm, GROUP_DIM), BF16)] * 2 + [_nbytes((D_MODEL, D_MODEL), BF16)] \
        + [_nbytes((tm, D_MODEL), F32)] * 2
    return pl.pallas_call(
        _mix_out_kernel,
        grid=(tokens // tm,),
        in_specs=[pl.BlockSpec((tm, GROUP_DIM), lambda m: (m, 0)),
                  pl.BlockSpec((tm, GROUP_DIM), lambda m: (m, 0)),
                  pl.BlockSpec((D_MODEL, D_MODEL), lambda m: (0, 0)),
                  pl.BlockSpec((tm, D_MODEL), lambda m: (m, 0)),
                  pl.BlockSpec((1, 1, D_MODEL), lambda m: (mod_row(m) * 6 + 2, 0, 0)),
                  pl.BlockSpec((1, D_MODEL), lambda m: (0, 0)),
                  pl.BlockSpec((1, D_MODEL), lambda m: (0, 0))],
        out_specs=pl.BlockSpec((tm, D_MODEL), lambda m: (m, 0)),
        out_shape=jax.ShapeDtypeStruct((tokens, D_MODEL), F32),
        compiler_params=pltpu.CompilerParams(
            dimension_semantics=("parallel",), vmem_limit_bytes=_vmem_limit(blocks)),
        name="mix_out",
    )(mix_r, mix_m, w_out_b, x2d, mod3, ln_g, ln_b)


def _ffn_kernel(x_ref, sh_ref, sc_ref, g2_ref, wu_ref, wg_ref, cv_ref, wd_ref, lg_ref, lb_ref,
                o_ref, h_scr, *, period):
    j = pl.program_id(1)

    def row_chunks():
        for rc in range(o_ref.shape[0] // MIX_ROW_CHUNK):
            yield slice(rc * MIX_ROW_CHUNK, (rc + 1) * MIX_ROW_CHUNK)

    @pl.when(j == 0)
    def _():
        for rows in row_chunks():
            h = _ln_rows(x_ref[rows, :]) * (1.0 + sc_ref[0]) + sh_ref[0]
            h_scr[rows, :] = h.astype(BF16)
        o_ref[...] = jnp.zeros_like(o_ref)

    lane = lax.broadcasted_iota(jnp.int32, (1, FF_TN), 1)
    duplicate = (j == FF_NJ - 1) & (lane < FF_OVERLAP)
    for rows in row_chunks():
        hb = h_scr[rows, :]
        u = jnp.dot(hb, wu_ref[...], preferred_element_type=F32)
        g = jnp.dot(hb, wg_ref[...], preferred_element_type=F32)
        hid = _silu(_conv3_rows(u, cv_ref[0], period)) * g
        hid = jnp.where(duplicate, 0.0, hid).astype(BF16)
        o_ref[rows, :] += jnp.dot(hid, wd_ref[...], preferred_element_type=F32)

    @pl.when(j == FF_NJ - 1)
    def _():
        for rows in row_chunks():
            y = ALPHA * x_ref[rows, :] + g2_ref[0] * o_ref[rows, :]
            o_ref[rows, :] = _ln_rows(y) * lg_ref[...] + lb_ref[...]


def _ff_offset(j):
    return pl.multiple_of(jnp.minimum(j * FF_TN, D_FF - FF_TN), LANES)


def _ffn_call(x1, mod3, mod_row, w_up_b, conv_tiles, w_down_b, ln_g, ln_b, *, period, tm):
    tokens = x1.shape[0]
    up_tile = (pl.Element(D_MODEL), pl.Element(FF_TN))
    blocks = [_nbytes((tm, D_MODEL), F32)] * 2 + [_nbytes((D_MODEL, FF_TN), BF16)] * 3 \
        + [_nbytes((8, FF_TN), F32)]
    scratch_bytes = _nbytes((tm, D_MODEL), BF16)
    return pl.pallas_call(
        functools.partial(_ffn_kernel, period=period),
        grid=(tokens // tm, FF_NJ),
        in_specs=[pl.BlockSpec((tm, D_MODEL), lambda m, j: (m, 0)),
                  pl.BlockSpec((1, 1, D_MODEL), lambda m, j: (mod_row(m) * 6 + 3, 0, 0)),
                  pl.BlockSpec((1, 1, D_MODEL), lambda m, j: (mod_row(m) * 6 + 4, 0, 0)),
                  pl.BlockSpec((1, 1, D_MODEL), lambda m, j: (mod_row(m) * 6 + 5, 0, 0)),
                  pl.BlockSpec(up_tile, lambda m, j: (0, _ff_offset(j))),
                  pl.BlockSpec(up_tile, lambda m, j: (0, pl.multiple_of(D_FF + _ff_offset(j), LANES))),
                  pl.BlockSpec((1, 3, FF_TN), lambda m, j: (j, 0, 0)),
                  pl.BlockSpec((pl.Element(FF_TN), pl.Element(D_MODEL)),
                               lambda m, j: (_ff_offset(j), 0)),
                  pl.BlockSpec((1, D_MODEL), lambda m, j: (0, 0)),
                  pl.BlockSpec((1, D_MODEL), lambda m, j: (0, 0))],
        out_specs=pl.BlockSpec((tm, D_MODEL), lambda m, j: (m, 0)),
        out_shape=jax.ShapeDtypeStruct((tokens, D_MODEL), F32),
        scratch_shapes=[pltpu.VMEM((tm, D_MODEL), BF16)],
        compiler_params=pltpu.CompilerParams(
            dimension_semantics=("parallel", "arbitrary"),
            vmem_limit_bytes=_vmem_limit(blocks, scratch_bytes)),
        name="ffn",
    )(x1, mod3, mod3, mod3, w_up_b, w_up_b, conv_tiles, w_down_b, ln_g, ln_b)


def _rope_tables(seq_len):
    quarter = HEAD_DIM // 4
    t = jnp.arange(seq_len)
    row = (t // GRID_W).astype(F32)
    col = (t % GRID_W).astype(F32)
    inv = ROPE_BASE ** (-jnp.arange(quarter, dtype=F32) / quarter)
    ang_r, ang_c = row[:, None] * inv, col[:, None] * inv
    cos_t = jnp.concatenate([jnp.cos(ang_r)] * 2 + [jnp.cos(ang_c)] * 2, axis=1)
    sin_t = jnp.concatenate([-jnp.sin(ang_r), jnp.sin(ang_r), -jnp.sin(ang_c), jnp.sin(ang_c)], axis=1)
    return cos_t, sin_t


def kernel(x_prompt, x_sample, state_ret, state_mlstm_C, state_mlstm_n, state_mlstm_m, c, c_ctx,
           w_mod, b_mod, w_in, b_gate, conv_qk, ret_theta, gn_ret, gn_mlstm, w_out,
           ln1_g, ln1_b, w_up, conv_ff, w_down, ln2_g, ln2_b):
    bp, lp, _ = x_prompt.shape
    bs, ls, _ = x_sample.shape
    layer = 0
    tm_mix, tm, tm_ffn = 1024, 512, 1024

    cvec = jnp.concatenate([c_ctx[None], c, jnp.zeros((8 - 1 - bs, D_MODEL), F32)], axis=0)
    mod = _mod_call(cvec, w_mod[layer], b_mod[layer][None])
    mod3 = mod.reshape(8 * 6, 1, D_MODEL)

    def row_prompt(tile):
        return lambda m: 0

    def row_sample(tile):
        return lambda m: 1 + (m * tile) // ls

    w_l = w_in[layer].T
    w_gate = w_l[N_PROJ:]
    b_gate_col = b_gate[layer].reshape(N_GATES, 1)
    ff_offsets = [min(j * FF_TN, D_FF - FF_TN) for j in range(FF_NJ)]
    conv_tiles = jnp.stack([conv_ff[layer][:, o:o + FF_TN] for o in ff_offsets])
    gn_r = gn_ret[layer].reshape(1, GROUP_DIM)
    gn_m = gn_mlstm[layer].reshape(1, GROUP_DIM)
    theta = ret_theta[layer]
    ln1 = (ln1_g[layer][None], ln1_b[layer][None])
    ln2 = (ln2_g[layer][None], ln2_b[layer][None])

    def mixer(x, mod_row, rope, period, bb, hp, ret_init, mlstm_init, emit_states, stage=(),
              streamed=False):
        batch, seq_len, _ = x.shape
        x2d = x.reshape(batch * seq_len, D_MODEL)
        h2d, gates = _ln_mod_call(x2d, mod3, mod_row(tm_mix), w_gate, b_gate_col, tm=tm_mix)
        proj, kt, *staged = _mix_in_call(h2d, w_l, conv_qk[layer], rope, seq_len=seq_len,
                                         period=period, tm=tm_mix, stage=stage)
        ret = _ret_call(proj, kt, theta, gn_r, ret_init, batch=batch, seq_len=seq_len, bb=bb,
                        hp=hp, emit_states=emit_states, streamed=streamed)
        mls = _mlstm_call(proj, kt, gates, gn_m, mlstm_init, batch=batch, seq_len=seq_len, bb=bb,
                          hp=hp, emit_states=emit_states, streamed=streamed)
        return x2d, ret, mls, staged

    def channel_mix(x2d, shape, ret, mls, mod_row, period, weights):
        w_up_b, w_down_b, w_out_b = weights
        x1 = _mix_out_call(ret[0], mls[0], w_out_b, x2d, mod3, mod_row(tm), *ln1, tm=tm)
        y = _ffn_call(x1, mod3, mod_row(tm_ffn), w_up_b, conv_tiles, w_down_b, *ln2,
                      period=period, tm=tm_ffn)
        return y.reshape(shape)

    xp, ret_p, mls_p, weights_b = mixer(x_prompt, row_prompt, None, lp, 2, N_HEADS, None, None,
                                        True, stage=(w_up[layer], w_down[layer], w_out[layer]))
    m0 = state_mlstm_m.reshape(-1)
    xs, ret_s, mls_s, _ = mixer(x_sample, row_sample, _rope_tables(ls), GRID_W, bs, N_HEADS,
                                state_ret, (state_mlstm_C, state_mlstm_n, m0), False,
                                streamed=True)
    y_prompt = channel_mix(xp, x_prompt.shape, ret_p, mls_p, row_prompt, lp, weights_b)
    y_sample = channel_mix(xs, x_sample.shape, ret_s, mls_s, row_sample, GRID_W, weights_b)
    ret_states, mls_states = ret_p[1:], mls_p[1:]

    new_ret = ret_states[0]
    new_c, new_n, new_m = mls_states
    new_m = new_m[:, :, 0].reshape(bp, 1, 2, N_HEADS)
    return (y_prompt, y_sample, new_ret, new_c, new_n, new_m)
```

```python
import functools

import jax
import jax.numpy as jnp
from jax import lax
from jax.experimental import pallas as pl
from jax.experimental.pallas import tpu as pltpu

F32 = jnp.float32
BF16 = jnp.bfloat16

D_MODEL = 2048
GRID_W = 64
HEAD_DIM = 256
N_HEADS = 4
GROUP_DIM = N_HEADS * HEAD_DIM
N_PROJ = 8 * GROUP_DIM
N_GATES = 4 * N_HEADS
D_FF = 5504
CHUNK = 128
ROPE_BASE = 10000.0
ALPHA = 2.0 ** 0.25
LN_EPS = 1e-6
K_SCALE = HEAD_DIM ** -0.5

LANES = 128
BF16_ROWS = 16
V7X_VMEM_LIMIT_CAP = 60 * 1024 * 1024
VMEM_TEMP_ALLOWANCE = 16 * 1024 * 1024

MIX_ROW_CHUNK = 512
N_STORED = 6 * GROUP_DIM
STREAM_CHUNKS = 2
FF_TN = 512
FF_NJ = -(-D_FF // FF_TN)
FF_OVERLAP = FF_NJ * FF_TN - D_FF


def _vmem_limit(block_bytes, scratch_bytes=0):
    est = 2 * sum(block_bytes) + scratch_bytes + VMEM_TEMP_ALLOWANCE
    return int(min(V7X_VMEM_LIMIT_CAP, est))


def _nbytes(shape, dtype):
    n = 1
    for s in shape:
        n *= s
    return n * jnp.dtype(dtype).itemsize


def _ln_rows(x):
    mu = jnp.mean(x, axis=-1, keepdims=True)
    xc = x - mu
    var = jnp.mean(xc * xc, axis=-1, keepdims=True)
    return xc * lax.rsqrt(var + LN_EPS)


def _silu(x):
    return x * jax.nn.sigmoid(x)


def _log_sigmoid(x):
    return jnp.minimum(x, 0.0) - jnp.log1p(jnp.exp(-jnp.abs(x)))


def _conv3_rows(u, w, period):
    rows = u.shape[0]
    t = lax.broadcasted_iota(jnp.int32, (rows, 1), 0) & (period - 1)
    prev = jnp.where(t == 0, 0.0, pltpu.roll(u, 1, 0))
    nxt = jnp.where(t == period - 1, 0.0, pltpu.roll(u, rows - 1, 0))
    return prev * w[0:1] + u * w[1:2] + nxt * w[2:3]


def _cumsum_lanes(x, lane, reverse):
    s = 1
    while s < LANES:
        if reverse:
            x = x + jnp.where(lane < LANES - s, pltpu.roll(x, LANES - s, 1), 0.0)
        else:
            x = x + jnp.where(lane >= s, pltpu.roll(x, s, 1), 0.0)
        s *= 2
    return x


def _mod_kernel(c_ref, w_ref, b_ref, o_ref):
    a = _silu(c_ref[...]).astype(BF16)
    o_ref[...] = jnp.dot(a, w_ref[...].astype(BF16), preferred_element_type=F32) + b_ref[...]


def _mod_call(cvec, w_mod, b_mod):
    rows, n = cvec.shape[0], w_mod.shape[1]
    tn = 1024
    blocks = [_nbytes((rows, D_MODEL), F32), _nbytes((D_MODEL, tn), F32),
              _nbytes((8, tn), F32), _nbytes((rows, tn), F32)]
    return pl.pallas_call(
        _mod_kernel,
        grid=(n // tn,),
        in_specs=[pl.BlockSpec((rows, D_MODEL), lambda j: (0, 0)),
                  pl.BlockSpec((D_MODEL, tn), lambda j: (0, j)),
                  pl.BlockSpec((1, tn), lambda j: (0, j))],
        out_specs=pl.BlockSpec((rows, tn), lambda j: (0, j)),
        out_shape=jax.ShapeDtypeStruct((rows, n), F32),
        compiler_params=pltpu.CompilerParams(
            dimension_semantics=("arbitrary",), vmem_limit_bytes=_vmem_limit(blocks)),
        name="mod",
    )(cvec, w_mod, b_mod)


def _ln_mod_kernel(x_ref, sh_ref, sc_ref, wg_ref, bg_ref, h_ref, gates_ref, *, tm):
    h = _ln_rows(x_ref[...]) * (1.0 + sc_ref[0]) + sh_ref[0]
    hb = h.astype(BF16)
    h_ref[...] = hb
    g = lax.dot_general(wg_ref[...].astype(BF16), hb, (((1,), (1,)), ((), ())),
                        preferred_element_type=F32) + bg_ref[...]
    row = lax.broadcasted_iota(jnp.int32, (N_GATES, LANES), 0)
    lane = lax.broadcasted_iota(jnp.int32, (N_GATES, LANES), 1)
    kind = lax.shift_right_logical(row, 2)
    for s in range(tm // LANES):
        gs = g[:, s * LANES:(s + 1) * LANES]
        ls = _log_sigmoid(gs)
        gates_ref[s] = jnp.where(kind == 1, _cumsum_lanes(ls, lane, False),
                                 jnp.where(kind == 3, _cumsum_lanes(ls, lane, True), gs))


def _ln_mod_call(x2d, mod3, mod_row, w_gate, b_gate_col, *, tm):
    tokens = x2d.shape[0]
    slabs = tm // LANES
    blocks = [_nbytes((tm, D_MODEL), F32), _nbytes((tm, D_MODEL), BF16),
              _nbytes((N_GATES, D_MODEL), F32), _nbytes((slabs, N_GATES, LANES), F32)]
    return pl.pallas_call(
        functools.partial(_ln_mod_kernel, tm=tm),
        grid=(tokens // tm,),
        in_specs=[pl.BlockSpec((tm, D_MODEL), lambda m: (m, 0)),
                  pl.BlockSpec((1, 1, D_MODEL), lambda m: (mod_row(m) * 6 + 0, 0, 0)),
                  pl.BlockSpec((1, 1, D_MODEL), lambda m: (mod_row(m) * 6 + 1, 0, 0)),
                  pl.BlockSpec((N_GATES, D_MODEL), lambda m: (0, 0)),
                  pl.BlockSpec((N_GATES, 1), lambda m: (0, 0))],
        out_specs=[pl.BlockSpec((tm, D_MODEL), lambda m: (m, 0)),
                   pl.BlockSpec((slabs, N_GATES, LANES), lambda m: (m, 0, 0))],
        out_shape=[jax.ShapeDtypeStruct((tokens, D_MODEL), BF16),
                   jax.ShapeDtypeStruct((tokens // LANES, N_GATES, LANES), F32)],
        compiler_params=pltpu.CompilerParams(
            dimension_semantics=("parallel",), vmem_limit_bytes=_vmem_limit(blocks)),
        name="ln_mod",
    )(x2d, mod3, mod3, w_gate, b_gate_col)


def _mix_in_kernel(*refs, tm, period, grid_mode, n_staged):
    refs = list(refs)
    h_ref, w_ref, cv_ref = refs[:3]
    pos = 3
    if grid_mode:
        cos_ref, sin_ref = refs[pos:pos + 2]
        pos += 2
    staged_in = refs[pos:pos + n_staged]
    pos += n_staged
    proj_ref, kt_ref = refs[pos:pos + 2]
    pos += 2
    staged_out = refs[pos:pos + n_staged]
    wb_scr = refs[pos + n_staged]
    group = pl.program_id(0)

    @pl.when(pl.program_id(1) == 0)
    def _():
        wb_scr[...] = w_ref[...].astype(BF16)

    def chunks(sizes=(tm // 2, tm // 2)):
        for src, dst in zip(staged_in, staged_out):
            dst[...] = src[...].astype(BF16)
        start = 0
        for n_rows in sizes:
            rows = slice(start, start + n_rows)
            yield start, rows, lax.dot_general(h_ref[rows, :], wb_scr[...],
                                               (((1,), (1,)), ((), ())), preferred_element_type=F32)
            start += n_rows
        assert start == tm

    def rope(a, rows):
        out = []
        for s in range(GROUP_DIM // LANES):
            xs = a[:, s * LANES:(s + 1) * LANES]
            t = (s % 2) * LANES
            out.append(xs * cos_ref[rows, t:t + LANES]
                       + pltpu.roll(xs, LANES // 2, 1) * sin_ref[rows, t:t + LANES])
        return jnp.concatenate(out, axis=1)

    def store_transposed(start, y):
        yt = y.T.astype(BF16)
        for s in range(y.shape[0] // LANES):
            kt_ref[start // LANES + s] = yt[:, s * LANES:(s + 1) * LANES]

    plain = (group == 2) | (group == 3) | (group == 6) | (group == 7)
    if not grid_mode:
        plain = plain | (group == 0)

    @pl.when(plain)
    def _():
        for _, rows, acc in chunks((tm,)):
            proj_ref[rows, :] = acc.astype(BF16)

    if grid_mode:
        @pl.when(group == 0)
        def _():
            for _, rows, acc in chunks():
                proj_ref[rows, :] = rope(acc, rows).astype(BF16)

    @pl.when(group == 1)
    def _():
        for rc, rows, acc in chunks():
            k = acc * K_SCALE
            store_transposed(rc, rope(k, rows) if grid_mode else k)

    @pl.when(group == 4)
    def _():
        for _, rows, acc in chunks():
            proj_ref[rows, :] = _silu(_conv3_rows(acc, cv_ref[...], period)).astype(BF16)

    @pl.when(group == 5)
    def _():
        for rc, _, acc in chunks():
            store_transposed(rc, _silu(_conv3_rows(acc, cv_ref[...], period)) * K_SCALE)


def _staging_specs(weights, n_groups, n_tiles):
    specs, shapes, block_bytes = [], [], []
    for w in weights:
        rows, cols = w.shape
        slab = BF16_ROWS
        while rows % slab or rows // slab > n_groups * n_tiles:
            slab += BF16_ROWS
        n_slabs = rows // slab
        specs.append(pl.BlockSpec(
            (slab, cols),
            lambda g, m, n_slabs=n_slabs: (jnp.minimum(g * n_tiles + m, n_slabs - 1), 0)))
        shapes.append(jax.ShapeDtypeStruct((rows, cols), BF16))
        block_bytes.append(_nbytes((slab, cols), F32) + _nbytes((slab, cols), BF16))
    return specs, shapes, block_bytes


def _mix_in_call(h2d, w_in_t, conv_qk, rope, *, seq_len, period, tm, stage=()):
    tokens = h2d.shape[0]
    grid_mode = rope is not None
    tiles_per_seq = seq_len // tm
    slabs = tm // LANES
    last = tokens // tm - 1
    n_groups = N_PROJ // GROUP_DIM
    stage_specs, stage_shapes, stage_bytes = _staging_specs(stage, n_groups, tokens // tm)

    def tile(g, m):
        return jnp.where(g % 2 == 1, last - m, m)

    def is_key(g):
        return (g == 1) | (g == 5)

    def proj_index(g, m):
        col = g - (g >= 1).astype(jnp.int32) - (g >= 5).astype(jnp.int32)
        return jnp.where(is_key(g), last, tile(g, m)), col

    def kt_index(g, m):
        row = jnp.where(g == 0, last, jnp.where(is_key(g), tile(g, m), 0))
        return row, (g >= 5).astype(jnp.int32), 0

    in_specs = [
        pl.BlockSpec((tm, D_MODEL), lambda g, m: (tile(g, m), 0)),
        pl.BlockSpec((GROUP_DIM, D_MODEL), lambda g, m: (g, 0)),
        pl.BlockSpec((3, GROUP_DIM), lambda g, m: (0, jnp.clip(g - 4, 0, 1))),
    ]
    args = [h2d, w_in_t, conv_qk]
    blocks = [_nbytes((tm, D_MODEL), BF16), _nbytes((GROUP_DIM, D_MODEL), F32),
              _nbytes((tm, GROUP_DIM), BF16) * 2, _nbytes((8, GROUP_DIM), F32)]
    if grid_mode:
        in_specs += [pl.BlockSpec((tm, HEAD_DIM),
                                  lambda g, m: (jnp.where(g <= 1, tile(g, m) % tiles_per_seq, 0),
                                                0))] * 2
        args += list(rope)
        blocks += [_nbytes((tm, HEAD_DIM), F32)] * 2
    in_specs += stage_specs
    args += list(stage)
    blocks += stage_bytes
    scratch_defs = [((GROUP_DIM, D_MODEL), BF16)]

    return pl.pallas_call(
        functools.partial(_mix_in_kernel, tm=tm, period=period, grid_mode=grid_mode,
                          n_staged=len(stage)),
        grid=(n_groups, tokens // tm),
        in_specs=in_specs,
        out_specs=[pl.BlockSpec((tm, GROUP_DIM), lambda g, m: proj_index(g, m)),
                   pl.BlockSpec((slabs, GROUP_DIM, LANES), lambda g, m: kt_index(g, m))]
        + stage_specs,
        out_shape=[jax.ShapeDtypeStruct((tokens, N_STORED), BF16),
                   jax.ShapeDtypeStruct((tokens // LANES, 2 * GROUP_DIM, LANES), BF16)]
        + stage_shapes,
        scratch_shapes=[pltpu.VMEM(s, d) for s, d in scratch_defs],
        compiler_params=pltpu.CompilerParams(
            dimension_semantics=("arbitrary", "arbitrary"),
            vmem_limit_bytes=_vmem_limit(blocks, sum(_nbytes(s, d) for s, d in scratch_defs))),
        name="mix_in_grid" if grid_mode else "mix_in_seq",
    )(*args)


_NN = (((2,), (1,)), ((0,), (0,)))
_NT = (((2,), (2,)), ((0,), (0,)))


def _bdot(x, y, dims):
    return lax.dot_general(x, y, dims, preferred_element_type=F32)


def _chunk_loop(nc, body, reverse=False):
    if nc <= 2:
        for i in range(nc):
            body(nc - 1 - i if reverse else i)
    else:
        def step(i, carry):
            body(nc - 1 - i if reverse else i)
            return carry
        lax.fori_loop(0, nc, step, 0)


def _rows(c, base=0):
    start = base + c * CHUNK
    if not isinstance(start, int):
        start = pl.multiple_of(start, CHUNK)
    return pl.ds(start, CHUNK)


def _is_chunk(c, value):
    return isinstance(c, int) and c == value


class _StepChunk:
    def __init__(self, index, local):
        self.index, self.local = index, local


def _chunk_index(c):
    return c.index if isinstance(c, _StepChunk) else c


class _Chunks:
    def __init__(self, nc, bb, hp, streamed):
        self.nc, self.bb, self.hp, self.streamed = nc, bb, hp, streamed

    def rows(self, ref, bi, c, cols):
        if self.streamed:
            return ref[bi, c.local * CHUNK:(c.local + 1) * CHUNK, cols]
        return ref[_rows(c, bi * self.nc * CHUNK), cols]

    def store_rows(self, ref, bi, c, cols, value):
        if self.streamed:
            ref[bi, c.local * CHUNK:(c.local + 1) * CHUNK, cols] = value
        else:
            ref[_rows(c, bi * self.nc * CHUNK), cols] = value

    def per_chunk(self, ref, bi, c):
        return ref[bi, c.local] if self.streamed else ref[bi * self.nc + c]

    def heads(self, ref, c):
        return jnp.stack([self.rows(ref, bi, c, slice(hh * HEAD_DIM, (hh + 1) * HEAD_DIM))
                          for bi in range(self.bb) for hh in range(self.hp)])

    def keys_t(self, kt_ref, c):
        parts = [self.per_chunk(kt_ref, bi, c).reshape(self.hp, HEAD_DIM, CHUNK)
                 for bi in range(self.bb)]
        return parts[0] if self.bb == 1 else jnp.concatenate(parts, axis=0)

    def gate_row(self, g_ref, bi, c, row):
        if self.streamed:
            return g_ref[bi, c.local, pl.ds(row, 1), :]
        return g_ref[bi * self.nc + c, pl.ds(row, 1), :]


def _run_scan(nc, streamed, init_state, fwd, bwd, emit_state):
    if not streamed:
        init_state(0)
        _chunk_loop(nc, fwd)
        if emit_state is not None:
            emit_state(0)
        init_state(1)
        _chunk_loop(nc, bwd, reverse=True)
        if emit_state is not None:
            emit_state(1)
        return
    assert emit_state is None
    s = pl.program_id(0)
    steps = _stream_steps(nc)

    @pl.when(s == 0)
    def _():
        init_state(0)

    @pl.when(s == steps)
    def _():
        init_state(1)

    @pl.when(s < steps)
    def _():
        for local in range(STREAM_CHUNKS):
            fwd(_StepChunk(s * STREAM_CHUNKS + local, local))

    @pl.when(s >= steps)
    def _():
        for local in reversed(range(STREAM_CHUNKS)):
            bwd(_StepChunk((2 * steps - 1 - s) * STREAM_CHUNKS + local, local))


def _per_seq(x, bb):
    return x if bb == 1 else jnp.concatenate([x] * bb, axis=0)


def _group_block(seq_len, hp, group, bb):
    per = N_HEADS // hp
    return pl.BlockSpec((bb * seq_len, hp * HEAD_DIM), lambda b, hg: (b, group * per + hg))


def _kt_block(nc, hp, group, bb):
    per = N_HEADS // hp
    return pl.BlockSpec((bb * nc, hp * HEAD_DIM, LANES), lambda b, hg: (b, group * per + hg, 0))


def _stream_steps(nc):
    assert nc % STREAM_CHUNKS == 0
    return nc // STREAM_CHUNKS


def _step_block(s, nc):
    steps = _stream_steps(nc)
    return jnp.where(s < steps, s, 2 * steps - 1 - s)


def _stream_rows_block(bb, nc, group):
    return pl.BlockSpec((bb, STREAM_CHUNKS * CHUNK, GROUP_DIM),
                        lambda s: (0, _step_block(s, nc), group))


def _stream_out_block(bb, nc):
    steps = _stream_steps(nc)
    return pl.BlockSpec((bb, STREAM_CHUNKS * CHUNK, GROUP_DIM),
                        lambda s: (0, jnp.minimum(2 * steps - 1 - s, steps - 1), 0))


def _stream_chunk_block(bb, nc, tail, tail_index):
    return pl.BlockSpec((bb, STREAM_CHUNKS) + tail, lambda s: (0, _step_block(s, nc)) + tail_index)


def _ret_kernel(*refs, nc, bb, hp, has_init, emit_states, streamed):
    refs = list(refs)
    theta_ref, q_ref, kt_ref, v_ref, rg_ref, gn_ref = refs[:6]
    pos = 6
    s0_ref = None
    if has_init:
        s0_ref = refs[pos]
        pos += 1
    o_ref = refs[pos]
    pos += 1
    st_ref = None
    if emit_states:
        st_ref = refs[pos]
        pos += 1
    hist, s_scr, decay_scr, qdec_scr, kdec_scr, cdec_scr = refs[pos:pos + 6]
    ck = _Chunks(nc, bb, hp, streamed)
    hg = 0 if streamed else pl.program_id(1)
    hsel = pl.ds(hg * hp, hp)

    @pl.when(pl.program_id(0) == 0)
    def _():
        ii = lax.broadcasted_iota(jnp.int32, (CHUNK, CHUNK), 0)
        jj = lax.broadcasted_iota(jnp.int32, (CHUNK, CHUNK), 1)
        d = (ii - jj).astype(F32)
        p_col = lax.broadcasted_iota(jnp.int32, (CHUNK, HEAD_DIM), 0).astype(F32)
        p_row = lax.broadcasted_iota(jnp.int32, (1, LANES), 1).astype(F32)
        for hh in range(hp):
            h = hg * hp + hh
            lg_f = _log_sigmoid(jnp.full((1, LANES), theta_ref[0, h], F32))
            lg_b = _log_sigmoid(jnp.full((1, LANES), theta_ref[1, h], F32))
            decay_scr[h] = (jnp.where(d >= 0, jnp.exp(lg_f * jnp.maximum(d, 0.0)), 0.0)
                            + jnp.where(d <= 0, jnp.exp(lg_b * jnp.maximum(-d, 0.0)), 0.0))
            qdec_scr[0, h] = jnp.exp(lg_f[:, :1] * (p_col + 1.0))
            qdec_scr[1, h] = jnp.exp(lg_b[:, :1] * (CHUNK - p_col))
            kdec_scr[0, h] = jnp.exp(lg_f * (CHUNK - 1.0 - p_row))
            kdec_scr[1, h] = jnp.exp(lg_b * p_row)
            cdec_scr[0, h] = jnp.exp(lg_f * float(CHUNK))
            cdec_scr[1, h] = jnp.exp(lg_b * float(CHUNK))

    def table(ref, *lead):
        return _per_seq(ref[(*lead, hsel)], bb)

    def kv_update(c, v, direction):
        kd = (ck.keys_t(kt_ref, c).astype(F32)
              * table(kdec_scr, direction)).astype(BF16)
        s_scr[...] = s_scr[...] * table(cdec_scr, direction)[:, :, :1] + _bdot(kd, v, _NN)

    def init_state(direction):
        if has_init:
            for bi in range(bb):
                s_scr[bi * hp:(bi + 1) * hp] = s0_ref[bi, 0, direction]
        else:
            s_scr[...] = jnp.zeros((bb * hp, HEAD_DIM, HEAD_DIM), F32)

    def emit_state(direction):
        for bi in range(bb):
            st_ref[bi, 0, direction] = s_scr[bi * hp:(bi + 1) * hp]

    def fwd(c):
        hist[_chunk_index(c)] = s_scr[...].astype(BF16)
        kv_update(c, ck.heads(v_ref, c), 0)

    def bwd(c):
        q, v = ck.heads(q_ref, c), ck.heads(v_ref, c)
        kt = ck.keys_t(kt_ref, c)
        att = (_bdot(q, kt, _NN) * table(decay_scr)).astype(BF16)
        o = _bdot(att, v, _NN)
        if has_init or not _is_chunk(c, 0):
            o = o + _bdot(q, hist[_chunk_index(c)], _NN) * table(qdec_scr, 0)
        if has_init or not _is_chunk(c, nc - 1):
            o = o + _bdot(q, s_scr[...].astype(BF16), _NN) * table(qdec_scr, 1)
        y = _ln_rows(o)
        for bi in range(bb):
            for hh in range(hp):
                cs = slice(hh * HEAD_DIM, (hh + 1) * HEAD_DIM)
                gate = _silu(ck.rows(rg_ref, bi, c, cs).astype(F32))
                ck.store_rows(o_ref, bi, c, cs,
                              (y[bi * hp + hh] * gn_ref[:, cs] * gate).astype(BF16))
        kv_update(c, v, 1)

    _run_scan(nc, streamed, init_state, fwd, bwd, emit_state if emit_states else None)


def _ret_call(proj, kt, theta, gn, s0, *, batch, seq_len, bb, hp, emit_states, streamed=False):
    nc = seq_len // CHUNK
    has_init = s0 is not None
    assert N_HEADS % hp == 0 and batch % bb == 0
    if streamed:
        assert bb == batch and hp == N_HEADS and not emit_states
        proj = proj.reshape(batch, seq_len, N_STORED)
        kt = kt.reshape(batch, nc, 2 * GROUP_DIM, LANES)
        state_spec = pl.BlockSpec((bb, 1, 2, hp, HEAD_DIM, HEAD_DIM), lambda s: (0,) * 6)
        in_specs = [pl.BlockSpec(memory_space=pltpu.SMEM),
                    _stream_rows_block(bb, nc, 0),
                    _stream_chunk_block(bb, nc, (GROUP_DIM, LANES), (0, 0)),
                    _stream_rows_block(bb, nc, 1), _stream_rows_block(bb, nc, 2),
                    pl.BlockSpec((1, GROUP_DIM), lambda s: (0, 0))]
        blocks = [_nbytes((bb, STREAM_CHUNKS * CHUNK, GROUP_DIM), BF16)] * 5
        out_specs = [_stream_out_block(bb, nc)]
        out_shape = [jax.ShapeDtypeStruct((batch, seq_len, GROUP_DIM), BF16)]
        grid, semantics = (2 * _stream_steps(nc),), ("arbitrary",)
    else:
        state_spec = pl.BlockSpec((bb, 1, 2, hp, HEAD_DIM, HEAD_DIM),
                                  lambda b, hg: (b, 0, 0, hg, 0, 0))
        in_specs = [pl.BlockSpec(memory_space=pltpu.SMEM),
                    _group_block(seq_len, hp, 0, bb), _kt_block(nc, hp, 0, bb),
                    _group_block(seq_len, hp, 1, bb), _group_block(seq_len, hp, 2, bb),
                    pl.BlockSpec((1, hp * HEAD_DIM), lambda b, hg: (0, hg))]
        blocks = [_nbytes((bb * seq_len, hp * HEAD_DIM), BF16)] * 5
        out_specs = [pl.BlockSpec((bb * seq_len, hp * HEAD_DIM), lambda b, hg: (b, hg))]
        out_shape = [jax.ShapeDtypeStruct((batch * seq_len, GROUP_DIM), BF16)]
        grid, semantics = (batch // bb, N_HEADS // hp), ("arbitrary", "arbitrary")
    args = [theta, proj, kt, proj, proj, gn]
    if has_init:
        in_specs.append(state_spec)
        args.append(s0)
        blocks.append(_nbytes((bb, 2, hp, HEAD_DIM, HEAD_DIM), F32))
    if emit_states:
        out_specs.append(state_spec)
        out_shape.append(jax.ShapeDtypeStruct((batch, 1, 2, N_HEADS, HEAD_DIM, HEAD_DIM), F32))
        blocks.append(_nbytes((bb, 2, hp, HEAD_DIM, HEAD_DIM), F32))
    scratch_defs = [((nc, bb * hp, HEAD_DIM, HEAD_DIM), BF16), ((bb * hp, HEAD_DIM, HEAD_DIM), F32),
                    ((N_HEADS, CHUNK, CHUNK), F32), ((2, N_HEADS, CHUNK, HEAD_DIM), F32),
                    ((2, N_HEADS, 1, LANES), F32), ((2, N_HEADS, 1, LANES), F32)]
    outs = pl.pallas_call(
        functools.partial(_ret_kernel, nc=nc, bb=bb, hp=hp, has_init=has_init,
                          emit_states=emit_states, streamed=streamed),
        grid=grid,
        in_specs=in_specs, out_specs=out_specs, out_shape=out_shape,
        scratch_shapes=[pltpu.VMEM(s, d) for s, d in scratch_defs],
        compiler_params=pltpu.CompilerParams(
            dimension_semantics=semantics,
            vmem_limit_bytes=_vmem_limit(blocks, sum(_nbytes(s, d) for s, d in scratch_defs))),
        name="ret_grid" if has_init else "ret_seq",
    )(*args)
    if streamed:
        outs = [outs[0].reshape(batch * seq_len, GROUP_DIM)] + list(outs[1:])
    return outs


def _mlstm_kernel(*refs, nc, bb, hp, has_init, emit_states, streamed, n_staged):
    refs = list(refs)
    q_ref, kt_ref, v_ref, mo_ref, g_ref, gn_ref = refs[:6]
    pos = 6
    c0_ref = n0_ref = m0_ref = None
    if has_init:
        c0_ref, n0_ref, m0_ref = refs[pos:pos + 3]
        pos += 3
    staged_in = refs[pos:pos + n_staged]
    pos += n_staged
    o_ref = refs[pos]
    pos += 1
    c_out = n_out = m_out = None
    if emit_states:
        c_out, n_out, m_out = refs[pos:pos + 3]
        pos += 3
    staged_out = refs[pos:pos + n_staged]
    pos += n_staged
    c_hist, n_hist, m_hist, c_scr, n_scr, m_scr = refs[pos:pos + 6]

    for src, dst in zip(staged_in, staged_out):
        dst[...] = src[...].astype(BF16)

    ck = _Chunks(nc, bb, hp, streamed)
    b = 0 if streamed else pl.program_id(0)
    hg = 0 if streamed else pl.program_id(1)
    ii = lax.broadcasted_iota(jnp.int32, (CHUNK, CHUNK), 0)
    jj = lax.broadcasted_iota(jnp.int32, (CHUNK, CHUNK), 1)
    eye = ii == jj
    lower = ii >= jj
    upper = ii <= jj

    ne = bb * hp

    def col(row):
        return jnp.sum(jnp.where(eye, row, 0.0), axis=2, keepdims=True)

    def gate_rows(c, kind):
        base = kind * N_HEADS + hg * hp
        return jnp.stack([ck.gate_row(g_ref, bi, c, base + hh)
                          for bi in range(bb) for hh in range(hp)])

    def direction_out(a, q, qf, v, c_b, n_row, m, b_row, i_row, mask):
        b_col = col(b_row)
        dm = jnp.where(mask, b_col - b_row + i_row, -jnp.inf)
        inter = b_col + m
        mt = jnp.maximum(inter, jnp.max(dm, axis=2, keepdims=True))
        w = jnp.exp(dm - mt)
        sp = jnp.exp(inter - mt)
        s = a * w
        num = _bdot(s.astype(BF16), v, _NN)
        den = jnp.sum(s, axis=2, keepdims=True)
        if c_b is not None:
            num = num + _bdot(q, c_b, _NN) * sp
            den = den + jnp.sum(qf * n_row, axis=2, keepdims=True) * sp
        return num / jnp.maximum(jnp.abs(den), jnp.exp(-mt))

    def state_update(c, v, b_row, i_row, last):
        m = m_scr[...][:, :, :1]
        b_last = b_row[:, :, last:last + 1]
        g = b_last - b_row + i_row
        m_new = jnp.maximum(b_last + m, jnp.max(g, axis=2, keepdims=True))
        wk = jnp.exp(g - m_new)
        sc = jnp.exp(b_last + m - m_new)
        kt = ck.keys_t(kt_ref, c)
        kw = (kt.astype(F32) * wk).astype(BF16)
        c_scr[...] = c_scr[...] * sc + _bdot(kw, v, _NN)
        wk8 = jnp.broadcast_to(wk, (ne, 8, CHUNK)).astype(BF16)
        n_scr[...] = n_scr[...] * sc + _bdot(wk8, kt, _NT)[:, :1, :]
        m_scr[...] = jnp.broadcast_to(m_new, (ne, 1, LANES))

    def init_state(direction):
        if has_init:
            for bi in range(bb):
                c_scr[bi * hp:(bi + 1) * hp] = c0_ref[bi, 0, direction]
            n_scr[...] = jnp.stack([n0_ref[bi, 0, direction, pl.ds(hg * hp + hh, 1), :]
                                    for bi in range(bb) for hh in range(hp)])
            m_scr[...] = jnp.stack([
                jnp.full((1, LANES),
                         m0_ref[((b * bb + bi) * 2 + direction) * N_HEADS + hg * hp + hh], F32)
                for bi in range(bb) for hh in range(hp)])
        else:
            c_scr[...] = jnp.zeros((ne, HEAD_DIM, HEAD_DIM), F32)
            n_scr[...] = jnp.zeros((ne, 1, HEAD_DIM), F32)
            m_scr[...] = jnp.zeros((ne, 1, LANES), F32)

    def emit_state(direction):
        for bi in range(bb):
            c_out[bi, 0, direction] = c_scr[bi * hp:(bi + 1) * hp]
            for hh in range(hp):
                n_out[bi, 0, direction, pl.ds(hh, 1), :] = n_scr[bi * hp + hh]
                m_out[bi, pl.ds(direction * N_HEADS + hh, 1), :] = m_scr[bi * hp + hh]

    def fwd(c):
        ci = _chunk_index(c)
        c_hist[ci] = c_scr[...].astype(BF16)
        n_hist[ci] = n_scr[...]
        m_hist[ci] = m_scr[...]
        state_update(c, ck.heads(v_ref, c), gate_rows(c, 1), gate_rows(c, 0), CHUNK - 1)

    def bwd(c):
        i_f, b_f, i_b, b_b = (gate_rows(c, kind) for kind in (0, 1, 2, 3))
        q, v = ck.heads(q_ref, c), ck.heads(v_ref, c)
        qf = q.astype(F32)
        a = _bdot(q, ck.keys_t(kt_ref, c), _NN)
        zero_f = not has_init and _is_chunk(c, 0)
        zero_b = not has_init and _is_chunk(c, nc - 1)
        ci = _chunk_index(c)
        h_f = direction_out(a, q, qf, v, None if zero_f else c_hist[ci], n_hist[ci],
                            m_hist[ci][:, :, :1], b_f, i_f, lower)
        h_b = direction_out(a, q, qf, v, None if zero_b else c_scr[...].astype(BF16),
                            n_scr[...], m_scr[...][:, :, :1], b_b, i_b, upper)
        y = _ln_rows(h_f + h_b)
        for bi in range(bb):
            for hh in range(hp):
                cs = slice(hh * HEAD_DIM, (hh + 1) * HEAD_DIM)
                gate = jax.nn.sigmoid(ck.rows(mo_ref, bi, c, cs).astype(F32))
                ck.store_rows(o_ref, bi, c, cs,
                              (gate * (y[bi * hp + hh] * gn_ref[:, cs])).astype(BF16))
        state_update(c, v, b_b, i_b, 0)

    _run_scan(nc, streamed, init_state, fwd, bwd, emit_state if emit_states else None)


def _mlstm_call(proj, kt, gates, gn, init, *, batch, seq_len, bb, hp, emit_states,
                streamed=False, stage=()):
    nc = seq_len // CHUNK
    has_init = init is not None
    assert N_HEADS % hp == 0 and (hp == N_HEADS or not emit_states) and batch % bb == 0
    if streamed:
        assert bb == batch and hp == N_HEADS and not emit_states and not stage
        proj = proj.reshape(batch, seq_len, N_STORED)
        kt = kt.reshape(batch, nc, 2 * GROUP_DIM, LANES)
        gates = gates.reshape(batch, nc, N_GATES, LANES)
        c_spec = pl.BlockSpec((bb, 1, 2, hp, HEAD_DIM, HEAD_DIM), lambda s: (0,) * 6)
        n_spec = pl.BlockSpec((bb, 1, 2, N_HEADS, HEAD_DIM), lambda s: (0,) * 5)
        in_specs = [_stream_rows_block(bb, nc, 3),
                    _stream_chunk_block(bb, nc, (GROUP_DIM, LANES), (1, 0)),
                    _stream_rows_block(bb, nc, 4), _stream_rows_block(bb, nc, 5),
                    _stream_chunk_block(bb, nc, (N_GATES, LANES), (0, 0)),
                    pl.BlockSpec((1, GROUP_DIM), lambda s: (0, 0))]
        blocks = ([_nbytes((bb, STREAM_CHUNKS * CHUNK, GROUP_DIM), BF16)] * 5
                  + [_nbytes((bb, STREAM_CHUNKS * N_GATES, LANES), F32)])
        out_specs = [_stream_out_block(bb, nc)]
        out_shape = [jax.ShapeDtypeStruct((batch, seq_len, GROUP_DIM), BF16)]
        grid, semantics = (2 * _stream_steps(nc),), ("arbitrary",)
    else:
        c_spec = pl.BlockSpec((bb, 1, 2, hp, HEAD_DIM, HEAD_DIM),
                              lambda b, hg: (b, 0, 0, hg, 0, 0))
        n_spec = pl.BlockSpec((bb, 1, 2, N_HEADS, HEAD_DIM), lambda b, hg: (b, 0, 0, 0, 0))
        in_specs = [_group_block(seq_len, hp, 3, bb), _kt_block(nc, hp, 1, bb),
                    _group_block(seq_len, hp, 4, bb), _group_block(seq_len, hp, 5, bb),
                    pl.BlockSpec((bb * nc, N_GATES, LANES), lambda b, hg: (b, 0, 0)),
                    pl.BlockSpec((1, hp * HEAD_DIM), lambda b, hg: (0, hg))]
        blocks = ([_nbytes((bb * seq_len, hp * HEAD_DIM), BF16)] * 5
                  + [_nbytes((bb * nc, N_GATES, LANES), F32)])
        out_specs = [pl.BlockSpec((bb * seq_len, hp * HEAD_DIM), lambda b, hg: (b, hg))]
        out_shape = [jax.ShapeDtypeStruct((batch * seq_len, GROUP_DIM), BF16)]
        grid, semantics = (batch // bb, N_HEADS // hp), ("arbitrary", "arbitrary")
    args = [proj, kt, proj, proj, gates, gn]
    if has_init:
        in_specs += [c_spec, n_spec, pl.BlockSpec(memory_space=pltpu.SMEM)]
        args += list(init)
        blocks.append(_nbytes((bb, 2, hp, HEAD_DIM, HEAD_DIM), F32))
    if emit_states:
        out_specs += [c_spec, n_spec, pl.BlockSpec((bb, 2 * N_HEADS, LANES), lambda b, hg: (b, 0, 0))]
        out_shape += [jax.ShapeDtypeStruct((batch, 1, 2, N_HEADS, HEAD_DIM, HEAD_DIM), F32),
                      jax.ShapeDtypeStruct((batch, 1, 2, N_HEADS, HEAD_DIM), F32),
                      jax.ShapeDtypeStruct((batch, 2 * N_HEADS, LANES), F32)]
        blocks.append(_nbytes((bb, 2, hp, HEAD_DIM, HEAD_DIM), F32))
    if stage:
        stage_specs, stage_shapes, stage_bytes = _staging_specs(stage, *grid)
        in_specs += stage_specs
        args += list(stage)
        out_specs += stage_specs
        out_shape += stage_shapes
        blocks += stage_bytes
    ne = bb * hp
    scratch_defs = [((nc, ne, HEAD_DIM, HEAD_DIM), BF16), ((nc, ne, 1, HEAD_DIM), F32),
                    ((nc, ne, 1, LANES), F32), ((ne, HEAD_DIM, HEAD_DIM), F32),
                    ((ne, 1, HEAD_DIM), F32), ((ne, 1, LANES), F32)]
    scratch_bytes = sum(_nbytes(s, d) for s, d in scratch_defs) + 16 * nc * ne * HEAD_DIM * 4
    outs = pl.pallas_call(
        functools.partial(_mlstm_kernel, nc=nc, bb=bb, hp=hp, has_init=has_init,
                          emit_states=emit_states, streamed=streamed, n_staged=len(stage)),
        grid=grid,
        in_specs=in_specs, out_specs=out_specs, out_shape=out_shape,
        scratch_shapes=[pltpu.VMEM(s, d) for s, d in scratch_defs],
        compiler_params=pltpu.CompilerParams(
            dimension_semantics=semantics,
            vmem_limit_bytes=_vmem_limit(blocks, scratch_bytes)),
        name="mlstm_grid" if has_init else "mlstm_seq",
    )(*args)
    if streamed:
        outs = [outs[0].reshape(batch * seq_len, GROUP_DIM)] + list(outs[1:])
    return outs


def _mix_out_kernel(mr_ref, mm_ref, w_ref, x_ref, g1_ref, lg_ref, lb_ref, o_ref):
    for rc in range(o_ref.shape[0] // MIX_ROW_CHUNK):
        rows = slice(rc * MIX_ROW_CHUNK, (rc + 1) * MIX_ROW_CHUNK)
        mix = (jnp.dot(mr_ref[rows, :], w_ref[0:GROUP_DIM, :], preferred_element_type=F32)
               + jnp.dot(mm_ref[rows, :], w_ref[GROUP_DIM:, :], preferred_element_type=F32))
        o_ref[rows, :] = (_ln_rows(ALPHA * x_ref[rows, :] + g1_ref[0] * mix) * lg_ref[...]
                          + lb_ref[...])


def _mix_out_call(mix_r, mix_m, w_out_b, x2d, mod3, mod_row, ln_g, ln_b, *, tm):
    tokens = x2d.shape[0]
    blocks = [_nbytes((tm, GROUP_DIM), BF16)] * 2 + [_nbytes((D_MODEL, D_MODEL), BF16)] \
        + [_nbytes((tm, D_MODEL), F32)] * 2
    return pl.pallas_call(
        _mix_out_kernel,
        grid=(tokens // tm,),
        in_specs=[pl.BlockSpec((tm, GROUP_DIM), lambda m: (m, 0)),
                  pl.BlockSpec((tm, GROUP_DIM), lambda m: (m, 0)),
                  pl.BlockSpec((D_MODEL, D_MODEL), lambda m: (0, 0)),
                  pl.BlockSpec((tm, D_MODEL), lambda m: (m, 0)),
                  pl.BlockSpec((1, 1, D_MODEL), lambda m: (mod_row(m) * 6 + 2, 0, 0)),
                  pl.BlockSpec((1, D_MODEL), lambda m: (0, 0)),
                  pl.BlockSpec((1, D_MODEL), lambda m: (0, 0))],
        out_specs=pl.BlockSpec((tm, D_MODEL), lambda m: (m, 0)),
        out_shape=jax.ShapeDtypeStruct((tokens, D_MODEL), F32),
        compiler_params=pltpu.CompilerParams(
            dimension_semantics=("parallel",), vmem_limit_bytes=_vmem_limit(blocks)),
        name="mix_out",
    )(mix_r, mix_m, w_out_b, x2d, mod3, ln_g, ln_b)


def _ffn_kernel(x_ref, sh_ref, sc_ref, g2_ref, wu_ref, wg_ref, cv_ref, wd_ref, lg_ref, lb_ref,
                o_ref, h_scr, *, period):
    j = pl.program_id(1)

    def row_chunks():
        for rc in range(o_ref.shape[0] // MIX_ROW_CHUNK):
            yield slice(rc * MIX_ROW_CHUNK, (rc + 1) * MIX_ROW_CHUNK)

    @pl.when(j == 0)
    def _():
        for rows in row_chunks():
            h = _ln_rows(x_ref[rows, :]) * (1.0 + sc_ref[0]) + sh_ref[0]
            h_scr[rows, :] = h.astype(BF16)
        o_ref[...] = jnp.zeros_like(o_ref)

    lane = lax.broadcasted_iota(jnp.int32, (1, FF_TN), 1)
    duplicate = (j == FF_NJ - 1) & (lane < FF_OVERLAP)
    for rows in row_chunks():
        hb = h_scr[rows, :]
        u = jnp.dot(hb, wu_ref[...], preferred_element_type=F32)
        g = jnp.dot(hb, wg_ref[...], preferred_element_type=F32)
        hid = _silu(_conv3_rows(u, cv_ref[0], period)) * g
        hid = jnp.where(duplicate, 0.0, hid).astype(BF16)
        o_ref[rows, :] += jnp.dot(hid, wd_ref[...], preferred_element_type=F32)

    @pl.when(j == FF_NJ - 1)
    def _():
        for rows in row_chunks():
            y = ALPHA * x_ref[rows, :] + g2_ref[0] * o_ref[rows, :]
            o_ref[rows, :] = _ln_rows(y) * lg_ref[...] + lb_ref[...]


def _ff_offset(j):
    return pl.multiple_of(jnp.minimum(j * FF_TN, D_FF - FF_TN), LANES)


def _ffn_call(x1, mod3, mod_row, w_up_b, conv_tiles, w_down_b, ln_g, ln_b, *, period, tm):
    tokens = x1.shape[0]
    up_tile = (pl.Element(D_MODEL), pl.Element(FF_TN))
    blocks = [_nbytes((tm, D_MODEL), F32)] * 2 + [_nbytes((D_MODEL, FF_TN), BF16)] * 3 \
        + [_nbytes((8, FF_TN), F32)]
    scratch_bytes = _nbytes((tm, D_MODEL), BF16)
    return pl.pallas_call(
        functools.partial(_ffn_kernel, period=period),
        grid=(tokens // tm, FF_NJ),
        in_specs=[pl.BlockSpec((tm, D_MODEL), lambda m, j: (m, 0)),
                  pl.BlockSpec((1, 1, D_MODEL), lambda m, j: (mod_row(m) * 6 + 3, 0, 0)),
                  pl.BlockSpec((1, 1, D_MODEL), lambda m, j: (mod_row(m) * 6 + 4, 0, 0)),
                  pl.BlockSpec((1, 1, D_MODEL), lambda m, j: (mod_row(m) * 6 + 5, 0, 0)),
                  pl.BlockSpec(up_tile, lambda m, j: (0, _ff_offset(j))),
                  pl.BlockSpec(up_tile, lambda m, j: (0, pl.multiple_of(D_FF + _ff_offset(j), LANES))),
                  pl.BlockSpec((1, 3, FF_TN), lambda m, j: (j, 0, 0)),
                  pl.BlockSpec((pl.Element(FF_TN), pl.Element(D_MODEL)),
                               lambda m, j: (_ff_offset(j), 0)),
                  pl.BlockSpec((1, D_MODEL), lambda m, j: (0, 0)),
                  pl.BlockSpec((1, D_MODEL), lambda m, j: (0, 0))],
        out_specs=pl.BlockSpec((tm, D_MODEL), lambda m, j: (m, 0)),
        out_shape=jax.ShapeDtypeStruct((tokens, D_MODEL), F32),
        scratch_shapes=[pltpu.VMEM((tm, D_MODEL), BF16)],
        compiler_params=pltpu.CompilerParams(
            dimension_semantics=("parallel", "arbitrary"),
            vmem_limit_bytes=_vmem_limit(blocks, scratch_bytes)),
        name="ffn",
    )(x1, mod3, mod3, mod3, w_up_b, w_up_b, conv_tiles, w_down_b, ln_g, ln_b)


def _rope_tables(seq_len):
    quarter = HEAD_DIM // 4
    t = jnp.arange(seq_len)
    row = (t // GRID_W).astype(F32)
    col = (t % GRID_W).astype(F32)
    inv = ROPE_BASE ** (-jnp.arange(quarter, dtype=F32) / quarter)
    ang_r, ang_c = row[:, None] * inv, col[:, None] * inv
    cos_t = jnp.concatenate([jnp.cos(ang_r)] * 2 + [jnp.cos(ang_c)] * 2, axis=1)
    sin_t = jnp.concatenate([-jnp.sin(ang_r), jnp.sin(ang_r), -jnp.sin(ang_c), jnp.sin(ang_c)], axis=1)
    return cos_t, sin_t


def kernel(x_prompt, x_sample, state_ret, state_mlstm_C, state_mlstm_n, state_mlstm_m, c, c_ctx,
           w_mod, b_mod, w_in, b_gate, conv_qk, ret_theta, gn_ret, gn_mlstm, w_out,
           ln1_g, ln1_b, w_up, conv_ff, w_down, ln2_g, ln2_b):
    bp, lp, _ = x_prompt.shape
    bs, ls, _ = x_sample.shape
    layer = 0
    tm_mix, tm, tm_ffn = 1024, 512, 1024

    cvec = jnp.concatenate([c_ctx[None], c, jnp.zeros((8 - 1 - bs, D_MODEL), F32)], axis=0)
    mod = _mod_call(cvec, w_mod[layer], b_mod[layer][None])
    mod3 = mod.reshape(8 * 6, 1, D_MODEL)

    def row_prompt(tile):
        return lambda m: 0

    def row_sample(tile):
        return lambda m: 1 + (m * tile) // ls

    w_l = w_in[layer].T
    w_gate = w_l[N_PROJ:]
    b_gate_col = b_gate[layer].reshape(N_GATES, 1)
    ff_offsets = [min(j * FF_TN, D_FF - FF_TN) for j in range(FF_NJ)]
    conv_tiles = jnp.stack([conv_ff[layer][:, o:o + FF_TN] for o in ff_offsets])
    gn_r = gn_ret[layer].reshape(1, GROUP_DIM)
    gn_m = gn_mlstm[layer].reshape(1, GROUP_DIM)
    theta = ret_theta[layer]
    ln1 = (ln1_g[layer][None], ln1_b[layer][None])
    ln2 = (ln2_g[layer][None], ln2_b[layer][None])

    def mixer(x, mod_row, rope, period, bb, hp, ret_init, mlstm_init, emit_states, stage=(),
              stage_scan=(), streamed=False):
        batch, seq_len, _ = x.shape
        x2d = x.reshape(batch * seq_len, D_MODEL)
        h2d, gates = _ln_mod_call(x2d, mod3, mod_row(tm_mix), w_gate, b_gate_col, tm=tm_mix)
        proj, kt, *staged = _mix_in_call(h2d, w_l, conv_qk[layer], rope, seq_len=seq_len,
                                         period=period, tm=tm_mix, stage=stage)
        ret = _ret_call(proj, kt, theta, gn_r, ret_init, batch=batch, seq_len=seq_len, bb=bb,
                        hp=hp, emit_states=emit_states, streamed=streamed)
        mls = _mlstm_call(proj, kt, gates, gn_m, mlstm_init, batch=batch, seq_len=seq_len, bb=bb,
                          hp=hp, emit_states=emit_states, streamed=streamed, stage=stage_scan)
        n_own = len(mls) - len(stage_scan)
        return x2d, ret, mls[:n_own], list(mls[n_own:]) + staged

    def channel_mix(x2d, shape, ret, mls, mod_row, period, weights):
        w_up_b, w_down_b, w_out_b = weights
        x1 = _mix_out_call(ret[0], mls[0], w_out_b, x2d, mod3, mod_row(tm), *ln1, tm=tm)
        y = _ffn_call(x1, mod3, mod_row(tm_ffn), w_up_b, conv_tiles, w_down_b, *ln2,
                      period=period, tm=tm_ffn)
        return y.reshape(shape)

    xp, ret_p, mls_p, weights_b = mixer(x_prompt, row_prompt, None, lp, 2, N_HEADS, None, None,
                                        True, stage=(w_down[layer], w_out[layer]),
                                        stage_scan=(w_up[layer],))
    m0 = state_mlstm_m.reshape(-1)
    xs, ret_s, mls_s, _ = mixer(x_sample, row_sample, _rope_tables(ls), GRID_W, bs, N_HEADS,
                                state_ret, (state_mlstm_C, state_mlstm_n, m0), False,
                                streamed=True)
    y_prompt = channel_mix(xp, x_prompt.shape, ret_p, mls_p, row_prompt, lp, weights_b)
    y_sample = channel_mix(xs, x_sample.shape, ret_s, mls_s, row_sample, GRID_W, weights_b)
    ret_states, mls_states = ret_p[1:], mls_p[1:]

    new_ret = ret_states[0]
    new_c, new_n, new_m = mls_states
    new_m = new_m[:, :, 0].reshape(bp, 1, 2, N_HEADS)
    return (y_prompt, y_sample, new_ret, new_c, new_n, new_m)
```

```python
import functools

import jax
import jax.numpy as jnp
from jax import lax
from jax.experimental import pallas as pl
from jax.experimental.pallas import tpu as pltpu

F32 = jnp.float32
BF16 = jnp.bfloat16

D_MODEL = 2048
GRID_W = 64
HEAD_DIM = 256
N_HEADS = 4
GROUP_DIM = N_HEADS * HEAD_DIM
N_PROJ = 8 * GROUP_DIM
N_GATES = 4 * N_HEADS
D_FF = 5504
CHUNK = 128
ROPE_BASE = 10000.0
ALPHA = 2.0 ** 0.25
LN_EPS = 1e-6
K_SCALE = HEAD_DIM ** -0.5

LANES = 128
BF16_ROWS = 16
V7X_VMEM_LIMIT_CAP = 60 * 1024 * 1024
VMEM_TEMP_ALLOWANCE = 16 * 1024 * 1024

MIX_ROW_CHUNK = 512
N_STORED = 6 * GROUP_DIM
STREAM_CHUNKS = 4
FF_TN = 512
FF_NJ = -(-D_FF // FF_TN)
FF_OVERLAP = FF_NJ * FF_TN - D_FF


def _vmem_limit(block_bytes, scratch_bytes=0):
    est = 2 * sum(block_bytes) + scratch_bytes + VMEM_TEMP_ALLOWANCE
    return int(min(V7X_VMEM_LIMIT_CAP, est))


def _nbytes(shape, dtype):
    n = 1
    for s in shape:
        n *= s
    return n * jnp.dtype(dtype).itemsize


def _ln_rows(x):
    mu = jnp.mean(x, axis=-1, keepdims=True)
    xc = x - mu
    var = jnp.mean(xc * xc, axis=-1, keepdims=True)
    return xc * lax.rsqrt(var + LN_EPS)


def _silu(x):
    return x * jax.nn.sigmoid(x)


def _log_sigmoid(x):
    return jnp.minimum(x, 0.0) - jnp.log1p(jnp.exp(-jnp.abs(x)))


def _conv3_rows(u, w, period):
    rows = u.shape[0]
    t = lax.broadcasted_iota(jnp.int32, (rows, 1), 0) & (period - 1)
    prev = jnp.where(t == 0, 0.0, pltpu.roll(u, 1, 0))
    nxt = jnp.where(t == period - 1, 0.0, pltpu.roll(u, rows - 1, 0))
    return prev * w[0:1] + u * w[1:2] + nxt * w[2:3]


def _cumsum_lanes(x, lane, reverse):
    s = 1
    while s < LANES:
        if reverse:
            x = x + jnp.where(lane < LANES - s, pltpu.roll(x, LANES - s, 1), 0.0)
        else:
            x = x + jnp.where(lane >= s, pltpu.roll(x, s, 1), 0.0)
        s *= 2
    return x


def _mod_kernel(c_ref, w_ref, b_ref, o_ref):
    a = _silu(c_ref[...]).astype(BF16)
    o_ref[...] = jnp.dot(a, w_ref[...].astype(BF16), preferred_element_type=F32) + b_ref[...]


def _mod_call(cvec, w_mod, b_mod):
    rows, n = cvec.shape[0], w_mod.shape[1]
    tn = 1024
    blocks = [_nbytes((rows, D_MODEL), F32), _nbytes((D_MODEL, tn), F32),
              _nbytes((8, tn), F32), _nbytes((rows, tn), F32)]
    return pl.pallas_call(
        _mod_kernel,
        grid=(n // tn,),
        in_specs=[pl.BlockSpec((rows, D_MODEL), lambda j: (0, 0)),
                  pl.BlockSpec((D_MODEL, tn), lambda j: (0, j)),
                  pl.BlockSpec((1, tn), lambda j: (0, j))],
        out_specs=pl.BlockSpec((rows, tn), lambda j: (0, j)),
        out_shape=jax.ShapeDtypeStruct((rows, n), F32),
        compiler_params=pltpu.CompilerParams(
            dimension_semantics=("arbitrary",), vmem_limit_bytes=_vmem_limit(blocks)),
        name="mod",
    )(cvec, w_mod, b_mod)


def _ln_mod_kernel(x_ref, sh_ref, sc_ref, wg_ref, bg_ref, h_ref, gates_ref, *, tm):
    h = _ln_rows(x_ref[...]) * (1.0 + sc_ref[0]) + sh_ref[0]
    hb = h.astype(BF16)
    h_ref[...] = hb
    g = lax.dot_general(wg_ref[...].astype(BF16), hb, (((1,), (1,)), ((), ())),
                        preferred_element_type=F32) + bg_ref[...]
    row = lax.broadcasted_iota(jnp.int32, (N_GATES, LANES), 0)
    lane = lax.broadcasted_iota(jnp.int32, (N_GATES, LANES), 1)
    kind = lax.shift_right_logical(row, 2)
    for s in range(tm // LANES):
        gs = g[:, s * LANES:(s + 1) * LANES]
        ls = _log_sigmoid(gs)
        gates_ref[s] = jnp.where(kind == 1, _cumsum_lanes(ls, lane, False),
                                 jnp.where(kind == 3, _cumsum_lanes(ls, lane, True), gs))


def _ln_mod_call(x2d, mod3, mod_row, w_gate, b_gate_col, *, tm):
    tokens = x2d.shape[0]
    slabs = tm // LANES
    blocks = [_nbytes((tm, D_MODEL), F32), _nbytes((tm, D_MODEL), BF16),
              _nbytes((N_GATES, D_MODEL), F32), _nbytes((slabs, N_GATES, LANES), F32)]
    return pl.pallas_call(
        functools.partial(_ln_mod_kernel, tm=tm),
        grid=(tokens // tm,),
        in_specs=[pl.BlockSpec((tm, D_MODEL), lambda m: (m, 0)),
                  pl.BlockSpec((1, 1, D_MODEL), lambda m: (mod_row(m) * 6 + 0, 0, 0)),
                  pl.BlockSpec((1, 1, D_MODEL), lambda m: (mod_row(m) * 6 + 1, 0, 0)),
                  pl.BlockSpec((N_GATES, D_MODEL), lambda m: (0, 0)),
                  pl.BlockSpec((N_GATES, 1), lambda m: (0, 0))],
        out_specs=[pl.BlockSpec((tm, D_MODEL), lambda m: (m, 0)),
                   pl.BlockSpec((slabs, N_GATES, LANES), lambda m: (m, 0, 0))],
        out_shape=[jax.ShapeDtypeStruct((tokens, D_MODEL), BF16),
                   jax.ShapeDtypeStruct((tokens // LANES, N_GATES, LANES), F32)],
        compiler_params=pltpu.CompilerParams(
            dimension_semantics=("parallel",), vmem_limit_bytes=_vmem_limit(blocks)),
        name="ln_mod",
    )(x2d, mod3, mod3, w_gate, b_gate_col)


def _mix_in_kernel(*refs, tm, period, grid_mode, n_staged):
    refs = list(refs)
    h_ref, w_ref, cv_ref = refs[:3]
    pos = 3
    if grid_mode:
        cos_ref, sin_ref = refs[pos:pos + 2]
        pos += 2
    staged_in = refs[pos:pos + n_staged]
    pos += n_staged
    proj_ref, kt_ref = refs[pos:pos + 2]
    pos += 2
    staged_out = refs[pos:pos + n_staged]
    wb_scr = refs[pos + n_staged]
    group = pl.program_id(0)

    @pl.when(pl.program_id(1) == 0)
    def _():
        wb_scr[...] = w_ref[...].astype(BF16)

    def chunks(sizes=(tm // 2, tm // 2)):
        for src, dst in zip(staged_in, staged_out):
            dst[...] = src[...].astype(BF16)
        start = 0
        for n_rows in sizes:
            rows = slice(start, start + n_rows)
            yield start, rows, lax.dot_general(h_ref[rows, :], wb_scr[...],
                                               (((1,), (1,)), ((), ())), preferred_element_type=F32)
            start += n_rows
        assert start == tm

    def rope(a, rows):
        out = []
        for s in range(GROUP_DIM // LANES):
            xs = a[:, s * LANES:(s + 1) * LANES]
            t = (s % 2) * LANES
            out.append(xs * cos_ref[rows, t:t + LANES]
                       + pltpu.roll(xs, LANES // 2, 1) * sin_ref[rows, t:t + LANES])
        return jnp.concatenate(out, axis=1)

    def store_transposed(start, y):
        yt = y.T.astype(BF16)
        for s in range(y.shape[0] // LANES):
            kt_ref[start // LANES + s] = yt[:, s * LANES:(s + 1) * LANES]

    plain = (group == 2) | (group == 3) | (group == 6) | (group == 7)
    if not grid_mode:
        plain = plain | (group == 0)

    @pl.when(plain)
    def _():
        for _, rows, acc in chunks((tm,)):
            proj_ref[rows, :] = acc.astype(BF16)

    if grid_mode:
        @pl.when(group == 0)
        def _():
            for _, rows, acc in chunks():
                proj_ref[rows, :] = rope(acc, rows).astype(BF16)

    @pl.when(group == 1)
    def _():
        for rc, rows, acc in chunks():
            k = acc * K_SCALE
            store_transposed(rc, rope(k, rows) if grid_mode else k)

    @pl.when(group == 4)
    def _():
        for _, rows, acc in chunks():
            proj_ref[rows, :] = _silu(_conv3_rows(acc, cv_ref[...], period)).astype(BF16)

    @pl.when(group == 5)
    def _():
        for rc, _, acc in chunks():
            store_transposed(rc, _silu(_conv3_rows(acc, cv_ref[...], period)) * K_SCALE)


def _staging_specs(weights, n_groups, n_tiles):
    specs, shapes, block_bytes = [], [], []
    for w in weights:
        rows, cols = w.shape
        slab = BF16_ROWS
        while rows % slab or rows // slab > n_groups * n_tiles:
            slab += BF16_ROWS
        n_slabs = rows // slab
        specs.append(pl.BlockSpec(
            (slab, cols),
            lambda g, m, n_slabs=n_slabs: (jnp.minimum(g * n_tiles + m, n_slabs - 1), 0)))
        shapes.append(jax.ShapeDtypeStruct((rows, cols), BF16))
        block_bytes.append(_nbytes((slab, cols), F32) + _nbytes((slab, cols), BF16))
    return specs, shapes, block_bytes


def _mix_in_call(h2d, w_in_t, conv_qk, rope, *, seq_len, period, tm, stage=()):
    tokens = h2d.shape[0]
    grid_mode = rope is not None
    tiles_per_seq = seq_len // tm
    slabs = tm // LANES
    last = tokens // tm - 1
    n_groups = N_PROJ // GROUP_DIM
    stage_specs, stage_shapes, stage_bytes = _staging_specs(stage, n_groups, tokens // tm)

    def tile(g, m):
        return jnp.where(g % 2 == 1, last - m, m)

    def is_key(g):
        return (g == 1) | (g == 5)

    def proj_index(g, m):
        col = g - (g >= 1).astype(jnp.int32) - (g >= 5).astype(jnp.int32)
        return jnp.where(is_key(g), last, tile(g, m)), col

    def kt_index(g, m):
        row = jnp.where(g == 0, last, jnp.where(is_key(g), tile(g, m), 0))
        return row, (g >= 5).astype(jnp.int32), 0

    in_specs = [
        pl.BlockSpec((tm, D_MODEL), lambda g, m: (tile(g, m), 0)),
        pl.BlockSpec((GROUP_DIM, D_MODEL), lambda g, m: (g, 0)),
        pl.BlockSpec((3, GROUP_DIM), lambda g, m: (0, jnp.clip(g - 4, 0, 1))),
    ]
    args = [h2d, w_in_t, conv_qk]
    blocks = [_nbytes((tm, D_MODEL), BF16), _nbytes((GROUP_DIM, D_MODEL), F32),
              _nbytes((tm, GROUP_DIM), BF16) * 2, _nbytes((8, GROUP_DIM), F32)]
    if grid_mode:
        in_specs += [pl.BlockSpec((tm, HEAD_DIM),
                                  lambda g, m: (jnp.where(g <= 1, tile(g, m) % tiles_per_seq, 0),
                                                0))] * 2
        args += list(rope)
        blocks += [_nbytes((tm, HEAD_DIM), F32)] * 2
    in_specs += stage_specs
    args += list(stage)
    blocks += stage_bytes
    scratch_defs = [((GROUP_DIM, D_MODEL), BF16)]

    return pl.pallas_call(
        functools.partial(_mix_in_kernel, tm=tm, period=period, grid_mode=grid_mode,
                          n_staged=len(stage)),
        grid=(n_groups, tokens // tm),
        in_specs=in_specs,
        out_specs=[pl.BlockSpec((tm, GROUP_DIM), lambda g, m: proj_index(g, m)),
                   pl.BlockSpec((slabs, GROUP_DIM, LANES), lambda g, m: kt_index(g, m))]
        + stage_specs,
        out_shape=[jax.ShapeDtypeStruct((tokens, N_STORED), BF16),
                   jax.ShapeDtypeStruct((tokens // LANES, 2 * GROUP_DIM, LANES), BF16)]
        + stage_shapes,
        scratch_shapes=[pltpu.VMEM(s, d) for s, d in scratch_defs],
        compiler_params=pltpu.CompilerParams(
            dimension_semantics=("arbitrary", "arbitrary"),
            vmem_limit_bytes=_vmem_limit(blocks, sum(_nbytes(s, d) for s, d in scratch_defs))),
        name="mix_in_grid" if grid_mode else "mix_in_seq",
    )(*args)


_NN = (((2,), (1,)), ((0,), (0,)))
_NT = (((2,), (2,)), ((0,), (0,)))


def _bdot(x, y, dims):
    return lax.dot_general(x, y, dims, preferred_element_type=F32)


def _chunk_loop(nc, body, reverse=False):
    if nc <= 2:
        for i in range(nc):
            body(nc - 1 - i if reverse else i)
    else:
        def step(i, carry):
            body(nc - 1 - i if reverse else i)
            return carry
        lax.fori_loop(0, nc, step, 0)


def _rows(c, base=0):
    start = base + c * CHUNK
    if not isinstance(start, int):
        start = pl.multiple_of(start, CHUNK)
    return pl.ds(start, CHUNK)


def _is_chunk(c, value):
    return isinstance(c, int) and c == value


class _StepChunk:
    def __init__(self, index, local):
        self.index, self.local = index, local


def _chunk_index(c):
    return c.index if isinstance(c, _StepChunk) else c


class _Chunks:
    def __init__(self, nc, bb, hp, streamed):
        self.nc, self.bb, self.hp, self.streamed = nc, bb, hp, streamed

    def rows(self, ref, bi, c, cols):
        if self.streamed:
            return ref[bi, c.local * CHUNK:(c.local + 1) * CHUNK, cols]
        return ref[_rows(c, bi * self.nc * CHUNK), cols]

    def store_rows(self, ref, bi, c, cols, value):
        if self.streamed:
            ref[bi, c.local * CHUNK:(c.local + 1) * CHUNK, cols] = value
        else:
            ref[_rows(c, bi * self.nc * CHUNK), cols] = value

    def per_chunk(self, ref, bi, c):
        return ref[bi, c.local] if self.streamed else ref[bi * self.nc + c]

    def heads(self, ref, c):
        return jnp.stack([self.rows(ref, bi, c, slice(hh * HEAD_DIM, (hh + 1) * HEAD_DIM))
                          for bi in range(self.bb) for hh in range(self.hp)])

    def keys_t(self, kt_ref, c):
        parts = [self.per_chunk(kt_ref, bi, c).reshape(self.hp, HEAD_DIM, CHUNK)
                 for bi in range(self.bb)]
        return parts[0] if self.bb == 1 else jnp.concatenate(parts, axis=0)

    def gate_row(self, g_ref, bi, c, row):
        if self.streamed:
            return g_ref[bi, c.local, pl.ds(row, 1), :]
        return g_ref[bi * self.nc + c, pl.ds(row, 1), :]


def _run_scan(nc, streamed, init_state, fwd, bwd, emit_state):
    if not streamed:
        init_state(0)
        _chunk_loop(nc, fwd)
        if emit_state is not None:
            emit_state(0)
        init_state(1)
        _chunk_loop(nc, bwd, reverse=True)
        if emit_state is not None:
            emit_state(1)
        return
    assert emit_state is None
    s = pl.program_id(0)
    steps = _stream_steps(nc)

    @pl.when(s == 0)
    def _():
        init_state(0)

    @pl.when(s == steps)
    def _():
        init_state(1)

    @pl.when(s < steps)
    def _():
        for local in range(STREAM_CHUNKS):
            fwd(_StepChunk(s * STREAM_CHUNKS + local, local))

    @pl.when(s >= steps)
    def _():
        for local in reversed(range(STREAM_CHUNKS)):
            bwd(_StepChunk((2 * steps - 1 - s) * STREAM_CHUNKS + local, local))


def _per_seq(x, bb):
    return x if bb == 1 else jnp.concatenate([x] * bb, axis=0)


def _group_block(seq_len, hp, group, bb):
    per = N_HEADS // hp
    return pl.BlockSpec((bb * seq_len, hp * HEAD_DIM), lambda b, hg: (b, group * per + hg))


def _kt_block(nc, hp, group, bb):
    per = N_HEADS // hp
    return pl.BlockSpec((bb * nc, hp * HEAD_DIM, LANES), lambda b, hg: (b, group * per + hg, 0))


def _stream_steps(nc):
    assert nc % STREAM_CHUNKS == 0
    return nc // STREAM_CHUNKS


def _step_block(s, nc):
    steps = _stream_steps(nc)
    return jnp.where(s < steps, s, 2 * steps - 1 - s)


def _stream_rows_block(bb, nc, group):
    return pl.BlockSpec((bb, STREAM_CHUNKS * CHUNK, GROUP_DIM),
                        lambda s: (0, _step_block(s, nc), group))


def _stream_out_block(bb, nc):
    steps = _stream_steps(nc)
    return pl.BlockSpec((bb, STREAM_CHUNKS * CHUNK, GROUP_DIM),
                        lambda s: (0, jnp.minimum(2 * steps - 1 - s, steps - 1), 0))


def _stream_chunk_block(bb, nc, tail, tail_index):
    return pl.BlockSpec((bb, STREAM_CHUNKS) + tail, lambda s: (0, _step_block(s, nc)) + tail_index)


def _ret_kernel(*refs, nc, bb, hp, has_init, emit_states, streamed):
    refs = list(refs)
    theta_ref, q_ref, kt_ref, v_ref, rg_ref, gn_ref = refs[:6]
    pos = 6
    s0_ref = None
    if has_init:
        s0_ref = refs[pos]
        pos += 1
    o_ref = refs[pos]
    pos += 1
    st_ref = None
    if emit_states:
        st_ref = refs[pos]
        pos += 1
    hist, s_scr, decay_scr, qdec_scr, kdec_scr, cdec_scr = refs[pos:pos + 6]
    ck = _Chunks(nc, bb, hp, streamed)
    hg = 0 if streamed else pl.program_id(1)
    hsel = pl.ds(hg * hp, hp)

    @pl.when(pl.program_id(0) == 0)
    def _():
        ii = lax.broadcasted_iota(jnp.int32, (CHUNK, CHUNK), 0)
        jj = lax.broadcasted_iota(jnp.int32, (CHUNK, CHUNK), 1)
        d = (ii - jj).astype(F32)
        p_col = lax.broadcasted_iota(jnp.int32, (CHUNK, HEAD_DIM), 0).astype(F32)
        p_row = lax.broadcasted_iota(jnp.int32, (1, LANES), 1).astype(F32)
        for hh in range(hp):
            h = hg * hp + hh
            lg_f = _log_sigmoid(jnp.full((1, LANES), theta_ref[0, h], F32))
            lg_b = _log_sigmoid(jnp.full((1, LANES), theta_ref[1, h], F32))
            decay_scr[h] = (jnp.where(d >= 0, jnp.exp(lg_f * jnp.maximum(d, 0.0)), 0.0)
                            + jnp.where(d <= 0, jnp.exp(lg_b * jnp.maximum(-d, 0.0)), 0.0))
            qdec_scr[0, h] = jnp.exp(lg_f[:, :1] * (p_col + 1.0))
            qdec_scr[1, h] = jnp.exp(lg_b[:, :1] * (CHUNK - p_col))
            kdec_scr[0, h] = jnp.exp(lg_f * (CHUNK - 1.0 - p_row))
            kdec_scr[1, h] = jnp.exp(lg_b * p_row)
            cdec_scr[0, h] = jnp.exp(lg_f * float(CHUNK))
            cdec_scr[1, h] = jnp.exp(lg_b * float(CHUNK))

    def table(ref, *lead):
        return _per_seq(ref[(*lead, hsel)], bb)

    def kv_update(c, v, direction):
        kd = (ck.keys_t(kt_ref, c).astype(F32)
              * table(kdec_scr, direction)).astype(BF16)
        s_scr[...] = s_scr[...] * table(cdec_scr, direction)[:, :, :1] + _bdot(kd, v, _NN)

    def init_state(direction):
        if has_init:
            for bi in range(bb):
                s_scr[bi * hp:(bi + 1) * hp] = s0_ref[bi, 0, direction]
        else:
            s_scr[...] = jnp.zeros((bb * hp, HEAD_DIM, HEAD_DIM), F32)

    def emit_state(direction):
        for bi in range(bb):
            st_ref[bi, 0, direction] = s_scr[bi * hp:(bi + 1) * hp]

    def fwd(c):
        hist[_chunk_index(c)] = s_scr[...].astype(BF16)
        kv_update(c, ck.heads(v_ref, c), 0)

    def bwd(c):
        q, v = ck.heads(q_ref, c), ck.heads(v_ref, c)
        kt = ck.keys_t(kt_ref, c)
        att = (_bdot(q, kt, _NN) * table(decay_scr)).astype(BF16)
        o = _bdot(att, v, _NN)
        if has_init or not _is_chunk(c, 0):
            o = o + _bdot(q, hist[_chunk_index(c)], _NN) * table(qdec_scr, 0)
        if has_init or not _is_chunk(c, nc - 1):
            o = o + _bdot(q, s_scr[...].astype(BF16), _NN) * table(qdec_scr, 1)
        y = _ln_rows(o)
        for bi in range(bb):
            for hh in range(hp):
                cs = slice(hh * HEAD_DIM, (hh + 1) * HEAD_DIM)
                gate = _silu(ck.rows(rg_ref, bi, c, cs).astype(F32))
                ck.store_rows(o_ref, bi, c, cs,
                              (y[bi * hp + hh] * gn_ref[:, cs] * gate).astype(BF16))
        kv_update(c, v, 1)

    _run_scan(nc, streamed, init_state, fwd, bwd, emit_state if emit_states else None)


def _ret_call(proj, kt, theta, gn, s0, *, batch, seq_len, bb, hp, emit_states, streamed=False):
    nc = seq_len // CHUNK
    has_init = s0 is not None
    assert N_HEADS % hp == 0 and batch % bb == 0
    if streamed:
        assert bb == batch and hp == N_HEADS and not emit_states
        proj = proj.reshape(batch, seq_len, N_STORED)
        kt = kt.reshape(batch, nc, 2 * GROUP_DIM, LANES)
        state_spec = pl.BlockSpec((bb, 1, 2, hp, HEAD_DIM, HEAD_DIM), lambda s: (0,) * 6)
        in_specs = [pl.BlockSpec(memory_space=pltpu.SMEM),
                    _stream_rows_block(bb, nc, 0),
                    _stream_chunk_block(bb, nc, (GROUP_DIM, LANES), (0, 0)),
                    _stream_rows_block(bb, nc, 1), _stream_rows_block(bb, nc, 2),
                    pl.BlockSpec((1, GROUP_DIM), lambda s: (0, 0))]
        blocks = [_nbytes((bb, STREAM_CHUNKS * CHUNK, GROUP_DIM), BF16)] * 5
        out_specs = [_stream_out_block(bb, nc)]
        out_shape = [jax.ShapeDtypeStruct((batch, seq_len, GROUP_DIM), BF16)]
        grid, semantics = (2 * _stream_steps(nc),), ("arbitrary",)
    else:
        state_spec = pl.BlockSpec((bb, 1, 2, hp, HEAD_DIM, HEAD_DIM),
                                  lambda b, hg: (b, 0, 0, hg, 0, 0))
        in_specs = [pl.BlockSpec(memory_space=pltpu.SMEM),
                    _group_block(seq_len, hp, 0, bb), _kt_block(nc, hp, 0, bb),
                    _group_block(seq_len, hp, 1, bb), _group_block(seq_len, hp, 2, bb),
                    pl.BlockSpec((1, hp * HEAD_DIM), lambda b, hg: (0, hg))]
        blocks = [_nbytes((bb * seq_len, hp * HEAD_DIM), BF16)] * 5
        out_specs = [pl.BlockSpec((bb * seq_len, hp * HEAD_DIM), lambda b, hg: (b, hg))]
        out_shape = [jax.ShapeDtypeStruct((batch * seq_len, GROUP_DIM), BF16)]
        grid, semantics = (batch // bb, N_HEADS // hp), ("arbitrary", "arbitrary")
    args = [theta, proj, kt, proj, proj, gn]
    if has_init:
        in_specs.append(state_spec)
        args.append(s0)
        blocks.append(_nbytes((bb, 2, hp, HEAD_DIM, HEAD_DIM), F32))
    if emit_states:
        out_specs.append(state_spec)
        out_shape.append(jax.ShapeDtypeStruct((batch, 1, 2, N_HEADS, HEAD_DIM, HEAD_DIM), F32))
        blocks.append(_nbytes((bb, 2, hp, HEAD_DIM, HEAD_DIM), F32))
    scratch_defs = [((nc, bb * hp, HEAD_DIM, HEAD_DIM), BF16), ((bb * hp, HEAD_DIM, HEAD_DIM), F32),
                    ((N_HEADS, CHUNK, CHUNK), F32), ((2, N_HEADS, CHUNK, HEAD_DIM), F32),
                    ((2, N_HEADS, 1, LANES), F32), ((2, N_HEADS, 1, LANES), F32)]
    outs = pl.pallas_call(
        functools.partial(_ret_kernel, nc=nc, bb=bb, hp=hp, has_init=has_init,
                          emit_states=emit_states, streamed=streamed),
        grid=grid,
        in_specs=in_specs, out_specs=out_specs, out_shape=out_shape,
        scratch_shapes=[pltpu.VMEM(s, d) for s, d in scratch_defs],
        compiler_params=pltpu.CompilerParams(
            dimension_semantics=semantics,
            vmem_limit_bytes=_vmem_limit(blocks, sum(_nbytes(s, d) for s, d in scratch_defs))),
        name="ret_grid" if has_init else "ret_seq",
    )(*args)
    if streamed:
        outs = [outs[0].reshape(batch * seq_len, GROUP_DIM)] + list(outs[1:])
    return outs


def _mlstm_kernel(*refs, nc, bb, hp, has_init, emit_states, streamed):
    refs = list(refs)
    q_ref, kt_ref, v_ref, mo_ref, g_ref, gn_ref = refs[:6]
    pos = 6
    c0_ref = n0_ref = m0_ref = None
    if has_init:
        c0_ref, n0_ref, m0_ref = refs[pos:pos + 3]
        pos += 3
    o_ref = refs[pos]
    pos += 1
    c_out = n_out = m_out = None
    if emit_states:
        c_out, n_out, m_out = refs[pos:pos + 3]
        pos += 3
    c_hist, n_hist, m_hist, c_scr, n_scr, m_scr = refs[pos:pos + 6]

    ck = _Chunks(nc, bb, hp, streamed)
    b = 0 if streamed else pl.program_id(0)
    hg = 0 if streamed else pl.program_id(1)
    ii = lax.broadcasted_iota(jnp.int32, (CHUNK, CHUNK), 0)
    jj = lax.broadcasted_iota(jnp.int32, (CHUNK, CHUNK), 1)
    eye = ii == jj
    lower = ii >= jj
    upper = ii <= jj

    ne = bb * hp

    def col(row):
        return jnp.sum(jnp.where(eye, row, 0.0), axis=2, keepdims=True)

    def gate_rows(c, kind):
        base = kind * N_HEADS + hg * hp
        return jnp.stack([ck.gate_row(g_ref, bi, c, base + hh)
                          for bi in range(bb) for hh in range(hp)])

    def direction_out(a, q, qf, v, c_b, n_row, m, b_row, i_row, mask):
        b_col = col(b_row)
        dm = jnp.where(mask, b_col - b_row + i_row, -jnp.inf)
        inter = b_col + m
        mt = jnp.maximum(inter, jnp.max(dm, axis=2, keepdims=True))
        w = jnp.exp(dm - mt)
        sp = jnp.exp(inter - mt)
        s = a * w
        num = _bdot(s.astype(BF16), v, _NN)
        den = jnp.sum(s, axis=2, keepdims=True)
        if c_b is not None:
            num = num + _bdot(q, c_b, _NN) * sp
            den = den + jnp.sum(qf * n_row, axis=2, keepdims=True) * sp
        return num / jnp.maximum(jnp.abs(den), jnp.exp(-mt))

    def state_update(c, v, b_row, i_row, last):
        m = m_scr[...][:, :, :1]
        b_last = b_row[:, :, last:last + 1]
        g = b_last - b_row + i_row
        m_new = jnp.maximum(b_last + m, jnp.max(g, axis=2, keepdims=True))
        wk = jnp.exp(g - m_new)
        sc = jnp.exp(b_last + m - m_new)
        kt = ck.keys_t(kt_ref, c)
        kw = (kt.astype(F32) * wk).astype(BF16)
        c_scr[...] = c_scr[...] * sc + _bdot(kw, v, _NN)
        wk8 = jnp.broadcast_to(wk, (ne, 8, CHUNK)).astype(BF16)
        n_scr[...] = n_scr[...] * sc + _bdot(wk8, kt, _NT)[:, :1, :]
        m_scr[...] = jnp.broadcast_to(m_new, (ne, 1, LANES))

    def init_state(direction):
        if has_init:
            for bi in range(bb):
                c_scr[bi * hp:(bi + 1) * hp] = c0_ref[bi, 0, direction]
            n_scr[...] = jnp.stack([n0_ref[bi, 0, direction, pl.ds(hg * hp + hh, 1), :]
                                    for bi in range(bb) for hh in range(hp)])
            m_scr[...] = jnp.stack([
                jnp.full((1, LANES),
                         m0_ref[((b * bb + bi) * 2 + direction) * N_HEADS + hg * hp + hh], F32)
                for bi in range(bb) for hh in range(hp)])
        else:
            c_scr[...] = jnp.zeros((ne, HEAD_DIM, HEAD_DIM), F32)
            n_scr[...] = jnp.zeros((ne, 1, HEAD_DIM), F32)
            m_scr[...] = jnp.zeros((ne, 1, LANES), F32)

    def emit_state(direction):
        for bi in range(bb):
            c_out[bi, 0, direction] = c_scr[bi * hp:(bi + 1) * hp]
            for hh in range(hp):
                n_out[bi, 0, direction, pl.ds(hh, 1), :] = n_scr[bi * hp + hh]
                m_out[bi, pl.ds(direction * N_HEADS + hh, 1), :] = m_scr[bi * hp + hh]

    def fwd(c):
        ci = _chunk_index(c)
        c_hist[ci] = c_scr[...].astype(BF16)
        n_hist[ci] = n_scr[...]
        m_hist[ci] = m_scr[...]
        state_update(c, ck.heads(v_ref, c), gate_rows(c, 1), gate_rows(c, 0), CHUNK - 1)

    def bwd(c):
        i_f, b_f, i_b, b_b = (gate_rows(c, kind) for kind in (0, 1, 2, 3))
        q, v = ck.heads(q_ref, c), ck.heads(v_ref, c)
        qf = q.astype(F32)
        a = _bdot(q, ck.keys_t(kt_ref, c), _NN)
        zero_f = not has_init and _is_chunk(c, 0)
        zero_b = not has_init and _is_chunk(c, nc - 1)
        ci = _chunk_index(c)
        h_f = direction_out(a, q, qf, v, None if zero_f else c_hist[ci], n_hist[ci],
                            m_hist[ci][:, :, :1], b_f, i_f, lower)
        h_b = direction_out(a, q, qf, v, None if zero_b else c_scr[...].astype(BF16),
                            n_scr[...], m_scr[...][:, :, :1], b_b, i_b, upper)
        y = _ln_rows(h_f + h_b)
        for bi in range(bb):
            for hh in range(hp):
                cs = slice(hh * HEAD_DIM, (hh + 1) * HEAD_DIM)
                gate = jax.nn.sigmoid(ck.rows(mo_ref, bi, c, cs).astype(F32))
                ck.store_rows(o_ref, bi, c, cs,
                              (gate * (y[bi * hp + hh] * gn_ref[:, cs])).astype(BF16))
        state_update(c, v, b_b, i_b, 0)

    _run_scan(nc, streamed, init_state, fwd, bwd, emit_state if emit_states else None)


def _mlstm_call(proj, kt, gates, gn, init, *, batch, seq_len, bb, hp, emit_states,
                streamed=False):
    nc = seq_len // CHUNK
    has_init = init is not None
    assert N_HEADS % hp == 0 and (hp == N_HEADS or not emit_states) and batch % bb == 0
    if streamed:
        assert bb == batch and hp == N_HEADS and not emit_states
        proj = proj.reshape(batch, seq_len, N_STORED)
        kt = kt.reshape(batch, nc, 2 * GROUP_DIM, LANES)
        gates = gates.reshape(batch, nc, N_GATES, LANES)
        c_spec = pl.BlockSpec((bb, 1, 2, hp, HEAD_DIM, HEAD_DIM), lambda s: (0,) * 6)
        n_spec = pl.BlockSpec((bb, 1, 2, N_HEADS, HEAD_DIM), lambda s: (0,) * 5)
        in_specs = [_stream_rows_block(bb, nc, 3),
                    _stream_chunk_block(bb, nc, (GROUP_DIM, LANES), (1, 0)),
                    _stream_rows_block(bb, nc, 4), _stream_rows_block(bb, nc, 5),
                    _stream_chunk_block(bb, nc, (N_GATES, LANES), (0, 0)),
                    pl.BlockSpec((1, GROUP_DIM), lambda s: (0, 0))]
        blocks = ([_nbytes((bb, STREAM_CHUNKS * CHUNK, GROUP_DIM), BF16)] * 5
                  + [_nbytes((bb, STREAM_CHUNKS * N_GATES, LANES), F32)])
        out_specs = [_stream_out_block(bb, nc)]
        out_shape = [jax.ShapeDtypeStruct((batch, seq_len, GROUP_DIM), BF16)]
        grid, semantics = (2 * _stream_steps(nc),), ("arbitrary",)
    else:
        c_spec = pl.BlockSpec((bb, 1, 2, hp, HEAD_DIM, HEAD_DIM),
                              lambda b, hg: (b, 0, 0, hg, 0, 0))
        n_spec = pl.BlockSpec((bb, 1, 2, N_HEADS, HEAD_DIM), lambda b, hg: (b, 0, 0, 0, 0))
        in_specs = [_group_block(seq_len, hp, 3, bb), _kt_block(nc, hp, 1, bb),
                    _group_block(seq_len, hp, 4, bb), _group_block(seq_len, hp, 5, bb),
                    pl.BlockSpec((bb * nc, N_GATES, LANES), lambda b, hg: (b, 0, 0)),
                    pl.BlockSpec((1, hp * HEAD_DIM), lambda b, hg: (0, hg))]
        blocks = ([_nbytes((bb * seq_len, hp * HEAD_DIM), BF16)] * 5
                  + [_nbytes((bb * nc, N_GATES, LANES), F32)])
        out_specs = [pl.BlockSpec((bb * seq_len, hp * HEAD_DIM), lambda b, hg: (b, hg))]
        out_shape = [jax.ShapeDtypeStruct((batch * seq_len, GROUP_DIM), BF16)]
        grid, semantics = (batch // bb, N_HEADS // hp), ("arbitrary", "arbitrary")
    args = [proj, kt, proj, proj, gates, gn]
    if has_init:
        in_specs += [c_spec, n_spec, pl.BlockSpec(memory_space=pltpu.SMEM)]
        args += list(init)
        blocks.append(_nbytes((bb, 2, hp, HEAD_DIM, HEAD_DIM), F32))
    if emit_states:
        out_specs += [c_spec, n_spec, pl.BlockSpec((bb, 2 * N_HEADS, LANES), lambda b, hg: (b, 0, 0))]
        out_shape += [jax.ShapeDtypeStruct((batch, 1, 2, N_HEADS, HEAD_DIM, HEAD_DIM), F32),
                      jax.ShapeDtypeStruct((batch, 1, 2, N_HEADS, HEAD_DIM), F32),
                      jax.ShapeDtypeStruct((batch, 2 * N_HEADS, LANES), F32)]
        blocks.append(_nbytes((bb, 2, hp, HEAD_DIM, HEAD_DIM), F32))
    ne = bb * hp
    scratch_defs = [((nc, ne, HEAD_DIM, HEAD_DIM), BF16), ((nc, ne, 1, HEAD_DIM), F32),
                    ((nc, ne, 1, LANES), F32), ((ne, HEAD_DIM, HEAD_DIM), F32),
                    ((ne, 1, HEAD_DIM), F32), ((ne, 1, LANES), F32)]
    scratch_bytes = sum(_nbytes(s, d) for s, d in scratch_defs) + 16 * nc * ne * HEAD_DIM * 4
    outs = pl.pallas_call(
        functools.partial(_mlstm_kernel, nc=nc, bb=bb, hp=hp, has_init=has_init,
                          emit_states=emit_states, streamed=streamed),
        grid=grid,
        in_specs=in_specs, out_specs=out_specs, out_shape=out_shape,
        scratch_shapes=[pltpu.VMEM(s, d) for s, d in scratch_defs],
        compiler_params=pltpu.CompilerParams(
            dimension_semantics=semantics,
            vmem_limit_bytes=_vmem_limit(blocks, scratch_bytes)),
        name="mlstm_grid" if has_init else "mlstm_seq",
    )(*args)
    if streamed:
        outs = [outs[0].reshape(batch * seq_len, GROUP_DIM)] + list(outs[1:])
    return outs


def _mix_out_kernel(mr_ref, mm_ref, w_ref, x_ref, g1_ref, lg_ref, lb_ref, o_ref):
    for rc in range(o_ref.shape[0] // MIX_ROW_CHUNK):
        rows = slice(rc * MIX_ROW_CHUNK, (rc + 1) * MIX_ROW_CHUNK)
        mix = (jnp.dot(mr_ref[rows, :], w_ref[0:GROUP_DIM, :], preferred_element_type=F32)
               + jnp.dot(mm_ref[rows, :], w_ref[GROUP_DIM:, :], preferred_element_type=F32))
        o_ref[rows, :] = (_ln_rows(ALPHA * x_ref[rows, :] + g1_ref[0] * mix) * lg_ref[...]
                          + lb_ref[...])


def _mix_out_call(mix_r, mix_m, w_out_b, x2d, mod3, mod_row, ln_g, ln_b, *, tm):
    tokens = x2d.shape[0]
    blocks = [_nbytes((tm, GROUP_DIM), BF16)] * 2 + [_nbytes((D_MODEL, D_MODEL), BF16)] \
        + [_nbytes((tm, D_MODEL), F32)] * 2
    return pl.pallas_call(
        _mix_out_kernel,
        grid=(tokens // tm,),
        in_specs=[pl.BlockSpec((tm, GROUP_DIM), lambda m: (m, 0)),
                  pl.BlockSpec((tm, GROUP_DIM), lambda m: (m, 0)),
                  pl.BlockSpec((D_MODEL, D_MODEL), lambda m: (0, 0)),
                  pl.BlockSpec((tm, D_MODEL), lambda m: (m, 0)),
                  pl.BlockSpec((1, 1, D_MODEL), lambda m: (mod_row(m) * 6 + 2, 0, 0)),
                  pl.BlockSpec((1, D_MODEL), lambda m: (0, 0)),
                  pl.BlockSpec((1, D_MODEL), lambda m: (0, 0))],
        out_specs=pl.BlockSpec((tm, D_MODEL), lambda m: (m, 0)),
        out_shape=jax.ShapeDtypeStruct((tokens, D_MODEL), F32),
        compiler_params=pltpu.CompilerParams(
            dimension_semantics=("parallel",), vmem_limit_bytes=_vmem_limit(blocks)),
        name="mix_out",
    )(mix_r, mix_m, w_out_b, x2d, mod3, ln_g, ln_b)


def _ffn_kernel(x_ref, sh_ref, sc_ref, g2_ref, wu_ref, wg_ref, cv_ref, wd_ref, lg_ref, lb_ref,
                o_ref, h_scr, *, period):
    j = pl.program_id(1)

    def row_chunks():
        for rc in range(o_ref.shape[0] // MIX_ROW_CHUNK):
            yield slice(rc * MIX_ROW_CHUNK, (rc + 1) * MIX_ROW_CHUNK)

    @pl.when(j == 0)
    def _():
        for rows in row_chunks():
            h = _ln_rows(x_ref[rows, :]) * (1.0 + sc_ref[0]) + sh_ref[0]
            h_scr[rows, :] = h.astype(BF16)
        o_ref[...] = jnp.zeros_like(o_ref)

    lane = lax.broadcasted_iota(jnp.int32, (1, FF_TN), 1)
    duplicate = (j == FF_NJ - 1) & (lane < FF_OVERLAP)
    for rows in row_chunks():
        hb = h_scr[rows, :]
        u = jnp.dot(hb, wu_ref[...], preferred_element_type=F32)
        g = jnp.dot(hb, wg_ref[...], preferred_element_type=F32)
        hid = _silu(_conv3_rows(u, cv_ref[0], period)) * g
        hid = jnp.where(duplicate, 0.0, hid).astype(BF16)
        o_ref[rows, :] += jnp.dot(hid, wd_ref[...], preferred_element_type=F32)

    @pl.when(j == FF_NJ - 1)
    def _():
        for rows in row_chunks():
            y = ALPHA * x_ref[rows, :] + g2_ref[0] * o_ref[rows, :]
            o_ref[rows, :] = _ln_rows(y) * lg_ref[...] + lb_ref[...]


def _ff_offset(j):
    return pl.multiple_of(jnp.minimum(j * FF_TN, D_FF - FF_TN), LANES)


def _ffn_call(x1, mod3, mod_row, w_up_b, conv_tiles, w_down_b, ln_g, ln_b, *, period, tm):
    tokens = x1.shape[0]
    up_tile = (pl.Element(D_MODEL), pl.Element(FF_TN))
    blocks = [_nbytes((tm, D_MODEL), F32)] * 2 + [_nbytes((D_MODEL, FF_TN), BF16)] * 3 \
        + [_nbytes((8, FF_TN), F32)]
    scratch_bytes = _nbytes((tm, D_MODEL), BF16)
    return pl.pallas_call(
        functools.partial(_ffn_kernel, period=period),
        grid=(tokens // tm, FF_NJ),
        in_specs=[pl.BlockSpec((tm, D_MODEL), lambda m, j: (m, 0)),
                  pl.BlockSpec((1, 1, D_MODEL), lambda m, j: (mod_row(m) * 6 + 3, 0, 0)),
                  pl.BlockSpec((1, 1, D_MODEL), lambda m, j: (mod_row(m) * 6 + 4, 0, 0)),
                  pl.BlockSpec((1, 1, D_MODEL), lambda m, j: (mod_row(m) * 6 + 5, 0, 0)),
                  pl.BlockSpec(up_tile, lambda m, j: (0, _ff_offset(j))),
                  pl.BlockSpec(up_tile, lambda m, j: (0, pl.multiple_of(D_FF + _ff_offset(j), LANES))),
                  pl.BlockSpec((1, 3, FF_TN), lambda m, j: (j, 0, 0)),
                  pl.BlockSpec((pl.Element(FF_TN), pl.Element(D_MODEL)),
                               lambda m, j: (_ff_offset(j), 0)),
                  pl.BlockSpec((1, D_MODEL), lambda m, j: (0, 0)),
                  pl.BlockSpec((1, D_MODEL), lambda m, j: (0, 0))],
        out_specs=pl.BlockSpec((tm, D_MODEL), lambda m, j: (m, 0)),
        out_shape=jax.ShapeDtypeStruct((tokens, D_MODEL), F32),
        scratch_shapes=[pltpu.VMEM((tm, D_MODEL), BF16)],
        compiler_params=pltpu.CompilerParams(
            dimension_semantics=("parallel", "arbitrary"),
            vmem_limit_bytes=_vmem_limit(blocks, scratch_bytes)),
        name="ffn",
    )(x1, mod3, mod3, mod3, w_up_b, w_up_b, conv_tiles, w_down_b, ln_g, ln_b)


def _rope_tables(seq_len):
    quarter = HEAD_DIM // 4
    t = jnp.arange(seq_len)
    row = (t // GRID_W).astype(F32)
    col = (t % GRID_W).astype(F32)
    inv = ROPE_BASE ** (-jnp.arange(quarter, dtype=F32) / quarter)
    ang_r, ang_c = row[:, None] * inv, col[:, None] * inv
    cos_t = jnp.concatenate([jnp.cos(ang_r)] * 2 + [jnp.cos(ang_c)] * 2, axis=1)
    sin_t = jnp.concatenate([-jnp.sin(ang_r), jnp.sin(ang_r), -jnp.sin(ang_c), jnp.sin(ang_c)], axis=1)
    return cos_t, sin_t


def kernel(x_prompt, x_sample, state_ret, state_mlstm_C, state_mlstm_n, state_mlstm_m, c, c_ctx,
           w_mod, b_mod, w_in, b_gate, conv_qk, ret_theta, gn_ret, gn_mlstm, w_out,
           ln1_g, ln1_b, w_up, conv_ff, w_down, ln2_g, ln2_b):
    bp, lp, _ = x_prompt.shape
    bs, ls, _ = x_sample.shape
    layer = 0
    tm_mix, tm, tm_ffn = 1024, 512, 1024

    cvec = jnp.concatenate([c_ctx[None], c, jnp.zeros((8 - 1 - bs, D_MODEL), F32)], axis=0)
    mod = _mod_call(cvec, w_mod[layer], b_mod[layer][None])
    mod3 = mod.reshape(8 * 6, 1, D_MODEL)

    def row_prompt(tile):
        return lambda m: 0

    def row_sample(tile):
        return lambda m: 1 + (m * tile) // ls

    w_l = w_in[layer].T
    w_gate = w_l[N_PROJ:]
    b_gate_col = b_gate[layer].reshape(N_GATES, 1)
    ff_offsets = [min(j * FF_TN, D_FF - FF_TN) for j in range(FF_NJ)]
    conv_tiles = jnp.stack([conv_ff[layer][:, o:o + FF_TN] for o in ff_offsets])
    gn_r = gn_ret[layer].reshape(1, GROUP_DIM)
    gn_m = gn_mlstm[layer].reshape(1, GROUP_DIM)
    theta = ret_theta[layer]
    ln1 = (ln1_g[layer][None], ln1_b[layer][None])
    ln2 = (ln2_g[layer][None], ln2_b[layer][None])

    def mixer(x, mod_row, rope, period, bb, hp, ret_init, mlstm_init, emit_states, stage=(),
              streamed=False):
        batch, seq_len, _ = x.shape
        x2d = x.reshape(batch * seq_len, D_MODEL)
        h2d, gates = _ln_mod_call(x2d, mod3, mod_row(tm_mix), w_gate, b_gate_col, tm=tm_mix)
        proj, kt, *staged = _mix_in_call(h2d, w_l, conv_qk[layer], rope, seq_len=seq_len,
                                         period=period, tm=tm_mix, stage=stage)
        ret = _ret_call(proj, kt, theta, gn_r, ret_init, batch=batch, seq_len=seq_len, bb=bb,
                        hp=hp, emit_states=emit_states, streamed=streamed)
        mls = _mlstm_call(proj, kt, gates, gn_m, mlstm_init, batch=batch, seq_len=seq_len, bb=bb,
                          hp=hp, emit_states=emit_states, streamed=streamed)
        return x2d, ret, mls, staged

    def channel_mix(x2d, shape, ret, mls, mod_row, period, weights):
        w_up_b, w_down_b, w_out_b = weights
        x1 = _mix_out_call(ret[0], mls[0], w_out_b, x2d, mod3, mod_row(tm), *ln1, tm=tm)
        y = _ffn_call(x1, mod3, mod_row(tm_ffn), w_up_b, conv_tiles, w_down_b, *ln2,
                      period=period, tm=tm_ffn)
        return y.reshape(shape)

    xp, ret_p, mls_p, weights_b = mixer(x_prompt, row_prompt, None, lp, 4, N_HEADS, None, None,
                                        True, stage=(w_up[layer], w_down[layer], w_out[layer]))
    m0 = state_mlstm_m.reshape(-1)
    xs, ret_s, mls_s, _ = mixer(x_sample, row_sample, _rope_tables(ls), GRID_W, bs, N_HEADS,
                                state_ret, (state_mlstm_C, state_mlstm_n, m0), False,
                                streamed=True)
    y_prompt = channel_mix(xp, x_prompt.shape, ret_p, mls_p, row_prompt, lp, weights_b)
    y_sample = channel_mix(xs, x_sample.shape, ret_s, mls_s, row_sample, GRID_W, weights_b)
    ret_states, mls_states = ret_p[1:], mls_p[1:]

    new_ret = ret_states[0]
    new_c, new_n, new_m = mls_states
    new_m = new_m[:, :, 0].reshape(bp, 1, 2, N_HEADS)
    return (y_prompt, y_sample, new_ret, new_c, new_n, new_m)
```

```python
import functools

import jax
import jax.numpy as jnp
from jax import lax
from jax.experimental import pallas as pl
from jax.experimental.pallas import tpu as pltpu

F32 = jnp.float32
BF16 = jnp.bfloat16

D_MODEL = 2048
GRID_W = 64
HEAD_DIM = 256
N_HEADS = 4
GROUP_DIM = N_HEADS * HEAD_DIM
N_PROJ = 8 * GROUP_DIM
N_GATES = 4 * N_HEADS
D_FF = 5504
CHUNK = 128
ROPE_BASE = 10000.0
ALPHA = 2.0 ** 0.25
LN_EPS = 1e-6
K_SCALE = HEAD_DIM ** -0.5

LANES = 128
BF16_ROWS = 16
V7X_VMEM_LIMIT_CAP = 60 * 1024 * 1024
VMEM_TEMP_ALLOWANCE = 16 * 1024 * 1024

MIX_ROW_CHUNK = 512
N_STORED = 6 * GROUP_DIM
STREAM_CHUNKS = 4
FF_TN = 512
FF_NJ = -(-D_FF // FF_TN)
FF_OVERLAP = FF_NJ * FF_TN - D_FF


def _vmem_limit(block_bytes, scratch_bytes=0):
    est = 2 * sum(block_bytes) + scratch_bytes + VMEM_TEMP_ALLOWANCE
    return int(min(V7X_VMEM_LIMIT_CAP, est))


def _nbytes(shape, dtype):
    n = 1
    for s in shape:
        n *= s
    return n * jnp.dtype(dtype).itemsize


def _ln_rows(x):
    mu = jnp.mean(x, axis=-1, keepdims=True)
    xc = x - mu
    var = jnp.mean(xc * xc, axis=-1, keepdims=True)
    return xc * lax.rsqrt(var + LN_EPS)


def _silu(x):
    return x * jax.nn.sigmoid(x)


def _log_sigmoid(x):
    return jnp.minimum(x, 0.0) - jnp.log1p(jnp.exp(-jnp.abs(x)))


def _conv3_rows(u, w, period):
    rows = u.shape[0]
    t = lax.broadcasted_iota(jnp.int32, (rows, 1), 0) & (period - 1)
    prev = jnp.where(t == 0, 0.0, pltpu.roll(u, 1, 0))
    nxt = jnp.where(t == period - 1, 0.0, pltpu.roll(u, rows - 1, 0))
    return prev * w[0:1] + u * w[1:2] + nxt * w[2:3]


def _cumsum_lanes(x, lane, reverse):
    s = 1
    while s < LANES:
        if reverse:
            x = x + jnp.where(lane < LANES - s, pltpu.roll(x, LANES - s, 1), 0.0)
        else:
            x = x + jnp.where(lane >= s, pltpu.roll(x, s, 1), 0.0)
        s *= 2
    return x


def _mod_kernel(c_ref, w_ref, b_ref, o_ref):
    a = _silu(c_ref[...]).astype(BF16)
    o_ref[...] = jnp.dot(a, w_ref[...].astype(BF16), preferred_element_type=F32) + b_ref[...]


def _mod_call(cvec, w_mod, b_mod):
    rows, n = cvec.shape[0], w_mod.shape[1]
    tn = 1024
    blocks = [_nbytes((rows, D_MODEL), F32), _nbytes((D_MODEL, tn), F32),
              _nbytes((8, tn), F32), _nbytes((rows, tn), F32)]
    return pl.pallas_call(
        _mod_kernel,
        grid=(n // tn,),
        in_specs=[pl.BlockSpec((rows, D_MODEL), lambda j: (0, 0)),
                  pl.BlockSpec((D_MODEL, tn), lambda j: (0, j)),
                  pl.BlockSpec((1, tn), lambda j: (0, j))],
        out_specs=pl.BlockSpec((rows, tn), lambda j: (0, j)),
        out_shape=jax.ShapeDtypeStruct((rows, n), F32),
        compiler_params=pltpu.CompilerParams(
            dimension_semantics=("arbitrary",), vmem_limit_bytes=_vmem_limit(blocks)),
        name="mod",
    )(cvec, w_mod, b_mod)


def _ln_mod_kernel(x_ref, sh_ref, sc_ref, wg_ref, bg_ref, h_ref, gates_ref, *, tm):
    h = _ln_rows(x_ref[...]) * (1.0 + sc_ref[0]) + sh_ref[0]
    hb = h.astype(BF16)
    h_ref[...] = hb
    g = lax.dot_general(wg_ref[...].astype(BF16), hb, (((1,), (1,)), ((), ())),
                        preferred_element_type=F32) + bg_ref[...]
    row = lax.broadcasted_iota(jnp.int32, (N_GATES, LANES), 0)
    lane = lax.broadcasted_iota(jnp.int32, (N_GATES, LANES), 1)
    kind = lax.shift_right_logical(row, 2)
    for s in range(tm // LANES):
        gs = g[:, s * LANES:(s + 1) * LANES]
        ls = _log_sigmoid(gs)
        gates_ref[s] = jnp.where(kind == 1, _cumsum_lanes(ls, lane, False),
                                 jnp.where(kind == 3, _cumsum_lanes(ls, lane, True), gs))


def _ln_mod_call(x2d, mod3, mod_row, w_gate, b_gate_col, *, tm):
    tokens = x2d.shape[0]
    slabs = tm // LANES
    blocks = [_nbytes((tm, D_MODEL), F32), _nbytes((tm, D_MODEL), BF16),
              _nbytes((N_GATES, D_MODEL), F32), _nbytes((slabs, N_GATES, LANES), F32)]
    return pl.pallas_call(
        functools.partial(_ln_mod_kernel, tm=tm),
        grid=(tokens // tm,),
        in_specs=[pl.BlockSpec((tm, D_MODEL), lambda m: (m, 0)),
                  pl.BlockSpec((1, 1, D_MODEL), lambda m: (mod_row(m) * 6 + 0, 0, 0)),
                  pl.BlockSpec((1, 1, D_MODEL), lambda m: (mod_row(m) * 6 + 1, 0, 0)),
                  pl.BlockSpec((N_GATES, D_MODEL), lambda m: (0, 0)),
                  pl.BlockSpec((N_GATES, 1), lambda m: (0, 0))],
        out_specs=[pl.BlockSpec((tm, D_MODEL), lambda m: (m, 0)),
                   pl.BlockSpec((slabs, N_GATES, LANES), lambda m: (m, 0, 0))],
        out_shape=[jax.ShapeDtypeStruct((tokens, D_MODEL), BF16),
                   jax.ShapeDtypeStruct((tokens // LANES, N_GATES, LANES), F32)],
        compiler_params=pltpu.CompilerParams(
            dimension_semantics=("parallel",), vmem_limit_bytes=_vmem_limit(blocks)),
        name="ln_mod",
    )(x2d, mod3, mod3, w_gate, b_gate_col)


def _mix_in_kernel(*refs, tm, period, grid_mode, n_staged):
    refs = list(refs)
    h_ref, w_ref, cv_ref = refs[:3]
    pos = 3
    if grid_mode:
        cos_ref, sin_ref = refs[pos:pos + 2]
        pos += 2
    staged_in = refs[pos:pos + n_staged]
    pos += n_staged
    proj_ref, kt_ref = refs[pos:pos + 2]
    pos += 2
    staged_out = refs[pos:pos + n_staged]
    wb_scr, wf_scr, w_sem = refs[pos + n_staged:pos + n_staged + 3]
    group = pl.program_id(0)
    first_tile = pl.program_id(1) == 0

    def weight_copy(g):
        return pltpu.make_async_copy(w_ref.at[pl.ds(g * GROUP_DIM, GROUP_DIM), :], wf_scr, w_sem)

    @pl.when(first_tile & (group == 0))
    def _():
        weight_copy(group).start()

    @pl.when(first_tile)
    def _():
        weight_copy(group).wait()
        wb_scr[...] = wf_scr[...].astype(BF16)

    @pl.when(first_tile & (group + 1 < pl.num_programs(0)))
    def _():
        weight_copy(group + 1).start()

    def chunks(sizes=(tm // 2, tm // 2)):
        for src, dst in zip(staged_in, staged_out):
            dst[...] = src[...].astype(BF16)
        start = 0
        for n_rows in sizes:
            rows = slice(start, start + n_rows)
            yield start, rows, lax.dot_general(h_ref[rows, :], wb_scr[...],
                                               (((1,), (1,)), ((), ())), preferred_element_type=F32)
            start += n_rows
        assert start == tm

    def rope(a, rows):
        out = []
        for s in range(GROUP_DIM // LANES):
            xs = a[:, s * LANES:(s + 1) * LANES]
            t = (s % 2) * LANES
            out.append(xs * cos_ref[rows, t:t + LANES]
                       + pltpu.roll(xs, LANES // 2, 1) * sin_ref[rows, t:t + LANES])
        return jnp.concatenate(out, axis=1)

    def store_transposed(start, y):
        yt = y.T.astype(BF16)
        for s in range(y.shape[0] // LANES):
            kt_ref[start // LANES + s] = yt[:, s * LANES:(s + 1) * LANES]

    plain = (group == 2) | (group == 3) | (group == 6) | (group == 7)
    if not grid_mode:
        plain = plain | (group == 0)

    @pl.when(plain)
    def _():
        for _, rows, acc in chunks((tm,)):
            proj_ref[rows, :] = acc.astype(BF16)

    if grid_mode:
        @pl.when(group == 0)
        def _():
            for _, rows, acc in chunks():
                proj_ref[rows, :] = rope(acc, rows).astype(BF16)

    @pl.when(group == 1)
    def _():
        for rc, rows, acc in chunks():
            k = acc * K_SCALE
            store_transposed(rc, rope(k, rows) if grid_mode else k)

    @pl.when(group == 4)
    def _():
        for _, rows, acc in chunks():
            proj_ref[rows, :] = _silu(_conv3_rows(acc, cv_ref[...], period)).astype(BF16)

    @pl.when(group == 5)
    def _():
        for rc, _, acc in chunks():
            store_transposed(rc, _silu(_conv3_rows(acc, cv_ref[...], period)) * K_SCALE)


def _staging_specs(weights, n_groups, n_tiles):
    specs, shapes, block_bytes = [], [], []
    for w in weights:
        rows, cols = w.shape
        slab = BF16_ROWS
        while rows % slab or rows // slab > n_groups * n_tiles:
            slab += BF16_ROWS
        n_slabs = rows // slab
        specs.append(pl.BlockSpec(
            (slab, cols),
            lambda g, m, n_slabs=n_slabs: (jnp.minimum(g * n_tiles + m, n_slabs - 1), 0)))
        shapes.append(jax.ShapeDtypeStruct((rows, cols), BF16))
        block_bytes.append(_nbytes((slab, cols), F32) + _nbytes((slab, cols), BF16))
    return specs, shapes, block_bytes


def _mix_in_call(h2d, w_in_t, conv_qk, rope, *, seq_len, period, tm, stage=()):
    tokens = h2d.shape[0]
    grid_mode = rope is not None
    tiles_per_seq = seq_len // tm
    slabs = tm // LANES
    last = tokens // tm - 1
    n_groups = N_PROJ // GROUP_DIM
    stage_specs, stage_shapes, stage_bytes = _staging_specs(stage, n_groups, tokens // tm)

    def tile(g, m):
        return jnp.where(g % 2 == 1, last - m, m)

    def is_key(g):
        return (g == 1) | (g == 5)

    def proj_index(g, m):
        col = g - (g >= 1).astype(jnp.int32) - (g >= 5).astype(jnp.int32)
        return jnp.where(is_key(g), last, tile(g, m)), col

    def kt_index(g, m):
        row = jnp.where(g == 0, last, jnp.where(is_key(g), tile(g, m), 0))
        return row, (g >= 5).astype(jnp.int32), 0

    in_specs = [
        pl.BlockSpec((tm, D_MODEL), lambda g, m: (tile(g, m), 0)),
        pl.BlockSpec(memory_space=pl.ANY),
        pl.BlockSpec((3, GROUP_DIM), lambda g, m: (0, jnp.clip(g - 4, 0, 1))),
    ]
    args = [h2d, w_in_t, conv_qk]
    blocks = [_nbytes((tm, D_MODEL), BF16), _nbytes((tm, GROUP_DIM), BF16) * 2,
              _nbytes((8, GROUP_DIM), F32)]
    if grid_mode:
        in_specs += [pl.BlockSpec((tm, HEAD_DIM),
                                  lambda g, m: (jnp.where(g <= 1, tile(g, m) % tiles_per_seq, 0),
                                                0))] * 2
        args += list(rope)
        blocks += [_nbytes((tm, HEAD_DIM), F32)] * 2
    in_specs += stage_specs
    args += list(stage)
    blocks += stage_bytes
    scratch_defs = [((GROUP_DIM, D_MODEL), BF16), ((GROUP_DIM, D_MODEL), F32)]

    return pl.pallas_call(
        functools.partial(_mix_in_kernel, tm=tm, period=period, grid_mode=grid_mode,
                          n_staged=len(stage)),
        grid=(n_groups, tokens // tm),
        in_specs=in_specs,
        out_specs=[pl.BlockSpec((tm, GROUP_DIM), lambda g, m: proj_index(g, m)),
                   pl.BlockSpec((slabs, GROUP_DIM, LANES), lambda g, m: kt_index(g, m))]
        + stage_specs,
        out_shape=[jax.ShapeDtypeStruct((tokens, N_STORED), BF16),
                   jax.ShapeDtypeStruct((tokens // LANES, 2 * GROUP_DIM, LANES), BF16)]
        + stage_shapes,
        scratch_shapes=[pltpu.VMEM(s, d) for s, d in scratch_defs] + [pltpu.SemaphoreType.DMA(())],
        compiler_params=pltpu.CompilerParams(
            dimension_semantics=("arbitrary", "arbitrary"),
            vmem_limit_bytes=_vmem_limit(blocks, sum(_nbytes(s, d) for s, d in scratch_defs))),
        name="mix_in_grid" if grid_mode else "mix_in_seq",
    )(*args)


_NN = (((2,), (1,)), ((0,), (0,)))
_NT = (((2,), (2,)), ((0,), (0,)))


def _bdot(x, y, dims):
    return lax.dot_general(x, y, dims, preferred_element_type=F32)


def _chunk_loop(nc, body, reverse=False):
    if nc <= 2:
        for i in range(nc):
            body(nc - 1 - i if reverse else i)
    else:
        def step(i, carry):
            body(nc - 1 - i if reverse else i)
            return carry
        lax.fori_loop(0, nc, step, 0)


def _rows(c, base=0):
    start = base + c * CHUNK
    if not isinstance(start, int):
        start = pl.multiple_of(start, CHUNK)
    return pl.ds(start, CHUNK)


def _is_chunk(c, value):
    return isinstance(c, int) and c == value


class _StepChunk:
    def __init__(self, index, local):
        self.index, self.local = index, local


def _chunk_index(c):
    return c.index if isinstance(c, _StepChunk) else c


class _Chunks:
    def __init__(self, nc, bb, hp, streamed):
        self.nc, self.bb, self.hp, self.streamed = nc, bb, hp, streamed

    def rows(self, ref, bi, c, cols):
        if self.streamed:
            return ref[bi, c.local * CHUNK:(c.local + 1) * CHUNK, cols]
        return ref[_rows(c, bi * self.nc * CHUNK), cols]

    def store_rows(self, ref, bi, c, cols, value):
        if self.streamed:
            ref[bi, c.local * CHUNK:(c.local + 1) * CHUNK, cols] = value
        else:
            ref[_rows(c, bi * self.nc * CHUNK), cols] = value

    def per_chunk(self, ref, bi, c):
        return ref[bi, c.local] if self.streamed else ref[bi * self.nc + c]

    def heads(self, ref, c):
        return jnp.stack([self.rows(ref, bi, c, slice(hh * HEAD_DIM, (hh + 1) * HEAD_DIM))
                          for bi in range(self.bb) for hh in range(self.hp)])

    def keys_t(self, kt_ref, c):
        parts = [self.per_chunk(kt_ref, bi, c).reshape(self.hp, HEAD_DIM, CHUNK)
                 for bi in range(self.bb)]
        return parts[0] if self.bb == 1 else jnp.concatenate(parts, axis=0)

    def gate_row(self, g_ref, bi, c, row):
        if self.streamed:
            return g_ref[bi, c.local, pl.ds(row, 1), :]
        return g_ref[bi * self.nc + c, pl.ds(row, 1), :]


def _run_scan(nc, streamed, init_state, fwd, bwd, emit_state):
    if not streamed:
        init_state(0)
        _chunk_loop(nc, fwd)
        if emit_state is not None:
            emit_state(0)
        init_state(1)
        _chunk_loop(nc, bwd, reverse=True)
        if emit_state is not None:
            emit_state(1)
        return
    assert emit_state is None
    s = pl.program_id(0)
    steps = _stream_steps(nc)

    @pl.when(s == 0)
    def _():
        init_state(0)

    @pl.when(s == steps)
    def _():
        init_state(1)

    @pl.when(s < steps)
    def _():
        for local in range(STREAM_CHUNKS):
            fwd(_StepChunk(s * STREAM_CHUNKS + local, local))

    @pl.when(s >= steps)
    def _():
        for local in reversed(range(STREAM_CHUNKS)):
            bwd(_StepChunk((2 * steps - 1 - s) * STREAM_CHUNKS + local, local))


def _per_seq(x, bb):
    return x if bb == 1 else jnp.concatenate([x] * bb, axis=0)


def _group_block(seq_len, hp, group, bb):
    per = N_HEADS // hp
    return pl.BlockSpec((bb * seq_len, hp * HEAD_DIM), lambda b, hg: (b, group * per + hg))


def _kt_block(nc, hp, group, bb):
    per = N_HEADS // hp
    return pl.BlockSpec((bb * nc, hp * HEAD_DIM, LANES), lambda b, hg: (b, group * per + hg, 0))


def _stream_steps(nc):
    assert nc % STREAM_CHUNKS == 0
    return nc // STREAM_CHUNKS


def _step_block(s, nc):
    steps = _stream_steps(nc)
    return jnp.where(s < steps, s, 2 * steps - 1 - s)


def _stream_rows_block(bb, nc, group):
    return pl.BlockSpec((bb, STREAM_CHUNKS * CHUNK, GROUP_DIM),
                        lambda s: (0, _step_block(s, nc), group))


def _stream_out_block(bb, nc):
    steps = _stream_steps(nc)
    return pl.BlockSpec((bb, STREAM_CHUNKS * CHUNK, GROUP_DIM),
                        lambda s: (0, jnp.minimum(2 * steps - 1 - s, steps - 1), 0))


def _stream_chunk_block(bb, nc, tail, tail_index):
    return pl.BlockSpec((bb, STREAM_CHUNKS) + tail, lambda s: (0, _step_block(s, nc)) + tail_index)


def _ret_kernel(*refs, nc, bb, hp, has_init, emit_states, streamed):
    refs = list(refs)
    theta_ref, q_ref, kt_ref, v_ref, rg_ref, gn_ref = refs[:6]
    pos = 6
    s0_ref = None
    if has_init:
        s0_ref = refs[pos]
        pos += 1
    o_ref = refs[pos]
    pos += 1
    st_ref = None
    if emit_states:
        st_ref = refs[pos]
        pos += 1
    hist, s_scr, decay_scr, qdec_scr, kdec_scr, cdec_scr = refs[pos:pos + 6]
    ck = _Chunks(nc, bb, hp, streamed)
    hg = 0 if streamed else pl.program_id(1)
    hsel = pl.ds(hg * hp, hp)

    @pl.when(pl.program_id(0) == 0)
    def _():
        ii = lax.broadcasted_iota(jnp.int32, (CHUNK, CHUNK), 0)
        jj = lax.broadcasted_iota(jnp.int32, (CHUNK, CHUNK), 1)
        d = (ii - jj).astype(F32)
        p_col = lax.broadcasted_iota(jnp.int32, (CHUNK, HEAD_DIM), 0).astype(F32)
        p_row = lax.broadcasted_iota(jnp.int32, (1, LANES), 1).astype(F32)
        for hh in range(hp):
            h = hg * hp + hh
            lg_f = _log_sigmoid(jnp.full((1, LANES), theta_ref[0, h], F32))
            lg_b = _log_sigmoid(jnp.full((1, LANES), theta_ref[1, h], F32))
            decay_scr[h] = (jnp.where(d >= 0, jnp.exp(lg_f * jnp.maximum(d, 0.0)), 0.0)
                            + jnp.where(d <= 0, jnp.exp(lg_b * jnp.maximum(-d, 0.0)), 0.0))
            qdec_scr[0, h] = jnp.exp(lg_f[:, :1] * (p_col + 1.0))
            qdec_scr[1, h] = jnp.exp(lg_b[:, :1] * (CHUNK - p_col))
            kdec_scr[0, h] = jnp.exp(lg_f * (CHUNK - 1.0 - p_row))
            kdec_scr[1, h] = jnp.exp(lg_b * p_row)
            cdec_scr[0, h] = jnp.exp(lg_f * float(CHUNK))
            cdec_scr[1, h] = jnp.exp(lg_b * float(CHUNK))

    def table(ref, *lead):
        return _per_seq(ref[(*lead, hsel)], bb)

    def kv_update(c, v, direction):
        kd = (ck.keys_t(kt_ref, c).astype(F32)
              * table(kdec_scr, direction)).astype(BF16)
        s_scr[...] = s_scr[...] * table(cdec_scr, direction)[:, :, :1] + _bdot(kd, v, _NN)

    def init_state(direction):
        if has_init:
            for bi in range(bb):
                s_scr[bi * hp:(bi + 1) * hp] = s0_ref[bi, 0, direction]
        else:
            s_scr[...] = jnp.zeros((bb * hp, HEAD_DIM, HEAD_DIM), F32)

    def emit_state(direction):
        for bi in range(bb):
            st_ref[bi, 0, direction] = s_scr[bi * hp:(bi + 1) * hp]

    def fwd(c):
        hist[_chunk_index(c)] = s_scr[...].astype(BF16)
        kv_update(c, ck.heads(v_ref, c), 0)

    def bwd(c):
        q, v = ck.heads(q_ref, c), ck.heads(v_ref, c)
        kt = ck.keys_t(kt_ref, c)
        att = (_bdot(q, kt, _NN) * table(decay_scr)).astype(BF16)
        o = _bdot(att, v, _NN)
        if has_init or not _is_chunk(c, 0):
            o = o + _bdot(q, hist[_chunk_index(c)], _NN) * table(qdec_scr, 0)
        if has_init or not _is_chunk(c, nc - 1):
            o = o + _bdot(q, s_scr[...].astype(BF16), _NN) * table(qdec_scr, 1)
        y = _ln_rows(o)
        for bi in range(bb):
            for hh in range(hp):
                cs = slice(hh * HEAD_DIM, (hh + 1) * HEAD_DIM)
                gate = _silu(ck.rows(rg_ref, bi, c, cs).astype(F32))
                ck.store_rows(o_ref, bi, c, cs,
                              (y[bi * hp + hh] * gn_ref[:, cs] * gate).astype(BF16))
        kv_update(c, v, 1)

    _run_scan(nc, streamed, init_state, fwd, bwd, emit_state if emit_states else None)


def _ret_call(proj, kt, theta, gn, s0, *, batch, seq_len, bb, hp, emit_states, streamed=False):
    nc = seq_len // CHUNK
    has_init = s0 is not None
    assert N_HEADS % hp == 0 and batch % bb == 0
    if streamed:
        assert bb == batch and hp == N_HEADS and not emit_states
        proj = proj.reshape(batch, seq_len, N_STORED)
        kt = kt.reshape(batch, nc, 2 * GROUP_DIM, LANES)
        state_spec = pl.BlockSpec((bb, 1, 2, hp, HEAD_DIM, HEAD_DIM), lambda s: (0,) * 6)
        in_specs = [pl.BlockSpec(memory_space=pltpu.SMEM),
                    _stream_rows_block(bb, nc, 0),
                    _stream_chunk_block(bb, nc, (GROUP_DIM, LANES), (0, 0)),
                    _stream_rows_block(bb, nc, 1), _stream_rows_block(bb, nc, 2),
                    pl.BlockSpec((1, GROUP_DIM), lambda s: (0, 0))]
        blocks = [_nbytes((bb, STREAM_CHUNKS * CHUNK, GROUP_DIM), BF16)] * 5
        out_specs = [_stream_out_block(bb, nc)]
        out_shape = [jax.ShapeDtypeStruct((batch, seq_len, GROUP_DIM), BF16)]
        grid, semantics = (2 * _stream_steps(nc),), ("arbitrary",)
    else:
        state_spec = pl.BlockSpec((bb, 1, 2, hp, HEAD_DIM, HEAD_DIM),
                                  lambda b, hg: (b, 0, 0, hg, 0, 0))
        in_specs = [pl.BlockSpec(memory_space=pltpu.SMEM),
                    _group_block(seq_len, hp, 0, bb), _kt_block(nc, hp, 0, bb),
                    _group_block(seq_len, hp, 1, bb), _group_block(seq_len, hp, 2, bb),
                    pl.BlockSpec((1, hp * HEAD_DIM), lambda b, hg: (0, hg))]
        blocks = [_nbytes((bb * seq_len, hp * HEAD_DIM), BF16)] * 5
        out_specs = [pl.BlockSpec((bb * seq_len, hp * HEAD_DIM), lambda b, hg: (b, hg))]
        out_shape = [jax.ShapeDtypeStruct((batch * seq_len, GROUP_DIM), BF16)]
        grid, semantics = (batch // bb, N_HEADS // hp), ("arbitrary", "arbitrary")
    args = [theta, proj, kt, proj, proj, gn]
    if has_init:
        in_specs.append(state_spec)
        args.append(s0)
        blocks.append(_nbytes((bb, 2, hp, HEAD_DIM, HEAD_DIM), F32))
    if emit_states:
        out_specs.append(state_spec)
        out_shape.append(jax.ShapeDtypeStruct((batch, 1, 2, N_HEADS, HEAD_DIM, HEAD_DIM), F32))
        blocks.append(_nbytes((bb, 2, hp, HEAD_DIM, HEAD_DIM), F32))
    scratch_defs = [((nc, bb * hp, HEAD_DIM, HEAD_DIM), BF16), ((bb * hp, HEAD_DIM, HEAD_DIM), F32),
                    ((N_HEADS, CHUNK, CHUNK), F32), ((2, N_HEADS, CHUNK, HEAD_DIM), F32),
                    ((2, N_HEADS, 1, LANES), F32), ((2, N_HEADS, 1, LANES), F32)]
    outs = pl.pallas_call(
        functools.partial(_ret_kernel, nc=nc, bb=bb, hp=hp, has_init=has_init,
                          emit_states=emit_states, streamed=streamed),
        grid=grid,
        in_specs=in_specs, out_specs=out_specs, out_shape=out_shape,
        scratch_shapes=[pltpu.VMEM(s, d) for s, d in scratch_defs],
        compiler_params=pltpu.CompilerParams(
            dimension_semantics=semantics,
            vmem_limit_bytes=_vmem_limit(blocks, sum(_nbytes(s, d) for s, d in scratch_defs))),
        name="ret_grid" if has_init else "ret_seq",
    )(*args)
    if streamed:
        outs = [outs[0].reshape(batch * seq_len, GROUP_DIM)] + list(outs[1:])
    return outs


def _mlstm_kernel(*refs, nc, bb, hp, has_init, emit_states, streamed):
    refs = list(refs)
    q_ref, kt_ref, v_ref, mo_ref, g_ref, gn_ref = refs[:6]
    pos = 6
    c0_ref = n0_ref = m0_ref = None
    if has_init:
        c0_ref, n0_ref, m0_ref = refs[pos:pos + 3]
        pos += 3
    o_ref = refs[pos]
    pos += 1
    c_out = n_out = m_out = None
    if emit_states:
        c_out, n_out, m_out = refs[pos:pos + 3]
        pos += 3
    c_hist, n_hist, m_hist, c_scr, n_scr, m_scr = refs[pos:pos + 6]

    ck = _Chunks(nc, bb, hp, streamed)
    b = 0 if streamed else pl.program_id(0)
    hg = 0 if streamed else pl.program_id(1)
    ii = lax.broadcasted_iota(jnp.int32, (CHUNK, CHUNK), 0)
    jj = lax.broadcasted_iota(jnp.int32, (CHUNK, CHUNK), 1)
    eye = ii == jj
    lower = ii >= jj
    upper = ii <= jj

    ne = bb * hp

    def col(row):
        return jnp.sum(jnp.where(eye, row, 0.0), axis=2, keepdims=True)

    def gate_rows(c, kind):
        base = kind * N_HEADS + hg * hp
        return jnp.stack([ck.gate_row(g_ref, bi, c, base + hh)
                          for bi in range(bb) for hh in range(hp)])

    def direction_out(a, q, qf, v, c_b, n_row, m, b_row, i_row, mask):
        b_col = col(b_row)
        dm = jnp.where(mask, b_col - b_row + i_row, -jnp.inf)
        inter = b_col + m
        mt = jnp.maximum(inter, jnp.max(dm, axis=2, keepdims=True))
        w = jnp.exp(dm - mt)
        sp = jnp.exp(inter - mt)
        s = a * w
        num = _bdot(s.astype(BF16), v, _NN)
        den = jnp.sum(s, axis=2, keepdims=True)
        if c_b is not None:
            num = num + _bdot(q, c_b, _NN) * sp
            den = den + jnp.sum(qf * n_row, axis=2, keepdims=True) * sp
        return num / jnp.maximum(jnp.abs(den), jnp.exp(-mt))

    def state_update(c, v, b_row, i_row, last):
        m = m_scr[...][:, :, :1]
        b_last = b_row[:, :, last:last + 1]
        g = b_last - b_row + i_row
        m_new = jnp.maximum(b_last + m, jnp.max(g, axis=2, keepdims=True))
        wk = jnp.exp(g - m_new)
        sc = jnp.exp(b_last + m - m_new)
        kt = ck.keys_t(kt_ref, c)
        kw = (kt.astype(F32) * wk).astype(BF16)
        c_scr[...] = c_scr[...] * sc + _bdot(kw, v, _NN)
        wk8 = jnp.broadcast_to(wk, (ne, 8, CHUNK)).astype(BF16)
        n_scr[...] = n_scr[...] * sc + _bdot(wk8, kt, _NT)[:, :1, :]
        m_scr[...] = jnp.broadcast_to(m_new, (ne, 1, LANES))

    def init_state(direction):
        if has_init:
            for bi in range(bb):
                c_scr[bi * hp:(bi + 1) * hp] = c0_ref[bi, 0, direction]
            n_scr[...] = jnp.stack([n0_ref[bi, 0, direction, pl.ds(hg * hp + hh, 1), :]
                                    for bi in range(bb) for hh in range(hp)])
            m_scr[...] = jnp.stack([
                jnp.full((1, LANES),
                         m0_ref[((b * bb + bi) * 2 + direction) * N_HEADS + hg * hp + hh], F32)
                for bi in range(bb) for hh in range(hp)])
        else:
            c_scr[...] = jnp.zeros((ne, HEAD_DIM, HEAD_DIM), F32)
            n_scr[...] = jnp.zeros((ne, 1, HEAD_DIM), F32)
            m_scr[...] = jnp.zeros((ne, 1, LANES), F32)

    def emit_state(direction):
        for bi in range(bb):
            c_out[bi, 0, direction] = c_scr[bi * hp:(bi + 1) * hp]
            for hh in range(hp):
                n_out[bi, 0, direction, pl.ds(hh, 1), :] = n_scr[bi * hp + hh]
                m_out[bi, pl.ds(direction * N_HEADS + hh, 1), :] = m_scr[bi * hp + hh]

    def fwd(c):
        ci = _chunk_index(c)
        c_hist[ci] = c_scr[...].astype(BF16)
        n_hist[ci] = n_scr[...]
        m_hist[ci] = m_scr[...]
        state_update(c, ck.heads(v_ref, c), gate_rows(c, 1), gate_rows(c, 0), CHUNK - 1)

    def bwd(c):
        i_f, b_f, i_b, b_b = (gate_rows(c, kind) for kind in (0, 1, 2, 3))
        q, v = ck.heads(q_ref, c), ck.heads(v_ref, c)
        qf = q.astype(F32)
        a = _bdot(q, ck.keys_t(kt_ref, c), _NN)
        zero_f = not has_init and _is_chunk(c, 0)
        zero_b = not has_init and _is_chunk(c, nc - 1)
        ci = _chunk_index(c)
        h_f = direction_out(a, q, qf, v, None if zero_f else c_hist[ci], n_hist[ci],
                            m_hist[ci][:, :, :1], b_f, i_f, lower)
        h_b = direction_out(a, q, qf, v, None if zero_b else c_scr[...].astype(BF16),
                            n_scr[...], m_scr[...][:, :, :1], b_b, i_b, upper)
        y = _ln_rows(h_f + h_b)
        for bi in range(bb):
            for hh in range(hp):
                cs = slice(hh * HEAD_DIM, (hh + 1) * HEAD_DIM)
                gate = jax.nn.sigmoid(ck.rows(mo_ref, bi, c, cs).astype(F32))
                ck.store_rows(o_ref, bi, c, cs,
                              (gate * (y[bi * hp + hh] * gn_ref[:, cs])).astype(BF16))
        state_update(c, v, b_b, i_b, 0)

    _run_scan(nc, streamed, init_state, fwd, bwd, emit_state if emit_states else None)


def _mlstm_call(proj, kt, gates, gn, init, *, batch, seq_len, bb, hp, emit_states,
                streamed=False):
    nc = seq_len // CHUNK
    has_init = init is not None
    assert N_HEADS % hp == 0 and (hp == N_HEADS or not emit_states) and batch % bb == 0
    if streamed:
        assert bb == batch and hp == N_HEADS and not emit_states
        proj = proj.reshape(batch, seq_len, N_STORED)
        kt = kt.reshape(batch, nc, 2 * GROUP_DIM, LANES)
        gates = gates.reshape(batch, nc, N_GATES, LANES)
        c_spec = pl.BlockSpec((bb, 1, 2, hp, HEAD_DIM, HEAD_DIM), lambda s: (0,) * 6)
        n_spec = pl.BlockSpec((bb, 1, 2, N_HEADS, HEAD_DIM), lambda s: (0,) * 5)
        in_specs = [_stream_rows_block(bb, nc, 3),
                    _stream_chunk_block(bb, nc, (GROUP_DIM, LANES), (1, 0)),
                    _stream_rows_block(bb, nc, 4), _stream_rows_block(bb, nc, 5),
                    _stream_chunk_block(bb, nc, (N_GATES, LANES), (0, 0)),
                    pl.BlockSpec((1, GROUP_DIM), lambda s: (0, 0))]
        blocks = ([_nbytes((bb, STREAM_CHUNKS * CHUNK, GROUP_DIM), BF16)] * 5
                  + [_nbytes((bb, STREAM_CHUNKS * N_GATES, LANES), F32)])
        out_specs = [_stream_out_block(bb, nc)]
        out_shape = [jax.ShapeDtypeStruct((batch, seq_len, GROUP_DIM), BF16)]
        grid, semantics = (2 * _stream_steps(nc),), ("arbitrary",)
    else:
        c_spec = pl.BlockSpec((bb, 1, 2, hp, HEAD_DIM, HEAD_DIM),
                              lambda b, hg: (b, 0, 0, hg, 0, 0))
        n_spec = pl.BlockSpec((bb, 1, 2, N_HEADS, HEAD_DIM), lambda b, hg: (b, 0, 0, 0, 0))
        in_specs = [_group_block(seq_len, hp, 3, bb), _kt_block(nc, hp, 1, bb),
                    _group_block(seq_len, hp, 4, bb), _group_block(seq_len, hp, 5, bb),
                    pl.BlockSpec((bb * nc, N_GATES, LANES), lambda b, hg: (b, 0, 0)),
                    pl.BlockSpec((1, hp * HEAD_DIM), lambda b, hg: (0, hg))]
        blocks = ([_nbytes((bb * seq_len, hp * HEAD_DIM), BF16)] * 5
                  + [_nbytes((bb * nc, N_GATES, LANES), F32)])
        out_specs = [pl.BlockSpec((bb * seq_len, hp * HEAD_DIM), lambda b, hg: (b, hg))]
        out_shape = [jax.ShapeDtypeStruct((batch * seq_len, GROUP_DIM), BF16)]
        grid, semantics = (batch // bb, N_HEADS // hp), ("arbitrary", "arbitrary")
    args = [proj, kt, proj, proj, gates, gn]
    if has_init:
        in_specs += [c_spec, n_spec, pl.BlockSpec(memory_space=pltpu.SMEM)]
        args += list(init)
        blocks.append(_nbytes((bb, 2, hp, HEAD_DIM, HEAD_DIM), F32))
    if emit_states:
        out_specs += [c_spec, n_spec, pl.BlockSpec((bb, 2 * N_HEADS, LANES), lambda b, hg: (b, 0, 0))]
        out_shape += [jax.ShapeDtypeStruct((batch, 1, 2, N_HEADS, HEAD_DIM, HEAD_DIM), F32),
                      jax.ShapeDtypeStruct((batch, 1, 2, N_HEADS, HEAD_DIM), F32),
                      jax.ShapeDtypeStruct((batch, 2 * N_HEADS, LANES), F32)]
        blocks.append(_nbytes((bb, 2, hp, HEAD_DIM, HEAD_DIM), F32))
    ne = bb * hp
    scratch_defs = [((nc, ne, HEAD_DIM, HEAD_DIM), BF16), ((nc, ne, 1, HEAD_DIM), F32),
                    ((nc, ne, 1, LANES), F32), ((ne, HEAD_DIM, HEAD_DIM), F32),
                    ((ne, 1, HEAD_DIM), F32), ((ne, 1, LANES), F32)]
    scratch_bytes = sum(_nbytes(s, d) for s, d in scratch_defs) + 16 * nc * ne * HEAD_DIM * 4
    outs = pl.pallas_call(
        functools.partial(_mlstm_kernel, nc=nc, bb=bb, hp=hp, has_init=has_init,
                          emit_states=emit_states, streamed=streamed),
        grid=grid,
        in_specs=in_specs, out_specs=out_specs, out_shape=out_shape,
        scratch_shapes=[pltpu.VMEM(s, d) for s, d in scratch_defs],
        compiler_params=pltpu.CompilerParams(
            dimension_semantics=semantics,
            vmem_limit_bytes=_vmem_limit(blocks, scratch_bytes)),
        name="mlstm_grid" if has_init else "mlstm_seq",
    )(*args)
    if streamed:
        outs = [outs[0].reshape(batch * seq_len, GROUP_DIM)] + list(outs[1:])
    return outs


def _mix_out_kernel(mr_ref, mm_ref, w_ref, x_ref, g1_ref, lg_ref, lb_ref, o_ref):
    for rc in range(o_ref.shape[0] // MIX_ROW_CHUNK):
        rows = slice(rc * MIX_ROW_CHUNK, (rc + 1) * MIX_ROW_CHUNK)
        mix = (jnp.dot(mr_ref[rows, :], w_ref[0:GROUP_DIM, :], preferred_element_type=F32)
               + jnp.dot(mm_ref[rows, :], w_ref[GROUP_DIM:, :], preferred_element_type=F32))
        o_ref[rows, :] = (_ln_rows(ALPHA * x_ref[rows, :] + g1_ref[0] * mix) * lg_ref[...]
                          + lb_ref[...])


def _mix_out_call(mix_r, mix_m, w_out_b, x2d, mod3, mod_row, ln_g, ln_b, *, tm):
    tokens = x2d.shape[0]
    blocks = [_nbytes((tm, GROUP_DIM), BF16)] * 2 + [_nbytes((D_MODEL, D_MODEL), BF16)] \
        + [_nbytes((tm, D_MODEL), F32)] * 2
    return pl.pallas_call(
        _mix_out_kernel,
        grid=(tokens // tm,),
        in_specs=[pl.BlockSpec((tm, GROUP_DIM), lambda m: (m, 0)),
                  pl.BlockSpec((tm, GROUP_DIM), lambda m: (m, 0)),
                  pl.BlockSpec((D_MODEL, D_MODEL), lambda m: (0, 0)),
                  pl.BlockSpec((tm, D_MODEL), lambda m: (m, 0)),
                  pl.BlockSpec((1, 1, D_MODEL), lambda m: (mod_row(m) * 6 + 2, 0, 0)),
                  pl.BlockSpec((1, D_MODEL), lambda m: (0, 0)),
                  pl.BlockSpec((1, D_MODEL), lambda m: (0, 0))],
        out_specs=pl.BlockSpec((tm, D_MODEL), lambda m: (m, 0)),
        out_shape=jax.ShapeDtypeStruct((tokens, D_MODEL), F32),
        compiler_params=pltpu.CompilerParams(
            dimension_semantics=("parallel",), vmem_limit_bytes=_vmem_limit(blocks)),
        name="mix_out",
    )(mix_r, mix_m, w_out_b, x2d, mod3, ln_g, ln_b)


def _ffn_kernel(x_ref, sh_ref, sc_ref, g2_ref, wu_ref, wg_ref, cv_ref, wd_ref, lg_ref, lb_ref,
                o_ref, h_scr, *, period):
    j = pl.program_id(1)

    def row_chunks():
        for rc in range(o_ref.shape[0] // MIX_ROW_CHUNK):
            yield slice(rc * MIX_ROW_CHUNK, (rc + 1) * MIX_ROW_CHUNK)

    @pl.when(j == 0)
    def _():
        for rows in row_chunks():
            h = _ln_rows(x_ref[rows, :]) * (1.0 + sc_ref[0]) + sh_ref[0]
            h_scr[rows, :] = h.astype(BF16)
        o_ref[...] = jnp.zeros_like(o_ref)

    lane = lax.broadcasted_iota(jnp.int32, (1, FF_TN), 1)
    duplicate = (j == FF_NJ - 1) & (lane < FF_OVERLAP)
    for rows in row_chunks():
        hb = h_scr[rows, :]
        u = jnp.dot(hb, wu_ref[...], preferred_element_type=F32)
        g = jnp.dot(hb, wg_ref[...], preferred_element_type=F32)
        hid = _silu(_conv3_rows(u, cv_ref[0], period)) * g
        hid = jnp.where(duplicate, 0.0, hid).astype(BF16)
        o_ref[rows, :] += jnp.dot(hid, wd_ref[...], preferred_element_type=F32)

    @pl.when(j == FF_NJ - 1)
    def _():
        for rows in row_chunks():
            y = ALPHA * x_ref[rows, :] + g2_ref[0] * o_ref[rows, :]
            o_ref[rows, :] = _ln_rows(y) * lg_ref[...] + lb_ref[...]


def _ff_offset(j):
    return pl.multiple_of(jnp.minimum(j * FF_TN, D_FF - FF_TN), LANES)


def _ffn_call(x1, mod3, mod_row, w_up_b, conv_tiles, w_down_b, ln_g, ln_b, *, period, tm):
    tokens = x1.shape[0]
    up_tile = (pl.Element(D_MODEL), pl.Element(FF_TN))
    blocks = [_nbytes((tm, D_MODEL), F32)] * 2 + [_nbytes((D_MODEL, FF_TN), BF16)] * 3 \
        + [_nbytes((8, FF_TN), F32)]
    scratch_bytes = _nbytes((tm, D_MODEL), BF16)
    return pl.pallas_call(
        functools.partial(_ffn_kernel, period=period),
        grid=(tokens // tm, FF_NJ),
        in_specs=[pl.BlockSpec((tm, D_MODEL), lambda m, j: (m, 0)),
                  pl.BlockSpec((1, 1, D_MODEL), lambda m, j: (mod_row(m) * 6 + 3, 0, 0)),
                  pl.BlockSpec((1, 1, D_MODEL), lambda m, j: (mod_row(m) * 6 + 4, 0, 0)),
                  pl.BlockSpec((1, 1, D_MODEL), lambda m, j: (mod_row(m) * 6 + 5, 0, 0)),
                  pl.BlockSpec(up_tile, lambda m, j: (0, _ff_offset(j))),
                  pl.BlockSpec(up_tile, lambda m, j: (0, pl.multiple_of(D_FF + _ff_offset(j), LANES))),
                  pl.BlockSpec((1, 3, FF_TN), lambda m, j: (j, 0, 0)),
                  pl.BlockSpec((pl.Element(FF_TN), pl.Element(D_MODEL)),
                               lambda m, j: (_ff_offset(j), 0)),
                  pl.BlockSpec((1, D_MODEL), lambda m, j: (0, 0)),
                  pl.BlockSpec((1, D_MODEL), lambda m, j: (0, 0))],
        out_specs=pl.BlockSpec((tm, D_MODEL), lambda m, j: (m, 0)),
        out_shape=jax.ShapeDtypeStruct((tokens, D_MODEL), F32),
        scratch_shapes=[pltpu.VMEM((tm, D_MODEL), BF16)],
        compiler_params=pltpu.CompilerParams(
            dimension_semantics=("parallel", "arbitrary"),
            vmem_limit_bytes=_vmem_limit(blocks, scratch_bytes)),
        name="ffn",
    )(x1, mod3, mod3, mod3, w_up_b, w_up_b, conv_tiles, w_down_b, ln_g, ln_b)


def _rope_tables(seq_len):
    quarter = HEAD_DIM // 4
    t = jnp.arange(seq_len)
    row = (t // GRID_W).astype(F32)
    col = (t % GRID_W).astype(F32)
    inv = ROPE_BASE ** (-jnp.arange(quarter, dtype=F32) / quarter)
    ang_r, ang_c = row[:, None] * inv, col[:, None] * inv
    cos_t = jnp.concatenate([jnp.cos(ang_r)] * 2 + [jnp.cos(ang_c)] * 2, axis=1)
    sin_t = jnp.concatenate([-jnp.sin(ang_r), jnp.sin(ang_r), -jnp.sin(ang_c), jnp.sin(ang_c)], axis=1)
    return cos_t, sin_t


def kernel(x_prompt, x_sample, state_ret, state_mlstm_C, state_mlstm_n, state_mlstm_m, c, c_ctx,
           w_mod, b_mod, w_in, b_gate, conv_qk, ret_theta, gn_ret, gn_mlstm, w_out,
           ln1_g, ln1_b, w_up, conv_ff, w_down, ln2_g, ln2_b):
    bp, lp, _ = x_prompt.shape
    bs, ls, _ = x_sample.shape
    layer = 0
    tm_mix, tm, tm_ffn = 1024, 512, 1024

    cvec = jnp.concatenate([c_ctx[None], c, jnp.zeros((8 - 1 - bs, D_MODEL), F32)], axis=0)
    mod = _mod_call(cvec, w_mod[layer], b_mod[layer][None])
    mod3 = mod.reshape(8 * 6, 1, D_MODEL)

    def row_prompt(tile):
        return lambda m: 0

    def row_sample(tile):
        return lambda m: 1 + (m * tile) // ls

    w_l = w_in[layer].T
    w_gate = w_l[N_PROJ:]
    b_gate_col = b_gate[layer].reshape(N_GATES, 1)
    ff_offsets = [min(j * FF_TN, D_FF - FF_TN) for j in range(FF_NJ)]
    conv_tiles = jnp.stack([conv_ff[layer][:, o:o + FF_TN] for o in ff_offsets])
    gn_r = gn_ret[layer].reshape(1, GROUP_DIM)
    gn_m = gn_mlstm[layer].reshape(1, GROUP_DIM)
    theta = ret_theta[layer]
    ln1 = (ln1_g[layer][None], ln1_b[layer][None])
    ln2 = (ln2_g[layer][None], ln2_b[layer][None])

    def mixer(x, mod_row, rope, period, bb, hp, ret_init, mlstm_init, emit_states, stage=(),
              streamed=False):
        batch, seq_len, _ = x.shape
        x2d = x.reshape(batch * seq_len, D_MODEL)
        h2d, gates = _ln_mod_call(x2d, mod3, mod_row(tm_mix), w_gate, b_gate_col, tm=tm_mix)
        proj, kt, *staged = _mix_in_call(h2d, w_l, conv_qk[layer], rope, seq_len=seq_len,
                                         period=period, tm=tm_mix, stage=stage)
        ret = _ret_call(proj, kt, theta, gn_r, ret_init, batch=batch, seq_len=seq_len, bb=bb,
                        hp=hp, emit_states=emit_states, streamed=streamed)
        mls = _mlstm_call(proj, kt, gates, gn_m, mlstm_init, batch=batch, seq_len=seq_len, bb=bb,
                          hp=hp, emit_states=emit_states, streamed=streamed)
        return x2d, ret, mls, staged

    def channel_mix(x2d, shape, ret, mls, mod_row, period, weights):
        w_up_b, w_down_b, w_out_b = weights
        x1 = _mix_out_call(ret[0], mls[0], w_out_b, x2d, mod3, mod_row(tm), *ln1, tm=tm)
        y = _ffn_call(x1, mod3, mod_row(tm_ffn), w_up_b, conv_tiles, w_down_b, *ln2,
                      period=period, tm=tm_ffn)
        return y.reshape(shape)

    xp, ret_p, mls_p, weights_b = mixer(x_prompt, row_prompt, None, lp, 4, N_HEADS, None, None,
                                        True, stage=(w_up[layer], w_down[layer], w_out[layer]))
    m0 = state_mlstm_m.reshape(-1)
    xs, ret_s, mls_s, _ = mixer(x_sample, row_sample, _rope_tables(ls), GRID_W, bs, N_HEADS,
                                state_ret, (state_mlstm_C, state_mlstm_n, m0), False,
                                streamed=True)
    y_prompt = channel_mix(xp, x_prompt.shape, ret_p, mls_p, row_prompt, lp, weights_b)
    y_sample = channel_mix(xs, x_sample.shape, ret_s, mls_s, row_sample, GRID_W, weights_b)
    ret_states, mls_states = ret_p[1:], mls_p[1:]

    new_ret = ret_states[0]
    new_c, new_n, new_m = mls_states
    new_m = new_m[:, :, 0].reshape(bp, 1, 2, N_HEADS)
    return (y_prompt, y_sample, new_ret, new_c, new_n, new_m)
```

```python
import functools

import jax
import jax.numpy as jnp
from jax import lax
from jax.experimental import pallas as pl
from jax.experimental.pallas import tpu as pltpu

F32 = jnp.float32
BF16 = jnp.bfloat16

D_MODEL = 2048
GRID_W = 64
HEAD_DIM = 256
N_HEADS = 4
GROUP_DIM = N_HEADS * HEAD_DIM
N_PROJ = 8 * GROUP_DIM
N_GATES = 4 * N_HEADS
D_FF = 5504
CHUNK = 128
ROPE_BASE = 10000.0
ALPHA = 2.0 ** 0.25
LN_EPS = 1e-6
K_SCALE = HEAD_DIM ** -0.5

LANES = 128
BF16_ROWS = 16
V7X_VMEM_LIMIT_CAP = 60 * 1024 * 1024
VMEM_TEMP_ALLOWANCE = 16 * 1024 * 1024

MIX_ROW_CHUNK = 512
N_STORED = 6 * GROUP_DIM
STREAM_CHUNKS = 4
FF_TN = 512
FF_NJ = -(-D_FF // FF_TN)
FF_OVERLAP = FF_NJ * FF_TN - D_FF


def _vmem_limit(block_bytes, scratch_bytes=0):
    est = 2 * sum(block_bytes) + scratch_bytes + VMEM_TEMP_ALLOWANCE
    return int(min(V7X_VMEM_LIMIT_CAP, est))


def _nbytes(shape, dtype):
    n = 1
    for s in shape:
        n *= s
    return n * jnp.dtype(dtype).itemsize


def _ln_rows(x):
    mu = jnp.mean(x, axis=-1, keepdims=True)
    xc = x - mu
    var = jnp.mean(xc * xc, axis=-1, keepdims=True)
    return xc * lax.rsqrt(var + LN_EPS)


def _silu(x):
    return x * jax.nn.sigmoid(x)


def _log_sigmoid(x):
    return jnp.minimum(x, 0.0) - jnp.log1p(jnp.exp(-jnp.abs(x)))


def _conv3_rows(u, w, period):
    rows = u.shape[0]
    t = lax.broadcasted_iota(jnp.int32, (rows, 1), 0) & (period - 1)
    prev = jnp.where(t == 0, 0.0, pltpu.roll(u, 1, 0))
    nxt = jnp.where(t == period - 1, 0.0, pltpu.roll(u, rows - 1, 0))
    return prev * w[0:1] + u * w[1:2] + nxt * w[2:3]


def _cumsum_lanes(x, lane, reverse):
    s = 1
    while s < LANES:
        if reverse:
            x = x + jnp.where(lane < LANES - s, pltpu.roll(x, LANES - s, 1), 0.0)
        else:
            x = x + jnp.where(lane >= s, pltpu.roll(x, s, 1), 0.0)
        s *= 2
    return x


def _mod_kernel(c_ref, w_ref, b_ref, o_ref):
    a = _silu(c_ref[...]).astype(BF16)
    o_ref[...] = jnp.dot(a, w_ref[...].astype(BF16), preferred_element_type=F32) + b_ref[...]


def _mod_call(cvec, w_mod, b_mod):
    rows, n = cvec.shape[0], w_mod.shape[1]
    tn = 1024
    blocks = [_nbytes((rows, D_MODEL), F32), _nbytes((D_MODEL, tn), F32),
              _nbytes((8, tn), F32), _nbytes((rows, tn), F32)]
    return pl.pallas_call(
        _mod_kernel,
        grid=(n // tn,),
        in_specs=[pl.BlockSpec((rows, D_MODEL), lambda j: (0, 0)),
                  pl.BlockSpec((D_MODEL, tn), lambda j: (0, j)),
                  pl.BlockSpec((1, tn), lambda j: (0, j))],
        out_specs=pl.BlockSpec((rows, tn), lambda j: (0, j)),
        out_shape=jax.ShapeDtypeStruct((rows, n), F32),
        compiler_params=pltpu.CompilerParams(
            dimension_semantics=("arbitrary",), vmem_limit_bytes=_vmem_limit(blocks)),
        name="mod",
    )(cvec, w_mod, b_mod)


def _ln_mod_kernel(x_ref, sh_ref, sc_ref, wg_ref, bg_ref, h_ref, gates_ref, *, tm):
    h = _ln_rows(x_ref[...]) * (1.0 + sc_ref[0]) + sh_ref[0]
    hb = h.astype(BF16)
    h_ref[...] = hb
    g = lax.dot_general(wg_ref[...].astype(BF16), hb, (((1,), (1,)), ((), ())),
                        preferred_element_type=F32) + bg_ref[...]
    row = lax.broadcasted_iota(jnp.int32, (N_GATES, LANES), 0)
    lane = lax.broadcasted_iota(jnp.int32, (N_GATES, LANES), 1)
    kind = lax.shift_right_logical(row, 2)
    for s in range(tm // LANES):
        gs = g[:, s * LANES:(s + 1) * LANES]
        ls = _log_sigmoid(gs)
        gates_ref[s] = jnp.where(kind == 1, _cumsum_lanes(ls, lane, False),
                                 jnp.where(kind == 3, _cumsum_lanes(ls, lane, True), gs))


def _ln_mod_call(x2d, mod3, mod_row, w_gate, b_gate_col, *, tm):
    tokens = x2d.shape[0]
    slabs = tm // LANES
    blocks = [_nbytes((tm, D_MODEL), F32), _nbytes((tm, D_MODEL), BF16),
              _nbytes((N_GATES, D_MODEL), F32), _nbytes((slabs, N_GATES, LANES), F32)]
    return pl.pallas_call(
        functools.partial(_ln_mod_kernel, tm=tm),
        grid=(tokens // tm,),
        in_specs=[pl.BlockSpec((tm, D_MODEL), lambda m: (m, 0)),
                  pl.BlockSpec((1, 1, D_MODEL), lambda m: (mod_row(m) * 6 + 0, 0, 0)),
                  pl.BlockSpec((1, 1, D_MODEL), lambda m: (mod_row(m) * 6 + 1, 0, 0)),
                  pl.BlockSpec((N_GATES, D_MODEL), lambda m: (0, 0)),
                  pl.BlockSpec((N_GATES, 1), lambda m: (0, 0))],
        out_specs=[pl.BlockSpec((tm, D_MODEL), lambda m: (m, 0)),
                   pl.BlockSpec((slabs, N_GATES, LANES), lambda m: (m, 0, 0))],
        out_shape=[jax.ShapeDtypeStruct((tokens, D_MODEL), BF16),
                   jax.ShapeDtypeStruct((tokens // LANES, N_GATES, LANES), F32)],
        compiler_params=pltpu.CompilerParams(
            dimension_semantics=("parallel",), vmem_limit_bytes=_vmem_limit(blocks)),
        name="ln_mod",
    )(x2d, mod3, mod3, w_gate, b_gate_col)


def _mix_in_kernel(*refs, tm, period, grid_mode, n_staged):
    refs = list(refs)
    h_ref, w_ref, cv_ref = refs[:3]
    pos = 3
    if grid_mode:
        cos_ref, sin_ref = refs[pos:pos + 2]
        pos += 2
    staged_in = refs[pos:pos + n_staged]
    pos += n_staged
    proj_ref, kt_ref = refs[pos:pos + 2]
    pos += 2
    staged_out = refs[pos:pos + n_staged]
    wb_scr, wf_scr, w_sem = refs[pos + n_staged:pos + n_staged + 3]
    group = pl.program_id(0)
    first_tile = pl.program_id(1) == 0

    def weight_copy(g):
        return pltpu.make_async_copy(w_ref.at[pl.ds(g * GROUP_DIM, GROUP_DIM), :], wf_scr, w_sem)

    @pl.when(first_tile & (group == 0))
    def _():
        weight_copy(group).start()

    @pl.when(first_tile)
    def _():
        weight_copy(group).wait()
        wb_scr[...] = wf_scr[...].astype(BF16)

    @pl.when(first_tile & (group + 1 < pl.num_programs(0)))
    def _():
        weight_copy(group + 1).start()

    def chunks(sizes=(tm // 2, tm // 2)):
        for src, dst in zip(staged_in, staged_out):
            dst[...] = src[...].astype(BF16)
        start = 0
        for n_rows in sizes:
            rows = slice(start, start + n_rows)
            yield start, rows, lax.dot_general(h_ref[rows, :], wb_scr[...],
                                               (((1,), (1,)), ((), ())), preferred_element_type=F32)
            start += n_rows
        assert start == tm

    def rope(a, rows):
        out = []
        for s in range(GROUP_DIM // LANES):
            xs = a[:, s * LANES:(s + 1) * LANES]
            t = (s % 2) * LANES
            out.append(xs * cos_ref[rows, t:t + LANES]
                       + pltpu.roll(xs, LANES // 2, 1) * sin_ref[rows, t:t + LANES])
        return jnp.concatenate(out, axis=1)

    def store_transposed(start, y):
        yt = y.T.astype(BF16)
        for s in range(y.shape[0] // LANES):
            kt_ref[start // LANES + s] = yt[:, s * LANES:(s + 1) * LANES]

    plain = (group == 2) | (group == 3) | (group == 6) | (group == 7)
    if not grid_mode:
        plain = plain | (group == 0)

    @pl.when(plain)
    def _():
        for _, rows, acc in chunks((tm,)):
            proj_ref[rows, :] = acc.astype(BF16)

    if grid_mode:
        @pl.when(group == 0)
        def _():
            for _, rows, acc in chunks():
                proj_ref[rows, :] = rope(acc, rows).astype(BF16)

    @pl.when(group == 1)
    def _():
        for rc, rows, acc in chunks():
            k = acc * K_SCALE
            store_transposed(rc, rope(k, rows) if grid_mode else k)

    @pl.when(group == 4)
    def _():
        for _, rows, acc in chunks():
            proj_ref[rows, :] = _silu(_conv3_rows(acc, cv_ref[...], period)).astype(BF16)

    @pl.when(group == 5)
    def _():
        for rc, _, acc in chunks():
            store_transposed(rc, _silu(_conv3_rows(acc, cv_ref[...], period)) * K_SCALE)


def _staging_specs(weights, n_groups, n_tiles):
    specs, shapes, block_bytes = [], [], []
    for w in weights:
        rows, cols = w.shape
        slab = BF16_ROWS
        while rows % slab or rows // slab > n_groups * n_tiles:
            slab += BF16_ROWS
        n_slabs = rows // slab
        specs.append(pl.BlockSpec(
            (slab, cols),
            lambda g, m, n_slabs=n_slabs: (jnp.minimum(g * n_tiles + m, n_slabs - 1), 0)))
        shapes.append(jax.ShapeDtypeStruct((rows, cols), BF16))
        block_bytes.append(_nbytes((slab, cols), F32) + _nbytes((slab, cols), BF16))
    return specs, shapes, block_bytes


def _mix_in_call(h2d, w_in_t, conv_qk, rope, *, seq_len, period, tm, stage=()):
    tokens = h2d.shape[0]
    grid_mode = rope is not None
    tiles_per_seq = seq_len // tm
    slabs = tm // LANES
    last = tokens // tm - 1
    n_groups = N_PROJ // GROUP_DIM
    stage_specs, stage_shapes, stage_bytes = _staging_specs(stage, n_groups, tokens // tm)

    def tile(g, m):
        return jnp.where(g % 2 == 1, last - m, m)

    def is_key(g):
        return (g == 1) | (g == 5)

    def proj_index(g, m):
        col = g - (g >= 1).astype(jnp.int32) - (g >= 5).astype(jnp.int32)
        return jnp.where(is_key(g), last, tile(g, m)), col

    def kt_index(g, m):
        row = jnp.where(g == 0, last, jnp.where(is_key(g), tile(g, m), 0))
        return row, (g >= 5).astype(jnp.int32), 0

    in_specs = [
        pl.BlockSpec((tm, D_MODEL), lambda g, m: (tile(g, m), 0)),
        pl.BlockSpec(memory_space=pl.ANY),
        pl.BlockSpec((3, GROUP_DIM), lambda g, m: (0, jnp.clip(g - 4, 0, 1))),
    ]
    args = [h2d, w_in_t, conv_qk]
    blocks = [_nbytes((tm, D_MODEL), BF16), _nbytes((tm, GROUP_DIM), BF16) * 2,
              _nbytes((8, GROUP_DIM), F32)]
    if grid_mode:
        in_specs += [pl.BlockSpec((tm, HEAD_DIM),
                                  lambda g, m: (jnp.where(g <= 1, tile(g, m) % tiles_per_seq, 0),
                                                0))] * 2
        args += list(rope)
        blocks += [_nbytes((tm, HEAD_DIM), F32)] * 2
    in_specs += stage_specs
    args += list(stage)
    blocks += stage_bytes
    scratch_defs = [((GROUP_DIM, D_MODEL), BF16), ((GROUP_DIM, D_MODEL), F32)]

    return pl.pallas_call(
        functools.partial(_mix_in_kernel, tm=tm, period=period, grid_mode=grid_mode,
                          n_staged=len(stage)),
        grid=(n_groups, tokens // tm),
        in_specs=in_specs,
        out_specs=[pl.BlockSpec((tm, GROUP_DIM), lambda g, m: proj_index(g, m)),
                   pl.BlockSpec((slabs, GROUP_DIM, LANES), lambda g, m: kt_index(g, m))]
        + stage_specs,
        out_shape=[jax.ShapeDtypeStruct((tokens, N_STORED), BF16),
                   jax.ShapeDtypeStruct((tokens // LANES, 2 * GROUP_DIM, LANES), BF16)]
        + stage_shapes,
        scratch_shapes=[pltpu.VMEM(s, d) for s, d in scratch_defs] + [pltpu.SemaphoreType.DMA(())],
        compiler_params=pltpu.CompilerParams(
            dimension_semantics=("arbitrary", "arbitrary"),
            vmem_limit_bytes=_vmem_limit(blocks, sum(_nbytes(s, d) for s, d in scratch_defs))),
        name="mix_in_grid" if grid_mode else "mix_in_seq",
    )(*args)


_NN = (((2,), (1,)), ((0,), (0,)))
_NT = (((2,), (2,)), ((0,), (0,)))


def _bdot(x, y, dims):
    return lax.dot_general(x, y, dims, preferred_element_type=F32)


def _chunk_loop(nc, body, reverse=False):
    if nc <= 2:
        for i in range(nc):
            body(nc - 1 - i if reverse else i)
    else:
        def step(i, carry):
            body(nc - 1 - i if reverse else i)
            return carry
        lax.fori_loop(0, nc, step, 0)


def _rows(c, base=0):
    start = base + c * CHUNK
    if not isinstance(start, int):
        start = pl.multiple_of(start, CHUNK)
    return pl.ds(start, CHUNK)


def _is_chunk(c, value):
    return isinstance(c, int) and c == value


class _StepChunk:
    def __init__(self, index, local):
        self.index, self.local = index, local


def _chunk_index(c):
    return c.index if isinstance(c, _StepChunk) else c


class _Chunks:
    def __init__(self, nc, bb, hp, streamed):
        self.nc, self.bb, self.hp, self.streamed = nc, bb, hp, streamed

    def rows(self, ref, bi, c, cols):
        if self.streamed:
            return ref[bi, c.local * CHUNK:(c.local + 1) * CHUNK, cols]
        return ref[_rows(c, bi * self.nc * CHUNK), cols]

    def store_rows(self, ref, bi, c, cols, value):
        if self.streamed:
            ref[bi, c.local * CHUNK:(c.local + 1) * CHUNK, cols] = value
        else:
            ref[_rows(c, bi * self.nc * CHUNK), cols] = value

    def per_chunk(self, ref, bi, c):
        return ref[bi, c.local] if self.streamed else ref[bi * self.nc + c]

    def heads(self, ref, c):
        return jnp.stack([self.rows(ref, bi, c, slice(hh * HEAD_DIM, (hh + 1) * HEAD_DIM))
                          for bi in range(self.bb) for hh in range(self.hp)])

    def keys_t(self, kt_ref, c):
        parts = [self.per_chunk(kt_ref, bi, c).reshape(self.hp, HEAD_DIM, CHUNK)
                 for bi in range(self.bb)]
        return parts[0] if self.bb == 1 else jnp.concatenate(parts, axis=0)

    def gate_row(self, g_ref, bi, c, row):
        if self.streamed:
            return g_ref[bi, c.local, pl.ds(row, 1), :]
        return g_ref[bi * self.nc + c, pl.ds(row, 1), :]


def _run_scan(nc, streamed, init_state, fwd, bwd, emit_state):
    if not streamed:
        init_state(0)
        _chunk_loop(nc, fwd)
        if emit_state is not None:
            emit_state(0)
        init_state(1)
        _chunk_loop(nc, bwd, reverse=True)
        if emit_state is not None:
            emit_state(1)
        return
    assert emit_state is None
    s = pl.program_id(0)
    steps = _stream_steps(nc)

    @pl.when(s == 0)
    def _():
        init_state(0)

    @pl.when(s == steps)
    def _():
        init_state(1)

    @pl.when(s < steps)
    def _():
        for local in range(STREAM_CHUNKS):
            fwd(_StepChunk(s * STREAM_CHUNKS + local, local))

    @pl.when(s >= steps)
    def _():
        for local in reversed(range(STREAM_CHUNKS)):
            bwd(_StepChunk((2 * steps - 1 - s) * STREAM_CHUNKS + local, local))


def _per_seq(x, bb):
    return x if bb == 1 else jnp.concatenate([x] * bb, axis=0)


def _group_block(seq_len, hp, group, bb):
    per = N_HEADS // hp
    return pl.BlockSpec((bb * seq_len, hp * HEAD_DIM), lambda b, hg: (b, group * per + hg))


def _kt_block(nc, hp, group, bb):
    per = N_HEADS // hp
    return pl.BlockSpec((bb * nc, hp * HEAD_DIM, LANES), lambda b, hg: (b, group * per + hg, 0))


def _stream_steps(nc):
    assert nc % STREAM_CHUNKS == 0
    return nc // STREAM_CHUNKS


def _step_block(s, nc):
    steps = _stream_steps(nc)
    return jnp.where(s < steps, s, 2 * steps - 1 - s)


def _stream_rows_block(bb, nc, group):
    return pl.BlockSpec((bb, STREAM_CHUNKS * CHUNK, GROUP_DIM),
                        lambda s: (0, _step_block(s, nc), group))


def _stream_out_block(bb, nc):
    steps = _stream_steps(nc)
    return pl.BlockSpec((bb, STREAM_CHUNKS * CHUNK, GROUP_DIM),
                        lambda s: (0, jnp.minimum(2 * steps - 1 - s, steps - 1), 0))


def _stream_chunk_block(bb, nc, tail, tail_index):
    return pl.BlockSpec((bb, STREAM_CHUNKS) + tail, lambda s: (0, _step_block(s, nc)) + tail_index)


def _ret_kernel(*refs, nc, bb, hp, has_init, emit_states, streamed):
    refs = list(refs)
    theta_ref, q_ref, kt_ref, v_ref, rg_ref, gn_ref = refs[:6]
    pos = 6
    s0_ref = None
    if has_init:
        s0_ref = refs[pos]
        pos += 1
    o_ref = refs[pos]
    pos += 1
    st_ref = None
    if emit_states:
        st_ref = refs[pos]
        pos += 1
    hist, s_scr, decay_scr, qdec_scr, kdec_scr, cdec_scr = refs[pos:pos + 6]
    ck = _Chunks(nc, bb, hp, streamed)
    hg = 0 if streamed else pl.program_id(1)
    hsel = pl.ds(hg * hp, hp)

    @pl.when(pl.program_id(0) == 0)
    def _():
        ii = lax.broadcasted_iota(jnp.int32, (CHUNK, CHUNK), 0)
        jj = lax.broadcasted_iota(jnp.int32, (CHUNK, CHUNK), 1)
        d = (ii - jj).astype(F32)
        p_col = lax.broadcasted_iota(jnp.int32, (CHUNK, HEAD_DIM), 0).astype(F32)
        p_row = lax.broadcasted_iota(jnp.int32, (1, LANES), 1).astype(F32)
        for hh in range(hp):
            h = hg * hp + hh
            lg_f = _log_sigmoid(jnp.full((1, LANES), theta_ref[0, h], F32))
            lg_b = _log_sigmoid(jnp.full((1, LANES), theta_ref[1, h], F32))
            decay_scr[h] = (jnp.where(d >= 0, jnp.exp(lg_f * jnp.maximum(d, 0.0)), 0.0)
                            + jnp.where(d <= 0, jnp.exp(lg_b * jnp.maximum(-d, 0.0)), 0.0))
            qdec_scr[0, h] = jnp.exp(lg_f[:, :1] * (p_col + 1.0))
            qdec_scr[1, h] = jnp.exp(lg_b[:, :1] * (CHUNK - p_col))
            kdec_scr[0, h] = jnp.exp(lg_f * (CHUNK - 1.0 - p_row))
            kdec_scr[1, h] = jnp.exp(lg_b * p_row)
            cdec_scr[0, h] = jnp.exp(lg_f * float(CHUNK))
            cdec_scr[1, h] = jnp.exp(lg_b * float(CHUNK))

    def table(ref, *lead):
        return _per_seq(ref[(*lead, hsel)], bb)

    def kv_update(c, v, direction):
        kd = (ck.keys_t(kt_ref, c).astype(F32)
              * table(kdec_scr, direction)).astype(BF16)
        s_scr[...] = s_scr[...] * table(cdec_scr, direction)[:, :, :1] + _bdot(kd, v, _NN)

    def init_state(direction):
        if has_init:
            for bi in range(bb):
                s_scr[bi * hp:(bi + 1) * hp] = s0_ref[bi, 0, direction]
        else:
            s_scr[...] = jnp.zeros((bb * hp, HEAD_DIM, HEAD_DIM), F32)

    def emit_state(direction):
        for bi in range(bb):
            st_ref[bi, 0, direction] = s_scr[bi * hp:(bi + 1) * hp]

    def fwd(c):
        hist[_chunk_index(c)] = s_scr[...].astype(BF16)
        kv_update(c, ck.heads(v_ref, c), 0)

    def bwd(c):
        q, v = ck.heads(q_ref, c), ck.heads(v_ref, c)
        kt = ck.keys_t(kt_ref, c)
        att = (_bdot(q, kt, _NN) * table(decay_scr)).astype(BF16)
        o = _bdot(att, v, _NN)
        if has_init or not _is_chunk(c, 0):
            o = o + _bdot(q, hist[_chunk_index(c)], _NN) * table(qdec_scr, 0)
        if has_init or not _is_chunk(c, nc - 1):
            o = o + _bdot(q, s_scr[...].astype(BF16), _NN) * table(qdec_scr, 1)
        y = _ln_rows(o)
        for bi in range(bb):
            for hh in range(hp):
                cs = slice(hh * HEAD_DIM, (hh + 1) * HEAD_DIM)
                gate = _silu(ck.rows(rg_ref, bi, c, cs).astype(F32))
                ck.store_rows(o_ref, bi, c, cs,
                              (y[bi * hp + hh] * gn_ref[:, cs] * gate).astype(BF16))
        kv_update(c, v, 1)

    _run_scan(nc, streamed, init_state, fwd, bwd, emit_state if emit_states else None)


def _ret_call(proj, kt, theta, gn, s0, *, batch, seq_len, bb, hp, emit_states, streamed=False):
    nc = seq_len // CHUNK
    has_init = s0 is not None
    assert N_HEADS % hp == 0 and batch % bb == 0
    if streamed:
        assert bb == batch and hp == N_HEADS and not emit_states
        proj = proj.reshape(batch, seq_len, N_STORED)
        kt = kt.reshape(batch, nc, 2 * GROUP_DIM, LANES)
        state_spec = pl.BlockSpec((bb, 1, 2, hp, HEAD_DIM, HEAD_DIM), lambda s: (0,) * 6)
        in_specs = [pl.BlockSpec(memory_space=pltpu.SMEM),
                    _stream_rows_block(bb, nc, 0),
                    _stream_chunk_block(bb, nc, (GROUP_DIM, LANES), (0, 0)),
                    _stream_rows_block(bb, nc, 1), _stream_rows_block(bb, nc, 2),
                    pl.BlockSpec((1, GROUP_DIM), lambda s: (0, 0))]
        blocks = [_nbytes((bb, STREAM_CHUNKS * CHUNK, GROUP_DIM), BF16)] * 5
        out_specs = [_stream_out_block(bb, nc)]
        out_shape = [jax.ShapeDtypeStruct((batch, seq_len, GROUP_DIM), BF16)]
        grid, semantics = (2 * _stream_steps(nc),), ("arbitrary",)
    else:
        state_spec = pl.BlockSpec((bb, 1, 2, hp, HEAD_DIM, HEAD_DIM),
                                  lambda b, hg: (b, 0, 0, hg, 0, 0))
        in_specs = [pl.BlockSpec(memory_space=pltpu.SMEM),
                    _group_block(seq_len, hp, 0, bb), _kt_block(nc, hp, 0, bb),
                    _group_block(seq_len, hp, 1, bb), _group_block(seq_len, hp, 2, bb),
                    pl.BlockSpec((1, hp * HEAD_DIM), lambda b, hg: (0, hg))]
        blocks = [_nbytes((bb * seq_len, hp * HEAD_DIM), BF16)] * 5
        out_specs = [pl.BlockSpec((bb * seq_len, hp * HEAD_DIM), lambda b, hg: (b, hg))]
        out_shape = [jax.ShapeDtypeStruct((batch * seq_len, GROUP_DIM), BF16)]
        grid, semantics = (batch // bb, N_HEADS // hp), ("arbitrary", "arbitrary")
    args = [theta, proj, kt, proj, proj, gn]
    if has_init:
        in_specs.append(state_spec)
        args.append(s0)
        blocks.append(_nbytes((bb, 2, hp, HEAD_DIM, HEAD_DIM), F32))
    if emit_states:
        out_specs.append(state_spec)
        out_shape.append(jax.ShapeDtypeStruct((batch, 1, 2, N_HEADS, HEAD_DIM, HEAD_DIM), F32))
        blocks.append(_nbytes((bb, 2, hp, HEAD_DIM, HEAD_DIM), F32))
    scratch_defs = [((nc, bb * hp, HEAD_DIM, HEAD_DIM), BF16), ((bb * hp, HEAD_DIM, HEAD_DIM), F32),
                    ((N_HEADS, CHUNK, CHUNK), F32), ((2, N_HEADS, CHUNK, HEAD_DIM), F32),
                    ((2, N_HEADS, 1, LANES), F32), ((2, N_HEADS, 1, LANES), F32)]
    outs = pl.pallas_call(
        functools.partial(_ret_kernel, nc=nc, bb=bb, hp=hp, has_init=has_init,
                          emit_states=emit_states, streamed=streamed),
        grid=grid,
        in_specs=in_specs, out_specs=out_specs, out_shape=out_shape,
        scratch_shapes=[pltpu.VMEM(s, d) for s, d in scratch_defs],
        compiler_params=pltpu.CompilerParams(
            dimension_semantics=semantics,
            vmem_limit_bytes=_vmem_limit(blocks, sum(_nbytes(s, d) for s, d in scratch_defs))),
        name="ret_grid" if has_init else "ret_seq",
    )(*args)
    if streamed:
        outs = [outs[0].reshape(batch * seq_len, GROUP_DIM)] + list(outs[1:])
    return outs


def _mlstm_kernel(*refs, nc, bb, hp, has_init, emit_states, streamed):
    refs = list(refs)
    q_ref, kt_ref, v_ref, mo_ref, g_ref, gn_ref = refs[:6]
    pos = 6
    c0_ref = n0_ref = m0_ref = None
    if has_init:
        c0_ref, n0_ref, m0_ref = refs[pos:pos + 3]
        pos += 3
    o_ref = refs[pos]
    pos += 1
    c_out = n_out = m_out = None
    if emit_states:
        c_out, n_out, m_out = refs[pos:pos + 3]
        pos += 3
    c_hist, n_hist, m_hist, c_scr, n_scr, m_scr = refs[pos:pos + 6]

    ck = _Chunks(nc, bb, hp, streamed)
    b = 0 if streamed else pl.program_id(0)
    hg = 0 if streamed else pl.program_id(1)
    ii = lax.broadcasted_iota(jnp.int32, (CHUNK, CHUNK), 0)
    jj = lax.broadcasted_iota(jnp.int32, (CHUNK, CHUNK), 1)
    eye = ii == jj
    lower = ii >= jj
    upper = ii <= jj

    ne = bb * hp

    def col(row):
        return jnp.sum(jnp.where(eye, row, 0.0), axis=2, keepdims=True)

    def gate_rows(c, kind):
        base = kind * N_HEADS + hg * hp
        return jnp.stack([ck.gate_row(g_ref, bi, c, base + hh)
                          for bi in range(bb) for hh in range(hp)])

    def direction_out(a, q, qf, v, c_b, n_row, m, b_row, i_row, mask):
        b_col = col(b_row)
        dm = jnp.where(mask, b_col - b_row + i_row, -jnp.inf)
        inter = b_col + m
        mt = jnp.maximum(inter, jnp.max(dm, axis=2, keepdims=True))
        w = jnp.exp(dm - mt)
        sp = jnp.exp(inter - mt)
        s = a * w
        num = _bdot(s.astype(BF16), v, _NN)
        den = jnp.sum(s, axis=2, keepdims=True)
        if c_b is not None:
            num = num + _bdot(q, c_b, _NN) * sp
            den = den + jnp.sum(qf * n_row, axis=2, keepdims=True) * sp
        return num / jnp.maximum(jnp.abs(den), jnp.exp(-mt))

    def state_update(c, v, b_row, i_row, last):
        m = m_scr[...][:, :, :1]
        b_last = b_row[:, :, last:last + 1]
        g = b_last - b_row + i_row
        m_new = jnp.maximum(b_last + m, jnp.max(g, axis=2, keepdims=True))
        wk = jnp.exp(g - m_new)
        sc = jnp.exp(b_last + m - m_new)
        kt = ck.keys_t(kt_ref, c)
        kw = (kt.astype(F32) * wk).astype(BF16)
        c_scr[...] = c_scr[...] * sc + _bdot(kw, v, _NN)
        wk8 = jnp.broadcast_to(wk, (ne, 8, CHUNK)).astype(BF16)
        n_scr[...] = n_scr[...] * sc + _bdot(wk8, kt, _NT)[:, :1, :]
        m_scr[...] = jnp.broadcast_to(m_new, (ne, 1, LANES))

    def init_state(direction):
        if has_init:
            for bi in range(bb):
                c_scr[bi * hp:(bi + 1) * hp] = c0_ref[bi, 0, direction]
            n_scr[...] = jnp.stack([n0_ref[bi, 0, direction, pl.ds(hg * hp + hh, 1), :]
                                    for bi in range(bb) for hh in range(hp)])
            m_scr[...] = jnp.stack([
                jnp.full((1, LANES),
                         m0_ref[((b * bb + bi) * 2 + direction) * N_HEADS + hg * hp + hh], F32)
                for bi in range(bb) for hh in range(hp)])
        else:
            c_scr[...] = jnp.zeros((ne, HEAD_DIM, HEAD_DIM), F32)
            n_scr[...] = jnp.zeros((ne, 1, HEAD_DIM), F32)
            m_scr[...] = jnp.zeros((ne, 1, LANES), F32)

    def emit_state(direction):
        for bi in range(bb):
            c_out[bi, 0, direction] = c_scr[bi * hp:(bi + 1) * hp]
            for hh in range(hp):
                n_out[bi, 0, direction, pl.ds(hh, 1), :] = n_scr[bi * hp + hh]
                m_out[bi, pl.ds(direction * N_HEADS + hh, 1), :] = m_scr[bi * hp + hh]

    def fwd(c):
        ci = _chunk_index(c)
        c_hist[ci] = c_scr[...].astype(BF16)
        n_hist[ci] = n_scr[...]
        m_hist[ci] = m_scr[...]
        state_update(c, ck.heads(v_ref, c), gate_rows(c, 1), gate_rows(c, 0), CHUNK - 1)

    def bwd(c):
        i_f, b_f, i_b, b_b = (gate_rows(c, kind) for kind in (0, 1, 2, 3))
        q, v = ck.heads(q_ref, c), ck.heads(v_ref, c)
        qf = q.astype(F32)
        a = _bdot(q, ck.keys_t(kt_ref, c), _NN)
        zero_f = not has_init and _is_chunk(c, 0)
        zero_b = not has_init and _is_chunk(c, nc - 1)
        ci = _chunk_index(c)
        h_f = direction_out(a, q, qf, v, None if zero_f else c_hist[ci], n_hist[ci],
                            m_hist[ci][:, :, :1], b_f, i_f, lower)
        h_b = direction_out(a, q, qf, v, None if zero_b else c_scr[...].astype(BF16),
                            n_scr[...], m_scr[...][:, :, :1], b_b, i_b, upper)
        y = _ln_rows(h_f + h_b)
        for bi in range(bb):
            for hh in range(hp):
                cs = slice(hh * HEAD_DIM, (hh + 1) * HEAD_DIM)
                gate = jax.nn.sigmoid(ck.rows(mo_ref, bi, c, cs).astype(F32))
                ck.store_rows(o_ref, bi, c, cs,
                              (gate * (y[bi * hp + hh] * gn_ref[:, cs])).astype(BF16))
        state_update(c, v, b_b, i_b, 0)

    _run_scan(nc, streamed, init_state, fwd, bwd, emit_state if emit_states else None)


def _mlstm_call(proj, kt, gates, gn, init, *, batch, seq_len, bb, hp, emit_states,
                streamed=False):
    nc = seq_len // CHUNK
    has_init = init is not None
    assert N_HEADS % hp == 0 and (hp == N_HEADS or not emit_states) and batch % bb == 0
    if streamed:
        assert bb == batch and hp == N_HEADS and not emit_states
        proj = proj.reshape(batch, seq_len, N_STORED)
        kt = kt.reshape(batch, nc, 2 * GROUP_DIM, LANES)
        gates = gates.reshape(batch, nc, N_GATES, LANES)
        c_spec = pl.BlockSpec((bb, 1, 2, hp, HEAD_DIM, HEAD_DIM), lambda s: (0,) * 6)
        n_spec = pl.BlockSpec((bb, 1, 2, N_HEADS, HEAD_DIM), lambda s: (0,) * 5)
        in_specs = [_stream_rows_block(bb, nc, 3),
                    _stream_chunk_block(bb, nc, (GROUP_DIM, LANES), (1, 0)),
                    _stream_rows_block(bb, nc, 4), _stream_rows_block(bb, nc, 5),
                    _stream_chunk_block(bb, nc, (N_GATES, LANES), (0, 0)),
                    pl.BlockSpec((1, GROUP_DIM), lambda s: (0, 0))]
        blocks = ([_nbytes((bb, STREAM_CHUNKS * CHUNK, GROUP_DIM), BF16)] * 5
                  + [_nbytes((bb, STREAM_CHUNKS * N_GATES, LANES), F32)])
        out_specs = [_stream_out_block(bb, nc)]
        out_shape = [jax.ShapeDtypeStruct((batch, seq_len, GROUP_DIM), BF16)]
        grid, semantics = (2 * _stream_steps(nc),), ("arbitrary",)
    else:
        c_spec = pl.BlockSpec((bb, 1, 2, hp, HEAD_DIM, HEAD_DIM),
                              lambda b, hg: (b, 0, 0, hg, 0, 0))
        n_spec = pl.BlockSpec((bb, 1, 2, N_HEADS, HEAD_DIM), lambda b, hg: (b, 0, 0, 0, 0))
        in_specs = [_group_block(seq_len, hp, 3, bb), _kt_block(nc, hp, 1, bb),
                    _group_block(seq_len, hp, 4, bb), _group_block(seq_len, hp, 5, bb),
                    pl.BlockSpec((bb * nc, N_GATES, LANES), lambda b, hg: (b, 0, 0)),
                    pl.BlockSpec((1, hp * HEAD_DIM), lambda b, hg: (0, hg))]
        blocks = ([_nbytes((bb * seq_len, hp * HEAD_DIM), BF16)] * 5
                  + [_nbytes((bb * nc, N_GATES, LANES), F32)])
        out_specs = [pl.BlockSpec((bb * seq_len, hp * HEAD_DIM), lambda b, hg: (b, hg))]
        out_shape = [jax.ShapeDtypeStruct((batch * seq_len, GROUP_DIM), BF16)]
        grid, semantics = (batch // bb, N_HEADS // hp), ("arbitrary", "arbitrary")
    args = [proj, kt, proj, proj, gates, gn]
    if has_init:
        in_specs += [c_spec, n_spec, pl.BlockSpec(memory_space=pltpu.SMEM)]
        args += list(init)
        blocks.append(_nbytes((bb, 2, hp, HEAD_DIM, HEAD_DIM), F32))
    if emit_states:
        out_specs += [c_spec, n_spec, pl.BlockSpec((bb, 2 * N_HEADS, LANES), lambda b, hg: (b, 0, 0))]
        out_shape += [jax.ShapeDtypeStruct((batch, 1, 2, N_HEADS, HEAD_DIM, HEAD_DIM), F32),
                      jax.ShapeDtypeStruct((batch, 1, 2, N_HEADS, HEAD_DIM), F32),
                      jax.ShapeDtypeStruct((batch, 2 * N_HEADS, LANES), F32)]
        blocks.append(_nbytes((bb, 2, hp, HEAD_DIM, HEAD_DIM), F32))
    ne = bb * hp
    scratch_defs = [((nc, ne, HEAD_DIM, HEAD_DIM), BF16), ((nc, ne, 1, HEAD_DIM), F32),
                    ((nc, ne, 1, LANES), F32), ((ne, HEAD_DIM, HEAD_DIM), F32),
                    ((ne, 1, HEAD_DIM), F32), ((ne, 1, LANES), F32)]
    scratch_bytes = sum(_nbytes(s, d) for s, d in scratch_defs) + 16 * nc * ne * HEAD_DIM * 4
    outs = pl.pallas_call(
        functools.partial(_mlstm_kernel, nc=nc, bb=bb, hp=hp, has_init=has_init,
                          emit_states=emit_states, streamed=streamed),
        grid=grid,
        in_specs=in_specs, out_specs=out_specs, out_shape=out_shape,
        scratch_shapes=[pltpu.VMEM(s, d) for s, d in scratch_defs],
        compiler_params=pltpu.CompilerParams(
            dimension_semantics=semantics,
            vmem_limit_bytes=_vmem_limit(blocks, scratch_bytes)),
        name="mlstm_grid" if has_init else "mlstm_seq",
    )(*args)
    if streamed:
        outs = [outs[0].reshape(batch * seq_len, GROUP_DIM)] + list(outs[1:])
    return outs


def _mix_out_kernel(mr_ref, mm_ref, w_ref, x_ref, g1_ref, lg_ref, lb_ref, o_ref):
    for rc in range(o_ref.shape[0] // MIX_ROW_CHUNK):
        rows = slice(rc * MIX_ROW_CHUNK, (rc + 1) * MIX_ROW_CHUNK)
        mix = (jnp.dot(mr_ref[rows, :], w_ref[0:GROUP_DIM, :], preferred_element_type=F32)
               + jnp.dot(mm_ref[rows, :], w_ref[GROUP_DIM:, :], preferred_element_type=F32))
        o_ref[rows, :] = (_ln_rows(ALPHA * x_ref[rows, :] + g1_ref[0] * mix) * lg_ref[...]
                          + lb_ref[...])


def _mix_out_call(mix_r, mix_m, w_out_b, x2d, mod3, mod_row, ln_g, ln_b, *, tm):
    tokens = x2d.shape[0]
    blocks = [_nbytes((tm, GROUP_DIM), BF16)] * 2 + [_nbytes((D_MODEL, D_MODEL), BF16)] \
        + [_nbytes((tm, D_MODEL), F32)] * 2
    return pl.pallas_call(
        _mix_out_kernel,
        grid=(tokens // tm,),
        in_specs=[pl.BlockSpec((tm, GROUP_DIM), lambda m: (m, 0)),
                  pl.BlockSpec((tm, GROUP_DIM), lambda m: (m, 0)),
                  pl.BlockSpec((D_MODEL, D_MODEL), lambda m: (0, 0)),
                  pl.BlockSpec((tm, D_MODEL), lambda m: (m, 0)),
                  pl.BlockSpec((1, 1, D_MODEL), lambda m: (mod_row(m) * 6 + 2, 0, 0)),
                  pl.BlockSpec((1, D_MODEL), lambda m: (0, 0)),
                  pl.BlockSpec((1, D_MODEL), lambda m: (0, 0))],
        out_specs=pl.BlockSpec((tm, D_MODEL), lambda m: (m, 0)),
        out_shape=jax.ShapeDtypeStruct((tokens, D_MODEL), F32),
        compiler_params=pltpu.CompilerParams(
            dimension_semantics=("parallel",), vmem_limit_bytes=_vmem_limit(blocks)),
        name="mix_out",
    )(mix_r, mix_m, w_out_b, x2d, mod3, ln_g, ln_b)


def _ffn_kernel(x_ref, sh_ref, sc_ref, g2_ref, wu_ref, wg_ref, cv_ref, wd_ref, lg_ref, lb_ref,
                o_ref, h_scr, *, period):
    j = pl.program_id(1)

    def row_chunks():
        for rc in range(o_ref.shape[0] // MIX_ROW_CHUNK):
            yield slice(rc * MIX_ROW_CHUNK, (rc + 1) * MIX_ROW_CHUNK)

    @pl.when(j == 0)
    def _():
        for rows in row_chunks():
            h = _ln_rows(x_ref[rows, :]) * (1.0 + sc_ref[0]) + sh_ref[0]
            h_scr[rows, :] = h.astype(BF16)
        o_ref[...] = jnp.zeros_like(o_ref)

    lane = lax.broadcasted_iota(jnp.int32, (1, FF_TN), 1)
    duplicate = (_ff_tile(pl.program_id(0), j) == FF_NJ - 1) & (lane < FF_OVERLAP)
    for rows in row_chunks():
        hb = h_scr[rows, :]
        u = jnp.dot(hb, wu_ref[...], preferred_element_type=F32)
        g = jnp.dot(hb, wg_ref[...], preferred_element_type=F32)
        hid = _silu(_conv3_rows(u, cv_ref[0], period)) * g
        hid = jnp.where(duplicate, 0.0, hid).astype(BF16)
        o_ref[rows, :] += jnp.dot(hid, wd_ref[...], preferred_element_type=F32)

    @pl.when(j == FF_NJ - 1)
    def _():
        for rows in row_chunks():
            y = ALPHA * x_ref[rows, :] + g2_ref[0] * o_ref[rows, :]
            o_ref[rows, :] = _ln_rows(y) * lg_ref[...] + lb_ref[...]


def _ff_tile(m, j):
    return jnp.where(m % 2 == 1, FF_NJ - 1 - j, j)


def _ff_offset(j):
    return pl.multiple_of(jnp.minimum(j * FF_TN, D_FF - FF_TN), LANES)


def _ffn_call(x1, mod3, mod_row, w_up_b, conv_tiles, w_down_b, ln_g, ln_b, *, period, tm):
    tokens = x1.shape[0]
    up_tile = (pl.Element(D_MODEL), pl.Element(FF_TN))
    blocks = [_nbytes((tm, D_MODEL), F32)] * 2 + [_nbytes((D_MODEL, FF_TN), BF16)] * 3 \
        + [_nbytes((8, FF_TN), F32)]
    scratch_bytes = _nbytes((tm, D_MODEL), BF16)
    return pl.pallas_call(
        functools.partial(_ffn_kernel, period=period),
        grid=(tokens // tm, FF_NJ),
        in_specs=[pl.BlockSpec((tm, D_MODEL), lambda m, j: (m, 0)),
                  pl.BlockSpec((1, 1, D_MODEL), lambda m, j: (mod_row(m) * 6 + 3, 0, 0)),
                  pl.BlockSpec((1, 1, D_MODEL), lambda m, j: (mod_row(m) * 6 + 4, 0, 0)),
                  pl.BlockSpec((1, 1, D_MODEL), lambda m, j: (mod_row(m) * 6 + 5, 0, 0)),
                  pl.BlockSpec(up_tile, lambda m, j: (0, _ff_offset(_ff_tile(m, j)))),
                  pl.BlockSpec(up_tile, lambda m, j: (
                      0, pl.multiple_of(D_FF + _ff_offset(_ff_tile(m, j)), LANES))),
                  pl.BlockSpec((1, 3, FF_TN), lambda m, j: (_ff_tile(m, j), 0, 0)),
                  pl.BlockSpec((pl.Element(FF_TN), pl.Element(D_MODEL)),
                               lambda m, j: (_ff_offset(_ff_tile(m, j)), 0)),
                  pl.BlockSpec((1, D_MODEL), lambda m, j: (0, 0)),
                  pl.BlockSpec((1, D_MODEL), lambda m, j: (0, 0))],
        out_specs=pl.BlockSpec((tm, D_MODEL), lambda m, j: (m, 0)),
        out_shape=jax.ShapeDtypeStruct((tokens, D_MODEL), F32),
        scratch_shapes=[pltpu.VMEM((tm, D_MODEL), BF16)],
        compiler_params=pltpu.CompilerParams(
            dimension_semantics=("parallel", "arbitrary"),
            vmem_limit_bytes=_vmem_limit(blocks, scratch_bytes)),
        name="ffn",
    )(x1, mod3, mod3, mod3, w_up_b, w_up_b, conv_tiles, w_down_b, ln_g, ln_b)


def _rope_tables(seq_len):
    quarter = HEAD_DIM // 4
    t = jnp.arange(seq_len)
    row = (t // GRID_W).astype(F32)
    col = (t % GRID_W).astype(F32)
    inv = ROPE_BASE ** (-jnp.arange(quarter, dtype=F32) / quarter)
    ang_r, ang_c = row[:, None] * inv, col[:, None] * inv
    cos_t = jnp.concatenate([jnp.cos(ang_r)] * 2 + [jnp.cos(ang_c)] * 2, axis=1)
    sin_t = jnp.concatenate([-jnp.sin(ang_r), jnp.sin(ang_r), -jnp.sin(ang_c), jnp.sin(ang_c)], axis=1)
    return cos_t, sin_t


def kernel(x_prompt, x_sample, state_ret, state_mlstm_C, state_mlstm_n, state_mlstm_m, c, c_ctx,
           w_mod, b_mod, w_in, b_gate, conv_qk, ret_theta, gn_ret, gn_mlstm, w_out,
           ln1_g, ln1_b, w_up, conv_ff, w_down, ln2_g, ln2_b):
    bp, lp, _ = x_prompt.shape
    bs, ls, _ = x_sample.shape
    layer = 0
    tm_mix, tm, tm_ffn = 1024, 512, 1024

    cvec = jnp.concatenate([c_ctx[None], c, jnp.zeros((8 - 1 - bs, D_MODEL), F32)], axis=0)
    mod = _mod_call(cvec, w_mod[layer], b_mod[layer][None])
    mod3 = mod.reshape(8 * 6, 1, D_MODEL)

    def row_prompt(tile):
        return lambda m: 0

    def row_sample(tile):
        return lambda m: 1 + (m * tile) // ls

    w_l = w_in[layer].T
    w_gate = w_l[N_PROJ:]
    b_gate_col = b_gate[layer].reshape(N_GATES, 1)
    ff_offsets = [min(j * FF_TN, D_FF - FF_TN) for j in range(FF_NJ)]
    conv_tiles = jnp.stack([conv_ff[layer][:, o:o + FF_TN] for o in ff_offsets])
    gn_r = gn_ret[layer].reshape(1, GROUP_DIM)
    gn_m = gn_mlstm[layer].reshape(1, GROUP_DIM)
    theta = ret_theta[layer]
    ln1 = (ln1_g[layer][None], ln1_b[layer][None])
    ln2 = (ln2_g[layer][None], ln2_b[layer][None])

    def mixer(x, mod_row, rope, period, bb, hp, ret_init, mlstm_init, emit_states, stage=(),
              streamed=False):
        batch, seq_len, _ = x.shape
        x2d = x.reshape(batch * seq_len, D_MODEL)
        h2d, gates = _ln_mod_call(x2d, mod3, mod_row(tm_mix), w_gate, b_gate_col, tm=tm_mix)
        proj, kt, *staged = _mix_in_call(h2d, w_l, conv_qk[layer], rope, seq_len=seq_len,
                                         period=period, tm=tm_mix, stage=stage)
        ret = _ret_call(proj, kt, theta, gn_r, ret_init, batch=batch, seq_len=seq_len, bb=bb,
                        hp=hp, emit_states=emit_states, streamed=streamed)
        mls = _mlstm_call(proj, kt, gates, gn_m, mlstm_init, batch=batch, seq_len=seq_len, bb=bb,
                          hp=hp, emit_states=emit_states, streamed=streamed)
        return x2d, ret, mls, staged

    def channel_mix(x2d, shape, ret, mls, mod_row, period, weights):
        w_up_b, w_down_b, w_out_b = weights
        x1 = _mix_out_call(ret[0], mls[0], w_out_b, x2d, mod3, mod_row(tm), *ln1, tm=tm)
        y = _ffn_call(x1, mod3, mod_row(tm_ffn), w_up_b, conv_tiles, w_down_b, *ln2,
                      period=period, tm=tm_ffn)
        return y.reshape(shape)

    xp, ret_p, mls_p, weights_b = mixer(x_prompt, row_prompt, None, lp, 4, N_HEADS, None, None,
                                        True, stage=(w_up[layer], w_down[layer], w_out[layer]))
    m0 = state_mlstm_m.reshape(-1)
    xs, ret_s, mls_s, _ = mixer(x_sample, row_sample, _rope_tables(ls), GRID_W, bs, N_HEADS,
                                state_ret, (state_mlstm_C, state_mlstm_n, m0), False,
                                streamed=True)
    y_prompt = channel_mix(xp, x_prompt.shape, ret_p, mls_p, row_prompt, lp, weights_b)
    y_sample = channel_mix(xs, x_sample.shape, ret_s, mls_s, row_sample, GRID_W, weights_b)
    ret_states, mls_states = ret_p[1:], mls_p[1:]

    new_ret = ret_states[0]
    new_c, new_n, new_m = mls_states
    new_m = new_m[:, :, 0].reshape(bp, 1, 2, N_HEADS)
    return (y_prompt, y_sample, new_ret, new_c, new_n, new_m)
```
